```python
import math
import jax, jax.numpy as jnp
from jax import lax
import numpy as np

D_MODEL = 1024
BATCH = 4
SEQ = 8192
DEPTH = 2

HEAD_DIM = 64
Q_BLOCK = 128
MIX_W = 512
N_BRANCH = 4
SWA_HEADS = 8
SWA_KV_HEADS = 2
SWA_WINDOW = 128
CONV_K = 3
NSA_HEADS = 8
NSA_KV_HEADS = 2
CMP_BLOCK = 32
CMP_STRIDE = 16
CMP_HIDDEN = 256
SEL_BLOCK = 64
SEL_TOPK = 16
NSA_WINDOW = 512
RET_HEADS = 4
RET_QK_DIM = 64
RET_V_DIM = 128
RET_CHUNK = 128
ROPE_BASE = 10000.0
D_FF = 2816
EPS = 1e-6

SWA_Q = SWA_HEADS * HEAD_DIM
SWA_KV = SWA_KV_HEADS * HEAD_DIM
NSA_Q = NSA_HEADS * HEAD_DIM
NSA_KV = NSA_KV_HEADS * HEAD_DIM
RET_QK = RET_HEADS * RET_QK_DIM
RET_V = RET_HEADS * RET_V_DIM
IN_WIDTHS = (
    SWA_Q, SWA_KV, SWA_KV,
    MIX_W, MIX_W, MIX_W,
    NSA_Q, NSA_KV, NSA_KV, NSA_KV, NSA_KV, NSA_KV, NSA_KV,
    NSA_HEADS * 3,
    RET_QK, RET_QK, RET_V, RET_V,
    N_BRANCH * D_MODEL,
)
IN_TOTAL = sum(IN_WIDTHS)
SPLIT_POINTS = tuple(int(v) for v in np.cumsum(IN_WIDTHS)[:-1])

kernel_name = "hybrid_parallel_gated_swa_conv_nsa_retention"


def rms_norm(x, g):
    xf = x.astype(jnp.float32)
    y = xf * lax.rsqrt(jnp.mean(xf * xf, axis=-1, keepdims=True) + EPS)
    return (y * g.astype(jnp.float32)).astype(x.dtype)


def heads(t, n):
    b, s, _ = t.shape
    return t.reshape(b, s, n, -1).transpose(0, 2, 1, 3)


def merge_heads(t):
    b, n, s, d = t.shape
    return t.transpose(0, 2, 1, 3).reshape(b, s, n * d)


def swiglu(x, w_gate, w_up, w_down):
    return (jax.nn.silu(x @ w_gate) * (x @ w_up)) @ w_down


def masked_softmax(s, mask, sink=None):
    s = jnp.where(mask, s.astype(jnp.float32), -jnp.inf)
    m = jnp.max(s, axis=-1, keepdims=True)
    if sink is not None:
        m = jnp.maximum(m, sink)
    m = jnp.where(jnp.isfinite(m), m, 0.0)
    p = jnp.exp(s - m)
    denom = jnp.sum(p, axis=-1, keepdims=True)
    if sink is not None:
        denom = denom + jnp.exp(sink - m)
    return p / jnp.maximum(denom, 1e-30)


def banded_attention(q, k, v, window, sink=None):
    b, h, t, d = q.shape
    g = k.shape[1]
    r = h // g
    n_blk = t // Q_BLOCK
    pad = window
    span = pad + Q_BLOCK
    kp = jnp.pad(k, ((0, 0), (0, 0), (pad, 0), (0, 0)))
    vp = jnp.pad(v, ((0, 0), (0, 0), (pad, 0), (0, 0)))
    qb = q.reshape(b, g, r, n_blk, Q_BLOCK, d).transpose(3, 0, 1, 2, 4, 5)
    scale = d ** -0.5

    def one_block(args):
        qi, i = args
        start = i * Q_BLOCK
        ki = lax.dynamic_slice_in_dim(kp, start, span, axis=2)
        vi = lax.dynamic_slice_in_dim(vp, start, span, axis=2)
        qpos = start + jnp.arange(Q_BLOCK)
        kpos = start - pad + jnp.arange(span)
        diff = qpos[:, None] - kpos[None, :]
        mask = (kpos[None, :] >= 0) & (diff >= 0) & (diff < window)
        s = jnp.einsum('bgrqd,bgkd->bgrqk', qi, ki) * scale
        p = masked_softmax(s, mask, sink)
        return jnp.einsum('bgrqk,bgkd->bgrqd', p.astype(vi.dtype), vi)

    out = lax.map(one_block, (qb, jnp.arange(n_blk)))
    return out.transpose(1, 2, 3, 0, 4, 5).reshape(b, h, t, d)


def swa_sink_mixer(q, k, v, q_gain, k_gain, sinks):
    q = rms_norm(heads(q, SWA_HEADS), q_gain)
    k = rms_norm(heads(k, SWA_KV_HEADS), k_gain)
    v = heads(v, SWA_KV_HEADS)
    sink = sinks.astype(jnp.float32).reshape(1, SWA_KV_HEADS, SWA_HEADS // SWA_KV_HEADS, 1, 1)
    return merge_heads(banded_attention(q, k, v, SWA_WINDOW, sink))


def short_conv_mixer(x_in, gate_b, gate_c, conv_w):
    z = gate_c * x_in
    y = lax.conv_general_dilated(z, conv_w.astype(z.dtype), window_strides=(1,),
                                 padding=((CONV_K - 1, 0),),
                                 dimension_numbers=('NWC', 'WIO', 'NWC'),
                                 feature_group_count=z.shape[-1])
    return gate_b * y


def nsa_mixer(q, kc, vc, ks, vs, kw, vw, gate_logits, q_gain, k_gain,
              pos_k, pos_v, wk1, wk2, wv1, wv2):
    b, t, _ = q.shape
    g = NSA_KV_HEADS
    r = NSA_HEADS // g
    d = HEAD_DIM
    q = rms_norm(heads(q, NSA_HEADS), q_gain)

    kw = rms_norm(heads(kw, g), k_gain[2])
    o_win = banded_attention(q, kw, heads(vw, g), NSA_WINDOW)

    n_cmp = (t - CMP_BLOCK) // CMP_STRIDE + 1
    cmp_start = jnp.arange(n_cmp) * CMP_STRIDE
    cmp_end = cmp_start + CMP_BLOCK - 1
    idx = cmp_start[:, None] + jnp.arange(CMP_BLOCK)[None, :]

    def compress(tok, pe, w1, w2):
        blk = heads(tok, g)[:, :, idx] + pe
        blk = blk.reshape(b, g, n_cmp, CMP_BLOCK * d)
        return jax.nn.gelu(blk @ w1) @ w2

    k_cmp = rms_norm(compress(kc, pos_k, wk1, wk2), k_gain[0])
    v_cmp = compress(vc, pos_v, wv1, wv2)

    n_sel = t // SEL_BLOCK
    sel_k = min(SEL_TOPK, n_sel)
    ks_blk = rms_norm(heads(ks, g), k_gain[1]).reshape(b, g, n_sel, SEL_BLOCK, d)
    vs_blk = heads(vs, g).reshape(b, g, n_sel, SEL_BLOCK, d)
    sel_start = jnp.arange(n_sel) * SEL_BLOCK
    overlap = ((cmp_start[:, None] < sel_start[None, :] + SEL_BLOCK)
               & (cmp_end[:, None] >= sel_start[None, :])).astype(jnp.float32)
    blk_id = jnp.arange(n_sel)
    bi = jnp.arange(b)[:, None, None, None]
    gi = jnp.arange(g)[None, :, None, None]

    n_blk = t // Q_BLOCK
    qb = q.reshape(b, g, r, n_blk, Q_BLOCK, d).transpose(3, 0, 1, 2, 4, 5)
    scale = d ** -0.5

    def one_block(args):
        qi, i = args
        qpos = i * Q_BLOCK + jnp.arange(Q_BLOCK)
        s = jnp.einsum('bgrqd,bgnd->bgrqn', qi, k_cmp) * scale
        p_cmp = masked_softmax(s, cmp_end[None, :] <= qpos[:, None])
        o_cmp = jnp.einsum('bgrqn,bgnd->bgrqd', p_cmp.astype(qi.dtype), v_cmp)
        imp = jnp.einsum('bgrqn,ns->bgqs', p_cmp, overlap)
        cur = qpos // SEL_BLOCK
        causal = blk_id[None, :] <= cur[:, None]
        forced = ((blk_id[None, :] == 0) | (blk_id[None, :] == cur[:, None])
                  | (blk_id[None, :] == cur[:, None] - 1))
        imp = jnp.where(forced, jnp.inf, imp)
        imp = jnp.where(causal, imp, -jnp.inf)
        top_s, top_i = lax.top_k(imp, sel_k)
        valid = top_s > -jnp.inf
        k_g = ks_blk[bi, gi, top_i]
        v_g = vs_blk[bi, gi, top_i]
        tok_pos = top_i[..., None] * SEL_BLOCK + jnp.arange(SEL_BLOCK)
        mask = valid[..., None] & (tok_pos <= qpos[:, None, None])
        mask = mask.reshape(b, g, 1, Q_BLOCK, sel_k * SEL_BLOCK)
        s = jnp.einsum('bgrqd,bgqkld->bgrqkl', qi, k_g) * scale
        s = s.reshape(b, g, r, Q_BLOCK, sel_k * SEL_BLOCK)
        p = masked_softmax(s, mask)
        v_g = v_g.reshape(b, g, Q_BLOCK, sel_k * SEL_BLOCK, d)
        o_sel = jnp.einsum('bgrqm,bgqmd->bgrqd', p.astype(v_g.dtype), v_g)
        return o_cmp, o_sel

    o_cmp, o_sel = lax.map(one_block, (qb, jnp.arange(n_blk)))
    o_cmp = o_cmp.transpose(1, 2, 3, 0, 4, 5).reshape(b, NSA_HEADS, t, d)
    o_sel = o_sel.transpose(1, 2, 3, 0, 4, 5).reshape(b, NSA_HEADS, t, d)
    gates = jax.nn.sigmoid(gate_logits.astype(jnp.float32)).astype(q.dtype)
    gates = gates.reshape(b, t, NSA_HEADS, 3).transpose(0, 2, 1, 3)
    o = gates[..., 0:1] * o_cmp + gates[..., 1:2] * o_sel + gates[..., 2:3] * o_win
    return merge_heads(o)


def rotary(x, pos):
    half = x.shape[-1] // 2
    inv = ROPE_BASE ** (-jnp.arange(half, dtype=jnp.float32) / half)
    ang = pos.astype(jnp.float32)[:, None] * inv[None, :]
    cos = jnp.cos(ang).astype(x.dtype)
    sin = jnp.sin(ang).astype(x.dtype)
    x1, x2 = x[..., :half], x[..., half:]
    return jnp.concatenate([x1 * cos - x2 * sin, x1 * sin + x2 * cos], axis=-1)


def retention_mixer(q, k, v, gate, norm_gain):
    b, t, _ = q.shape
    h, c = RET_HEADS, RET_CHUNK
    nc = t // c
    dt = q.dtype
    pos = jnp.arange(t)
    q = rotary(heads(q, h), pos)
    k = rotary(heads(k, h), pos) * (RET_QK_DIM ** -0.5)
    v = heads(v, h)
    log_gamma = jnp.log(1.0 - 2.0 ** (-5.0 - jnp.arange(h, dtype=jnp.float32)))
    j = jnp.arange(c, dtype=jnp.float32)
    diff = j[:, None] - j[None, :]
    dmask = jnp.where(diff >= 0, jnp.exp(diff * log_gamma[:, None, None]), 0.0)
    qc = q.reshape(b, h, nc, c, RET_QK_DIM)
    kc = k.reshape(b, h, nc, c, RET_QK_DIM)
    vc = v.reshape(b, h, nc, c, RET_V_DIM)
    att = jnp.einsum('bhncd,bhnmd->bhncm', qc, kc) * dmask[:, None].astype(dt)
    o = jnp.einsum('bhncm,bhnme->bhnce', att, vc)
    zeta = jnp.exp((c - 1 - j) * log_gamma[:, None]).astype(dt)
    s_chunk = jnp.einsum('bhnmd,bhnme->nbhde', kc * zeta[:, None, :, None], vc)
    decay_chunk = jnp.exp(c * log_gamma).astype(dt)[None, :, None, None]

    def step(r_prev, s_i):
        return r_prev * decay_chunk + s_i, r_prev

    _, r_before = lax.scan(step, jnp.zeros_like(s_chunk[0]), s_chunk)
    xi = jnp.exp((j + 1.0) * log_gamma[:, None]).astype(dt)
    o = o + jnp.einsum('bhncd,nbhde->bhnce', qc * xi[:, None, :, None], r_before)
    o = o.reshape(b, h, t, RET_V_DIM).astype(jnp.float32)
    mu = jnp.mean(o, axis=-1, keepdims=True)
    var = jnp.mean(jnp.square(o - mu), axis=-1, keepdims=True)
    o = ((o - mu) * lax.rsqrt(var + EPS)).astype(dt)
    o = merge_heads(o) * norm_gain
    return jax.nn.silu(gate) * o


def setup_inputs(seed: int = 0) -> dict:
    key = jax.random.key(seed)
    ks = iter(jax.random.split(key, 40))
    f32 = jnp.float32

    def nrm(shape, scale):
        return jax.random.normal(next(ks), shape, f32) * scale

    def gain(shape):
        return 1.0 + 0.02 * jax.random.normal(next(ks), shape, f32)

    L, D = DEPTH, D_MODEL
    return {
        "x": jax.random.normal(next(ks), (BATCH, SEQ, D), f32),
        "ffn1_norm": gain((L, D)),
        "ffn1_w_gate": nrm((L, D, D_FF), D ** -0.5),
        "ffn1_w_up": nrm((L, D, D_FF), D ** -0.5),
        "ffn1_w_down": nrm((L, D_FF, D), D_FF ** -0.5),
        "mix_norm": gain((L, D)),
        "w_in": nrm((L, D, IN_TOTAL), D ** -0.5),
        "merge_gate_bias": nrm((L, N_BRANCH * D), 0.01),
        "swa_q_gain": gain((L, HEAD_DIM)),
        "swa_k_gain": gain((L, HEAD_DIM)),
        "swa_sinks": nrm((L, SWA_HEADS), 1.0),
        "conv_w": nrm((L, CONV_K, 1, MIX_W), CONV_K ** -0.5),
        "nsa_q_gain": gain((L, HEAD_DIM)),
        "nsa_k_gain": gain((L, 3, HEAD_DIM)),
        "cmp_pos_k": nrm((L, CMP_BLOCK, HEAD_DIM), 0.02),
        "cmp_pos_v": nrm((L, CMP_BLOCK, HEAD_DIM), 0.02),
        "cmp_wk1": nrm((L, CMP_BLOCK * HEAD_DIM, CMP_HIDDEN), (CMP_BLOCK * HEAD_DIM) ** -0.5),
        "cmp_wk2": nrm((L, CMP_HIDDEN, HEAD_DIM), CMP_HIDDEN ** -0.5),
        "cmp_wv1": nrm((L, CMP_BLOCK * HEAD_DIM, CMP_HIDDEN), (CMP_BLOCK * HEAD_DIM) ** -0.5),
        "cmp_wv2": nrm((L, CMP_HIDDEN, HEAD_DIM), CMP_HIDDEN ** -0.5),
        "ret_norm_gain": gain((L, RET_V)),
        "w_branch": nrm((L, N_BRANCH, MIX_W, D), MIX_W ** -0.5),
        "w_out": nrm((L, D, D), D ** -0.5),
        "ffn2_norm": gain((L, D)),
        "ffn2_w_gate": nrm((L, D, D_FF), D ** -0.5),
        "ffn2_w_up": nrm((L, D, D_FF), D ** -0.5),
        "ffn2_w_down": nrm((L, D_FF, D), D_FF ** -0.5),
    }


def reference(x, ffn1_norm, ffn1_w_gate, ffn1_w_up, ffn1_w_down, mix_norm, w_in,
              merge_gate_bias, swa_q_gain, swa_k_gain, swa_sinks, conv_w, nsa_q_gain,
              nsa_k_gain, cmp_pos_k, cmp_pos_v, cmp_wk1, cmp_wk2, cmp_wv1, cmp_wv2,
              ret_norm_gain, w_branch, w_out, ffn2_norm, ffn2_w_gate, ffn2_w_up,
              ffn2_w_down):
    b, t, _ = x.shape
    for l in range(DEPTH):
        x = x + 0.5 * swiglu(rms_norm(x, ffn1_norm[l]), ffn1_w_gate[l], ffn1_w_up[l], ffn1_w_down[l])
        u = rms_norm(x, mix_norm[l])
        (a_q, a_k, a_v, b_x, b_b, b_c,
         c_q, c_kc, c_vc, c_ks, c_vs, c_kw, c_vw, c_g,
         d_q, d_k, d_v, d_g, gate_logits) = jnp.split(u @ w_in[l], SPLIT_POINTS, axis=-1)
        y_a = swa_sink_mixer(a_q, a_k, a_v, swa_q_gain[l], swa_k_gain[l], swa_sinks[l])
        y_b = short_conv_mixer(b_x, b_b, b_c, conv_w[l])
        y_c = nsa_mixer(c_q, c_kc, c_vc, c_ks, c_vs, c_kw, c_vw, c_g, nsa_q_gain[l],
                        nsa_k_gain[l], cmp_pos_k[l], cmp_pos_v[l], cmp_wk1[l], cmp_wk2[l],
                        cmp_wv1[l], cmp_wv2[l])
        y_d = retention_mixer(d_q, d_k, d_v, d_g, ret_norm_gain[l])
        ys = jnp.stack([y_a, y_b, y_c, y_d], axis=2)
        branch = jnp.einsum('btnw,nwd->btnd', ys, w_branch[l])
        gates = jax.nn.sigmoid((gate_logits + merge_gate_bias[l]).astype(jnp.float32))
        gates = gates.astype(x.dtype).reshape(b, t, N_BRANCH, D_MODEL)
        merged = jnp.sum(gates * branch, axis=2)
        x = x + merged @ w_out[l]
        x = x + 0.5 * swiglu(rms_norm(x, ffn2_norm[l]), ffn2_w_gate[l], ffn2_w_up[l], ffn2_w_down[l])
    return x
```

```python
import functools

import jax
import jax.numpy as jnp
from jax import lax
from jax.experimental import pallas as pl
from jax.experimental.pallas import tpu as pltpu

F32 = jnp.float32
BF16 = jnp.bfloat16

HEAD_DIM = 64
Q_BLOCK = 128
MIX_W = 512
N_BRANCH = 4
SWA_HEADS = 8
SWA_KV_HEADS = 2
SWA_WINDOW = 128
NSA_HEADS = 8
NSA_KV_HEADS = 2
CMP_BLOCK = 32
CMP_STRIDE = 16
SEL_BLOCK = 64
SEL_TOPK = 16
NSA_WINDOW = 512
RET_HEADS = 4
RET_QK_DIM = 64
RET_V_DIM = 128
RET_CHUNK = 128
ROPE_BASE = 10000.0
EPS = 1e-6
GROUP_R = 4
SEL_KEY_TILE = 256
VMEM_LIMIT = 52 * 1024 * 1024

NEG_INF = float("-inf")


def _params(sem, vmem=None):
    return pltpu.CompilerParams(dimension_semantics=sem, vmem_limit_bytes=vmem)


def _sigmoid(x):
    return 1.0 / (1.0 + jnp.exp(-x))


def _rms_rows(x, g):
    return x * lax.rsqrt(jnp.mean(x * x, axis=-1, keepdims=True) + EPS) * g


def _dot(a, b):
    return jnp.dot(a, b, preferred_element_type=F32)


def _dot_nt(a, b):
    return lax.dot_general(a, b, (((1,), (1,)), ((), ())), preferred_element_type=F32)


def _dot_tn(a, b):
    return lax.dot_general(a, b, (((0,), (0,)), ((), ())), preferred_element_type=F32)


def _ffn_kernel(x_ref, g_ref, wg_ref, wu_ref, wd_ref, o_ref, xn_ref, acc_ref):
    f = pl.program_id(1)

    @pl.when(f == 0)
    def _():
        xn_ref[...] = _rms_rows(x_ref[...], g_ref[...]).astype(BF16)
        acc_ref[...] = jnp.zeros_like(acc_ref)

    xn = xn_ref[...]
    a = _dot(xn, wg_ref[...])
    b = _dot(xn, wu_ref[...])
    h = (a * _sigmoid(a)) * b
    acc_ref[...] += _dot(h.astype(BF16), wd_ref[...])

    @pl.when(f == pl.num_programs(1) - 1)
    def _():
        o_ref[...] = x_ref[...] + 0.5 * acc_ref[...]


def _ffn(x2, g, wg, wu, wd, tm=512, nf=2):
    n, d = x2.shape
    dff = wg.shape[1]
    tf = dff // nf
    return pl.pallas_call(
        _ffn_kernel,
        grid=(n // tm, nf),
        in_specs=[
            pl.BlockSpec((tm, d), lambda i, f: (i, 0)),
            pl.BlockSpec((1, d), lambda i, f: (0, 0)),
            pl.BlockSpec((d, tf), lambda i, f: (0, f)),
            pl.BlockSpec((d, tf), lambda i, f: (0, f)),
            pl.BlockSpec((tf, d), lambda i, f: (f, 0)),
        ],
        out_specs=pl.BlockSpec((tm, d), lambda i, f: (i, 0)),
        out_shape=jax.ShapeDtypeStruct((n, d), F32),
        scratch_shapes=[pltpu.VMEM((tm, d), BF16), pltpu.VMEM((tm, d), F32)],
        compiler_params=_params(("parallel", "arbitrary"), VMEM_LIMIT),
        name="ffn",
    )(x2, g.reshape(1, d), wg.astype(BF16), wu.astype(BF16), wd.astype(BF16))


def _norm_matmul_kernel(x_ref, g_ref, w_ref, o_ref):
    xn = _rms_rows(x_ref[...], g_ref[...]).astype(BF16)
    o_ref[...] = _dot(xn, w_ref[...])


def _norm_matmul(x2, g, w, tm=512):
    n, d = x2.shape
    c = w.shape[1]
    return pl.pallas_call(
        _norm_matmul_kernel,
        grid=(n // tm,),
        in_specs=[
            pl.BlockSpec((tm, d), lambda i: (i, 0)),
            pl.BlockSpec((1, d), lambda i: (0, 0)),
            pl.BlockSpec((d, c), lambda i: (0, 0)),
        ],
        out_specs=pl.BlockSpec((tm, c), lambda i: (i, 0)),
        out_shape=jax.ShapeDtypeStruct((n, c), F32),
        compiler_params=_params(("parallel",), VMEM_LIMIT),
        name="in_proj",
    )(x2, g.reshape(1, d), w.astype(BF16))


def _headnorm_kernel(x_ref, g_ref, o_ref, *, scale):
    y = _rms_rows(x_ref[...], g_ref[...])
    if scale != 1.0:
        y = y * scale
    o_ref[...] = y.astype(o_ref.dtype)


def _headnorm(xh, gain, scale=1.0, tt=1024):
    b, h, t, d = xh.shape
    tt = min(tt, t)
    return pl.pallas_call(
        functools.partial(_headnorm_kernel, scale=scale),
        grid=(b, h, t // tt),
        in_specs=[
            pl.BlockSpec((None, None, tt, d), lambda i, j, k: (i, j, k, 0)),
            pl.BlockSpec((1, d), lambda i, j, k: (0, 0)),
        ],
        out_specs=pl.BlockSpec((None, None, tt, d), lambda i, j, k: (i, j, k, 0)),
        out_shape=jax.ShapeDtypeStruct((b, h, t, d), BF16),
        compiler_params=_params(("parallel", "parallel", "parallel")),
        name="headnorm",
    )(xh, gain.reshape(1, d))


def _gate_and_merge(o, gsig, branch):
    outs = []
    for r in range(GROUP_R):
        c = r * 3 + branch
        outs.append(o[r * Q_BLOCK:(r + 1) * Q_BLOCK] * gsig[:, c:c + 1])
    return jnp.concatenate(outs, axis=1)


def _banded_kernel(*refs, window, has_sink, gate_branch):
    refs = list(refs)
    sink_ref = refs.pop(0) if has_sink else None
    q_ref, k_ref, v_ref = refs[:3]
    gate_ref = refs[3] if gate_branch is not None else None
    o_ref = refs[-1]
    g = pl.program_id(1)
    i = pl.program_id(2)
    rq = GROUP_R * Q_BLOCK
    span = window + Q_BLOCK
    q = q_ref[...].reshape(rq, HEAD_DIM)
    start = pl.multiple_of(jnp.maximum(i * Q_BLOCK - window, 0), Q_BLOCK)
    k = k_ref[pl.ds(start, span), :]
    v = v_ref[pl.ds(start, span), :]
    s = _dot_nt(q, k)
    row = lax.broadcasted_iota(jnp.int32, (rq, span), 0)
    col = lax.broadcasted_iota(jnp.int32, (rq, span), 1)
    diff = (i * Q_BLOCK + (row & (Q_BLOCK - 1))) - (start + col)
    s = jnp.where((diff >= 0) & (diff < window), s, NEG_INF)
    m = jnp.max(s, axis=-1, keepdims=True)
    if has_sink:
        rcol = lax.broadcasted_iota(jnp.int32, (rq, 1), 0)
        sink = jnp.zeros((rq, 1), F32)
        for r in range(GROUP_R):
            sink = jnp.where((rcol >= r * Q_BLOCK) & (rcol < (r + 1) * Q_BLOCK),
                             sink_ref[g * GROUP_R + r], sink)
        m = jnp.maximum(m, sink)
    m = jnp.where(m == NEG_INF, 0.0, m)
    p = jnp.exp(s - m)
    denom = jnp.sum(p, axis=-1, keepdims=True)
    if has_sink:
        denom = denom + jnp.exp(sink - m)
    o = _dot(p.astype(BF16), v) / jnp.maximum(denom, 1e-30)
    if gate_branch is None:
        o_ref[...] = jnp.concatenate(
            [o[r * Q_BLOCK:(r + 1) * Q_BLOCK] for r in range(GROUP_R)], axis=1)
    else:
        o_ref[...] = _gate_and_merge(o, _sigmoid(gate_ref[...]), gate_branch)


def _banded(qn, kn, vb, window, sinks=None, gates=None, gate_branch=None):
    b, h, t, d = qn.shape
    g = kn.shape[1]
    in_specs = []
    args = []
    if sinks is not None:
        in_specs.append(pl.BlockSpec(memory_space=pltpu.SMEM))
        args.append(sinks.astype(F32))
    in_specs += [
        pl.BlockSpec((None, GROUP_R, Q_BLOCK, d), lambda bi, gi, i: (bi, gi, i, 0)),
        pl.BlockSpec((None, None, t, d), lambda bi, gi, i: (bi, gi, 0, 0)),
        pl.BlockSpec((None, None, t, d), lambda bi, gi, i: (bi, gi, 0, 0)),
    ]
    args += [qn, kn, vb]
    if gates is not None:
        in_specs.append(pl.BlockSpec((None, Q_BLOCK, 128), lambda bi, gi, i: (bi, i, gi)))
        args.append(gates)
    return pl.pallas_call(
        functools.partial(_banded_kernel, window=window, has_sink=sinks is not None,
                          gate_branch=gate_branch if gates is not None else None),
        grid=(b, g, t // Q_BLOCK),
        in_specs=in_specs,
        out_specs=pl.BlockSpec((None, Q_BLOCK, GROUP_R * d), lambda bi, gi, i: (bi, i, gi)),
        out_shape=jax.ShapeDtypeStruct((b, t, h * d), F32),
        compiler_params=_params(("parallel", "parallel", "arbitrary"), VMEM_LIMIT),
        name="banded_attn_w%d" % window,
    )(*args)


def _conv_kernel(x_ref, b_ref, c_ref, xp_ref, cp_ref, w_ref, o_ref):
    ti = pl.program_id(1)
    z = c_ref[...] * x_ref[...]
    tt = z.shape[0]
    zp = jnp.where(ti > 0, cp_ref[...] * xp_ref[...], 0.0)
    row = lax.broadcasted_iota(jnp.int32, z.shape, 0)
    z1 = jnp.where(row == 0, zp[7:8, :], pltpu.roll(z, 1, 0))
    z2 = pltpu.roll(z, 2, 0)
    z2 = jnp.where(row == 0, zp[6:7, :], jnp.where(row == 1, zp[7:8, :], z2))
    w = w_ref[...]
    o_ref[...] = b_ref[...] * (w[0:1, :] * z2 + w[1:2, :] * z1 + w[2:3, :] * z)


def _conv(proj_b, conv_w, tt=512):
    b, t, _ = proj_b.shape
    w = MIX_W
    hb = tt // 8
    prev = lambda bi, ti: (bi, jnp.maximum(ti * hb - 1, 0), 0)
    prev_c = lambda bi, ti: (bi, jnp.maximum(ti * hb - 1, 0), 2)
    return pl.pallas_call(
        _conv_kernel,
        grid=(b, t // tt),
        in_specs=[
            pl.BlockSpec((None, tt, w), lambda bi, ti: (bi, ti, 0)),
            pl.BlockSpec((None, tt, w), lambda bi, ti: (bi, ti, 1)),
            pl.BlockSpec((None, tt, w), lambda bi, ti: (bi, ti, 2)),
            pl.BlockSpec((None, 8, w), prev),
            pl.BlockSpec((None, 8, w), prev_c),
            pl.BlockSpec((8, w), lambda bi, ti: (0, 0)),
        ],
        out_specs=pl.BlockSpec((None, tt, w), lambda bi, ti: (bi, ti, 0)),
        out_shape=jax.ShapeDtypeStruct((b, t, w), F32),
        compiler_params=_params(("parallel", "parallel")),
        name="short_conv",
    )(proj_b, proj_b, proj_b, proj_b, proj_b,
      jnp.pad(conv_w.reshape(conv_w.shape[0], w).astype(F32), ((0, 8 - conv_w.shape[0]), (0, 0))))


def _gelu_tanh(x):
    return x * (0.5 * (1.0 + jnp.tanh(0.7978845608028654 * (x + 0.044715 * (x * x * x)))))


def _compress_kernel(tk_ref, tv_ref, pek_ref, pev_ref, wk1_ref, wk2_ref, wv1_ref, wv2_ref,
                     kg_ref, ko_ref, vo_ref):
    half = tk_ref.shape[1]
    nrow = tk_ref.shape[0]

    def mlp(a, pe_ref, w1_ref, w2_ref):
        a0 = (a + pe_ref[0:1, :]).astype(BF16)
        a1 = (a + pe_ref[1:2, :]).astype(BF16)
        p1 = _dot(a0, w1_ref[0:half, :])
        p2 = _dot(a1, w1_ref[half:2 * half, :])
        hdn = p1 + pltpu.roll(p2, nrow - 1, 0)
        return _dot(_gelu_tanh(hdn).astype(BF16), w2_ref[...])

    kc = mlp(tk_ref[...], pek_ref, wk1_ref, wk2_ref)
    ko_ref[...] = _rms_rows(kc, kg_ref[...]).astype(BF16)
    vo_ref[...] = mlp(tv_ref[...], pev_ref, wv1_ref, wv2_ref).astype(BF16)


def _compress(kc_h, vc_h, pos_k, pos_v, wk1, wk2, wv1, wv2, k_gain):
    b, g, t, d = kc_h.shape
    nrow = t // CMP_STRIDE
    half = CMP_STRIDE * d
    hid = wk1.shape[1]
    tk = kc_h.reshape(b * g, nrow, half)
    tv = vc_h.reshape(b * g, nrow, half)
    full = lambda shape: pl.BlockSpec(shape, lambda i: tuple(0 for _ in shape))
    return pl.pallas_call(
        _compress_kernel,
        grid=(b * g,),
        in_specs=[
            pl.BlockSpec((None, nrow, half), lambda i: (i, 0, 0)),
            pl.BlockSpec((None, nrow, half), lambda i: (i, 0, 0)),
            full((2, half)), full((2, half)),
            full((2 * half, hid)), full((hid, d)),
            full((2 * half, hid)), full((hid, d)),
            full((1, d)),
        ],
        out_specs=[pl.BlockSpec((None, nrow, d), lambda i: (i, 0, 0)),
                   pl.BlockSpec((None, nrow, d), lambda i: (i, 0, 0))],
        out_shape=[jax.ShapeDtypeStruct((b * g, nrow, d), BF16),
                   jax.ShapeDtypeStruct((b * g, nrow, d), BF16)],
        compiler_params=_params(("parallel",), VMEM_LIMIT),
        name="nsa_compress",
    )(tk, tv, pos_k.reshape(2, half), pos_v.reshape(2, half),
      wk1.astype(BF16), wk2.astype(BF16), wv1.astype(BF16), wv2.astype(BF16),
      k_gain.reshape(1, d))


def _cmp_attn_kernel(q_ref, kc_ref, vc_ref, gate_ref, o_ref, sel_ref, *, sel_k):
    i = pl.program_id(2)
    rq = GROUP_R * Q_BLOCK
    nc = kc_ref.shape[0]
    ns = sel_ref.shape[1]
    q = q_ref[...].reshape(rq, HEAD_DIM)
    s = _dot_nt(q, kc_ref[...])
    row = lax.broadcasted_iota(jnp.int32, (rq, nc), 0)
    n = lax.broadcasted_iota(jnp.int32, (rq, nc), 1)
    qpos = i * Q_BLOCK + (row & (Q_BLOCK - 1))
    s = jnp.where(n * CMP_STRIDE + (CMP_BLOCK - 1) <= qpos, s, NEG_INF)
    m = jnp.max(s, axis=-1, keepdims=True)
    m = jnp.where(m == NEG_INF, 0.0, m)
    p = jnp.exp(s - m)
    denom = jnp.sum(p, axis=-1, keepdims=True)
    pb = (p / jnp.maximum(denom, 1e-30)).astype(BF16)
    o = _dot(pb, vc_ref[...])
    o_ref[...] = _gate_and_merge(o, _sigmoid(gate_ref[...]), 0)

    nn = lax.broadcasted_iota(jnp.int32, (nc, ns), 0) * CMP_STRIDE
    ss = lax.broadcasted_iota(jnp.int32, (nc, ns), 1) * SEL_BLOCK
    overlap = jnp.where((nn < ss + SEL_BLOCK) & (nn + (CMP_BLOCK - 1) >= ss), 1.0, 0.0).astype(BF16)
    imp_r = _dot(pb, overlap)
    imp = imp_r[0:Q_BLOCK]
    for r in range(1, GROUP_R):
        imp = imp + imp_r[r * Q_BLOCK:(r + 1) * Q_BLOCK]
    qp = i * Q_BLOCK + lax.broadcasted_iota(jnp.int32, (Q_BLOCK, ns), 0)
    blk = lax.broadcasted_iota(jnp.int32, (Q_BLOCK, ns), 1)
    cur = qp // SEL_BLOCK
    forced = (blk == 0) | (blk == cur) | (blk == cur - 1)
    imp = jnp.where(forced, jnp.inf, imp)
    imp = jnp.where(blk <= cur, imp, NEG_INF)
    blkf = blk.astype(F32)
    sel = jnp.zeros((Q_BLOCK, ns), F32)
    for _ in range(sel_k):
        mx = jnp.max(imp, axis=-1, keepdims=True)
        first = jnp.min(jnp.where(imp == mx, blkf, float(ns)), axis=-1, keepdims=True)
        hit = blkf == first
        sel = jnp.where(hit & (mx > NEG_INF), 1.0, sel)
        imp = jnp.where(hit, NEG_INF, imp)
    sel_ref[...] = sel.astype(BF16)


def _cmp_attn(qn, k_cmp, v_cmp, gates):
    b, h, t, d = qn.shape
    g = h // GROUP_R
    nc = k_cmp.shape[1]
    ns = t // SEL_BLOCK
    return pl.pallas_call(
        functools.partial(_cmp_attn_kernel, sel_k=min(SEL_TOPK, ns)),
        grid=(b, g, t // Q_BLOCK),
        in_specs=[
            pl.BlockSpec((None, GROUP_R, Q_BLOCK, d), lambda bi, gi, i: (bi, gi, i, 0)),
            pl.BlockSpec((None, nc, d), lambda bi, gi, i: (bi * NSA_KV_HEADS + gi, 0, 0)),
            pl.BlockSpec((None, nc, d), lambda bi, gi, i: (bi * NSA_KV_HEADS + gi, 0, 0)),
            pl.BlockSpec((None, Q_BLOCK, 128), lambda bi, gi, i: (bi, i, gi)),
        ],
        out_specs=[
            pl.BlockSpec((None, Q_BLOCK, GROUP_R * d), lambda bi, gi, i: (bi, i, gi)),
            pl.BlockSpec((None, None, Q_BLOCK, ns), lambda bi, gi, i: (bi, gi, i, 0)),
        ],
        out_shape=[jax.ShapeDtypeStruct((b, t, h * d), F32),
                   jax.ShapeDtypeStruct((b, g, t, ns), BF16)],
        compiler_params=_params(("parallel", "parallel", "arbitrary"), VMEM_LIMIT),
        name="nsa_cmp_topk",
    )(qn, k_cmp, v_cmp, gates)


def _sel_attn_kernel(q_ref, ks_ref, vs_ref, sel_ref, gate_ref, o_ref, m_ref, l_ref, acc_ref):
    i = pl.program_id(2)
    rq = GROUP_R * Q_BLOCK
    tk = SEL_KEY_TILE
    ns = sel_ref.shape[1]
    q = q_ref[...].reshape(rq, HEAD_DIM)
    sel = sel_ref[...]
    m_ref[...] = jnp.full(m_ref.shape, NEG_INF, F32)
    l_ref[...] = jnp.zeros_like(l_ref)
    acc_ref[...] = jnp.zeros_like(acc_ref)
    n_tiles = (i * Q_BLOCK + Q_BLOCK + tk - 1) // tk

    def step(j, carry):
        k0 = pl.multiple_of(j * tk, tk)
        s = _dot_nt(q, ks_ref[pl.ds(k0, tk), :])
        srow = lax.broadcasted_iota(jnp.int32, (ns, tk), 0)
        scol = lax.broadcasted_iota(jnp.int32, (ns, tk), 1)
        expand = jnp.where(srow == j * (tk // SEL_BLOCK) + scol // SEL_BLOCK, 1.0, 0.0).astype(BF16)
        selx = _dot(sel, expand)
        row = lax.broadcasted_iota(jnp.int32, (Q_BLOCK, tk), 0)
        col = lax.broadcasted_iota(jnp.int32, (Q_BLOCK, tk), 1)
        ok = (selx > 0.5) & (k0 + col <= i * Q_BLOCK + row)
        bias = jnp.where(ok, 0.0, NEG_INF)
        s = (s.reshape(GROUP_R, Q_BLOCK, tk) + bias[None]).reshape(rq, tk)
        m_prev = m_ref[...]
        m_new = jnp.maximum(m_prev, jnp.max(s, axis=-1, keepdims=True))
        m_safe = jnp.where(m_new == NEG_INF, 0.0, m_new)
        alpha = jnp.exp(m_prev - m_safe)
        p = jnp.exp(s - m_safe)
        l_ref[...] = alpha * l_ref[...] + jnp.sum(p, axis=-1, keepdims=True)
        acc_ref[...] = alpha * acc_ref[...] + _dot(p.astype(BF16), vs_ref[pl.ds(k0, tk), :])
        m_ref[...] = m_new
        return carry

    lax.fori_loop(0, n_tiles, step, 0)
    o = acc_ref[...] / jnp.maximum(l_ref[...], 1e-30)
    o_ref[...] = _gate_and_merge(o, _sigmoid(gate_ref[...]), 1)


def _sel_attn(qn, ks_n, vs_b, sel, gates):
    b, h, t, d = qn.shape
    g = h // GROUP_R
    ns = sel.shape[-1]
    rq = GROUP_R * Q_BLOCK
    return pl.pallas_call(
        _sel_attn_kernel,
        grid=(b, g, t // Q_BLOCK),
        in_specs=[
            pl.BlockSpec((None, GROUP_R, Q_BLOCK, d), lambda bi, gi, i: (bi, gi, i, 0)),
            pl.BlockSpec((None, None, t, d), lambda bi, gi, i: (bi, gi, 0, 0)),
            pl.BlockSpec((None, None, t, d), lambda bi, gi, i: (bi, gi, 0, 0)),
            pl.BlockSpec((None, None, Q_BLOCK, ns), lambda bi, gi, i: (bi, gi, i, 0)),
            pl.BlockSpec((None, Q_BLOCK, 128), lambda bi, gi, i: (bi, i, gi)),
        ],
        out_specs=pl.BlockSpec((None, Q_BLOCK, GROUP_R * d), lambda bi, gi, i: (bi, i, gi)),
        out_shape=jax.ShapeDtypeStruct((b, t, h * d), F32),
        scratch_shapes=[pltpu.VMEM((rq, 1), F32), pltpu.VMEM((rq, 1), F32),
                        pltpu.VMEM((rq, d), F32)],
        compiler_params=_params(("parallel", "parallel", "arbitrary"), VMEM_LIMIT),
        name="nsa_selected",
    )(qn, ks_n, vs_b, sel, gates)


def _retention_kernel(lg_ref, q_ref, k_ref, v_ref, gt_ref, cos_ref, sin_ref, ng_ref, o_ref, r_ref):
    hh = pl.program_id(1)
    ci = pl.program_id(2)
    c = RET_CHUNK
    half = RET_QK_DIM // 2

    @pl.when(ci == 0)
    def _():
        r_ref[...] = jnp.zeros_like(r_ref)

    lg = lg_ref[hh]
    cosf = cos_ref[...]
    sinf = sin_ref[...]

    def rot(x):
        return x * cosf + jnp.concatenate([x[:, half:], x[:, :half]], axis=1) * sinf

    q = rot(q_ref[...])
    k = rot(k_ref[...]) * (RET_QK_DIM ** -0.5)
    vb = v_ref[...].astype(BF16)
    ii = lax.broadcasted_iota(jnp.int32, (c, c), 0)
    jj = lax.broadcasted_iota(jnp.int32, (c, c), 1)
    d = (ii - jj).astype(F32)
    dmask = jnp.where(d >= 0, jnp.exp(d * lg), 0.0)
    att = _dot_nt(q.astype(BF16), k.astype(BF16)) * dmask
    o = _dot(att.astype(BF16), vb)
    jcol = lax.broadcasted_iota(jnp.int32, (c, 1), 0).astype(F32)
    xi = jnp.exp((jcol + 1.0) * lg)
    zeta = jnp.exp((c - 1.0 - jcol) * lg)
    r_prev = r_ref[...]
    o = o + _dot((q * xi).astype(BF16), r_prev.astype(BF16))
    s_chunk = _dot_tn((k * zeta).astype(BF16), vb)
    decay = jnp.exp(jnp.zeros((1, RET_V_DIM), F32) + c * lg)
    r_ref[...] = r_prev * decay + s_chunk
    mu = jnp.mean(o, axis=-1, keepdims=True)
    var = jnp.mean(jnp.square(o - mu), axis=-1, keepdims=True)
    on = (o - mu) * lax.rsqrt(var + EPS)
    gt = gt_ref[...]
    o_ref[...] = (gt * _sigmoid(gt)) * (on * ng_ref[...])


def _retention(qh, kh, proj_d, norm_gain):
    b, h, t, dk = qh.shape
    c = RET_CHUNK
    dv = RET_V_DIM
    v_blk0 = (2 * h * dk) // dv
    g_blk0 = v_blk0 + h
    half = dk // 2
    inv = ROPE_BASE ** (-jnp.arange(half, dtype=F32) / half)
    ang = jnp.arange(t).astype(F32)[:, None] * inv[None, :]
    cos = jnp.cos(ang)
    sin = jnp.sin(ang)
    cosf = jnp.concatenate([cos, cos], axis=-1)
    sinf = jnp.concatenate([-sin, sin], axis=-1)
    log_gamma = jnp.log(1.0 - 2.0 ** (-5.0 - jnp.arange(h, dtype=F32)))
    return pl.pallas_call(
        _retention_kernel,
        grid=(b, h, t // c),
        in_specs=[
            pl.BlockSpec(memory_space=pltpu.SMEM),
            pl.BlockSpec((None, None, c, dk), lambda bi, hi, ci: (bi, hi, ci, 0)),
            pl.BlockSpec((None, None, c, dk), lambda bi, hi, ci: (bi, hi, ci, 0)),
            pl.BlockSpec((None, c, dv), lambda bi, hi, ci: (bi, ci, v_blk0 + hi)),
            pl.BlockSpec((None, c, dv), lambda bi, hi, ci: (bi, ci, g_blk0 + hi)),
            pl.BlockSpec((c, dk), lambda bi, hi, ci: (ci, 0)),
            pl.BlockSpec((c, dk), lambda bi, hi, ci: (ci, 0)),
            pl.BlockSpec((1, dv), lambda bi, hi, ci: (0, hi)),
        ],
        out_specs=pl.BlockSpec((None, c, dv), lambda bi, hi, ci: (bi, ci, hi)),
        out_shape=jax.ShapeDtypeStruct((b, t, h * dv), F32),
        scratch_shapes=[pltpu.VMEM((dk, dv), F32)],
        compiler_params=_params(("parallel", "parallel", "arbitrary")),
        name="retention",
    )(log_gamma, qh, kh, proj_d, proj_d, cosf, sinf, norm_gain.reshape(1, h * dv))


def _merge_kernel(x_ref, g_ref, wgate_ref, bias_ref, ya_ref, yb_ref, yc0_ref, yc1_ref, yc2_ref,
                  yd_ref, wb_ref, wo_ref, o_ref):
    x = x_ref[...]
    d = x.shape[1]
    u = _rms_rows(x, g_ref[...]).astype(BF16)
    ys = (ya_ref[...], yb_ref[...], yc0_ref[...] + yc1_ref[...] + yc2_ref[...], yd_ref[...])
    merged = jnp.zeros(x.shape, F32)
    for n in range(N_BRANCH):
        logits = _dot(u, wgate_ref[:, n * d:(n + 1) * d]) + bias_ref[:, n * d:(n + 1) * d]
        merged = merged + _sigmoid(logits) * _dot(ys[n].astype(BF16), wb_ref[n])
    o_ref[...] = x + _dot(merged.astype(BF16), wo_ref[...])


def _merge(x2, g, w_gate, bias, ys, w_branch, w_out, tm=256):
    n, d = x2.shape
    w = MIX_W
    row = lambda width: pl.BlockSpec((tm, width), lambda i: (i, 0))
    full = lambda shape: pl.BlockSpec(shape, lambda i: tuple(0 for _ in shape))
    return pl.pallas_call(
        _merge_kernel,
        grid=(n // tm,),
        in_specs=[row(d), full((1, d)), full((d, N_BRANCH * d)), full((1, N_BRANCH * d))]
        + [row(w)] * 6 + [full((N_BRANCH, w, d)), full((d, d))],
        out_specs=row(d),
        out_shape=jax.ShapeDtypeStruct((n, d), F32),
        compiler_params=_params(("parallel",), VMEM_LIMIT),
        name="merge_out",
    )(x2, g.reshape(1, d), w_gate.astype(BF16), bias.reshape(1, N_BRANCH * d),
      *[y.reshape(n, w) for y in ys], w_branch.astype(BF16), w_out.astype(BF16))


def _heads(t2, b, t, n):
    return t2.reshape(b, t, n, -1).transpose(0, 2, 1, 3)


def _mixers(x2, b, t, mix_norm, w_in, merge_gate_bias, swa_q_gain, swa_k_gain, swa_sinks, conv_w,
            nsa_q_gain, nsa_k_gain, cmp_pos_k, cmp_pos_v, cmp_wk1, cmp_wk2, cmp_wv1, cmp_wv2,
            ret_norm_gain, w_branch, w_out):
    d_model = x2.shape[1]
    hd = HEAD_DIM
    swa_q, swa_kv = SWA_HEADS * hd, SWA_KV_HEADS * hd
    nsa_q, nsa_kv = NSA_HEADS * hd, NSA_KV_HEADS * hd
    ret_qk, ret_v = RET_HEADS * RET_QK_DIM, RET_HEADS * RET_V_DIM
    n_gate = NSA_HEADS * 3
    o_a = 0
    o_b = o_a + swa_q + 2 * swa_kv
    o_c = o_b + 3 * MIX_W
    o_cg = o_c + nsa_q + 6 * nsa_kv
    o_d = o_cg + n_gate
    o_g = o_d + 2 * ret_qk + 2 * ret_v
    scale = hd ** -0.5

    per_g = GROUP_R * 3
    w_cg = jnp.concatenate(
        [jnp.pad(w_in[:, o_cg + gi * per_g:o_cg + (gi + 1) * per_g], ((0, 0), (0, 128 - per_g)))
         for gi in range(NSA_KV_HEADS)], axis=1)
    proj_a = _norm_matmul(x2, mix_norm, w_in[:, o_a:o_b])
    proj_b = _norm_matmul(x2, mix_norm, w_in[:, o_b:o_c])
    proj_c = _norm_matmul(x2, mix_norm, jnp.concatenate([w_in[:, o_c:o_cg], w_cg], axis=1))
    proj_d = _norm_matmul(x2, mix_norm, w_in[:, o_d:o_g])

    a_qn = _headnorm(_heads(proj_a[:, :swa_q], b, t, SWA_HEADS), swa_q_gain, scale)
    a_kn = _headnorm(_heads(proj_a[:, swa_q:swa_q + swa_kv], b, t, SWA_KV_HEADS), swa_k_gain)
    a_v = _heads(proj_a[:, swa_q + swa_kv:], b, t, SWA_KV_HEADS).astype(BF16)
    y_a = _banded(a_qn, a_kn, a_v, SWA_WINDOW, sinks=swa_sinks)

    y_b = _conv(proj_b.reshape(b, t, 3 * MIX_W), conv_w)

    def c_kv(j):
        lo = nsa_q + j * nsa_kv
        return _heads(proj_c[:, lo:lo + nsa_kv], b, t, NSA_KV_HEADS)

    c_qn = _headnorm(_heads(proj_c[:, :nsa_q], b, t, NSA_HEADS), nsa_q_gain, scale)
    gates_c = proj_c[:, nsa_q + 6 * nsa_kv:].reshape(b, t, NSA_KV_HEADS * 128)
    k_cmp, v_cmp = _compress(c_kv(0), c_kv(1), cmp_pos_k, cmp_pos_v, cmp_wk1, cmp_wk2,
                             cmp_wv1, cmp_wv2, nsa_k_gain[0])
    y_cmp, sel = _cmp_attn(c_qn, k_cmp, v_cmp, gates_c)
    y_sel = _sel_attn(c_qn, _headnorm(c_kv(2), nsa_k_gain[1]), c_kv(3).astype(BF16), sel, gates_c)
    y_win = _banded(c_qn, _headnorm(c_kv(4), nsa_k_gain[2]), c_kv(5).astype(BF16), NSA_WINDOW,
                    gates=gates_c, gate_branch=2)

    y_d = _retention(_heads(proj_d[:, :ret_qk], b, t, RET_HEADS),
                     _heads(proj_d[:, ret_qk:2 * ret_qk], b, t, RET_HEADS),
                     proj_d.reshape(b, t, 2 * ret_qk + 2 * ret_v), ret_norm_gain)

    return _merge(x2, mix_norm, w_in[:, o_g:], merge_gate_bias,
                  (y_a, y_b, y_cmp, y_sel, y_win, y_d), w_branch, w_out)


def kernel(x, ffn1_norm, ffn1_w_gate, ffn1_w_up, ffn1_w_down, mix_norm, w_in, merge_gate_bias, swa_q_gain, swa_k_gain, swa_sinks, conv_w, nsa_q_gain, nsa_k_gain, cmp_pos_k, cmp_pos_v, cmp_wk1, cmp_wk2, cmp_wv1, cmp_wv2, ret_norm_gain, w_branch, w_out, ffn2_norm, ffn2_w_gate, ffn2_w_up, ffn2_w_down):
    b, t, d = x.shape
    x2 = x.reshape(b * t, d)
    for l in range(ffn1_norm.shape[0]):
        x2 = _ffn(x2, ffn1_norm[l], ffn1_w_gate[l], ffn1_w_up[l], ffn1_w_down[l])
        x2 = _mixers(x2, b, t, mix_norm[l], w_in[l], merge_gate_bias[l], swa_q_gain[l],
                     swa_k_gain[l], swa_sinks[l], conv_w[l], nsa_q_gain[l], nsa_k_gain[l],
                     cmp_pos_k[l], cmp_pos_v[l], cmp_wk1[l], cmp_wk2[l], cmp_wv1[l], cmp_wv2[l],
                     ret_norm_gain[l], w_branch[l], w_out[l])
        x2 = _ffn(x2, ffn2_norm[l], ffn2_w_gate[l], ffn2_w_up[l], ffn2_w_down[l])
    return x2.reshape(b, t, d)
```

```python
import functools

import jax
import jax.numpy as jnp
from jax import lax
from jax.experimental import pallas as pl
from jax.experimental.pallas import tpu as pltpu

F32 = jnp.float32
BF16 = jnp.bfloat16

HEAD_DIM = 64
Q_BLOCK = 128
MIX_W = 512
N_BRANCH = 4
SWA_HEADS = 8
SWA_KV_HEADS = 2
SWA_WINDOW = 128
NSA_HEADS = 8
NSA_KV_HEADS = 2
CMP_BLOCK = 32
CMP_STRIDE = 16
SEL_BLOCK = 64
SEL_TOPK = 16
NSA_WINDOW = 512
RET_HEADS = 4
RET_QK_DIM = 64
RET_V_DIM = 128
RET_CHUNK = 128
ROPE_BASE = 10000.0
EPS = 1e-6
GROUP_R = 4
SEL_KEY_TILE = 512
VMEM_LIMIT = 52 * 1024 * 1024

NEG_INF = float("-inf")


def _params(sem, vmem=None):
    return pltpu.CompilerParams(dimension_semantics=sem, vmem_limit_bytes=vmem)


def _sigmoid(x):
    return 1.0 / (1.0 + jnp.exp(-x))


def _rms_rows(x, g):
    return x * lax.rsqrt(jnp.mean(x * x, axis=-1, keepdims=True) + EPS) * g


def _dot(a, b):
    return jnp.dot(a, b, preferred_element_type=F32)


def _dot_nt(a, b):
    return lax.dot_general(a, b, (((1,), (1,)), ((), ())), preferred_element_type=F32)


def _dot_tn(a, b):
    return lax.dot_general(a, b, (((0,), (0,)), ((), ())), preferred_element_type=F32)


def _ffn_kernel(x_ref, g_ref, wg_ref, wu_ref, wd_ref, o_ref, xn_ref, acc_ref):
    f = pl.program_id(1)

    @pl.when(f == 0)
    def _():
        xn_ref[...] = _rms_rows(x_ref[...], g_ref[...]).astype(BF16)
        acc_ref[...] = jnp.zeros_like(acc_ref)

    xn = xn_ref[...]
    a = _dot(xn, wg_ref[...])
    b = _dot(xn, wu_ref[...])
    h = (a * _sigmoid(a)) * b
    acc_ref[...] += _dot(h.astype(BF16), wd_ref[...])

    @pl.when(f == pl.num_programs(1) - 1)
    def _():
        o_ref[...] = x_ref[...] + 0.5 * acc_ref[...]


def _ffn(x2, g, wg, wu, wd, tm=512, nf=2):
    n, d = x2.shape
    dff = wg.shape[1]
    tf = dff // nf
    return pl.pallas_call(
        _ffn_kernel,
        grid=(n // tm, nf),
        in_specs=[
            pl.BlockSpec((tm, d), lambda i, f: (i, 0)),
            pl.BlockSpec((1, d), lambda i, f: (0, 0)),
            pl.BlockSpec((d, tf), lambda i, f: (0, f)),
            pl.BlockSpec((d, tf), lambda i, f: (0, f)),
            pl.BlockSpec((tf, d), lambda i, f: (f, 0)),
        ],
        out_specs=pl.BlockSpec((tm, d), lambda i, f: (i, 0)),
        out_shape=jax.ShapeDtypeStruct((n, d), F32),
        scratch_shapes=[pltpu.VMEM((tm, d), BF16), pltpu.VMEM((tm, d), F32)],
        compiler_params=_params(("parallel", "arbitrary"), VMEM_LIMIT),
        name="ffn",
    )(x2, g.reshape(1, d), wg.astype(BF16), wu.astype(BF16), wd.astype(BF16))


def _norm_matmul_kernel(x_ref, g_ref, w_ref, o_ref):
    xn = _rms_rows(x_ref[...], g_ref[...]).astype(BF16)
    o_ref[...] = _dot(xn, w_ref[...])


def _norm_matmul(x2, g, w, tm=512):
    n, d = x2.shape
    c = w.shape[1]
    return pl.pallas_call(
        _norm_matmul_kernel,
        grid=(n // tm,),
        in_specs=[
            pl.BlockSpec((tm, d), lambda i: (i, 0)),
            pl.BlockSpec((1, d), lambda i: (0, 0)),
            pl.BlockSpec((d, c), lambda i: (0, 0)),
        ],
        out_specs=pl.BlockSpec((tm, c), lambda i: (i, 0)),
        out_shape=jax.ShapeDtypeStruct((n, c), F32),
        compiler_params=_params(("parallel",), VMEM_LIMIT),
        name="in_proj",
    )(x2, g.reshape(1, d), w.astype(BF16))


def _headnorm_kernel(x_ref, g_ref, o_ref, *, scale):
    y = _rms_rows(x_ref[...], g_ref[...])
    if scale != 1.0:
        y = y * scale
    o_ref[...] = y.astype(o_ref.dtype)


def _headnorm(xh, gain, scale=1.0, tt=1024):
    b, h, t, d = xh.shape
    tt = min(tt, t)
    return pl.pallas_call(
        functools.partial(_headnorm_kernel, scale=scale),
        grid=(b, h, t // tt),
        in_specs=[
            pl.BlockSpec((None, None, tt, d), lambda i, j, k: (i, j, k, 0)),
            pl.BlockSpec((1, d), lambda i, j, k: (0, 0)),
        ],
        out_specs=pl.BlockSpec((None, None, tt, d), lambda i, j, k: (i, j, k, 0)),
        out_shape=jax.ShapeDtypeStruct((b, h, t, d), BF16),
        compiler_params=_params(("parallel", "parallel", "parallel")),
        name="headnorm",
    )(xh, gain.reshape(1, d))


def _gate_and_merge(o, gsig, branch):
    outs = []
    for r in range(GROUP_R):
        c = r * 3 + branch
        outs.append(o[r * Q_BLOCK:(r + 1) * Q_BLOCK] * gsig[:, c:c + 1])
    return jnp.concatenate(outs, axis=1)


def _banded_kernel(*refs, window, has_sink, gate_branch):
    refs = list(refs)
    sink_ref = refs.pop(0) if has_sink else None
    q_ref, k_ref, v_ref = refs[:3]
    gate_ref = refs[3] if gate_branch is not None else None
    o_ref = refs[-1]
    g = pl.program_id(1)
    i = pl.program_id(2)
    rq = GROUP_R * Q_BLOCK
    span = window + Q_BLOCK
    q = q_ref[...].reshape(rq, HEAD_DIM)
    start = pl.multiple_of(jnp.maximum(i * Q_BLOCK - window, 0), Q_BLOCK)
    k = k_ref[pl.ds(start, span), :]
    v = v_ref[pl.ds(start, span), :]
    s = _dot_nt(q, k)
    row = lax.broadcasted_iota(jnp.int32, (rq, span), 0)
    col = lax.broadcasted_iota(jnp.int32, (rq, span), 1)
    diff = (i * Q_BLOCK + (row & (Q_BLOCK - 1))) - (start + col)
    s = jnp.where((diff >= 0) & (diff < window), s, NEG_INF)
    m = jnp.max(s, axis=-1, keepdims=True)
    if has_sink:
        rcol = lax.broadcasted_iota(jnp.int32, (rq, 1), 0)
        sink = jnp.zeros((rq, 1), F32)
        for r in range(GROUP_R):
            sink = jnp.where((rcol >= r * Q_BLOCK) & (rcol < (r + 1) * Q_BLOCK),
                             sink_ref[g * GROUP_R + r], sink)
        m = jnp.maximum(m, sink)
    m = jnp.where(m == NEG_INF, 0.0, m)
    p = jnp.exp(s - m)
    denom = jnp.sum(p, axis=-1, keepdims=True)
    if has_sink:
        denom = denom + jnp.exp(sink - m)
    o = _dot(p.astype(BF16), v) / jnp.maximum(denom, 1e-30)
    if gate_branch is None:
        o_ref[...] = jnp.concatenate(
            [o[r * Q_BLOCK:(r + 1) * Q_BLOCK] for r in range(GROUP_R)], axis=1)
    else:
        o_ref[...] = _gate_and_merge(o, _sigmoid(gate_ref[...]), gate_branch)


def _banded(qn, kn, vb, window, sinks=None, gates=None, gate_branch=None):
    b, h, t, d = qn.shape
    g = kn.shape[1]
    in_specs = []
    args = []
    if sinks is not None:
        in_specs.append(pl.BlockSpec(memory_space=pltpu.SMEM))
        args.append(sinks.astype(F32))
    in_specs += [
        pl.BlockSpec((None, GROUP_R, Q_BLOCK, d), lambda bi, gi, i: (bi, gi, i, 0)),
        pl.BlockSpec((None, None, t, d), lambda bi, gi, i: (bi, gi, 0, 0)),
        pl.BlockSpec((None, None, t, d), lambda bi, gi, i: (bi, gi, 0, 0)),
    ]
    args += [qn, kn, vb]
    if gates is not None:
        in_specs.append(pl.BlockSpec((None, Q_BLOCK, 128), lambda bi, gi, i: (bi, i, gi)))
        args.append(gates)
    return pl.pallas_call(
        functools.partial(_banded_kernel, window=window, has_sink=sinks is not None,
                          gate_branch=gate_branch if gates is not None else None),
        grid=(b, g, t // Q_BLOCK),
        in_specs=in_specs,
        out_specs=pl.BlockSpec((None, Q_BLOCK, GROUP_R * d), lambda bi, gi, i: (bi, i, gi)),
        out_shape=jax.ShapeDtypeStruct((b, t, h * d), F32),
        compiler_params=_params(("parallel", "parallel", "arbitrary"), VMEM_LIMIT),
        name="banded_attn_w%d" % window,
    )(*args)


def _conv_kernel(x_ref, b_ref, c_ref, xp_ref, cp_ref, w_ref, o_ref):
    ti = pl.program_id(1)
    z = c_ref[...] * x_ref[...]
    tt = z.shape[0]
    zp = jnp.where(ti > 0, cp_ref[...] * xp_ref[...], 0.0)
    row = lax.broadcasted_iota(jnp.int32, z.shape, 0)
    z1 = jnp.where(row == 0, zp[7:8, :], pltpu.roll(z, 1, 0))
    z2 = pltpu.roll(z, 2, 0)
    z2 = jnp.where(row == 0, zp[6:7, :], jnp.where(row == 1, zp[7:8, :], z2))
    w = w_ref[...]
    o_ref[...] = b_ref[...] * (w[0:1, :] * z2 + w[1:2, :] * z1 + w[2:3, :] * z)


def _conv(proj_b, conv_w, tt=512):
    b, t, _ = proj_b.shape
    w = MIX_W
    hb = tt // 8
    prev = lambda bi, ti: (bi, jnp.maximum(ti * hb - 1, 0), 0)
    prev_c = lambda bi, ti: (bi, jnp.maximum(ti * hb - 1, 0), 2)
    return pl.pallas_call(
        _conv_kernel,
        grid=(b, t // tt),
        in_specs=[
            pl.BlockSpec((None, tt, w), lambda bi, ti: (bi, ti, 0)),
            pl.BlockSpec((None, tt, w), lambda bi, ti: (bi, ti, 1)),
            pl.BlockSpec((None, tt, w), lambda bi, ti: (bi, ti, 2)),
            pl.BlockSpec((None, 8, w), prev),
            pl.BlockSpec((None, 8, w), prev_c),
            pl.BlockSpec((8, w), lambda bi, ti: (0, 0)),
        ],
        out_specs=pl.BlockSpec((None, tt, w), lambda bi, ti: (bi, ti, 0)),
        out_shape=jax.ShapeDtypeStruct((b, t, w), F32),
        compiler_params=_params(("parallel", "parallel")),
        name="short_conv",
    )(proj_b, proj_b, proj_b, proj_b, proj_b,
      jnp.pad(conv_w.reshape(conv_w.shape[0], w).astype(F32), ((0, 8 - conv_w.shape[0]), (0, 0))))


def _gelu_tanh(x):
    return x * (0.5 * (1.0 + jnp.tanh(0.7978845608028654 * (x + 0.044715 * (x * x * x)))))


def _compress_kernel(tk_ref, tv_ref, pek_ref, pev_ref, wk1_ref, wk2_ref, wv1_ref, wv2_ref,
                     kg_ref, ko_ref, vo_ref):
    half = tk_ref.shape[1]
    nrow = tk_ref.shape[0]

    def mlp(a, pe_ref, w1_ref, w2_ref):
        a0 = (a + pe_ref[0:1, :]).astype(BF16)
        a1 = (a + pe_ref[1:2, :]).astype(BF16)
        p1 = _dot(a0, w1_ref[0:half, :])
        p2 = _dot(a1, w1_ref[half:2 * half, :])
        hdn = p1 + pltpu.roll(p2, nrow - 1, 0)
        return _dot(_gelu_tanh(hdn).astype(BF16), w2_ref[...])

    kc = mlp(tk_ref[...], pek_ref, wk1_ref, wk2_ref)
    ko_ref[...] = _rms_rows(kc, kg_ref[...]).astype(BF16)
    vo_ref[...] = mlp(tv_ref[...], pev_ref, wv1_ref, wv2_ref).astype(BF16)


def _compress(kc_h, vc_h, pos_k, pos_v, wk1, wk2, wv1, wv2, k_gain):
    b, g, t, d = kc_h.shape
    nrow = t // CMP_STRIDE
    half = CMP_STRIDE * d
    hid = wk1.shape[1]
    tk = kc_h.reshape(b * g, nrow, half)
    tv = vc_h.reshape(b * g, nrow, half)
    full = lambda shape: pl.BlockSpec(shape, lambda i: tuple(0 for _ in shape))
    return pl.pallas_call(
        _compress_kernel,
        grid=(b * g,),
        in_specs=[
            pl.BlockSpec((None, nrow, half), lambda i: (i, 0, 0)),
            pl.BlockSpec((None, nrow, half), lambda i: (i, 0, 0)),
            full((2, half)), full((2, half)),
            full((2 * half, hid)), full((hid, d)),
            full((2 * half, hid)), full((hid, d)),
            full((1, d)),
        ],
        out_specs=[pl.BlockSpec((None, nrow, d), lambda i: (i, 0, 0)),
                   pl.BlockSpec((None, nrow, d), lambda i: (i, 0, 0))],
        out_shape=[jax.ShapeDtypeStruct((b * g, nrow, d), BF16),
                   jax.ShapeDtypeStruct((b * g, nrow, d), BF16)],
        compiler_params=_params(("parallel",), VMEM_LIMIT),
        name="nsa_compress",
    )(tk, tv, pos_k.reshape(2, half), pos_v.reshape(2, half),
      wk1.astype(BF16), wk2.astype(BF16), wv1.astype(BF16), wv2.astype(BF16),
      k_gain.reshape(1, d))


def _untranspose_heads(o_t):
    return jnp.concatenate(
        [o_t[:, r * Q_BLOCK:(r + 1) * Q_BLOCK].T for r in range(GROUP_R)], axis=0)


def _transpose_q(qn):
    b, h, t, d = qn.shape
    g = h // GROUP_R
    qt = qn.reshape(b, g, GROUP_R, t // Q_BLOCK, Q_BLOCK, d).transpose(0, 1, 3, 5, 2, 4)
    return qt.reshape(b, g, t // Q_BLOCK, d, GROUP_R * Q_BLOCK)


def _cmp_attn_kernel(qt_ref, kc_ref, vct_ref, gate_ref, o_ref, selt_ref, *, sel_k):
    i = pl.program_id(2)
    rq = GROUP_R * Q_BLOCK
    nc = kc_ref.shape[0]
    ns = selt_ref.shape[0]
    st = _dot(kc_ref[...], qt_ref[...])
    n = lax.broadcasted_iota(jnp.int32, (nc, rq), 0)
    lane = lax.broadcasted_iota(jnp.int32, (nc, rq), 1)
    qpos = i * Q_BLOCK + (lane & (Q_BLOCK - 1))
    st = jnp.where(n * CMP_STRIDE + (CMP_BLOCK - 1) <= qpos, st, NEG_INF)
    m = jnp.max(st, axis=0, keepdims=True)
    m = jnp.where(m == NEG_INF, 0.0, m)
    p = jnp.exp(st - m)
    denom = jnp.sum(p, axis=0, keepdims=True)
    pb = (p / jnp.maximum(denom, 1e-30)).astype(BF16)
    o_t = _dot(vct_ref[...], pb)
    o_ref[...] = _gate_and_merge(_untranspose_heads(o_t), _sigmoid(gate_ref[...]), 0)

    ss = lax.broadcasted_iota(jnp.int32, (ns, nc), 0) * SEL_BLOCK
    nn = lax.broadcasted_iota(jnp.int32, (ns, nc), 1) * CMP_STRIDE
    overlap_t = jnp.where((nn < ss + SEL_BLOCK) & (nn + (CMP_BLOCK - 1) >= ss), 1.0, 0.0).astype(BF16)
    imp_r = _dot(overlap_t, pb)
    imp = imp_r[:, 0:Q_BLOCK]
    for r in range(1, GROUP_R):
        imp = imp + imp_r[:, r * Q_BLOCK:(r + 1) * Q_BLOCK]
    blk = lax.broadcasted_iota(jnp.int32, (ns, Q_BLOCK), 0)
    qp = i * Q_BLOCK + lax.broadcasted_iota(jnp.int32, (ns, Q_BLOCK), 1)
    cur = qp // SEL_BLOCK
    forced = (blk == 0) | (blk == cur) | (blk == cur - 1)
    imp = jnp.where(forced, jnp.inf, imp)
    imp = jnp.where(blk <= cur, imp, NEG_INF)
    blkf = blk.astype(F32)
    sel = jnp.zeros((ns, Q_BLOCK), F32)
    for _ in range(sel_k):
        mx = jnp.max(imp, axis=0, keepdims=True)
        first = jnp.min(jnp.where(imp == mx, blkf, float(ns)), axis=0, keepdims=True)
        hit = blkf == first
        sel = jnp.where(hit & (mx > NEG_INF), 1.0, sel)
        imp = jnp.where(hit, NEG_INF, imp)
    selt_ref[...] = sel


def _cmp_attn(qt, k_cmp, v_cmp_t, gates):
    b, g, nblk, d, rq = qt.shape
    t = nblk * Q_BLOCK
    nc = k_cmp.shape[1]
    ns = t // SEL_BLOCK
    return pl.pallas_call(
        functools.partial(_cmp_attn_kernel, sel_k=min(SEL_TOPK, ns)),
        grid=(b, g, nblk),
        in_specs=[
            pl.BlockSpec((None, None, None, d, rq), lambda bi, gi, i: (bi, gi, i, 0, 0)),
            pl.BlockSpec((None, nc, d), lambda bi, gi, i: (bi * NSA_KV_HEADS + gi, 0, 0)),
            pl.BlockSpec((None, d, nc), lambda bi, gi, i: (bi * NSA_KV_HEADS + gi, 0, 0)),
            pl.BlockSpec((None, Q_BLOCK, 128), lambda bi, gi, i: (bi, i, gi)),
        ],
        out_specs=[
            pl.BlockSpec((None, Q_BLOCK, GROUP_R * d), lambda bi, gi, i: (bi, i, gi)),
            pl.BlockSpec((None, None, ns, Q_BLOCK), lambda bi, gi, i: (bi, gi, 0, i)),
        ],
        out_shape=[jax.ShapeDtypeStruct((b, t, g * GROUP_R * d), F32),
                   jax.ShapeDtypeStruct((b, g, ns, t), F32)],
        compiler_params=_params(("parallel", "parallel", "arbitrary"), VMEM_LIMIT),
        name="nsa_cmp_topk",
    )(qt, k_cmp, v_cmp_t, gates)


def _sel_attn_kernel(qt_ref, ks_ref, vst_ref, selt_ref, gate_ref, o_ref, m_ref, l_ref, acc_ref):
    i = pl.program_id(2)
    tk = SEL_KEY_TILE
    bpt = tk // SEL_BLOCK
    qt = qt_ref[...]
    m_ref[...] = jnp.full(m_ref.shape, NEG_INF, F32)
    l_ref[...] = jnp.zeros_like(l_ref)
    acc_ref[...] = jnp.zeros_like(acc_ref)
    n_tiles = (i * Q_BLOCK + Q_BLOCK + tk - 1) // tk

    def tile(j, causal):
        k0 = pl.multiple_of(j * tk, tk)
        st = _dot(ks_ref[pl.ds(k0, tk), :], qt)
        sel_rows = selt_ref[pl.ds(pl.multiple_of(j * bpt, bpt), bpt), :]
        bias_rows = jnp.where(sel_rows > 0.5, 0.0, NEG_INF)
        bias = jnp.concatenate(
            [jnp.broadcast_to(bias_rows[s:s + 1, :], (SEL_BLOCK, Q_BLOCK)) for s in range(bpt)],
            axis=0)
        if causal:
            key = k0 + lax.broadcasted_iota(jnp.int32, (tk, Q_BLOCK), 0)
            qpos = i * Q_BLOCK + lax.broadcasted_iota(jnp.int32, (tk, Q_BLOCK), 1)
            bias = jnp.where(key <= qpos, bias, NEG_INF)
        st = st + jnp.concatenate([bias] * GROUP_R, axis=1)
        m_prev = m_ref[...]
        m_new = jnp.maximum(m_prev, jnp.max(st, axis=0, keepdims=True))
        m_safe = jnp.where(m_new == NEG_INF, 0.0, m_new)
        alpha = jnp.exp(m_prev - m_safe)
        p = jnp.exp(st - m_safe)
        l_ref[...] = alpha * l_ref[...] + jnp.sum(p, axis=0, keepdims=True)
        acc_ref[...] = alpha * acc_ref[...] + _dot(vst_ref[j], p.astype(BF16))
        m_ref[...] = m_new

    def step(j, carry):
        tile(j, False)
        return carry

    lax.fori_loop(0, n_tiles - 1, step, 0)
    tile(n_tiles - 1, True)
    o_t = acc_ref[...] / jnp.maximum(l_ref[...], 1e-30)
    o_ref[...] = _gate_and_merge(_untranspose_heads(o_t), _sigmoid(gate_ref[...]), 1)


def _sel_attn(qt, ks_n, vs_b, sel_t, gates):
    b, g, nblk, d, rq = qt.shape
    t = nblk * Q_BLOCK
    ns = sel_t.shape[2]
    tk = SEL_KEY_TILE
    vs_t = vs_b.reshape(b, g, t // tk, tk, d).transpose(0, 1, 2, 4, 3)
    return pl.pallas_call(
        _sel_attn_kernel,
        grid=(b, g, nblk),
        in_specs=[
            pl.BlockSpec((None, None, None, d, rq), lambda bi, gi, i: (bi, gi, i, 0, 0)),
            pl.BlockSpec((None, None, t, d), lambda bi, gi, i: (bi, gi, 0, 0)),
            pl.BlockSpec((None, None, t // tk, d, tk), lambda bi, gi, i: (bi, gi, 0, 0, 0)),
            pl.BlockSpec((None, None, ns, Q_BLOCK), lambda bi, gi, i: (bi, gi, 0, i)),
            pl.BlockSpec((None, Q_BLOCK, 128), lambda bi, gi, i: (bi, i, gi)),
        ],
        out_specs=pl.BlockSpec((None, Q_BLOCK, GROUP_R * d), lambda bi, gi, i: (bi, i, gi)),
        out_shape=jax.ShapeDtypeStruct((b, t, g * GROUP_R * d), F32),
        scratch_shapes=[pltpu.VMEM((1, rq), F32), pltpu.VMEM((1, rq), F32),
                        pltpu.VMEM((d, rq), F32)],
        compiler_params=_params(("parallel", "parallel", "arbitrary"), VMEM_LIMIT),
        name="nsa_selected",
    )(qt, ks_n, vs_t, sel_t, gates)


def _retention_kernel(lg_ref, q_ref, k_ref, v_ref, gt_ref, cos_ref, sin_ref, ng_ref, o_ref, r_ref):
    hh = pl.program_id(1)
    ci = pl.program_id(2)
    c = RET_CHUNK
    half = RET_QK_DIM // 2

    @pl.when(ci == 0)
    def _():
        r_ref[...] = jnp.zeros_like(r_ref)

    lg = lg_ref[hh]
    cosf = cos_ref[...]
    sinf = sin_ref[...]

    def rot(x):
        return x * cosf + jnp.concatenate([x[:, half:], x[:, :half]], axis=1) * sinf

    q = rot(q_ref[...])
    k = rot(k_ref[...]) * (RET_QK_DIM ** -0.5)
    vb = v_ref[...].astype(BF16)
    ii = lax.broadcasted_iota(jnp.int32, (c, c), 0)
    jj = lax.broadcasted_iota(jnp.int32, (c, c), 1)
    d = (ii - jj).astype(F32)
    dmask = jnp.where(d >= 0, jnp.exp(d * lg), 0.0)
    att = _dot_nt(q.astype(BF16), k.astype(BF16)) * dmask
    o = _dot(att.astype(BF16), vb)
    jcol = lax.broadcasted_iota(jnp.int32, (c, 1), 0).astype(F32)
    xi = jnp.exp((jcol + 1.0) * lg)
    zeta = jnp.exp((c - 1.0 - jcol) * lg)
    r_prev = r_ref[...]
    o = o + _dot((q * xi).astype(BF16), r_prev.astype(BF16))
    s_chunk = _dot_tn((k * zeta).astype(BF16), vb)
    decay = jnp.exp(jnp.zeros((1, RET_V_DIM), F32) + c * lg)
    r_ref[...] = r_prev * decay + s_chunk
    mu = jnp.mean(o, axis=-1, keepdims=True)
    var = jnp.mean(jnp.square(o - mu), axis=-1, keepdims=True)
    on = (o - mu) * lax.rsqrt(var + EPS)
    gt = gt_ref[...]
    o_ref[...] = (gt * _sigmoid(gt)) * (on * ng_ref[...])


def _retention(qh, kh, proj_d, norm_gain):
    b, h, t, dk = qh.shape
    c = RET_CHUNK
    dv = RET_V_DIM
    v_blk0 = (2 * h * dk) // dv
    g_blk0 = v_blk0 + h
    half = dk // 2
    inv = ROPE_BASE ** (-jnp.arange(half, dtype=F32) / half)
    ang = jnp.arange(t).astype(F32)[:, None] * inv[None, :]
    cos = jnp.cos(ang)
    sin = jnp.sin(ang)
    cosf = jnp.concatenate([cos, cos], axis=-1)
    sinf = jnp.concatenate([-sin, sin], axis=-1)
    log_gamma = jnp.log(1.0 - 2.0 ** (-5.0 - jnp.arange(h, dtype=F32)))
    return pl.pallas_call(
        _retention_kernel,
        grid=(b, h, t // c),
        in_specs=[
            pl.BlockSpec(memory_space=pltpu.SMEM),
            pl.BlockSpec((None, None, c, dk), lambda bi, hi, ci: (bi, hi, ci, 0)),
            pl.BlockSpec((None, None, c, dk), lambda bi, hi, ci: (bi, hi, ci, 0)),
            pl.BlockSpec((None, c, dv), lambda bi, hi, ci: (bi, ci, v_blk0 + hi)),
            pl.BlockSpec((None, c, dv), lambda bi, hi, ci: (bi, ci, g_blk0 + hi)),
            pl.BlockSpec((c, dk), lambda bi, hi, ci: (ci, 0)),
            pl.BlockSpec((c, dk), lambda bi, hi, ci: (ci, 0)),
            pl.BlockSpec((1, dv), lambda bi, hi, ci: (0, hi)),
        ],
        out_specs=pl.BlockSpec((None, c, dv), lambda bi, hi, ci: (bi, ci, hi)),
        out_shape=jax.ShapeDtypeStruct((b, t, h * dv), F32),
        scratch_shapes=[pltpu.VMEM((dk, dv), F32)],
        compiler_params=_params(("parallel", "parallel", "arbitrary")),
        name="retention",
    )(log_gamma, qh, kh, proj_d, proj_d, cosf, sinf, norm_gain.reshape(1, h * dv))


def _merge_kernel(x_ref, g_ref, wgate_ref, bias_ref, ya_ref, yb_ref, yc0_ref, yc1_ref, yc2_ref,
                  yd_ref, wb_ref, wo_ref, o_ref):
    x = x_ref[...]
    d = x.shape[1]
    u = _rms_rows(x, g_ref[...]).astype(BF16)
    ys = (ya_ref[...], yb_ref[...], yc0_ref[...] + yc1_ref[...] + yc2_ref[...], yd_ref[...])
    merged = jnp.zeros(x.shape, F32)
    for n in range(N_BRANCH):
        logits = _dot(u, wgate_ref[:, n * d:(n + 1) * d]) + bias_ref[:, n * d:(n + 1) * d]
        merged = merged + _sigmoid(logits) * _dot(ys[n].astype(BF16), wb_ref[n])
    o_ref[...] = x + _dot(merged.astype(BF16), wo_ref[...])


def _merge(x2, g, w_gate, bias, ys, w_branch, w_out, tm=256):
    n, d = x2.shape
    w = MIX_W
    row = lambda width: pl.BlockSpec((tm, width), lambda i: (i, 0))
    full = lambda shape: pl.BlockSpec(shape, lambda i: tuple(0 for _ in shape))
    return pl.pallas_call(
        _merge_kernel,
        grid=(n // tm,),
        in_specs=[row(d), full((1, d)), full((d, N_BRANCH * d)), full((1, N_BRANCH * d))]
        + [row(w)] * 6 + [full((N_BRANCH, w, d)), full((d, d))],
        out_specs=row(d),
        out_shape=jax.ShapeDtypeStruct((n, d), F32),
        compiler_params=_params(("parallel",), VMEM_LIMIT),
        name="merge_out",
    )(x2, g.reshape(1, d), w_gate.astype(BF16), bias.reshape(1, N_BRANCH * d),
      *[y.reshape(n, w) for y in ys], w_branch.astype(BF16), w_out.astype(BF16))


def _heads(t2, b, t, n):
    return t2.reshape(b, t, n, -1).transpose(0, 2, 1, 3)


def _mixers(x2, b, t, mix_norm, w_in, merge_gate_bias, swa_q_gain, swa_k_gain, swa_sinks, conv_w,
            nsa_q_gain, nsa_k_gain, cmp_pos_k, cmp_pos_v, cmp_wk1, cmp_wk2, cmp_wv1, cmp_wv2,
            ret_norm_gain, w_branch, w_out):
    d_model = x2.shape[1]
    hd = HEAD_DIM
    swa_q, swa_kv = SWA_HEADS * hd, SWA_KV_HEADS * hd
    nsa_q, nsa_kv = NSA_HEADS * hd, NSA_KV_HEADS * hd
    ret_qk, ret_v = RET_HEADS * RET_QK_DIM, RET_HEADS * RET_V_DIM
    n_gate = NSA_HEADS * 3
    o_a = 0
    o_b = o_a + swa_q + 2 * swa_kv
    o_c = o_b + 3 * MIX_W
    o_cg = o_c + nsa_q + 6 * nsa_kv
    o_d = o_cg + n_gate
    o_g = o_d + 2 * ret_qk + 2 * ret_v
    scale = hd ** -0.5

    per_g = GROUP_R * 3
    w_cg = jnp.concatenate(
        [jnp.pad(w_in[:, o_cg + gi * per_g:o_cg + (gi + 1) * per_g], ((0, 0), (0, 128 - per_g)))
         for gi in range(NSA_KV_HEADS)], axis=1)
    proj_a = _norm_matmul(x2, mix_norm, w_in[:, o_a:o_b])
    proj_b = _norm_matmul(x2, mix_norm, w_in[:, o_b:o_c])
    proj_c = _norm_matmul(x2, mix_norm, jnp.concatenate([w_in[:, o_c:o_cg], w_cg], axis=1))
    proj_d = _norm_matmul(x2, mix_norm, w_in[:, o_d:o_g])

    a_qn = _headnorm(_heads(proj_a[:, :swa_q], b, t, SWA_HEADS), swa_q_gain, scale)
    a_kn = _headnorm(_heads(proj_a[:, swa_q:swa_q + swa_kv], b, t, SWA_KV_HEADS), swa_k_gain)
    a_v = _heads(proj_a[:, swa_q + swa_kv:], b, t, SWA_KV_HEADS).astype(BF16)
    y_a = _banded(a_qn, a_kn, a_v, SWA_WINDOW, sinks=swa_sinks)

    y_b = _conv(proj_b.reshape(b, t, 3 * MIX_W), conv_w)

    def c_kv(j):
        lo = nsa_q + j * nsa_kv
        return _heads(proj_c[:, lo:lo + nsa_kv], b, t, NSA_KV_HEADS)

    c_qn = _headnorm(_heads(proj_c[:, :nsa_q], b, t, NSA_HEADS), nsa_q_gain, scale)
    gates_c = proj_c[:, nsa_q + 6 * nsa_kv:].reshape(b, t, NSA_KV_HEADS * 128)
    k_cmp, v_cmp = _compress(c_kv(0), c_kv(1), cmp_pos_k, cmp_pos_v, cmp_wk1, cmp_wk2,
                             cmp_wv1, cmp_wv2, nsa_k_gain[0])
    c_qt = _transpose_q(c_qn)
    y_cmp, sel_t = _cmp_attn(c_qt, k_cmp, v_cmp.transpose(0, 2, 1), gates_c)
    y_sel = _sel_attn(c_qt, _headnorm(c_kv(2), nsa_k_gain[1]), c_kv(3).astype(BF16), sel_t, gates_c)
    y_win = _banded(c_qn, _headnorm(c_kv(4), nsa_k_gain[2]), c_kv(5).astype(BF16), NSA_WINDOW,
                    gates=gates_c, gate_branch=2)

    y_d = _retention(_heads(proj_d[:, :ret_qk], b, t, RET_HEADS),
                     _heads(proj_d[:, ret_qk:2 * ret_qk], b, t, RET_HEADS),
                     proj_d.reshape(b, t, 2 * ret_qk + 2 * ret_v), ret_norm_gain)

    return _merge(x2, mix_norm, w_in[:, o_g:], merge_gate_bias,
                  (y_a, y_b, y_cmp, y_sel, y_win, y_d), w_branch, w_out)


def kernel(x, ffn1_norm, ffn1_w_gate, ffn1_w_up, ffn1_w_down, mix_norm, w_in, merge_gate_bias, swa_q_gain, swa_k_gain, swa_sinks, conv_w, nsa_q_gain, nsa_k_gain, cmp_pos_k, cmp_pos_v, cmp_wk1, cmp_wk2, cmp_wv1, cmp_wv2, ret_norm_gain, w_branch, w_out, ffn2_norm, ffn2_w_gate, ffn2_w_up, ffn2_w_down):
    b, t, d = x.shape
    x2 = x.reshape(b * t, d)
    for l in range(ffn1_norm.shape[0]):
        x2 = _ffn(x2, ffn1_norm[l], ffn1_w_gate[l], ffn1_w_up[l], ffn1_w_down[l])
        x2 = _mixers(x2, b, t, mix_norm[l], w_in[l], merge_gate_bias[l], swa_q_gain[l],
                     swa_k_gain[l], swa_sinks[l], conv_w[l], nsa_q_gain[l], nsa_k_gain[l],
                     cmp_pos_k[l], cmp_pos_v[l], cmp_wk1[l], cmp_wk2[l], cmp_wv1[l], cmp_wv2[l],
                     ret_norm_gain[l], w_branch[l], w_out[l])
        x2 = _ffn(x2, ffn2_norm[l], ffn2_w_gate[l], ffn2_w_up[l], ffn2_w_down[l])
    return x2.reshape(b, t, d)
```

```python
import functools

import jax
import jax.numpy as jnp
from jax import lax
from jax.experimental import pallas as pl
from jax.experimental.pallas import tpu as pltpu

F32 = jnp.float32
BF16 = jnp.bfloat16

HEAD_DIM = 64
Q_BLOCK = 128
MIX_W = 512
N_BRANCH = 4
SWA_HEADS = 8
SWA_KV_HEADS = 2
SWA_WINDOW = 128
NSA_HEADS = 8
NSA_KV_HEADS = 2
CMP_BLOCK = 32
CMP_STRIDE = 16
SEL_BLOCK = 64
SEL_TOPK = 16
NSA_WINDOW = 512
RET_HEADS = 4
RET_QK_DIM = 64
RET_V_DIM = 128
RET_CHUNK = 128
ROPE_BASE = 10000.0
EPS = 1e-6
GROUP_R = 4
SEL_KEY_TILE = 512
VMEM_LIMIT = 52 * 1024 * 1024

NEG_INF = float("-inf")


def _params(sem, vmem=None):
    return pltpu.CompilerParams(dimension_semantics=sem, vmem_limit_bytes=vmem)


def _sigmoid(x):
    return 1.0 / (1.0 + jnp.exp(-x))


def _rms_rows(x, g):
    return x * lax.rsqrt(jnp.mean(x * x, axis=-1, keepdims=True) + EPS) * g


def _dot(a, b):
    return jnp.dot(a, b, preferred_element_type=F32)


def _dot_nt(a, b):
    return lax.dot_general(a, b, (((1,), (1,)), ((), ())), preferred_element_type=F32)


def _dot_tn(a, b):
    return lax.dot_general(a, b, (((0,), (0,)), ((), ())), preferred_element_type=F32)


def _ffn_kernel(x_ref, g_ref, wg_ref, wu_ref, wd_ref, o_ref, xn_ref, acc_ref):
    f = pl.program_id(1)

    @pl.when(f == 0)
    def _():
        xn_ref[...] = _rms_rows(x_ref[...], g_ref[...]).astype(BF16)
        acc_ref[...] = jnp.zeros_like(acc_ref)

    xn = xn_ref[...]
    a = _dot(xn, wg_ref[...])
    b = _dot(xn, wu_ref[...])
    h = (a * _sigmoid(a)) * b
    acc_ref[...] += _dot(h.astype(BF16), wd_ref[...])

    @pl.when(f == pl.num_programs(1) - 1)
    def _():
        o_ref[...] = x_ref[...] + 0.5 * acc_ref[...]


def _ffn(x2, g, wg, wu, wd, tm=512, nf=2):
    n, d = x2.shape
    dff = wg.shape[1]
    tf = dff // nf
    return pl.pallas_call(
        _ffn_kernel,
        grid=(n // tm, nf),
        in_specs=[
            pl.BlockSpec((tm, d), lambda i, f: (i, 0)),
            pl.BlockSpec((1, d), lambda i, f: (0, 0)),
            pl.BlockSpec((d, tf), lambda i, f: (0, f)),
            pl.BlockSpec((d, tf), lambda i, f: (0, f)),
            pl.BlockSpec((tf, d), lambda i, f: (f, 0)),
        ],
        out_specs=pl.BlockSpec((tm, d), lambda i, f: (i, 0)),
        out_shape=jax.ShapeDtypeStruct((n, d), F32),
        scratch_shapes=[pltpu.VMEM((tm, d), BF16), pltpu.VMEM((tm, d), F32)],
        compiler_params=_params(("parallel", "arbitrary"), VMEM_LIMIT),
        name="ffn",
    )(x2, g.reshape(1, d), wg.astype(BF16), wu.astype(BF16), wd.astype(BF16))


def _norm_matmul_kernel(x_ref, g_ref, w_ref, o_ref):
    xn = _rms_rows(x_ref[...], g_ref[...]).astype(BF16)
    o_ref[...] = _dot(xn, w_ref[...])


def _norm_matmul(x2, g, w, tm=512):
    n, d = x2.shape
    c = w.shape[1]
    return pl.pallas_call(
        _norm_matmul_kernel,
        grid=(n // tm,),
        in_specs=[
            pl.BlockSpec((tm, d), lambda i: (i, 0)),
            pl.BlockSpec((1, d), lambda i: (0, 0)),
            pl.BlockSpec((d, c), lambda i: (0, 0)),
        ],
        out_specs=pl.BlockSpec((tm, c), lambda i: (i, 0)),
        out_shape=jax.ShapeDtypeStruct((n, c), F32),
        compiler_params=_params(("parallel",), VMEM_LIMIT),
        name="in_proj",
    )(x2, g.reshape(1, d), w.astype(BF16))


def _headnorm_kernel(x_ref, g_ref, o_ref, *, scale):
    y = _rms_rows(x_ref[...], g_ref[...])
    if scale != 1.0:
        y = y * scale
    o_ref[...] = y.astype(o_ref.dtype)


def _headnorm(xh, gain, scale=1.0, tt=1024):
    b, h, t, d = xh.shape
    tt = min(tt, t)
    return pl.pallas_call(
        functools.partial(_headnorm_kernel, scale=scale),
        grid=(b, h, t // tt),
        in_specs=[
            pl.BlockSpec((None, None, tt, d), lambda i, j, k: (i, j, k, 0)),
            pl.BlockSpec((1, d), lambda i, j, k: (0, 0)),
        ],
        out_specs=pl.BlockSpec((None, None, tt, d), lambda i, j, k: (i, j, k, 0)),
        out_shape=jax.ShapeDtypeStruct((b, h, t, d), BF16),
        compiler_params=_params(("parallel", "parallel", "parallel")),
        name="headnorm",
    )(xh, gain.reshape(1, d))


def _gate_and_merge(o, gsig, branch):
    outs = []
    for r in range(GROUP_R):
        c = r * 3 + branch
        outs.append(o[r * Q_BLOCK:(r + 1) * Q_BLOCK] * gsig[:, c:c + 1])
    return jnp.concatenate(outs, axis=1)


def _banded_kernel(*refs, window, has_sink, gate_branch):
    refs = list(refs)
    sink_ref = refs.pop(0) if has_sink else None
    q_ref, k_ref, v_ref = refs[:3]
    gate_ref = refs[3] if gate_branch is not None else None
    o_ref = refs[-1]
    g = pl.program_id(1)
    i = pl.program_id(2)
    rq = GROUP_R * Q_BLOCK
    span = window + Q_BLOCK
    q = q_ref[...].reshape(rq, HEAD_DIM)
    start = pl.multiple_of(jnp.maximum(i * Q_BLOCK - window, 0), Q_BLOCK)
    k = k_ref[pl.ds(start, span), :]
    v = v_ref[pl.ds(start, span), :]
    s = _dot_nt(q, k)
    row = lax.broadcasted_iota(jnp.int32, (rq, span), 0)
    col = lax.broadcasted_iota(jnp.int32, (rq, span), 1)
    diff = (i * Q_BLOCK + (row & (Q_BLOCK - 1))) - (start + col)
    s = jnp.where((diff >= 0) & (diff < window), s, NEG_INF)
    m = jnp.max(s, axis=-1, keepdims=True)
    if has_sink:
        rcol = lax.broadcasted_iota(jnp.int32, (rq, 1), 0)
        sink = jnp.zeros((rq, 1), F32)
        for r in range(GROUP_R):
            sink = jnp.where((rcol >= r * Q_BLOCK) & (rcol < (r + 1) * Q_BLOCK),
                             sink_ref[g * GROUP_R + r], sink)
        m = jnp.maximum(m, sink)
    m = jnp.where(m == NEG_INF, 0.0, m)
    p = jnp.exp(s - m)
    denom = jnp.sum(p, axis=-1, keepdims=True)
    if has_sink:
        denom = denom + jnp.exp(sink - m)
    o = _dot(p.astype(BF16), v) / jnp.maximum(denom, 1e-30)
    if gate_branch is None:
        o_ref[...] = jnp.concatenate(
            [o[r * Q_BLOCK:(r + 1) * Q_BLOCK] for r in range(GROUP_R)], axis=1)
    else:
        o_ref[...] = _gate_and_merge(o, _sigmoid(gate_ref[...]), gate_branch)


def _banded(qn, kn, vb, window, sinks=None, gates=None, gate_branch=None):
    b, h, t, d = qn.shape
    g = kn.shape[1]
    in_specs = []
    args = []
    if sinks is not None:
        in_specs.append(pl.BlockSpec(memory_space=pltpu.SMEM))
        args.append(sinks.astype(F32))
    in_specs += [
        pl.BlockSpec((None, GROUP_R, Q_BLOCK, d), lambda bi, gi, i: (bi, gi, i, 0)),
        pl.BlockSpec((None, None, t, d), lambda bi, gi, i: (bi, gi, 0, 0)),
        pl.BlockSpec((None, None, t, d), lambda bi, gi, i: (bi, gi, 0, 0)),
    ]
    args += [qn, kn, vb]
    if gates is not None:
        in_specs.append(pl.BlockSpec((None, Q_BLOCK, 128), lambda bi, gi, i: (bi, i, gi)))
        args.append(gates)
    return pl.pallas_call(
        functools.partial(_banded_kernel, window=window, has_sink=sinks is not None,
                          gate_branch=gate_branch if gates is not None else None),
        grid=(b, g, t // Q_BLOCK),
        in_specs=in_specs,
        out_specs=pl.BlockSpec((None, Q_BLOCK, GROUP_R * d), lambda bi, gi, i: (bi, i, gi)),
        out_shape=jax.ShapeDtypeStruct((b, t, h * d), F32),
        compiler_params=_params(("parallel", "parallel", "arbitrary"), VMEM_LIMIT),
        name="banded_attn_w%d" % window,
    )(*args)


def _conv_kernel(x_ref, b_ref, c_ref, xp_ref, cp_ref, w_ref, o_ref):
    ti = pl.program_id(1)
    z = c_ref[...] * x_ref[...]
    tt = z.shape[0]
    zp = jnp.where(ti > 0, cp_ref[...] * xp_ref[...], 0.0)
    row = lax.broadcasted_iota(jnp.int32, z.shape, 0)
    z1 = jnp.where(row == 0, zp[7:8, :], pltpu.roll(z, 1, 0))
    z2 = pltpu.roll(z, 2, 0)
    z2 = jnp.where(row == 0, zp[6:7, :], jnp.where(row == 1, zp[7:8, :], z2))
    w = w_ref[...]
    o_ref[...] = b_ref[...] * (w[0:1, :] * z2 + w[1:2, :] * z1 + w[2:3, :] * z)


def _conv(proj_b, conv_w, tt=512):
    b, t, _ = proj_b.shape
    w = MIX_W
    hb = tt // 8
    prev = lambda bi, ti: (bi, jnp.maximum(ti * hb - 1, 0), 0)
    prev_c = lambda bi, ti: (bi, jnp.maximum(ti * hb - 1, 0), 2)
    return pl.pallas_call(
        _conv_kernel,
        grid=(b, t // tt),
        in_specs=[
            pl.BlockSpec((None, tt, w), lambda bi, ti: (bi, ti, 0)),
            pl.BlockSpec((None, tt, w), lambda bi, ti: (bi, ti, 1)),
            pl.BlockSpec((None, tt, w), lambda bi, ti: (bi, ti, 2)),
            pl.BlockSpec((None, 8, w), prev),
            pl.BlockSpec((None, 8, w), prev_c),
            pl.BlockSpec((8, w), lambda bi, ti: (0, 0)),
        ],
        out_specs=pl.BlockSpec((None, tt, w), lambda bi, ti: (bi, ti, 0)),
        out_shape=jax.ShapeDtypeStruct((b, t, w), F32),
        compiler_params=_params(("parallel", "parallel")),
        name="short_conv",
    )(proj_b, proj_b, proj_b, proj_b, proj_b,
      jnp.pad(conv_w.reshape(conv_w.shape[0], w).astype(F32), ((0, 8 - conv_w.shape[0]), (0, 0))))


def _gelu_tanh(x):
    return x * (0.5 * (1.0 + jnp.tanh(0.7978845608028654 * (x + 0.044715 * (x * x * x)))))


def _compress_kernel(tk_ref, tv_ref, pek_ref, pev_ref, wk1_ref, wk2_ref, wv1_ref, wv2_ref,
                     kg_ref, ko_ref, vo_ref):
    half = tk_ref.shape[1]
    nrow = tk_ref.shape[0]

    def mlp(a, pe_ref, w1_ref, w2_ref):
        a0 = (a + pe_ref[0:1, :]).astype(BF16)
        a1 = (a + pe_ref[1:2, :]).astype(BF16)
        p1 = _dot(a0, w1_ref[0:half, :])
        p2 = _dot(a1, w1_ref[half:2 * half, :])
        hdn = p1 + pltpu.roll(p2, nrow - 1, 0)
        return _dot(_gelu_tanh(hdn).astype(BF16), w2_ref[...])

    kc = mlp(tk_ref[...], pek_ref, wk1_ref, wk2_ref)
    ko_ref[...] = _rms_rows(kc, kg_ref[...]).astype(BF16)
    vo_ref[...] = mlp(tv_ref[...], pev_ref, wv1_ref, wv2_ref).astype(BF16)


def _compress(kc_h, vc_h, pos_k, pos_v, wk1, wk2, wv1, wv2, k_gain):
    b, g, t, d = kc_h.shape
    nrow = t // CMP_STRIDE
    half = CMP_STRIDE * d
    hid = wk1.shape[1]
    tk = kc_h.reshape(b * g, nrow, half)
    tv = vc_h.reshape(b * g, nrow, half)
    full = lambda shape: pl.BlockSpec(shape, lambda i: tuple(0 for _ in shape))
    return pl.pallas_call(
        _compress_kernel,
        grid=(b * g,),
        in_specs=[
            pl.BlockSpec((None, nrow, half), lambda i: (i, 0, 0)),
            pl.BlockSpec((None, nrow, half), lambda i: (i, 0, 0)),
            full((2, half)), full((2, half)),
            full((2 * half, hid)), full((hid, d)),
            full((2 * half, hid)), full((hid, d)),
            full((1, d)),
        ],
        out_specs=[pl.BlockSpec((None, nrow, d), lambda i: (i, 0, 0)),
                   pl.BlockSpec((None, nrow, d), lambda i: (i, 0, 0))],
        out_shape=[jax.ShapeDtypeStruct((b * g, nrow, d), BF16),
                   jax.ShapeDtypeStruct((b * g, nrow, d), BF16)],
        compiler_params=_params(("parallel",), VMEM_LIMIT),
        name="nsa_compress",
    )(tk, tv, pos_k.reshape(2, half), pos_v.reshape(2, half),
      wk1.astype(BF16), wk2.astype(BF16), wv1.astype(BF16), wv2.astype(BF16),
      k_gain.reshape(1, d))


def _untranspose_heads(o_t):
    return jnp.concatenate(
        [o_t[:, r * Q_BLOCK:(r + 1) * Q_BLOCK].T for r in range(GROUP_R)], axis=0)


def _transpose_q(qn):
    b, h, t, d = qn.shape
    g = h // GROUP_R
    qt = qn.reshape(b, g, GROUP_R, t // Q_BLOCK, Q_BLOCK, d).transpose(0, 1, 3, 5, 2, 4)
    return qt.reshape(b, g, t // Q_BLOCK, d, GROUP_R * Q_BLOCK)


def _cmp_attn_kernel(qt_ref, kc_ref, vct_ref, gate_ref, o_ref, selt_ref, *, sel_k):
    i = pl.program_id(2)
    rq = GROUP_R * Q_BLOCK
    nc = kc_ref.shape[0]
    ns = selt_ref.shape[0]
    st = _dot(kc_ref[...], qt_ref[...])
    n = lax.broadcasted_iota(jnp.int32, (nc, rq), 0)
    lane = lax.broadcasted_iota(jnp.int32, (nc, rq), 1)
    qpos = i * Q_BLOCK + (lane & (Q_BLOCK - 1))
    st = jnp.where(n * CMP_STRIDE + (CMP_BLOCK - 1) <= qpos, st, NEG_INF)
    m = jnp.max(st, axis=0, keepdims=True)
    m = jnp.where(m == NEG_INF, 0.0, m)
    p = jnp.exp(st - m)
    denom = jnp.sum(p, axis=0, keepdims=True)
    pb = (p / jnp.maximum(denom, 1e-30)).astype(BF16)
    o_t = _dot(vct_ref[...], pb)
    o_ref[...] = _gate_and_merge(_untranspose_heads(o_t), _sigmoid(gate_ref[...]), 0)

    ss = lax.broadcasted_iota(jnp.int32, (ns, nc), 0) * SEL_BLOCK
    nn = lax.broadcasted_iota(jnp.int32, (ns, nc), 1) * CMP_STRIDE
    overlap_t = jnp.where((nn < ss + SEL_BLOCK) & (nn + (CMP_BLOCK - 1) >= ss), 1.0, 0.0).astype(BF16)
    imp_r = _dot(overlap_t, pb)
    imp = imp_r[:, 0:Q_BLOCK]
    for r in range(1, GROUP_R):
        imp = imp + imp_r[:, r * Q_BLOCK:(r + 1) * Q_BLOCK]
    blk = lax.broadcasted_iota(jnp.int32, (ns, Q_BLOCK), 0)
    qp = i * Q_BLOCK + lax.broadcasted_iota(jnp.int32, (ns, Q_BLOCK), 1)
    cur = qp // SEL_BLOCK
    forced = (blk == 0) | (blk == cur) | (blk == cur - 1)
    imp = jnp.where(forced, jnp.inf, imp)
    imp = jnp.where(blk <= cur, imp, NEG_INF)
    blkf = blk.astype(F32)
    sel = jnp.zeros((ns, Q_BLOCK), F32)
    for _ in range(sel_k):
        mx = jnp.max(imp, axis=0, keepdims=True)
        first = jnp.min(jnp.where(imp == mx, blkf, float(ns)), axis=0, keepdims=True)
        hit = blkf == first
        sel = jnp.where(hit & (mx > NEG_INF), 1.0, sel)
        imp = jnp.where(hit, NEG_INF, imp)
    selt_ref[...] = sel


def _cmp_attn(qt, k_cmp, v_cmp_t, gates):
    b, g, nblk, d, rq = qt.shape
    t = nblk * Q_BLOCK
    nc = k_cmp.shape[1]
    ns = t // SEL_BLOCK
    return pl.pallas_call(
        functools.partial(_cmp_attn_kernel, sel_k=min(SEL_TOPK, ns)),
        grid=(b, g, nblk),
        in_specs=[
            pl.BlockSpec((None, None, None, d, rq), lambda bi, gi, i: (bi, gi, i, 0, 0)),
            pl.BlockSpec((None, nc, d), lambda bi, gi, i: (bi * NSA_KV_HEADS + gi, 0, 0)),
            pl.BlockSpec((None, d, nc), lambda bi, gi, i: (bi * NSA_KV_HEADS + gi, 0, 0)),
            pl.BlockSpec((None, Q_BLOCK, 128), lambda bi, gi, i: (bi, i, gi)),
        ],
        out_specs=[
            pl.BlockSpec((None, Q_BLOCK, GROUP_R * d), lambda bi, gi, i: (bi, i, gi)),
            pl.BlockSpec((None, None, ns, Q_BLOCK), lambda bi, gi, i: (bi, gi, 0, i)),
        ],
        out_shape=[jax.ShapeDtypeStruct((b, t, g * GROUP_R * d), F32),
                   jax.ShapeDtypeStruct((b, g, ns, t), F32)],
        compiler_params=_params(("parallel", "parallel", "arbitrary"), VMEM_LIMIT),
        name="nsa_cmp_topk",
    )(qt, k_cmp, v_cmp_t, gates)


def _sel_attn_kernel(qt_ref, ks_ref, vst_ref, selt_ref, gate_ref, o_ref, sa_ref, sb_ref, m_ref,
                     acc_ref):
    i = pl.program_id(2)
    tk = SEL_KEY_TILE
    bpt = tk // SEL_BLOCK
    spt = tk // Q_BLOCK
    d = HEAD_DIM
    m_ref[...] = jnp.full(m_ref.shape, NEG_INF, F32)
    acc_ref[...] = jnp.zeros_like(acc_ref)
    n_tiles = (i * Q_BLOCK + Q_BLOCK + tk - 1) // tk
    tri = jnp.where(lax.broadcasted_iota(jnp.int32, (Q_BLOCK, Q_BLOCK), 0)
                    <= lax.broadcasted_iota(jnp.int32, (Q_BLOCK, Q_BLOCK), 1), 0.0, NEG_INF)

    def scores(j, s_ref):
        k0 = pl.multiple_of(j * tk, tk)
        st = _dot(ks_ref[pl.ds(k0, tk), :], qt_ref[...])
        sel_rows = selt_ref[pl.ds(pl.multiple_of(j * bpt, bpt), bpt), :]
        bias_rows = jnp.where(sel_rows > 0.5, 0.0, NEG_INF)
        parts = []
        for u in range(spt):
            sub = jnp.concatenate(
                [jnp.broadcast_to(bias_rows[s:s + 1, :], (SEL_BLOCK, Q_BLOCK))
                 for s in range(u * Q_BLOCK // SEL_BLOCK, (u + 1) * Q_BLOCK // SEL_BLOCK)], axis=0)
            parts.append(sub + jnp.where(j * spt + u == i, tri, 0.0))
        bias = jnp.concatenate(parts, axis=0)
        s_ref[...] = st + jnp.concatenate([bias] * GROUP_R, axis=1)

    def consume(j, s_ref):
        st = s_ref[...]
        m_prev = m_ref[...]
        m_new = jnp.maximum(m_prev, jnp.max(st, axis=0, keepdims=True))
        m_safe = jnp.where(m_new == NEG_INF, 0.0, m_new)
        alpha = jnp.exp(m_prev - m_safe)
        p = jnp.exp((st - m_safe).astype(BF16))
        acc_ref[...] = alpha * acc_ref[...] + _dot(vst_ref[j], p)
        m_ref[...] = m_new

    scores(0, sa_ref)
    n_pairs = (n_tiles - 1) // 2

    def step(jj, carry):
        j = 2 * jj
        scores(j + 1, sb_ref)
        consume(j, sa_ref)
        scores(j + 2, sa_ref)
        consume(j + 1, sb_ref)
        return carry

    lax.fori_loop(0, n_pairs, step, 0)
    j_rem = 2 * n_pairs

    @pl.when(n_tiles - j_rem == 2)
    def _():
        scores(j_rem + 1, sb_ref)

    consume(j_rem, sa_ref)

    @pl.when(n_tiles - j_rem == 2)
    def _():
        consume(j_rem + 1, sb_ref)

    o_t = acc_ref[0:d, :] / jnp.maximum(acc_ref[d:d + 1, :], 1e-30)
    o_ref[...] = _gate_and_merge(_untranspose_heads(o_t), _sigmoid(gate_ref[...]), 1)


def _sel_attn(qt, ks_n, vs_b, sel_t, gates):
    b, g, nblk, d, rq = qt.shape
    t = nblk * Q_BLOCK
    ns = sel_t.shape[2]
    tk = SEL_KEY_TILE
    vs_t = vs_b.reshape(b, g, t // tk, tk, d).transpose(0, 1, 2, 4, 3)
    dpad = d + 16
    vs_t = jnp.concatenate([vs_t, jnp.ones((b, g, t // tk, dpad - d, tk), BF16)], axis=3)
    return pl.pallas_call(
        _sel_attn_kernel,
        grid=(b, g, nblk),
        in_specs=[
            pl.BlockSpec((None, None, None, d, rq), lambda bi, gi, i: (bi, gi, i, 0, 0)),
            pl.BlockSpec((None, None, t, d), lambda bi, gi, i: (bi, gi, 0, 0)),
            pl.BlockSpec((None, None, t // tk, dpad, tk), lambda bi, gi, i: (bi, gi, 0, 0, 0)),
            pl.BlockSpec((None, None, ns, Q_BLOCK), lambda bi, gi, i: (bi, gi, 0, i)),
            pl.BlockSpec((None, Q_BLOCK, 128), lambda bi, gi, i: (bi, i, gi)),
        ],
        out_specs=pl.BlockSpec((None, Q_BLOCK, GROUP_R * d), lambda bi, gi, i: (bi, i, gi)),
        out_shape=jax.ShapeDtypeStruct((b, t, g * GROUP_R * d), F32),
        scratch_shapes=[pltpu.VMEM((tk, rq), F32), pltpu.VMEM((tk, rq), F32),
                        pltpu.VMEM((1, rq), F32), pltpu.VMEM((dpad, rq), F32)],
        compiler_params=_params(("parallel", "parallel", "arbitrary"), VMEM_LIMIT),
        name="nsa_selected",
    )(qt, ks_n, vs_t, sel_t, gates)


def _retention_kernel(lg_ref, q_ref, k_ref, v_ref, gt_ref, cos_ref, sin_ref, ng_ref, o_ref, r_ref):
    ci = pl.program_id(1)
    c = RET_CHUNK
    dk, dv = RET_QK_DIM, RET_V_DIM
    half = dk // 2

    @pl.when(ci == 0)
    def _():
        r_ref[...] = jnp.zeros_like(r_ref)

    cosf = cos_ref[...]
    sinf = sin_ref[...]

    def rot(x):
        return x * cosf + jnp.concatenate([x[:, half:], x[:, :half]], axis=1) * sinf

    ii = lax.broadcasted_iota(jnp.int32, (c, c), 0)
    jj = lax.broadcasted_iota(jnp.int32, (c, c), 1)
    d = (ii - jj).astype(F32)
    jcol = lax.broadcasted_iota(jnp.int32, (c, 1), 0).astype(F32)
    for hh in range(RET_HEADS):
        lg = lg_ref[hh]
        q = rot(q_ref[:, hh * dk:(hh + 1) * dk])
        k = rot(k_ref[:, hh * dk:(hh + 1) * dk]) * (dk ** -0.5)
        vb = v_ref[:, hh * dv:(hh + 1) * dv].astype(BF16)
        dmask = jnp.where(d >= 0, jnp.exp(d * lg), 0.0)
        att = _dot_nt(q.astype(BF16), k.astype(BF16)) * dmask
        o = _dot(att.astype(BF16), vb)
        xi = jnp.exp((jcol + 1.0) * lg)
        zeta = jnp.exp((c - 1.0 - jcol) * lg)
        r_prev = r_ref[hh]
        o = o + _dot((q * xi).astype(BF16), r_prev.astype(BF16))
        s_chunk = _dot_tn((k * zeta).astype(BF16), vb)
        decay = jnp.exp(jnp.zeros((1, dv), F32) + c * lg)
        r_ref[hh] = r_prev * decay + s_chunk
        mu = jnp.mean(o, axis=-1, keepdims=True)
        var = jnp.mean(jnp.square(o - mu), axis=-1, keepdims=True)
        on = (o - mu) * lax.rsqrt(var + EPS)
        gt = gt_ref[:, hh * dv:(hh + 1) * dv]
        o_ref[:, hh * dv:(hh + 1) * dv] = (gt * _sigmoid(gt)) * (on * ng_ref[:, hh * dv:(hh + 1) * dv])


def _retention(proj_d, norm_gain):
    b, t, _ = proj_d.shape
    h, dk = RET_HEADS, RET_QK_DIM
    c = RET_CHUNK
    dv = RET_V_DIM
    assert 2 * h * dk == h * dv
    half = dk // 2
    inv = ROPE_BASE ** (-jnp.arange(half, dtype=F32) / half)
    ang = jnp.arange(t).astype(F32)[:, None] * inv[None, :]
    cos = jnp.cos(ang)
    sin = jnp.sin(ang)
    cosf = jnp.concatenate([cos, cos], axis=-1)
    sinf = jnp.concatenate([-sin, sin], axis=-1)
    log_gamma = jnp.log(1.0 - 2.0 ** (-5.0 - jnp.arange(h, dtype=F32)))
    return pl.pallas_call(
        _retention_kernel,
        grid=(b, t // c),
        in_specs=[
            pl.BlockSpec(memory_space=pltpu.SMEM),
            pl.BlockSpec((None, c, h * dk), lambda bi, ci: (bi, ci, 0)),
            pl.BlockSpec((None, c, h * dk), lambda bi, ci: (bi, ci, 1)),
            pl.BlockSpec((None, c, h * dv), lambda bi, ci: (bi, ci, 1)),
            pl.BlockSpec((None, c, h * dv), lambda bi, ci: (bi, ci, 2)),
            pl.BlockSpec((c, dk), lambda bi, ci: (ci, 0)),
            pl.BlockSpec((c, dk), lambda bi, ci: (ci, 0)),
            pl.BlockSpec((1, h * dv), lambda bi, ci: (0, 0)),
        ],
        out_specs=pl.BlockSpec((None, c, h * dv), lambda bi, ci: (bi, ci, 0)),
        out_shape=jax.ShapeDtypeStruct((b, t, h * dv), F32),
        scratch_shapes=[pltpu.VMEM((h, dk, dv), F32)],
        compiler_params=_params(("parallel", "arbitrary")),
        name="retention",
    )(log_gamma, proj_d, proj_d, proj_d, proj_d, cosf, sinf, norm_gain.reshape(1, h * dv))


def _merge_kernel(x_ref, g_ref, wgate_ref, bias_ref, ya_ref, yb_ref, yc0_ref, yc1_ref, yc2_ref,
                  yd_ref, wb_ref, wo_ref, o_ref):
    x = x_ref[...]
    d = x.shape[1]
    u = _rms_rows(x, g_ref[...]).astype(BF16)
    ys = (ya_ref[...], yb_ref[...], yc0_ref[...] + yc1_ref[...] + yc2_ref[...], yd_ref[...])
    merged = jnp.zeros(x.shape, F32)
    for n in range(N_BRANCH):
        logits = _dot(u, wgate_ref[:, n * d:(n + 1) * d]) + bias_ref[:, n * d:(n + 1) * d]
        merged = merged + _sigmoid(logits) * _dot(ys[n].astype(BF16), wb_ref[n])
    o_ref[...] = x + _dot(merged.astype(BF16), wo_ref[...])


def _merge(x2, g, w_gate, bias, ys, w_branch, w_out, tm=256):
    n, d = x2.shape
    w = MIX_W
    row = lambda width: pl.BlockSpec((tm, width), lambda i: (i, 0))
    full = lambda shape: pl.BlockSpec(shape, lambda i: tuple(0 for _ in shape))
    return pl.pallas_call(
        _merge_kernel,
        grid=(n // tm,),
        in_specs=[row(d), full((1, d)), full((d, N_BRANCH * d)), full((1, N_BRANCH * d))]
        + [row(w)] * 6 + [full((N_BRANCH, w, d)), full((d, d))],
        out_specs=row(d),
        out_shape=jax.ShapeDtypeStruct((n, d), F32),
        compiler_params=_params(("parallel",), VMEM_LIMIT),
        name="merge_out",
    )(x2, g.reshape(1, d), w_gate.astype(BF16), bias.reshape(1, N_BRANCH * d),
      *[y.reshape(n, w) for y in ys], w_branch.astype(BF16), w_out.astype(BF16))


def _heads(t2, b, t, n):
    return t2.reshape(b, t, n, -1).transpose(0, 2, 1, 3)


def _mixers(x2, b, t, mix_norm, w_in, merge_gate_bias, swa_q_gain, swa_k_gain, swa_sinks, conv_w,
            nsa_q_gain, nsa_k_gain, cmp_pos_k, cmp_pos_v, cmp_wk1, cmp_wk2, cmp_wv1, cmp_wv2,
            ret_norm_gain, w_branch, w_out):
    d_model = x2.shape[1]
    hd = HEAD_DIM
    swa_q, swa_kv = SWA_HEADS * hd, SWA_KV_HEADS * hd
    nsa_q, nsa_kv = NSA_HEADS * hd, NSA_KV_HEADS * hd
    ret_qk, ret_v = RET_HEADS * RET_QK_DIM, RET_HEADS * RET_V_DIM
    n_gate = NSA_HEADS * 3
    o_a = 0
    o_b = o_a + swa_q + 2 * swa_kv
    o_c = o_b + 3 * MIX_W
    o_cg = o_c + nsa_q + 6 * nsa_kv
    o_d = o_cg + n_gate
    o_g = o_d + 2 * ret_qk + 2 * ret_v
    scale = hd ** -0.5

    per_g = GROUP_R * 3
    w_cg = jnp.concatenate(
        [jnp.pad(w_in[:, o_cg + gi * per_g:o_cg + (gi + 1) * per_g], ((0, 0), (0, 128 - per_g)))
         for gi in range(NSA_KV_HEADS)], axis=1)
    proj_a = _norm_matmul(x2, mix_norm, w_in[:, o_a:o_b])
    proj_b = _norm_matmul(x2, mix_norm, w_in[:, o_b:o_c])
    proj_c = _norm_matmul(x2, mix_norm, jnp.concatenate([w_in[:, o_c:o_cg], w_cg], axis=1))
    proj_d = _norm_matmul(x2, mix_norm, w_in[:, o_d:o_g])

    a_qn = _headnorm(_heads(proj_a[:, :swa_q], b, t, SWA_HEADS), swa_q_gain, scale)
    a_kn = _headnorm(_heads(proj_a[:, swa_q:swa_q + swa_kv], b, t, SWA_KV_HEADS), swa_k_gain)
    a_v = _heads(proj_a[:, swa_q + swa_kv:], b, t, SWA_KV_HEADS).astype(BF16)
    y_a = _banded(a_qn, a_kn, a_v, SWA_WINDOW, sinks=swa_sinks)

    y_b = _conv(proj_b.reshape(b, t, 3 * MIX_W), conv_w)

    def c_kv(j):
        lo = nsa_q + j * nsa_kv
        return _heads(proj_c[:, lo:lo + nsa_kv], b, t, NSA_KV_HEADS)

    c_qn = _headnorm(_heads(proj_c[:, :nsa_q], b, t, NSA_HEADS), nsa_q_gain, scale)
    gates_c = proj_c[:, nsa_q + 6 * nsa_kv:].reshape(b, t, NSA_KV_HEADS * 128)
    k_cmp, v_cmp = _compress(c_kv(0), c_kv(1), cmp_pos_k, cmp_pos_v, cmp_wk1, cmp_wk2,
                             cmp_wv1, cmp_wv2, nsa_k_gain[0])
    c_qt = _transpose_q(c_qn)
    y_cmp, sel_t = _cmp_attn(c_qt, k_cmp, v_cmp.transpose(0, 2, 1), gates_c)
    y_sel = _sel_attn(c_qt, _headnorm(c_kv(2), nsa_k_gain[1]), c_kv(3).astype(BF16), sel_t, gates_c)
    y_win = _banded(c_qn, _headnorm(c_kv(4), nsa_k_gain[2]), c_kv(5).astype(BF16), NSA_WINDOW,
                    gates=gates_c, gate_branch=2)

    y_d = _retention(proj_d.reshape(b, t, 2 * ret_qk + 2 * ret_v), ret_norm_gain)

    return _merge(x2, mix_norm, w_in[:, o_g:], merge_gate_bias,
                  (y_a, y_b, y_cmp, y_sel, y_win, y_d), w_branch, w_out)


def kernel(x, ffn1_norm, ffn1_w_gate, ffn1_w_up, ffn1_w_down, mix_norm, w_in, merge_gate_bias, swa_q_gain, swa_k_gain, swa_sinks, conv_w, nsa_q_gain, nsa_k_gain, cmp_pos_k, cmp_pos_v, cmp_wk1, cmp_wk2, cmp_wv1, cmp_wv2, ret_norm_gain, w_branch, w_out, ffn2_norm, ffn2_w_gate, ffn2_w_up, ffn2_w_down):
    b, t, d = x.shape
    x2 = x.reshape(b * t, d)
    for l in range(ffn1_norm.shape[0]):
        x2 = _ffn(x2, ffn1_norm[l], ffn1_w_gate[l], ffn1_w_up[l], ffn1_w_down[l])
        x2 = _mixers(x2, b, t, mix_norm[l], w_in[l], merge_gate_bias[l], swa_q_gain[l],
                     swa_k_gain[l], swa_sinks[l], conv_w[l], nsa_q_gain[l], nsa_k_gain[l],
                     cmp_pos_k[l], cmp_pos_v[l], cmp_wk1[l], cmp_wk2[l], cmp_wv1[l], cmp_wv2[l],
                     ret_norm_gain[l], w_branch[l], w_out[l])
        x2 = _ffn(x2, ffn2_norm[l], ffn2_w_gate[l], ffn2_w_up[l], ffn2_w_down[l])
    return x2.reshape(b, t, d)
```

```python
import functools

import jax
import jax.numpy as jnp
from jax import lax
from jax.experimental import pallas as pl
from jax.experimental.pallas import tpu as pltpu

F32 = jnp.float32
BF16 = jnp.bfloat16

HEAD_DIM = 64
Q_BLOCK = 128
MIX_W = 512
N_BRANCH = 4
SWA_HEADS = 8
SWA_KV_HEADS = 2
SWA_WINDOW = 128
NSA_HEADS = 8
NSA_KV_HEADS = 2
CMP_BLOCK = 32
CMP_STRIDE = 16
SEL_BLOCK = 64
SEL_TOPK = 16
NSA_WINDOW = 512
RET_HEADS = 4
RET_QK_DIM = 64
RET_V_DIM = 128
RET_CHUNK = 128
ROPE_BASE = 10000.0
EPS = 1e-6
GROUP_R = 4
KV_HEADS = 2
N_GATE_ROWS = 16
SEL_KEY_TILE = 512
SEL_V_ROWS = HEAD_DIM + 16
PROJ_TM = 512
VMEM_LIMIT = 52 * 1024 * 1024

NEG_INF = float("-inf")


def _params(sem, vmem=None):
    return pltpu.CompilerParams(dimension_semantics=sem, vmem_limit_bytes=vmem)


def _sigmoid(x):
    return 1.0 / (1.0 + jnp.exp(-x))


def _rms_rows(x, g):
    return x * lax.rsqrt(jnp.mean(x * x, axis=-1, keepdims=True) + EPS) * g


def _dot(a, b):
    return jnp.dot(a, b, preferred_element_type=F32)


def _dot_nt(a, b):
    return lax.dot_general(a, b, (((1,), (1,)), ((), ())), preferred_element_type=F32)


def _dot_tn(a, b):
    return lax.dot_general(a, b, (((0,), (0,)), ((), ())), preferred_element_type=F32)


def _ffn_kernel(x_ref, g_ref, wg_ref, wu_ref, wd_ref, o_ref, xn_ref, acc_ref):
    f = pl.program_id(1)

    @pl.when(f == 0)
    def _():
        xn_ref[...] = _rms_rows(x_ref[...], g_ref[...]).astype(BF16)
        acc_ref[...] = jnp.zeros_like(acc_ref)

    xn = xn_ref[...]
    a = _dot(xn, wg_ref[...])
    b = _dot(xn, wu_ref[...])
    h = (a * _sigmoid(a)) * b
    acc_ref[...] += _dot(h.astype(BF16), wd_ref[...])

    @pl.when(f == pl.num_programs(1) - 1)
    def _():
        o_ref[...] = x_ref[...] + 0.5 * acc_ref[...]


def _ffn(x2, g, wg, wu, wd, tm=512, nf=2):
    n, d = x2.shape
    dff = wg.shape[1]
    tf = dff // nf
    return pl.pallas_call(
        _ffn_kernel,
        grid=(n // tm, nf),
        in_specs=[
            pl.BlockSpec((tm, d), lambda i, f: (i, 0)),
            pl.BlockSpec((1, d), lambda i, f: (0, 0)),
            pl.BlockSpec((d, tf), lambda i, f: (0, f)),
            pl.BlockSpec((d, tf), lambda i, f: (0, f)),
            pl.BlockSpec((tf, d), lambda i, f: (f, 0)),
        ],
        out_specs=pl.BlockSpec((tm, d), lambda i, f: (i, 0)),
        out_shape=jax.ShapeDtypeStruct((n, d), F32),
        scratch_shapes=[pltpu.VMEM((tm, d), BF16), pltpu.VMEM((tm, d), F32)],
        compiler_params=_params(("parallel", "arbitrary"), VMEM_LIMIT),
        name="ffn",
    )(x2, g.reshape(1, d), wg.astype(BF16), wu.astype(BF16), wd.astype(BF16))


def _norm_matmul_kernel(x_ref, g_ref, w_ref, o_ref):
    xn = _rms_rows(x_ref[...], g_ref[...]).astype(BF16)
    o_ref[...] = _dot(xn, w_ref[...])


def _norm_matmul(x2, g, w, tm=512):
    n, d = x2.shape
    c = w.shape[1]
    return pl.pallas_call(
        _norm_matmul_kernel,
        grid=(n // tm,),
        in_specs=[
            pl.BlockSpec((tm, d), lambda i: (i, 0)),
            pl.BlockSpec((1, d), lambda i: (0, 0)),
            pl.BlockSpec((d, c), lambda i: (0, 0)),
        ],
        out_specs=pl.BlockSpec((tm, c), lambda i: (i, 0)),
        out_shape=jax.ShapeDtypeStruct((n, c), F32),
        compiler_params=_params(("parallel",), VMEM_LIMIT),
        name="in_proj",
    )(x2, g.reshape(1, d), w.astype(BF16))


def _proj_heads_kernel(x_ref, g_ref, wt_ref, hg_ref, aq_ref, ak_ref, av_ref, cq_ref, ckc_ref,
                       cvc_ref, cks_ref, cvs_ref, ckw_ref, cvw_ref, gt_ref):
    d = HEAD_DIM
    xn = _rms_rows(x_ref[...], g_ref[...]).astype(BF16)
    acc = _dot_nt(wt_ref[...], xn)
    tm = xn.shape[0]
    lane_tiles = tm // 128
    scale = d ** -0.5
    kv_w = KV_HEADS * d

    def head_norm(row0, gain_idx, mult):
        hb = acc[row0:row0 + d]
        gain = jnp.concatenate([hg_ref[gain_idx]] * lane_tiles, axis=1)
        y = hb * lax.rsqrt(jnp.mean(hb * hb, axis=0, keepdims=True) + EPS) * gain
        return y * mult if mult != 1.0 else y

    def q_heads(row0, gain_idx, out_ref, n_heads):
        for h in range(n_heads):
            out_ref[h] = head_norm(row0 + h * d, gain_idx, scale).astype(BF16)

    def k_rows(row0, gain_idx):
        return jnp.concatenate([head_norm(row0 + g * d, gain_idx, 1.0) for g in range(KV_HEADS)],
                               axis=0).T

    def v_tiles(row0, out_ref):
        for g in range(KV_HEADS):
            for u in range(lane_tiles):
                out_ref[g, u] = acc[row0 + g * d:row0 + (g + 1) * d,
                                    u * 128:(u + 1) * 128].astype(BF16)

    row = 0
    q_heads(row, 0, aq_ref, SWA_HEADS)
    row += SWA_HEADS * d
    ak_ref[...] = k_rows(row, 1).astype(BF16)
    row += kv_w
    v_tiles(row, av_ref)
    row += kv_w
    q_heads(row, 2, cq_ref, NSA_HEADS)
    row += NSA_HEADS * d
    ckc_ref[...] = acc[row:row + kv_w].T
    row += kv_w
    cvc_ref[...] = acc[row:row + kv_w].T
    row += kv_w
    cks_ref[...] = k_rows(row, 3).astype(BF16)
    row += kv_w
    for g in range(KV_HEADS):
        cvs_ref[g, 0, 0:d, :] = acc[row + g * d:row + (g + 1) * d].astype(BF16)
        cvs_ref[g, 0, d:SEL_V_ROWS, :] = jnp.ones((SEL_V_ROWS - d, tm), BF16)
    row += kv_w
    ckw_ref[...] = k_rows(row, 4).astype(BF16)
    row += kv_w
    v_tiles(row, cvw_ref)
    row += kv_w
    for g in range(KV_HEADS):
        gt_ref[g] = _sigmoid(acc[row + g * N_GATE_ROWS:row + (g + 1) * N_GATE_ROWS])


def _proj_heads(x2, b, t, g, w_t, head_gains):
    n, dm = x2.shape
    d = HEAD_DIM
    tm = PROJ_TM
    assert tm == SEL_KEY_TILE and t % tm == 0
    rows = w_t.shape[0]
    tpb = t // tm
    lt = tm // 128
    kv = KV_HEADS
    qt_spec = lambda h: pl.BlockSpec((None, h, d, tm), lambda bi, ti: (bi, 0, 0, ti))
    row_spec = pl.BlockSpec((None, tm, kv * d), lambda bi, ti: (bi, ti, 0))
    vt_spec = pl.BlockSpec((None, kv, lt, d, 128), lambda bi, ti: (bi, 0, ti, 0, 0))
    sds = jax.ShapeDtypeStruct
    return pl.pallas_call(
        _proj_heads_kernel,
        grid=(b, tpb),
        in_specs=[
            pl.BlockSpec((tm, dm), lambda bi, ti: (bi * tpb + ti, 0)),
            pl.BlockSpec((1, dm), lambda bi, ti: (0, 0)),
            pl.BlockSpec((rows, dm), lambda bi, ti: (0, 0)),
            pl.BlockSpec(head_gains.shape, lambda bi, ti: (0, 0, 0)),
        ],
        out_specs=[
            qt_spec(SWA_HEADS), row_spec, vt_spec,
            qt_spec(NSA_HEADS), row_spec, row_spec, row_spec,
            pl.BlockSpec((None, kv, 1, SEL_V_ROWS, tm), lambda bi, ti: (bi, 0, ti, 0, 0)),
            row_spec, vt_spec,
            pl.BlockSpec((None, kv, N_GATE_ROWS, tm), lambda bi, ti: (bi, 0, 0, ti)),
        ],
        out_shape=[
            sds((b, SWA_HEADS, d, t), BF16), sds((b, t, kv * d), BF16),
            sds((b, kv, t // 128, d, 128), BF16),
            sds((b, NSA_HEADS, d, t), BF16), sds((b, t, kv * d), F32), sds((b, t, kv * d), F32),
            sds((b, t, kv * d), BF16), sds((b, kv, tpb, SEL_V_ROWS, tm), BF16),
            sds((b, t, kv * d), BF16), sds((b, kv, t // 128, d, 128), BF16),
            sds((b, kv, N_GATE_ROWS, t), F32),
        ],
        compiler_params=_params(("parallel", "parallel"), VMEM_LIMIT),
        name="proj_heads",
    )(x2, g.reshape(1, dm), w_t, head_gains)


def _padded_q(qt_ref, g):
    q4 = jnp.concatenate([qt_ref[r] for r in range(GROUP_R)], axis=1)
    z = jnp.zeros_like(q4)
    return jnp.where(g == 0, jnp.concatenate([q4, z], axis=0), jnp.concatenate([z, q4], axis=0))


def _gate_untranspose(o_t, gate_ref, branch):
    outs = []
    for r in range(GROUP_R):
        blk = o_t[:, r * Q_BLOCK:(r + 1) * Q_BLOCK]
        if gate_ref is not None:
            c = r * 3 + branch
            blk = blk * gate_ref[c:c + 1, :]
        outs.append(blk.T)
    return jnp.concatenate(outs, axis=1)


def _banded_kernel(*refs, window, has_sink, gate_branch):
    refs = list(refs)
    sink_ref = refs.pop(0) if has_sink else None
    qt_ref, k_ref, vt_ref = refs[:3]
    gate_ref = refs[3] if gate_branch is not None else None
    o_ref = refs[-1]
    g = pl.program_id(1)
    i = pl.program_id(2)
    rq = GROUP_R * Q_BLOCK
    n_sub = window // Q_BLOCK + 1
    span = n_sub * Q_BLOCK
    start = pl.multiple_of(jnp.maximum(i * Q_BLOCK - window, 0), Q_BLOCK)
    st = _dot(k_ref[pl.ds(start, span), :], _padded_q(qt_ref, g))
    kpos = start + lax.broadcasted_iota(jnp.int32, (span, rq), 0)
    lane = lax.broadcasted_iota(jnp.int32, (span, rq), 1)
    diff = (i * Q_BLOCK + (lane & (Q_BLOCK - 1))) - kpos
    st = jnp.where((diff >= 0) & (diff < window), st, NEG_INF)
    m = jnp.max(st, axis=0, keepdims=True)
    if has_sink:
        lrow = lax.broadcasted_iota(jnp.int32, (1, rq), 1)
        sink = jnp.zeros((1, rq), F32)
        for r in range(GROUP_R):
            sink = jnp.where((lrow >= r * Q_BLOCK) & (lrow < (r + 1) * Q_BLOCK),
                             sink_ref[g * GROUP_R + r], sink)
        m = jnp.maximum(m, sink)
    m = jnp.where(m == NEG_INF, 0.0, m)
    p = jnp.exp(st - m)
    denom = jnp.sum(p, axis=0, keepdims=True)
    if has_sink:
        denom = denom + jnp.exp(sink - m)
    pb = p.astype(BF16)
    u0 = start // Q_BLOCK
    o_t = _dot(vt_ref[u0], pb[0:Q_BLOCK])
    for u in range(1, n_sub):
        o_t = o_t + _dot(vt_ref[u0 + u], pb[u * Q_BLOCK:(u + 1) * Q_BLOCK])
    o_t = o_t / jnp.maximum(denom, 1e-30)
    o_ref[...] = _gate_untranspose(o_t, gate_ref, gate_branch)


def _banded(qt, k_rows, vt, window, sinks=None, gates=None, gate_branch=None):
    b, h, d, t = qt.shape
    g = h // GROUP_R
    in_specs = []
    args = []
    if sinks is not None:
        in_specs.append(pl.BlockSpec(memory_space=pltpu.SMEM))
        args.append(sinks.astype(F32))
    in_specs += [
        pl.BlockSpec((None, GROUP_R, d, Q_BLOCK), lambda bi, gi, i: (bi, gi, 0, i)),
        pl.BlockSpec((None, t, KV_HEADS * d), lambda bi, gi, i: (bi, 0, 0)),
        pl.BlockSpec((None, None, t // Q_BLOCK, d, Q_BLOCK), lambda bi, gi, i: (bi, gi, 0, 0, 0)),
    ]
    args += [qt, k_rows, vt]
    if gates is not None:
        in_specs.append(pl.BlockSpec((None, None, N_GATE_ROWS, Q_BLOCK),
                                     lambda bi, gi, i: (bi, gi, 0, i)))
        args.append(gates)
    return pl.pallas_call(
        functools.partial(_banded_kernel, window=window, has_sink=sinks is not None,
                          gate_branch=gate_branch if gates is not None else None),
        grid=(b, g, t // Q_BLOCK),
        in_specs=in_specs,
        out_specs=pl.BlockSpec((None, Q_BLOCK, GROUP_R * d), lambda bi, gi, i: (bi, i, gi)),
        out_shape=jax.ShapeDtypeStruct((b, t, h * d), F32),
        compiler_params=_params(("parallel", "parallel", "arbitrary"), VMEM_LIMIT),
        name="banded_attn_w%d" % window,
    )(*args)


def _conv_kernel(x_ref, b_ref, c_ref, xp_ref, cp_ref, w_ref, o_ref):
    ti = pl.program_id(1)
    z = c_ref[...] * x_ref[...]
    zp = jnp.where(ti > 0, cp_ref[...] * xp_ref[...], 0.0)
    row = lax.broadcasted_iota(jnp.int32, z.shape, 0)
    z1 = jnp.where(row == 0, zp[7:8, :], pltpu.roll(z, 1, 0))
    z2 = pltpu.roll(z, 2, 0)
    z2 = jnp.where(row == 0, zp[6:7, :], jnp.where(row == 1, zp[7:8, :], z2))
    w = w_ref[...]
    o_ref[...] = b_ref[...] * (w[0:1, :] * z2 + w[1:2, :] * z1 + w[2:3, :] * z)


def _conv(proj_b, conv_w, tt=512):
    b, t, _ = proj_b.shape
    w = MIX_W
    hb = tt // 8
    prev = lambda bi, ti: (bi, jnp.maximum(ti * hb - 1, 0), 0)
    prev_c = lambda bi, ti: (bi, jnp.maximum(ti * hb - 1, 0), 2)
    return pl.pallas_call(
        _conv_kernel,
        grid=(b, t // tt),
        in_specs=[
            pl.BlockSpec((None, tt, w), lambda bi, ti: (bi, ti, 0)),
            pl.BlockSpec((None, tt, w), lambda bi, ti: (bi, ti, 1)),
            pl.BlockSpec((None, tt, w), lambda bi, ti: (bi, ti, 2)),
            pl.BlockSpec((None, 8, w), prev),
            pl.BlockSpec((None, 8, w), prev_c),
            pl.BlockSpec((8, w), lambda bi, ti: (0, 0)),
        ],
        out_specs=pl.BlockSpec((None, tt, w), lambda bi, ti: (bi, ti, 0)),
        out_shape=jax.ShapeDtypeStruct((b, t, w), F32),
        compiler_params=_params(("parallel", "parallel")),
        name="short_conv",
    )(proj_b, proj_b, proj_b, proj_b, proj_b,
      jnp.pad(conv_w.reshape(conv_w.shape[0], w).astype(F32), ((0, 8 - conv_w.shape[0]), (0, 0))))


def _gelu_tanh(x):
    return x * (0.5 * (1.0 + jnp.tanh(0.7978845608028654 * (x + 0.044715 * (x * x * x)))))


def _compress_kernel(tk_ref, tv_ref, pek_ref, pev_ref, wk1_ref, wk2_ref, wv1_ref, wv2_ref,
                     kg_ref, ko_ref, vo_ref):
    nrow = tk_ref.shape[0]

    def mlp(a, pe_ref, w1_ref, w2_ref):
        a0 = (a + pe_ref[0:1, :]).astype(BF16)
        a1 = (a + pe_ref[1:2, :]).astype(BF16)
        p1 = _dot(a0, w1_ref[0])
        p2 = _dot(a1, w1_ref[1])
        hdn = p1 + pltpu.roll(p2, nrow - 1, 0)
        return _dot(_gelu_tanh(hdn).astype(BF16), w2_ref[...])

    kc = mlp(tk_ref[...], pek_ref, wk1_ref, wk2_ref)
    ko_ref[...] = _rms_rows(kc, kg_ref[...]).astype(BF16)
    vo_ref[...] = mlp(tv_ref[...], pev_ref, wv1_ref, wv2_ref).astype(BF16)


def _compress(kc_rows, vc_rows, pos_k, pos_v, wk1, wk2, wv1, wv2, k_gain):
    b, t, kvd = kc_rows.shape
    kv = KV_HEADS
    d = kvd // kv
    nrow = t // CMP_STRIDE
    wide = CMP_STRIDE * kvd
    hid = wk1.shape[1]

    def expand_w1(w1):
        w = w1.reshape(2, CMP_STRIDE, 1, d, hid)
        per_head = []
        for g in range(kv):
            pads = [jnp.zeros_like(w)] * kv
            pads[g] = w
            per_head.append(jnp.concatenate(pads, axis=2).reshape(2, wide, hid))
        return jnp.stack(per_head).astype(BF16)

    def expand_pe(pe):
        return jnp.broadcast_to(pe.reshape(2, CMP_STRIDE, 1, d), (2, CMP_STRIDE, kv, d)).reshape(2, wide)

    tok = pl.BlockSpec((None, nrow, wide), lambda bi, gi: (bi, 0, 0))
    full = lambda shape: pl.BlockSpec(shape, lambda bi, gi: tuple(0 for _ in shape))
    w1_spec = pl.BlockSpec((None, 2, wide, hid), lambda bi, gi: (gi, 0, 0, 0))
    out = pl.BlockSpec((None, nrow, d), lambda bi, gi: (bi * kv + gi, 0, 0))
    return pl.pallas_call(
        _compress_kernel,
        grid=(b, kv),
        in_specs=[tok, tok, full((2, wide)), full((2, wide)), w1_spec, full((hid, d)),
                  w1_spec, full((hid, d)), full((1, d))],
        out_specs=[out, out],
        out_shape=[jax.ShapeDtypeStruct((b * kv, nrow, d), BF16),
                   jax.ShapeDtypeStruct((b * kv, nrow, d), BF16)],
        compiler_params=_params(("parallel", "arbitrary"), VMEM_LIMIT),
        name="nsa_compress",
    )(kc_rows.reshape(b, nrow, wide), vc_rows.reshape(b, nrow, wide), expand_pe(pos_k),
      expand_pe(pos_v), expand_w1(wk1), wk2.astype(BF16), expand_w1(wv1), wv2.astype(BF16),
      k_gain.reshape(1, d))


def _cmp_attn_kernel(qt_ref, kc_ref, vct_ref, gate_ref, o_ref, selt_ref, *, sel_k):
    i = pl.program_id(2)
    rq = GROUP_R * Q_BLOCK
    nc = kc_ref.shape[0]
    ns = selt_ref.shape[0]
    qt = jnp.concatenate([qt_ref[r] for r in range(GROUP_R)], axis=1)
    st = _dot(kc_ref[...], qt)
    n = lax.broadcasted_iota(jnp.int32, (nc, rq), 0)
    lane = lax.broadcasted_iota(jnp.int32, (nc, rq), 1)
    qpos = i * Q_BLOCK + (lane & (Q_BLOCK - 1))
    st = jnp.where(n * CMP_STRIDE + (CMP_BLOCK - 1) <= qpos, st, NEG_INF)
    m = jnp.max(st, axis=0, keepdims=True)
    m = jnp.where(m == NEG_INF, 0.0, m)
    p = jnp.exp(st - m)
    denom = jnp.sum(p, axis=0, keepdims=True)
    pb = (p / jnp.maximum(denom, 1e-30)).astype(BF16)
    o_t = _dot(vct_ref[...], pb)
    o_ref[...] = _gate_untranspose(o_t, gate_ref, 0)

    ss = lax.broadcasted_iota(jnp.int32, (ns, nc), 0) * SEL_BLOCK
    nn = lax.broadcasted_iota(jnp.int32, (ns, nc), 1) * CMP_STRIDE
    overlap_t = jnp.where((nn < ss + SEL_BLOCK) & (nn + (CMP_BLOCK - 1) >= ss), 1.0, 0.0).astype(BF16)
    imp_r = _dot(overlap_t, pb)
    imp = imp_r[:, 0:Q_BLOCK]
    for r in range(1, GROUP_R):
        imp = imp + imp_r[:, r * Q_BLOCK:(r + 1) * Q_BLOCK]
    blk = lax.broadcasted_iota(jnp.int32, (ns, Q_BLOCK), 0)
    qp = i * Q_BLOCK + lax.broadcasted_iota(jnp.int32, (ns, Q_BLOCK), 1)
    cur = qp // SEL_BLOCK
    forced = (blk == 0) | (blk == cur) | (blk == cur - 1)
    imp = jnp.where(forced, jnp.inf, imp)
    imp = jnp.where(blk <= cur, imp, NEG_INF)
    blkf = blk.astype(F32)
    sel = jnp.zeros((ns, Q_BLOCK), F32)
    for _ in range(sel_k):
        mx = jnp.max(imp, axis=0, keepdims=True)
        first = jnp.min(jnp.where(imp == mx, blkf, float(ns)), axis=0, keepdims=True)
        hit = blkf == first
        sel = jnp.where(hit & (mx > NEG_INF), 1.0, sel)
        imp = jnp.where(hit, NEG_INF, imp)
    selt_ref[...] = sel


def _cmp_attn(qt, k_cmp, v_cmp_t, gates):
    b, h, d, t = qt.shape
    g = h // GROUP_R
    nc = k_cmp.shape[1]
    ns = t // SEL_BLOCK
    return pl.pallas_call(
        functools.partial(_cmp_attn_kernel, sel_k=min(SEL_TOPK, ns)),
        grid=(b, g, t // Q_BLOCK),
        in_specs=[
            pl.BlockSpec((None, GROUP_R, d, Q_BLOCK), lambda bi, gi, i: (bi, gi, 0, i)),
            pl.BlockSpec((None, nc, d), lambda bi, gi, i: (bi * KV_HEADS + gi, 0, 0)),
            pl.BlockSpec((None, d, nc), lambda bi, gi, i: (bi * KV_HEADS + gi, 0, 0)),
            pl.BlockSpec((None, None, N_GATE_ROWS, Q_BLOCK), lambda bi, gi, i: (bi, gi, 0, i)),
        ],
        out_specs=[
            pl.BlockSpec((None, Q_BLOCK, GROUP_R * d), lambda bi, gi, i: (bi, i, gi)),
            pl.BlockSpec((None, None, ns, Q_BLOCK), lambda bi, gi, i: (bi, gi, 0, i)),
        ],
        out_shape=[jax.ShapeDtypeStruct((b, t, h * d), F32),
                   jax.ShapeDtypeStruct((b, g, ns, t), F32)],
        compiler_params=_params(("parallel", "parallel", "arbitrary"), VMEM_LIMIT),
        name="nsa_cmp_topk",
    )(qt, k_cmp, v_cmp_t, gates)


def _sel_attn_kernel(qt_ref, ks_ref, vst_ref, selt_ref, gate_ref, o_ref, sa_ref, sb_ref, m_ref,
                     acc_ref):
    g = pl.program_id(1)
    i = pl.program_id(2)
    tk = SEL_KEY_TILE
    bpt = tk // SEL_BLOCK
    spt = tk // Q_BLOCK
    d = HEAD_DIM
    m_ref[...] = jnp.full(m_ref.shape, NEG_INF, F32)
    acc_ref[...] = jnp.zeros_like(acc_ref)
    n_tiles = (i * Q_BLOCK + Q_BLOCK + tk - 1) // tk
    qpad = _padded_q(qt_ref, g)
    tri = jnp.where(lax.broadcasted_iota(jnp.int32, (Q_BLOCK, Q_BLOCK), 0)
                    <= lax.broadcasted_iota(jnp.int32, (Q_BLOCK, Q_BLOCK), 1), 0.0, NEG_INF)

    def scores(j, s_ref):
        k0 = pl.multiple_of(j * tk, tk)
        st = _dot(ks_ref[pl.ds(k0, tk), :], qpad)
        sel_rows = selt_ref[pl.ds(pl.multiple_of(j * bpt, bpt), bpt), :]
        bias_rows = jnp.where(sel_rows > 0.5, 0.0, NEG_INF)
        parts = []
        for u in range(spt):
            sub = jnp.concatenate(
                [jnp.broadcast_to(bias_rows[s:s + 1, :], (SEL_BLOCK, Q_BLOCK))
                 for s in range(u * Q_BLOCK // SEL_BLOCK, (u + 1) * Q_BLOCK // SEL_BLOCK)], axis=0)
            parts.append(sub + jnp.where(j * spt + u == i, tri, 0.0))
        bias = jnp.concatenate(parts, axis=0)
        s_ref[...] = st + jnp.concatenate([bias] * GROUP_R, axis=1)

    def consume(j, s_ref):
        st = s_ref[...]
        m_prev = m_ref[...]
        m_new = jnp.maximum(m_prev, jnp.max(st, axis=0, keepdims=True))
        m_safe = jnp.where(m_new == NEG_INF, 0.0, m_new)
        alpha = jnp.exp(m_prev - m_safe)
        p = jnp.exp((st - m_safe).astype(BF16))
        acc_ref[...] = alpha * acc_ref[...] + _dot(vst_ref[j], p)
        m_ref[...] = m_new

    scores(0, sa_ref)
    n_pairs = (n_tiles - 1) // 2

    def step(jj, carry):
        j = 2 * jj
        scores(j + 1, sb_ref)
        consume(j, sa_ref)
        scores(j + 2, sa_ref)
        consume(j + 1, sb_ref)
        return carry

    lax.fori_loop(0, n_pairs, step, 0)
    j_rem = 2 * n_pairs

    @pl.when(n_tiles - j_rem == 2)
    def _():
        scores(j_rem + 1, sb_ref)

    consume(j_rem, sa_ref)

    @pl.when(n_tiles - j_rem == 2)
    def _():
        consume(j_rem + 1, sb_ref)

    o_t = acc_ref[0:d, :] / jnp.maximum(acc_ref[d:d + 1, :], 1e-30)
    o_ref[...] = _gate_untranspose(o_t, gate_ref, 1)


def _sel_attn(qt, ks_rows, vs_t, sel_t, gates):
    b, h, d, t = qt.shape
    g = h // GROUP_R
    ns = sel_t.shape[2]
    tk = SEL_KEY_TILE
    rq = GROUP_R * Q_BLOCK
    return pl.pallas_call(
        _sel_attn_kernel,
        grid=(b, g, t // Q_BLOCK),
        in_specs=[
            pl.BlockSpec((None, GROUP_R, d, Q_BLOCK), lambda bi, gi, i: (bi, gi, 0, i)),
            pl.BlockSpec((None, t, KV_HEADS * d), lambda bi, gi, i: (bi, 0, 0)),
            pl.BlockSpec((None, None, t // tk, SEL_V_ROWS, tk), lambda bi, gi, i: (bi, gi, 0, 0, 0)),
            pl.BlockSpec((None, None, ns, Q_BLOCK), lambda bi, gi, i: (bi, gi, 0, i)),
            pl.BlockSpec((None, None, N_GATE_ROWS, Q_BLOCK), lambda bi, gi, i: (bi, gi, 0, i)),
        ],
        out_specs=pl.BlockSpec((None, Q_BLOCK, GROUP_R * d), lambda bi, gi, i: (bi, i, gi)),
        out_shape=jax.ShapeDtypeStruct((b, t, h * d), F32),
        scratch_shapes=[pltpu.VMEM((tk, rq), F32), pltpu.VMEM((tk, rq), F32),
                        pltpu.VMEM((1, rq), F32), pltpu.VMEM((SEL_V_ROWS, rq), F32)],
        compiler_params=_params(("parallel", "parallel", "arbitrary"), VMEM_LIMIT),
        name="nsa_selected",
    )(qt, ks_rows, vs_t, sel_t, gates)


def _retention_kernel(lg_ref, q_ref, k_ref, v_ref, gt_ref, cos_ref, sin_ref, ng_ref, o_ref, r_ref):
    ci = pl.program_id(1)
    c = RET_CHUNK
    dk, dv = RET_QK_DIM, RET_V_DIM
    half = dk // 2

    @pl.when(ci == 0)
    def _():
        r_ref[...] = jnp.zeros_like(r_ref)

    cosf = cos_ref[...]
    sinf = sin_ref[...]

    def rot(x):
        return x * cosf + jnp.concatenate([x[:, half:], x[:, :half]], axis=1) * sinf

    ii = lax.broadcasted_iota(jnp.int32, (c, c), 0)
    jj = lax.broadcasted_iota(jnp.int32, (c, c), 1)
    d = (ii - jj).astype(F32)
    jcol = lax.broadcasted_iota(jnp.int32, (c, 1), 0).astype(F32)
    for hh in range(RET_HEADS):
        lg = lg_ref[hh]
        q = rot(q_ref[:, hh * dk:(hh + 1) * dk])
        k = rot(k_ref[:, hh * dk:(hh + 1) * dk]) * (dk ** -0.5)
        vb = v_ref[:, hh * dv:(hh + 1) * dv].astype(BF16)
        dmask = jnp.where(d >= 0, jnp.exp(d * lg), 0.0)
        att = _dot_nt(q.astype(BF16), k.astype(BF16)) * dmask
        o = _dot(att.astype(BF16), vb)
        xi = jnp.exp((jcol + 1.0) * lg)
        zeta = jnp.exp((c - 1.0 - jcol) * lg)
        r_prev = r_ref[hh]
        o = o + _dot((q * xi).astype(BF16), r_prev.astype(BF16))
        s_chunk = _dot_tn((k * zeta).astype(BF16), vb)
        decay = jnp.exp(jnp.zeros((1, dv), F32) + c * lg)
        r_ref[hh] = r_prev * decay + s_chunk
        mu = jnp.mean(o, axis=-1, keepdims=True)
        var = jnp.mean(jnp.square(o - mu), axis=-1, keepdims=True)
        on = (o - mu) * lax.rsqrt(var + EPS)
        gt = gt_ref[:, hh * dv:(hh + 1) * dv]
        o_ref[:, hh * dv:(hh + 1) * dv] = (gt * _sigmoid(gt)) * (on * ng_ref[:, hh * dv:(hh + 1) * dv])


def _retention(proj_d, norm_gain):
    b, t, _ = proj_d.shape
    h, dk = RET_HEADS, RET_QK_DIM
    c = RET_CHUNK
    dv = RET_V_DIM
    assert 2 * h * dk == h * dv
    half = dk // 2
    inv = ROPE_BASE ** (-jnp.arange(half, dtype=F32) / half)
    ang = jnp.arange(t).astype(F32)[:, None] * inv[None, :]
    cos = jnp.cos(ang)
    sin = jnp.sin(ang)
    cosf = jnp.concatenate([cos, cos], axis=-1)
    sinf = jnp.concatenate([-sin, sin], axis=-1)
    log_gamma = jnp.log(1.0 - 2.0 ** (-5.0 - jnp.arange(h, dtype=F32)))
    return pl.pallas_call(
        _retention_kernel,
        grid=(b, t // c),
        in_specs=[
            pl.BlockSpec(memory_space=pltpu.SMEM),
            pl.BlockSpec((None, c, h * dk), lambda bi, ci: (bi, ci, 0)),
            pl.BlockSpec((None, c, h * dk), lambda bi, ci: (bi, ci, 1)),
            pl.BlockSpec((None, c, h * dv), lambda bi, ci: (bi, ci, 1)),
            pl.BlockSpec((None, c, h * dv), lambda bi, ci: (bi, ci, 2)),
            pl.BlockSpec((c, dk), lambda bi, ci: (ci, 0)),
            pl.BlockSpec((c, dk), lambda bi, ci: (ci, 0)),
            pl.BlockSpec((1, h * dv), lambda bi, ci: (0, 0)),
        ],
        out_specs=pl.BlockSpec((None, c, h * dv), lambda bi, ci: (bi, ci, 0)),
        out_shape=jax.ShapeDtypeStruct((b, t, h * dv), F32),
        scratch_shapes=[pltpu.VMEM((h, dk, dv), F32)],
        compiler_params=_params(("parallel", "arbitrary")),
        name="retention",
    )(log_gamma, proj_d, proj_d, proj_d, proj_d, cosf, sinf, norm_gain.reshape(1, h * dv))


def _merge_kernel(x_ref, g_ref, wgate_ref, bias_ref, ya_ref, yb_ref, yc0_ref, yc1_ref, yc2_ref,
                  yd_ref, wb_ref, wo_ref, o_ref):
    x = x_ref[...]
    d = x.shape[1]
    u = _rms_rows(x, g_ref[...]).astype(BF16)
    ys = (ya_ref[...], yb_ref[...], yc0_ref[...] + yc1_ref[...] + yc2_ref[...], yd_ref[...])
    merged = jnp.zeros(x.shape, F32)
    for n in range(N_BRANCH):
        logits = _dot(u, wgate_ref[:, n * d:(n + 1) * d]) + bias_ref[:, n * d:(n + 1) * d]
        merged = merged + _sigmoid(logits) * _dot(ys[n].astype(BF16), wb_ref[n])
    o_ref[...] = x + _dot(merged.astype(BF16), wo_ref[...])


def _merge(x2, g, w_gate, bias, ys, w_branch, w_out, tm=256):
    n, d = x2.shape
    w = MIX_W
    row = lambda width: pl.BlockSpec((tm, width), lambda i: (i, 0))
    full = lambda shape: pl.BlockSpec(shape, lambda i: tuple(0 for _ in shape))
    return pl.pallas_call(
        _merge_kernel,
        grid=(n // tm,),
        in_specs=[row(d), full((1, d)), full((d, N_BRANCH * d)), full((1, N_BRANCH * d))]
        + [row(w)] * 6 + [full((N_BRANCH, w, d)), full((d, d))],
        out_specs=row(d),
        out_shape=jax.ShapeDtypeStruct((n, d), F32),
        compiler_params=_params(("parallel",), VMEM_LIMIT),
        name="merge_out",
    )(x2, g.reshape(1, d), w_gate.astype(BF16), bias.reshape(1, N_BRANCH * d),
      *[y.reshape(n, w) for y in ys], w_branch.astype(BF16), w_out.astype(BF16))


def _mixers(x2, b, t, mix_norm, w_in, merge_gate_bias, swa_q_gain, swa_k_gain, swa_sinks, conv_w,
            nsa_q_gain, nsa_k_gain, cmp_pos_k, cmp_pos_v, cmp_wk1, cmp_wk2, cmp_wv1, cmp_wv2,
            ret_norm_gain, w_branch, w_out):
    hd = HEAD_DIM
    swa_q, swa_kv = SWA_HEADS * hd, SWA_KV_HEADS * hd
    nsa_q, nsa_kv = NSA_HEADS * hd, NSA_KV_HEADS * hd
    ret_qk, ret_v = RET_HEADS * RET_QK_DIM, RET_HEADS * RET_V_DIM
    n_gate = NSA_HEADS * 3
    o_a = 0
    o_b = o_a + swa_q + 2 * swa_kv
    o_c = o_b + 3 * MIX_W
    o_cg = o_c + nsa_q + 6 * nsa_kv
    o_d = o_cg + n_gate
    o_g = o_d + 2 * ret_qk + 2 * ret_v

    per_g = GROUP_R * 3
    w_gates = [jnp.pad(w_in[:, o_cg + gi * per_g:o_cg + (gi + 1) * per_g],
                       ((0, 0), (0, N_GATE_ROWS - per_g))) for gi in range(NSA_KV_HEADS)]
    w_t = jnp.concatenate([w_in[:, o_a:o_b], w_in[:, o_c:o_cg]] + w_gates, axis=1).T.astype(BF16)
    head_gains = jnp.stack([swa_q_gain, swa_k_gain, nsa_q_gain, nsa_k_gain[1], nsa_k_gain[2]])
    head_gains = jnp.broadcast_to(head_gains.astype(F32)[:, :, None], (5, hd, 128))
    (a_qt, a_k, a_vt, c_qt, c_kc, c_vc, c_ks, c_vst, c_kw, c_vwt, gates_c) = _proj_heads(
        x2, b, t, mix_norm, w_t, head_gains)
    proj_b = _norm_matmul(x2, mix_norm, w_in[:, o_b:o_c])
    proj_d = _norm_matmul(x2, mix_norm, w_in[:, o_d:o_g])

    y_a = _banded(a_qt, a_k, a_vt, SWA_WINDOW, sinks=swa_sinks)

    y_b = _conv(proj_b.reshape(b, t, 3 * MIX_W), conv_w)

    k_cmp, v_cmp = _compress(c_kc, c_vc, cmp_pos_k, cmp_pos_v, cmp_wk1, cmp_wk2, cmp_wv1, cmp_wv2,
                             nsa_k_gain[0])
    y_cmp, sel_t = _cmp_attn(c_qt, k_cmp, v_cmp.transpose(0, 2, 1), gates_c)
    y_sel = _sel_attn(c_qt, c_ks, c_vst, sel_t, gates_c)
    y_win = _banded(c_qt, c_kw, c_vwt, NSA_WINDOW, gates=gates_c, gate_branch=2)

    y_d = _retention(proj_d.reshape(b, t, 2 * ret_qk + 2 * ret_v), ret_norm_gain)

    return _merge(x2, mix_norm, w_in[:, o_g:], merge_gate_bias,
                  (y_a, y_b, y_cmp, y_sel, y_win, y_d), w_branch, w_out)


def kernel(x, ffn1_norm, ffn1_w_gate, ffn1_w_up, ffn1_w_down, mix_norm, w_in, merge_gate_bias, swa_q_gain, swa_k_gain, swa_sinks, conv_w, nsa_q_gain, nsa_k_gain, cmp_pos_k, cmp_pos_v, cmp_wk1, cmp_wk2, cmp_wv1, cmp_wv2, ret_norm_gain, w_branch, w_out, ffn2_norm, ffn2_w_gate, ffn2_w_up, ffn2_w_down):
    b, t, d = x.shape
    x2 = x.reshape(b * t, d)
    for l in range(ffn1_norm.shape[0]):
        x2 = _ffn(x2, ffn1_norm[l], ffn1_w_gate[l], ffn1_w_up[l], ffn1_w_down[l])
        x2 = _mixers(x2, b, t, mix_norm[l], w_in[l], merge_gate_bias[l], swa_q_gain[l],
                     swa_k_gain[l], swa_sinks[l], conv_w[l], nsa_q_gain[l], nsa_k_gain[l],
                     cmp_pos_k[l], cmp_pos_v[l], cmp_wk1[l], cmp_wk2[l], cmp_wv1[l], cmp_wv2[l],
                     ret_norm_gain[l], w_branch[l], w_out[l])
        x2 = _ffn(x2, ffn2_norm[l], ffn2_w_gate[l], ffn2_w_up[l], ffn2_w_down[l])
    return x2.reshape(b, t, d)
```

```python
import functools

import jax
import jax.numpy as jnp
from jax import lax
from jax.experimental import pallas as pl
from jax.experimental.pallas import tpu as pltpu

F32 = jnp.float32
BF16 = jnp.bfloat16

HEAD_DIM = 64
Q_BLOCK = 128
MIX_W = 512
N_BRANCH = 4
SWA_HEADS = 8
SWA_KV_HEADS = 2
SWA_WINDOW = 128
NSA_HEADS = 8
NSA_KV_HEADS = 2
CMP_BLOCK = 32
CMP_STRIDE = 16
SEL_BLOCK = 64
SEL_TOPK = 16
NSA_WINDOW = 512
RET_HEADS = 4
RET_QK_DIM = 64
RET_V_DIM = 128
RET_CHUNK = 128
ROPE_BASE = 10000.0
EPS = 1e-6
GROUP_R = 4
KV_HEADS = 2
N_GATE_ROWS = 16
SEL_KEY_TILE = 512
SEL_V_ROWS = HEAD_DIM + 16
PROJ_TM = 512
BANDED_QB = 4
CMP_QB = 2
VMEM_LIMIT = 52 * 1024 * 1024

NEG_INF = float("-inf")


def _params(sem, vmem=None):
    return pltpu.CompilerParams(dimension_semantics=sem, vmem_limit_bytes=vmem)


def _sigmoid(x):
    return 1.0 / (1.0 + jnp.exp(-x))


def _rms_rows(x, g):
    return x * lax.rsqrt(jnp.mean(x * x, axis=-1, keepdims=True) + EPS) * g


def _dot(a, b):
    return jnp.dot(a, b, preferred_element_type=F32)


def _dot_nt(a, b):
    return lax.dot_general(a, b, (((1,), (1,)), ((), ())), preferred_element_type=F32)


def _dot_tn(a, b):
    return lax.dot_general(a, b, (((0,), (0,)), ((), ())), preferred_element_type=F32)


def _ffn_kernel(x_ref, g_ref, wg_ref, wu_ref, wd_ref, o_ref, xn_ref, acc_ref):
    f = pl.program_id(1)

    @pl.when(f == 0)
    def _():
        xn_ref[...] = _rms_rows(x_ref[...], g_ref[...]).astype(BF16)
        acc_ref[...] = jnp.zeros_like(acc_ref)

    xn = xn_ref[...]
    a = _dot(xn, wg_ref[...])
    b = _dot(xn, wu_ref[...])
    h = (a * _sigmoid(a)) * b
    acc_ref[...] += _dot(h.astype(BF16), wd_ref[...])

    @pl.when(f == pl.num_programs(1) - 1)
    def _():
        o_ref[...] = x_ref[...] + 0.5 * acc_ref[...]


def _ffn(x2, g, wg, wu, wd, tm=512, nf=2):
    n, d = x2.shape
    dff = wg.shape[1]
    tf = dff // nf
    return pl.pallas_call(
        _ffn_kernel,
        grid=(n // tm, nf),
        in_specs=[
            pl.BlockSpec((tm, d), lambda i, f: (i, 0)),
            pl.BlockSpec((1, d), lambda i, f: (0, 0)),
            pl.BlockSpec((d, tf), lambda i, f: (0, f)),
            pl.BlockSpec((d, tf), lambda i, f: (0, f)),
            pl.BlockSpec((tf, d), lambda i, f: (f, 0)),
        ],
        out_specs=pl.BlockSpec((tm, d), lambda i, f: (i, 0)),
        out_shape=jax.ShapeDtypeStruct((n, d), F32),
        scratch_shapes=[pltpu.VMEM((tm, d), BF16), pltpu.VMEM((tm, d), F32)],
        compiler_params=_params(("parallel", "arbitrary"), VMEM_LIMIT),
        name="ffn",
    )(x2, g.reshape(1, d), wg.astype(BF16), wu.astype(BF16), wd.astype(BF16))


def _norm_matmul_kernel(x_ref, g_ref, w_ref, o_ref):
    xn = _rms_rows(x_ref[...], g_ref[...]).astype(BF16)
    o_ref[...] = _dot(xn, w_ref[...])


def _norm_matmul(x2, g, w, tm=512):
    n, d = x2.shape
    c = w.shape[1]
    return pl.pallas_call(
        _norm_matmul_kernel,
        grid=(n // tm,),
        in_specs=[
            pl.BlockSpec((tm, d), lambda i: (i, 0)),
            pl.BlockSpec((1, d), lambda i: (0, 0)),
            pl.BlockSpec((d, c), lambda i: (0, 0)),
        ],
        out_specs=pl.BlockSpec((tm, c), lambda i: (i, 0)),
        out_shape=jax.ShapeDtypeStruct((n, c), F32),
        compiler_params=_params(("parallel",), VMEM_LIMIT),
        name="in_proj",
    )(x2, g.reshape(1, d), w.astype(BF16))


def _proj_heads_kernel(x_ref, g_ref, wt_ref, hg_ref, aq_ref, ak_ref, av_ref, cq_ref, ckc_ref,
                       cvc_ref, cks_ref, cvs_ref, ckw_ref, cvw_ref, gt_ref):
    d = HEAD_DIM
    xn = _rms_rows(x_ref[...], g_ref[...]).astype(BF16)
    acc = _dot_nt(wt_ref[...], xn)
    tm = xn.shape[0]
    lane_tiles = tm // 128
    scale = d ** -0.5
    kv_w = KV_HEADS * d

    def head_norm(row0, gain_idx, mult):
        hb = acc[row0:row0 + d]
        gain = jnp.concatenate([hg_ref[gain_idx]] * lane_tiles, axis=1)
        y = hb * lax.rsqrt(jnp.mean(hb * hb, axis=0, keepdims=True) + EPS) * gain
        return y * mult if mult != 1.0 else y

    def q_heads(row0, gain_idx, out_ref, n_heads):
        for h in range(n_heads):
            out_ref[h] = head_norm(row0 + h * d, gain_idx, scale).astype(BF16)

    def k_rows(row0, gain_idx):
        return jnp.concatenate([head_norm(row0 + g * d, gain_idx, 1.0) for g in range(KV_HEADS)],
                               axis=0).T

    def v_tiles(row0, out_ref):
        for g in range(KV_HEADS):
            for u in range(lane_tiles):
                out_ref[g, u] = acc[row0 + g * d:row0 + (g + 1) * d,
                                    u * 128:(u + 1) * 128].astype(BF16)

    row = 0
    q_heads(row, 0, aq_ref, SWA_HEADS)
    row += SWA_HEADS * d
    ak_ref[...] = k_rows(row, 1).astype(BF16)
    row += kv_w
    v_tiles(row, av_ref)
    row += kv_w
    q_heads(row, 2, cq_ref, NSA_HEADS)
    row += NSA_HEADS * d
    ckc_ref[...] = acc[row:row + kv_w].T
    row += kv_w
    cvc_ref[...] = acc[row:row + kv_w].T
    row += kv_w
    cks_ref[...] = k_rows(row, 3).astype(BF16)
    row += kv_w
    for g in range(KV_HEADS):
        cvs_ref[g, 0, 0:d, :] = acc[row + g * d:row + (g + 1) * d].astype(BF16)
        cvs_ref[g, 0, d:SEL_V_ROWS, :] = jnp.ones((SEL_V_ROWS - d, tm), BF16)
    row += kv_w
    ckw_ref[...] = k_rows(row, 4).astype(BF16)
    row += kv_w
    v_tiles(row, cvw_ref)
    row += kv_w
    for g in range(KV_HEADS):
        gt_ref[g] = _sigmoid(acc[row + g * N_GATE_ROWS:row + (g + 1) * N_GATE_ROWS])


def _proj_heads(x2, b, t, g, w_t, head_gains):
    n, dm = x2.shape
    d = HEAD_DIM
    tm = PROJ_TM
    assert tm == SEL_KEY_TILE and t % tm == 0
    rows = w_t.shape[0]
    tpb = t // tm
    lt = tm // 128
    kv = KV_HEADS
    qt_spec = lambda h: pl.BlockSpec((None, h, d, tm), lambda bi, ti: (bi, 0, 0, ti))
    row_spec = pl.BlockSpec((None, tm, kv * d), lambda bi, ti: (bi, ti, 0))
    vt_spec = pl.BlockSpec((None, kv, lt, d, 128), lambda bi, ti: (bi, 0, ti, 0, 0))
    sds = jax.ShapeDtypeStruct
    return pl.pallas_call(
        _proj_heads_kernel,
        grid=(b, tpb),
        in_specs=[
            pl.BlockSpec((tm, dm), lambda bi, ti: (bi * tpb + ti, 0)),
            pl.BlockSpec((1, dm), lambda bi, ti: (0, 0)),
            pl.BlockSpec((rows, dm), lambda bi, ti: (0, 0)),
            pl.BlockSpec(head_gains.shape, lambda bi, ti: (0, 0, 0)),
        ],
        out_specs=[
            qt_spec(SWA_HEADS), row_spec, vt_spec,
            qt_spec(NSA_HEADS), row_spec, row_spec, row_spec,
            pl.BlockSpec((None, kv, 1, SEL_V_ROWS, tm), lambda bi, ti: (bi, 0, ti, 0, 0)),
            row_spec, vt_spec,
            pl.BlockSpec((None, kv, N_GATE_ROWS, tm), lambda bi, ti: (bi, 0, 0, ti)),
        ],
        out_shape=[
            sds((b, SWA_HEADS, d, t), BF16), sds((b, t, kv * d), BF16),
            sds((b, kv, t // 128, d, 128), BF16),
            sds((b, NSA_HEADS, d, t), BF16), sds((b, t, kv * d), F32), sds((b, t, kv * d), F32),
            sds((b, t, kv * d), BF16), sds((b, kv, tpb, SEL_V_ROWS, tm), BF16),
            sds((b, t, kv * d), BF16), sds((b, kv, t // 128, d, 128), BF16),
            sds((b, kv, N_GATE_ROWS, t), F32),
        ],
        compiler_params=_params(("parallel", "parallel"), VMEM_LIMIT),
        name="proj_heads",
    )(x2, g.reshape(1, dm), w_t, head_gains)


def _group_q(qt_ref, qb=0):
    return jnp.concatenate(
        [qt_ref[r, :, qb * Q_BLOCK:(qb + 1) * Q_BLOCK] for r in range(GROUP_R)], axis=1)


def _padded_q(qt_ref, g, qb=0):
    q4 = _group_q(qt_ref, qb)
    z = jnp.zeros_like(q4)
    return jnp.where(g == 0, jnp.concatenate([q4, z], axis=0), jnp.concatenate([z, q4], axis=0))


def _gate_untranspose(o_t, gate_ref, branch, qb=0):
    outs = []
    for r in range(GROUP_R):
        blk = o_t[:, r * Q_BLOCK:(r + 1) * Q_BLOCK]
        if gate_ref is not None:
            c = r * 3 + branch
            blk = blk * gate_ref[c:c + 1, qb * Q_BLOCK:(qb + 1) * Q_BLOCK]
        outs.append(blk.T)
    return jnp.concatenate(outs, axis=1)


def _banded_kernel(*refs, window, has_sink, gate_branch):
    refs = list(refs)
    sink_ref = refs.pop(0) if has_sink else None
    qt_ref, k_ref, vt_ref = refs[:3]
    gate_ref = refs[3] if gate_branch is not None else None
    o_ref = refs[-1]
    g = pl.program_id(1)
    rq = GROUP_R * Q_BLOCK
    n_sub = window // Q_BLOCK + 1
    span = n_sub * Q_BLOCK
    if has_sink:
        lrow = lax.broadcasted_iota(jnp.int32, (1, rq), 1)
        sink = jnp.zeros((1, rq), F32)
        for r in range(GROUP_R):
            sink = jnp.where((lrow >= r * Q_BLOCK) & (lrow < (r + 1) * Q_BLOCK),
                             sink_ref[g * GROUP_R + r], sink)
    for qb in range(BANDED_QB):
        i = pl.program_id(2) * BANDED_QB + qb
        start = pl.multiple_of(jnp.maximum(i * Q_BLOCK - window, 0), Q_BLOCK)
        st = _dot(k_ref[pl.ds(start, span), :], _padded_q(qt_ref, g, qb))
        kpos = start + lax.broadcasted_iota(jnp.int32, (span, rq), 0)
        lane = lax.broadcasted_iota(jnp.int32, (span, rq), 1)
        diff = (i * Q_BLOCK + (lane & (Q_BLOCK - 1))) - kpos
        st = jnp.where((diff >= 0) & (diff < window), st, NEG_INF)
        m = jnp.max(st, axis=0, keepdims=True)
        if has_sink:
            m = jnp.maximum(m, sink)
        m = jnp.where(m == NEG_INF, 0.0, m)
        p = jnp.exp(st - m)
        denom = jnp.sum(p, axis=0, keepdims=True)
        if has_sink:
            denom = denom + jnp.exp(sink - m)
        pb = p.astype(BF16)
        u0 = start // Q_BLOCK
        o_t = _dot(vt_ref[u0], pb[0:Q_BLOCK])
        for u in range(1, n_sub):
            o_t = o_t + _dot(vt_ref[u0 + u], pb[u * Q_BLOCK:(u + 1) * Q_BLOCK])
        o_t = o_t / jnp.maximum(denom, 1e-30)
        o_ref[qb * Q_BLOCK:(qb + 1) * Q_BLOCK, :] = _gate_untranspose(o_t, gate_ref, gate_branch, qb)


def _banded(qt, k_rows, vt, window, sinks=None, gates=None, gate_branch=None):
    b, h, d, t = qt.shape
    g = h // GROUP_R
    in_specs = []
    args = []
    if sinks is not None:
        in_specs.append(pl.BlockSpec(memory_space=pltpu.SMEM))
        args.append(sinks.astype(F32))
    qw = BANDED_QB * Q_BLOCK
    in_specs += [
        pl.BlockSpec((None, GROUP_R, d, qw), lambda bi, gi, i: (bi, gi, 0, i)),
        pl.BlockSpec((None, t, KV_HEADS * d), lambda bi, gi, i: (bi, 0, 0)),
        pl.BlockSpec((None, None, t // Q_BLOCK, d, Q_BLOCK), lambda bi, gi, i: (bi, gi, 0, 0, 0)),
    ]
    args += [qt, k_rows, vt]
    if gates is not None:
        in_specs.append(pl.BlockSpec((None, None, N_GATE_ROWS, qw),
                                     lambda bi, gi, i: (bi, gi, 0, i)))
        args.append(gates)
    return pl.pallas_call(
        functools.partial(_banded_kernel, window=window, has_sink=sinks is not None,
                          gate_branch=gate_branch if gates is not None else None),
        grid=(b, g, t // qw),
        in_specs=in_specs,
        out_specs=pl.BlockSpec((None, qw, GROUP_R * d), lambda bi, gi, i: (bi, i, gi)),
        out_shape=jax.ShapeDtypeStruct((b, t, h * d), F32),
        compiler_params=_params(("parallel", "parallel", "arbitrary"), VMEM_LIMIT),
        name="banded_attn_w%d" % window,
    )(*args)


def _conv_kernel(x_ref, b_ref, c_ref, xp_ref, cp_ref, w_ref, o_ref):
    ti = pl.program_id(1)
    z = c_ref[...] * x_ref[...]
    zp = jnp.where(ti > 0, cp_ref[...] * xp_ref[...], 0.0)
    row = lax.broadcasted_iota(jnp.int32, z.shape, 0)
    z1 = jnp.where(row == 0, zp[7:8, :], pltpu.roll(z, 1, 0))
    z2 = pltpu.roll(z, 2, 0)
    z2 = jnp.where(row == 0, zp[6:7, :], jnp.where(row == 1, zp[7:8, :], z2))
    w = w_ref[...]
    o_ref[...] = b_ref[...] * (w[0:1, :] * z2 + w[1:2, :] * z1 + w[2:3, :] * z)


def _conv(proj_b, conv_w, tt=512):
    b, t, _ = proj_b.shape
    w = MIX_W
    hb = tt // 8
    prev = lambda bi, ti: (bi, jnp.maximum(ti * hb - 1, 0), 0)
    prev_c = lambda bi, ti: (bi, jnp.maximum(ti * hb - 1, 0), 2)
    return pl.pallas_call(
        _conv_kernel,
        grid=(b, t // tt),
        in_specs=[
            pl.BlockSpec((None, tt, w), lambda bi, ti: (bi, ti, 0)),
            pl.BlockSpec((None, tt, w), lambda bi, ti: (bi, ti, 1)),
            pl.BlockSpec((None, tt, w), lambda bi, ti: (bi, ti, 2)),
            pl.BlockSpec((None, 8, w), prev),
            pl.BlockSpec((None, 8, w), prev_c),
            pl.BlockSpec((8, w), lambda bi, ti: (0, 0)),
        ],
        out_specs=pl.BlockSpec((None, tt, w), lambda bi, ti: (bi, ti, 0)),
        out_shape=jax.ShapeDtypeStruct((b, t, w), F32),
        compiler_params=_params(("parallel", "parallel")),
        name="short_conv",
    )(proj_b, proj_b, proj_b, proj_b, proj_b,
      jnp.pad(conv_w.reshape(conv_w.shape[0], w).astype(F32), ((0, 8 - conv_w.shape[0]), (0, 0))))


def _gelu_tanh(x):
    return x * (0.5 * (1.0 + jnp.tanh(0.7978845608028654 * (x + 0.044715 * (x * x * x)))))


def _compress_kernel(tk_ref, tv_ref, pek_ref, pev_ref, wk1_ref, wk2_ref, wv1_ref, wv2_ref,
                     kg_ref, ko_ref, vo_ref):
    nrow = tk_ref.shape[0]

    def mlp(a, pe_ref, w1_ref, w2_ref):
        a0 = (a + pe_ref[0:1, :]).astype(BF16)
        a1 = (a + pe_ref[1:2, :]).astype(BF16)
        p1 = _dot(a0, w1_ref[0])
        p2 = _dot(a1, w1_ref[1])
        hdn = p1 + pltpu.roll(p2, nrow - 1, 0)
        return _dot(_gelu_tanh(hdn).astype(BF16), w2_ref[...])

    kc = mlp(tk_ref[...], pek_ref, wk1_ref, wk2_ref)
    ko_ref[...] = _rms_rows(kc, kg_ref[...]).astype(BF16)
    vo_ref[...] = mlp(tv_ref[...], pev_ref, wv1_ref, wv2_ref).astype(BF16)


def _compress(kc_rows, vc_rows, pos_k, pos_v, wk1, wk2, wv1, wv2, k_gain):
    b, t, kvd = kc_rows.shape
    kv = KV_HEADS
    d = kvd // kv
    nrow = t // CMP_STRIDE
    wide = CMP_STRIDE * kvd
    hid = wk1.shape[1]

    def expand_w1(w1):
        w = w1.reshape(2, CMP_STRIDE, 1, d, hid)
        per_head = []
        for g in range(kv):
            pads = [jnp.zeros_like(w)] * kv
            pads[g] = w
            per_head.append(jnp.concatenate(pads, axis=2).reshape(2, wide, hid))
        return jnp.stack(per_head).astype(BF16)

    def expand_pe(pe):
        return jnp.broadcast_to(pe.reshape(2, CMP_STRIDE, 1, d), (2, CMP_STRIDE, kv, d)).reshape(2, wide)

    tok = pl.BlockSpec((None, nrow, wide), lambda bi, gi: (bi, 0, 0))
    full = lambda shape: pl.BlockSpec(shape, lambda bi, gi: tuple(0 for _ in shape))
    w1_spec = pl.BlockSpec((None, 2, wide, hid), lambda bi, gi: (gi, 0, 0, 0))
    out = pl.BlockSpec((None, nrow, d), lambda bi, gi: (bi * kv + gi, 0, 0))
    return pl.pallas_call(
        _compress_kernel,
        grid=(b, kv),
        in_specs=[tok, tok, full((2, wide)), full((2, wide)), w1_spec, full((hid, d)),
                  w1_spec, full((hid, d)), full((1, d))],
        out_specs=[out, out],
        out_shape=[jax.ShapeDtypeStruct((b * kv, nrow, d), BF16),
                   jax.ShapeDtypeStruct((b * kv, nrow, d), BF16)],
        compiler_params=_params(("parallel", "arbitrary"), VMEM_LIMIT),
        name="nsa_compress",
    )(kc_rows.reshape(b, nrow, wide), vc_rows.reshape(b, nrow, wide), expand_pe(pos_k),
      expand_pe(pos_v), expand_w1(wk1), wk2.astype(BF16), expand_w1(wv1), wv2.astype(BF16),
      k_gain.reshape(1, d))


def _cmp_attn_kernel(qt_ref, kc_ref, vct_ref, gate_ref, o_ref, selt_ref, imp_ref, *, sel_k):
    step = pl.program_id(2)
    rq = GROUP_R * Q_BLOCK
    nc = kc_ref.shape[0]
    ns = selt_ref.shape[0]
    blk = lax.broadcasted_iota(jnp.int32, (ns, Q_BLOCK), 0)

    def attend(nc_eff):
        for qb in range(CMP_QB):
            i = step * CMP_QB + qb
            st = _dot(kc_ref[0:nc_eff, :], _group_q(qt_ref, qb))
            n = lax.broadcasted_iota(jnp.int32, (nc_eff, rq), 0)
            lane = lax.broadcasted_iota(jnp.int32, (nc_eff, rq), 1)
            qpos = i * Q_BLOCK + (lane & (Q_BLOCK - 1))
            st = jnp.where(n * CMP_STRIDE + (CMP_BLOCK - 1) <= qpos, st, NEG_INF)
            m = jnp.max(st, axis=0, keepdims=True)
            m = jnp.where(m == NEG_INF, 0.0, m)
            p = jnp.exp(st - m)
            denom = jnp.sum(p, axis=0, keepdims=True)
            pb = (p / jnp.maximum(denom, 1e-30)).astype(BF16)
            o_t = _dot(vct_ref[:, 0:nc_eff], pb)
            o_ref[qb * Q_BLOCK:(qb + 1) * Q_BLOCK, :] = _gate_untranspose(o_t, gate_ref, 0, qb)
            ss = lax.broadcasted_iota(jnp.int32, (ns, nc_eff), 0) * SEL_BLOCK
            nn = lax.broadcasted_iota(jnp.int32, (ns, nc_eff), 1) * CMP_STRIDE
            overlap_t = jnp.where((nn < ss + SEL_BLOCK) & (nn + (CMP_BLOCK - 1) >= ss),
                                  1.0, 0.0).astype(BF16)
            imp_r = _dot(overlap_t, pb)
            imp = imp_r[:, 0:Q_BLOCK]
            for r in range(1, GROUP_R):
                imp = imp + imp_r[:, r * Q_BLOCK:(r + 1) * Q_BLOCK]
            qp = i * Q_BLOCK + lax.broadcasted_iota(jnp.int32, (ns, Q_BLOCK), 1)
            cur = qp // SEL_BLOCK
            forced = (blk == 0) | (blk == cur) | (blk == cur - 1)
            imp = jnp.where(forced, jnp.inf, imp)
            imp_ref[qb] = jnp.where(blk <= cur, imp, NEG_INF)

    variant_rows = 128
    n_var = nc // variant_rows
    n_ending = (step + 1) * (CMP_QB * Q_BLOCK // CMP_STRIDE) - 1
    variant = jnp.minimum((n_ending + variant_rows - 1) // variant_rows, n_var) - 1
    for v in range(n_var):
        pl.when(variant == v)(functools.partial(attend, (v + 1) * variant_rows))

    blkf = blk.astype(F32)
    imps = [imp_ref[qb] for qb in range(CMP_QB)]
    sels = [jnp.zeros((ns, Q_BLOCK), F32) for _ in range(CMP_QB)]
    for _ in range(sel_k):
        for qb in range(CMP_QB):
            mx = jnp.max(imps[qb], axis=0, keepdims=True)
            first = jnp.min(jnp.where(imps[qb] == mx, blkf, float(ns)), axis=0, keepdims=True)
            hit = blkf == first
            sels[qb] = jnp.where(hit & (mx > NEG_INF), 1.0, sels[qb])
            imps[qb] = jnp.where(hit, NEG_INF, imps[qb])
    for qb in range(CMP_QB):
        selt_ref[:, qb * Q_BLOCK:(qb + 1) * Q_BLOCK] = sels[qb]


def _cmp_attn(qt, k_cmp, v_cmp_t, gates):
    b, h, d, t = qt.shape
    g = h // GROUP_R
    nc = k_cmp.shape[1]
    ns = t // SEL_BLOCK
    qw = CMP_QB * Q_BLOCK
    assert nc % 128 == 0
    return pl.pallas_call(
        functools.partial(_cmp_attn_kernel, sel_k=min(SEL_TOPK, ns)),
        grid=(b, g, t // qw),
        in_specs=[
            pl.BlockSpec((None, GROUP_R, d, qw), lambda bi, gi, i: (bi, gi, 0, i)),
            pl.BlockSpec((None, nc, d), lambda bi, gi, i: (bi * KV_HEADS + gi, 0, 0)),
            pl.BlockSpec((None, d, nc), lambda bi, gi, i: (bi * KV_HEADS + gi, 0, 0)),
            pl.BlockSpec((None, None, N_GATE_ROWS, qw), lambda bi, gi, i: (bi, gi, 0, i)),
        ],
        out_specs=[
            pl.BlockSpec((None, qw, GROUP_R * d), lambda bi, gi, i: (bi, i, gi)),
            pl.BlockSpec((None, None, ns, qw), lambda bi, gi, i: (bi, gi, 0, i)),
        ],
        out_shape=[jax.ShapeDtypeStruct((b, t, h * d), F32),
                   jax.ShapeDtypeStruct((b, g, ns, t), F32)],
        scratch_shapes=[pltpu.VMEM((CMP_QB, ns, Q_BLOCK), F32)],
        compiler_params=_params(("parallel", "parallel", "arbitrary"), VMEM_LIMIT),
        name="nsa_cmp_topk",
    )(qt, k_cmp, v_cmp_t, gates)


def _sel_attn_kernel(qt_ref, ks_ref, vst_ref, selt_ref, gate_ref, o_ref, sa_ref, sb_ref, m_ref,
                     acc_ref):
    g = pl.program_id(1)
    i = pl.program_id(2)
    tk = SEL_KEY_TILE
    bpt = tk // SEL_BLOCK
    spt = tk // Q_BLOCK
    d = HEAD_DIM
    m_ref[...] = jnp.full(m_ref.shape, NEG_INF, F32)
    acc_ref[...] = jnp.zeros_like(acc_ref)
    n_tiles = (i * Q_BLOCK + Q_BLOCK + tk - 1) // tk
    qpad = _padded_q(qt_ref, g)
    tri = jnp.where(lax.broadcasted_iota(jnp.int32, (Q_BLOCK, Q_BLOCK), 0)
                    <= lax.broadcasted_iota(jnp.int32, (Q_BLOCK, Q_BLOCK), 1), 0.0, NEG_INF)

    def scores(j, s_ref):
        k0 = pl.multiple_of(j * tk, tk)
        st = _dot(ks_ref[pl.ds(k0, tk), :], qpad)
        sel_rows = selt_ref[pl.ds(pl.multiple_of(j * bpt, bpt), bpt), :]
        bias_rows = jnp.where(sel_rows > 0.5, 0.0, NEG_INF)
        parts = []
        for u in range(spt):
            sub = jnp.concatenate(
                [jnp.broadcast_to(bias_rows[s:s + 1, :], (SEL_BLOCK, Q_BLOCK))
                 for s in range(u * Q_BLOCK // SEL_BLOCK, (u + 1) * Q_BLOCK // SEL_BLOCK)], axis=0)
            parts.append(sub + jnp.where(j * spt + u == i, tri, 0.0))
        bias = jnp.concatenate(parts, axis=0)
        s_ref[...] = st + jnp.concatenate([bias] * GROUP_R, axis=1)

    def consume(j, s_ref):
        st = s_ref[...]
        m_prev = m_ref[...]
        m_new = jnp.maximum(m_prev, jnp.max(st, axis=0, keepdims=True))
        m_safe = jnp.where(m_new == NEG_INF, 0.0, m_new)
        alpha = jnp.exp(m_prev - m_safe)
        p = jnp.exp((st - m_safe).astype(BF16))
        acc_ref[...] = alpha * acc_ref[...] + _dot(vst_ref[j], p)
        m_ref[...] = m_new

    scores(0, sa_ref)
    n_pairs = (n_tiles - 1) // 2

    def step(jj, carry):
        j = 2 * jj
        scores(j + 1, sb_ref)
        consume(j, sa_ref)
        scores(j + 2, sa_ref)
        consume(j + 1, sb_ref)
        return carry

    lax.fori_loop(0, n_pairs, step, 0)
    j_rem = 2 * n_pairs

    @pl.when(n_tiles - j_rem == 2)
    def _():
        scores(j_rem + 1, sb_ref)

    consume(j_rem, sa_ref)

    @pl.when(n_tiles - j_rem == 2)
    def _():
        consume(j_rem + 1, sb_ref)

    o_t = acc_ref[0:d, :] / jnp.maximum(acc_ref[d:d + 1, :], 1e-30)
    o_ref[...] = _gate_untranspose(o_t, gate_ref, 1)


def _sel_attn(qt, ks_rows, vs_t, sel_t, gates):
    b, h, d, t = qt.shape
    g = h // GROUP_R
    ns = sel_t.shape[2]
    tk = SEL_KEY_TILE
    rq = GROUP_R * Q_BLOCK
    return pl.pallas_call(
        _sel_attn_kernel,
        grid=(b, g, t // Q_BLOCK),
        in_specs=[
            pl.BlockSpec((None, GROUP_R, d, Q_BLOCK), lambda bi, gi, i: (bi, gi, 0, i)),
            pl.BlockSpec((None, t, KV_HEADS * d), lambda bi, gi, i: (bi, 0, 0)),
            pl.BlockSpec((None, None, t // tk, SEL_V_ROWS, tk), lambda bi, gi, i: (bi, gi, 0, 0, 0)),
            pl.BlockSpec((None, None, ns, Q_BLOCK), lambda bi, gi, i: (bi, gi, 0, i)),
            pl.BlockSpec((None, None, N_GATE_ROWS, Q_BLOCK), lambda bi, gi, i: (bi, gi, 0, i)),
        ],
        out_specs=pl.BlockSpec((None, Q_BLOCK, GROUP_R * d), lambda bi, gi, i: (bi, i, gi)),
        out_shape=jax.ShapeDtypeStruct((b, t, h * d), F32),
        scratch_shapes=[pltpu.VMEM((tk, rq), F32), pltpu.VMEM((tk, rq), F32),
                        pltpu.VMEM((1, rq), F32), pltpu.VMEM((SEL_V_ROWS, rq), F32)],
        compiler_params=_params(("parallel", "parallel", "arbitrary"), VMEM_LIMIT),
        name="nsa_selected",
    )(qt, ks_rows, vs_t, sel_t, gates)


def _retention_kernel(lg_ref, q_ref, k_ref, v_ref, gt_ref, cos_ref, sin_ref, ng_ref, o_ref, r_ref):
    ci = pl.program_id(1)
    c = RET_CHUNK
    dk, dv = RET_QK_DIM, RET_V_DIM
    half = dk // 2

    @pl.when(ci == 0)
    def _():
        r_ref[...] = jnp.zeros_like(r_ref)

    cosf = cos_ref[...]
    sinf = sin_ref[...]

    def rot(x):
        return x * cosf + jnp.concatenate([x[:, half:], x[:, :half]], axis=1) * sinf

    ii = lax.broadcasted_iota(jnp.int32, (c, c), 0)
    jj = lax.broadcasted_iota(jnp.int32, (c, c), 1)
    d = (ii - jj).astype(F32)
    jcol = lax.broadcasted_iota(jnp.int32, (c, 1), 0).astype(F32)
    for hh in range(RET_HEADS):
        lg = lg_ref[hh]
        q = rot(q_ref[:, hh * dk:(hh + 1) * dk])
        k = rot(k_ref[:, hh * dk:(hh + 1) * dk]) * (dk ** -0.5)
        vb = v_ref[:, hh * dv:(hh + 1) * dv].astype(BF16)
        dmask = jnp.where(d >= 0, jnp.exp(d * lg), 0.0)
        att = _dot_nt(q.astype(BF16), k.astype(BF16)) * dmask
        o = _dot(att.astype(BF16), vb)
        xi = jnp.exp((jcol + 1.0) * lg)
        zeta = jnp.exp((c - 1.0 - jcol) * lg)
        r_prev = r_ref[hh]
        o = o + _dot((q * xi).astype(BF16), r_prev.astype(BF16))
        s_chunk = _dot_tn((k * zeta).astype(BF16), vb)
        decay = jnp.exp(jnp.zeros((1, dv), F32) + c * lg)
        r_ref[hh] = r_prev * decay + s_chunk
        mu = jnp.mean(o, axis=-1, keepdims=True)
        var = jnp.mean(jnp.square(o - mu), axis=-1, keepdims=True)
        on = (o - mu) * lax.rsqrt(var + EPS)
        gt = gt_ref[:, hh * dv:(hh + 1) * dv]
        o_ref[:, hh * dv:(hh + 1) * dv] = (gt * _sigmoid(gt)) * (on * ng_ref[:, hh * dv:(hh + 1) * dv])


def _retention(proj_d, norm_gain):
    b, t, _ = proj_d.shape
    h, dk = RET_HEADS, RET_QK_DIM
    c = RET_CHUNK
    dv = RET_V_DIM
    assert 2 * h * dk == h * dv
    half = dk // 2
    inv = ROPE_BASE ** (-jnp.arange(half, dtype=F32) / half)
    ang = jnp.arange(t).astype(F32)[:, None] * inv[None, :]
    cos = jnp.cos(ang)
    sin = jnp.sin(ang)
    cosf = jnp.concatenate([cos, cos], axis=-1)
    sinf = jnp.concatenate([-sin, sin], axis=-1)
    log_gamma = jnp.log(1.0 - 2.0 ** (-5.0 - jnp.arange(h, dtype=F32)))
    return pl.pallas_call(
        _retention_kernel,
        grid=(b, t // c),
        in_specs=[
            pl.BlockSpec(memory_space=pltpu.SMEM),
            pl.BlockSpec((None, c, h * dk), lambda bi, ci: (bi, ci, 0)),
            pl.BlockSpec((None, c, h * dk), lambda bi, ci: (bi, ci, 1)),
            pl.BlockSpec((None, c, h * dv), lambda bi, ci: (bi, ci, 1)),
            pl.BlockSpec((None, c, h * dv), lambda bi, ci: (bi, ci, 2)),
            pl.BlockSpec((c, dk), lambda bi, ci: (ci, 0)),
            pl.BlockSpec((c, dk), lambda bi, ci: (ci, 0)),
            pl.BlockSpec((1, h * dv), lambda bi, ci: (0, 0)),
        ],
        out_specs=pl.BlockSpec((None, c, h * dv), lambda bi, ci: (bi, ci, 0)),
        out_shape=jax.ShapeDtypeStruct((b, t, h * dv), F32),
        scratch_shapes=[pltpu.VMEM((h, dk, dv), F32)],
        compiler_params=_params(("parallel", "arbitrary")),
        name="retention",
    )(log_gamma, proj_d, proj_d, proj_d, proj_d, cosf, sinf, norm_gain.reshape(1, h * dv))


def _merge_kernel(x_ref, g_ref, wgate_ref, bias_ref, ya_ref, yb_ref, yc0_ref, yc1_ref, yc2_ref,
                  yd_ref, wb_ref, wo_ref, o_ref):
    x = x_ref[...]
    d = x.shape[1]
    u = _rms_rows(x, g_ref[...]).astype(BF16)
    ys = (ya_ref[...], yb_ref[...], yc0_ref[...] + yc1_ref[...] + yc2_ref[...], yd_ref[...])
    merged = jnp.zeros(x.shape, F32)
    for n in range(N_BRANCH):
        logits = _dot(u, wgate_ref[:, n * d:(n + 1) * d]) + bias_ref[:, n * d:(n + 1) * d]
        merged = merged + _sigmoid(logits) * _dot(ys[n].astype(BF16), wb_ref[n])
    o_ref[...] = x + _dot(merged.astype(BF16), wo_ref[...])


def _merge(x2, g, w_gate, bias, ys, w_branch, w_out, tm=256):
    n, d = x2.shape
    w = MIX_W
    row = lambda width: pl.BlockSpec((tm, width), lambda i: (i, 0))
    full = lambda shape: pl.BlockSpec(shape, lambda i: tuple(0 for _ in shape))
    return pl.pallas_call(
        _merge_kernel,
        grid=(n // tm,),
        in_specs=[row(d), full((1, d)), full((d, N_BRANCH * d)), full((1, N_BRANCH * d))]
        + [row(w)] * 6 + [full((N_BRANCH, w, d)), full((d, d))],
        out_specs=row(d),
        out_shape=jax.ShapeDtypeStruct((n, d), F32),
        compiler_params=_params(("parallel",), VMEM_LIMIT),
        name="merge_out",
    )(x2, g.reshape(1, d), w_gate.astype(BF16), bias.reshape(1, N_BRANCH * d),
      *[y.reshape(n, w) for y in ys], w_branch.astype(BF16), w_out.astype(BF16))


def _mixers(x2, b, t, mix_norm, w_in, merge_gate_bias, swa_q_gain, swa_k_gain, swa_sinks, conv_w,
            nsa_q_gain, nsa_k_gain, cmp_pos_k, cmp_pos_v, cmp_wk1, cmp_wk2, cmp_wv1, cmp_wv2,
            ret_norm_gain, w_branch, w_out):
    hd = HEAD_DIM
    swa_q, swa_kv = SWA_HEADS * hd, SWA_KV_HEADS * hd
    nsa_q, nsa_kv = NSA_HEADS * hd, NSA_KV_HEADS * hd
    ret_qk, ret_v = RET_HEADS * RET_QK_DIM, RET_HEADS * RET_V_DIM
    n_gate = NSA_HEADS * 3
    o_a = 0
    o_b = o_a + swa_q + 2 * swa_kv
    o_c = o_b + 3 * MIX_W
    o_cg = o_c + nsa_q + 6 * nsa_kv
    o_d = o_cg + n_gate
    o_g = o_d + 2 * ret_qk + 2 * ret_v

    per_g = GROUP_R * 3
    w_gates = [jnp.pad(w_in[:, o_cg + gi * per_g:o_cg + (gi + 1) * per_g],
                       ((0, 0), (0, N_GATE_ROWS - per_g))) for gi in range(NSA_KV_HEADS)]
    w_t = jnp.concatenate([w_in[:, o_a:o_b], w_in[:, o_c:o_cg]] + w_gates, axis=1).T.astype(BF16)
    head_gains = jnp.stack([swa_q_gain, swa_k_gain, nsa_q_gain, nsa_k_gain[1], nsa_k_gain[2]])
    head_gains = jnp.broadcast_to(head_gains.astype(F32)[:, :, None], (5, hd, 128))
    (a_qt, a_k, a_vt, c_qt, c_kc, c_vc, c_ks, c_vst, c_kw, c_vwt, gates_c) = _proj_heads(
        x2, b, t, mix_norm, w_t, head_gains)
    proj_b = _norm_matmul(x2, mix_norm, w_in[:, o_b:o_c])
    proj_d = _norm_matmul(x2, mix_norm, w_in[:, o_d:o_g])

    y_a = _banded(a_qt, a_k, a_vt, SWA_WINDOW, sinks=swa_sinks)

    y_b = _conv(proj_b.reshape(b, t, 3 * MIX_W), conv_w)

    k_cmp, v_cmp = _compress(c_kc, c_vc, cmp_pos_k, cmp_pos_v, cmp_wk1, cmp_wk2, cmp_wv1, cmp_wv2,
                             nsa_k_gain[0])
    y_cmp, sel_t = _cmp_attn(c_qt, k_cmp, v_cmp.transpose(0, 2, 1), gates_c)
    y_sel = _sel_attn(c_qt, c_ks, c_vst, sel_t, gates_c)
    y_win = _banded(c_qt, c_kw, c_vwt, NSA_WINDOW, gates=gates_c, gate_branch=2)

    y_d = _retention(proj_d.reshape(b, t, 2 * ret_qk + 2 * ret_v), ret_norm_gain)

    return _merge(x2, mix_norm, w_in[:, o_g:], merge_gate_bias,
                  (y_a, y_b, y_cmp, y_sel, y_win, y_d), w_branch, w_out)


def kernel(x, ffn1_norm, ffn1_w_gate, ffn1_w_up, ffn1_w_down, mix_norm, w_in, merge_gate_bias, swa_q_gain, swa_k_gain, swa_sinks, conv_w, nsa_q_gain, nsa_k_gain, cmp_pos_k, cmp_pos_v, cmp_wk1, cmp_wk2, cmp_wv1, cmp_wv2, ret_norm_gain, w_branch, w_out, ffn2_norm, ffn2_w_gate, ffn2_w_up, ffn2_w_down):
    b, t, d = x.shape
    x2 = x.reshape(b * t, d)
    for l in range(ffn1_norm.shape[0]):
        x2 = _ffn(x2, ffn1_norm[l], ffn1_w_gate[l], ffn1_w_up[l], ffn1_w_down[l])
        x2 = _mixers(x2, b, t, mix_norm[l], w_in[l], merge_gate_bias[l], swa_q_gain[l],
                     swa_k_gain[l], swa_sinks[l], conv_w[l], nsa_q_gain[l], nsa_k_gain[l],
                     cmp_pos_k[l], cmp_pos_v[l], cmp_wk1[l], cmp_wk2[l], cmp_wv1[l], cmp_wv2[l],
                     ret_norm_gain[l], w_branch[l], w_out[l])
        x2 = _ffn(x2, ffn2_norm[l], ffn2_w_gate[l], ffn2_w_up[l], ffn2_w_down[l])
    return x2.reshape(b, t, d)
```

```python
import functools

import jax
import jax.numpy as jnp
from jax import lax
from jax.experimental import pallas as pl
from jax.experimental.pallas import tpu as pltpu

F32 = jnp.float32
BF16 = jnp.bfloat16

HEAD_DIM = 64
Q_BLOCK = 128
MIX_W = 512
N_BRANCH = 4
SWA_HEADS = 8
SWA_KV_HEADS = 2
SWA_WINDOW = 128
NSA_HEADS = 8
NSA_KV_HEADS = 2
CMP_BLOCK = 32
CMP_STRIDE = 16
SEL_BLOCK = 64
SEL_TOPK = 16
NSA_WINDOW = 512
RET_HEADS = 4
RET_QK_DIM = 64
RET_V_DIM = 128
RET_CHUNK = 128
ROPE_BASE = 10000.0
EPS = 1e-6
GROUP_R = 4
KV_HEADS = 2
N_GATE_ROWS = 16
SEL_KEY_TILE = 512
SEL_V_ROWS = HEAD_DIM + 16
PROJ_TM = 512
BANDED_QB = 4
CMP_QB = 2
VMEM_LIMIT = 52 * 1024 * 1024

NEG_INF = float("-inf")
LOG2_E = 1.4426950408889634


def _params(sem, vmem=None):
    return pltpu.CompilerParams(dimension_semantics=sem, vmem_limit_bytes=vmem)


def _sigmoid(x):
    return 1.0 / (1.0 + jnp.exp(-x))


def _rms_rows(x, g):
    return x * lax.rsqrt(jnp.mean(x * x, axis=-1, keepdims=True) + EPS) * g


def _dot(a, b):
    return jnp.dot(a, b, preferred_element_type=F32)


def _dot_nt(a, b):
    return lax.dot_general(a, b, (((1,), (1,)), ((), ())), preferred_element_type=F32)


def _dot_tn(a, b):
    return lax.dot_general(a, b, (((0,), (0,)), ((), ())), preferred_element_type=F32)


def _ffn_kernel(x_ref, g_ref, wg_ref, wu_ref, wd_ref, o_ref, xn_ref, acc_ref):
    f = pl.program_id(1)

    @pl.when(f == 0)
    def _():
        xn_ref[...] = _rms_rows(x_ref[...], g_ref[...]).astype(BF16)
        acc_ref[...] = jnp.zeros_like(acc_ref)

    xn = xn_ref[...]
    a = _dot(xn, wg_ref[...])
    b = _dot(xn, wu_ref[...])
    h = (a * _sigmoid(a)) * b
    acc_ref[...] += _dot(h.astype(BF16), wd_ref[...])

    @pl.when(f == pl.num_programs(1) - 1)
    def _():
        o_ref[...] = x_ref[...] + 0.5 * acc_ref[...]


def _ffn(x2, g, wg, wu, wd, tm=512, nf=2):
    n, d = x2.shape
    dff = wg.shape[1]
    tf = dff // nf
    return pl.pallas_call(
        _ffn_kernel,
        grid=(n // tm, nf),
        in_specs=[
            pl.BlockSpec((tm, d), lambda i, f: (i, 0)),
            pl.BlockSpec((1, d), lambda i, f: (0, 0)),
            pl.BlockSpec((d, tf), lambda i, f: (0, f)),
            pl.BlockSpec((d, tf), lambda i, f: (0, f)),
            pl.BlockSpec((tf, d), lambda i, f: (f, 0)),
        ],
        out_specs=pl.BlockSpec((tm, d), lambda i, f: (i, 0)),
        out_shape=jax.ShapeDtypeStruct((n, d), F32),
        scratch_shapes=[pltpu.VMEM((tm, d), BF16), pltpu.VMEM((tm, d), F32)],
        compiler_params=_params(("parallel", "arbitrary"), VMEM_LIMIT),
        name="ffn",
    )(x2, g.reshape(1, d), wg.astype(BF16), wu.astype(BF16), wd.astype(BF16))


def _norm_matmul_kernel(x_ref, g_ref, w_ref, o_ref):
    xn = _rms_rows(x_ref[...], g_ref[...]).astype(BF16)
    o_ref[...] = _dot(xn, w_ref[...])


def _norm_matmul(x2, g, w, tm=512):
    n, d = x2.shape
    c = w.shape[1]
    return pl.pallas_call(
        _norm_matmul_kernel,
        grid=(n // tm,),
        in_specs=[
            pl.BlockSpec((tm, d), lambda i: (i, 0)),
            pl.BlockSpec((1, d), lambda i: (0, 0)),
            pl.BlockSpec((d, c), lambda i: (0, 0)),
        ],
        out_specs=pl.BlockSpec((tm, c), lambda i: (i, 0)),
        out_shape=jax.ShapeDtypeStruct((n, c), F32),
        compiler_params=_params(("parallel",), VMEM_LIMIT),
        name="in_proj",
    )(x2, g.reshape(1, d), w.astype(BF16))


def _proj_heads_kernel(x_ref, g_ref, wt_ref, hg_ref, aq_ref, ak_ref, av_ref, cq_ref, cq2_ref,
                       ckc_ref, cvc_ref, cks_ref, cvs_ref, ckw_ref, cvw_ref, gt_ref):
    d = HEAD_DIM
    xn = _rms_rows(x_ref[...], g_ref[...]).astype(BF16)
    acc = _dot_nt(wt_ref[...], xn)
    tm = xn.shape[0]
    lane_tiles = tm // 128
    scale = d ** -0.5
    kv_w = KV_HEADS * d

    def head_norm(row0, gain_idx, mult):
        hb = acc[row0:row0 + d]
        gain = jnp.concatenate([hg_ref[gain_idx]] * lane_tiles, axis=1)
        y = hb * lax.rsqrt(jnp.mean(hb * hb, axis=0, keepdims=True) + EPS) * gain
        return y * mult if mult != 1.0 else y

    def q_heads(row0, gain_idx, out_ref, n_heads, log2_ref=None):
        for h in range(n_heads):
            y = head_norm(row0 + h * d, gain_idx, 1.0)
            out_ref[h] = (y * scale).astype(BF16)
            if log2_ref is not None:
                log2_ref[h] = (y * (scale * LOG2_E)).astype(BF16)

    def k_rows(row0, gain_idx):
        return jnp.concatenate([head_norm(row0 + g * d, gain_idx, 1.0) for g in range(KV_HEADS)],
                               axis=0).T

    def v_tiles(row0, out_ref):
        for g in range(KV_HEADS):
            for u in range(lane_tiles):
                out_ref[g, u] = acc[row0 + g * d:row0 + (g + 1) * d,
                                    u * 128:(u + 1) * 128].astype(BF16)

    row = 0
    q_heads(row, 0, aq_ref, SWA_HEADS)
    row += SWA_HEADS * d
    ak_ref[...] = k_rows(row, 1).astype(BF16)
    row += kv_w
    v_tiles(row, av_ref)
    row += kv_w
    q_heads(row, 2, cq_ref, NSA_HEADS, cq2_ref)
    row += NSA_HEADS * d
    ckc_ref[...] = acc[row:row + kv_w].T
    row += kv_w
    cvc_ref[...] = acc[row:row + kv_w].T
    row += kv_w
    cks_ref[...] = k_rows(row, 3).astype(BF16)
    row += kv_w
    for g in range(KV_HEADS):
        cvs_ref[g, 0, 0:d, :] = acc[row + g * d:row + (g + 1) * d].astype(BF16)
        cvs_ref[g, 0, d:SEL_V_ROWS, :] = jnp.ones((SEL_V_ROWS - d, tm), BF16)
    row += kv_w
    ckw_ref[...] = k_rows(row, 4).astype(BF16)
    row += kv_w
    v_tiles(row, cvw_ref)
    row += kv_w
    for g in range(KV_HEADS):
        gt_ref[g] = _sigmoid(acc[row + g * N_GATE_ROWS:row + (g + 1) * N_GATE_ROWS])


def _proj_heads(x2, b, t, g, w_t, head_gains):
    n, dm = x2.shape
    d = HEAD_DIM
    tm = PROJ_TM
    assert tm == SEL_KEY_TILE and t % tm == 0
    rows = w_t.shape[0]
    tpb = t // tm
    lt = tm // 128
    kv = KV_HEADS
    qt_spec = lambda h: pl.BlockSpec((None, h, d, tm), lambda bi, ti: (bi, 0, 0, ti))
    row_spec = pl.BlockSpec((None, tm, kv * d), lambda bi, ti: (bi, ti, 0))
    vt_spec = pl.BlockSpec((None, kv, lt, d, 128), lambda bi, ti: (bi, 0, ti, 0, 0))
    sds = jax.ShapeDtypeStruct
    return pl.pallas_call(
        _proj_heads_kernel,
        grid=(b, tpb),
        in_specs=[
            pl.BlockSpec((tm, dm), lambda bi, ti: (bi * tpb + ti, 0)),
            pl.BlockSpec((1, dm), lambda bi, ti: (0, 0)),
            pl.BlockSpec((rows, dm), lambda bi, ti: (0, 0)),
            pl.BlockSpec(head_gains.shape, lambda bi, ti: (0, 0, 0)),
        ],
        out_specs=[
            qt_spec(SWA_HEADS), row_spec, vt_spec,
            qt_spec(NSA_HEADS), qt_spec(NSA_HEADS), row_spec, row_spec, row_spec,
            pl.BlockSpec((None, kv, 1, SEL_V_ROWS, tm), lambda bi, ti: (bi, 0, ti, 0, 0)),
            row_spec, vt_spec,
            pl.BlockSpec((None, kv, N_GATE_ROWS, tm), lambda bi, ti: (bi, 0, 0, ti)),
        ],
        out_shape=[
            sds((b, SWA_HEADS, d, t), BF16), sds((b, t, kv * d), BF16),
            sds((b, kv, t // 128, d, 128), BF16),
            sds((b, NSA_HEADS, d, t), BF16), sds((b, NSA_HEADS, d, t), BF16),
            sds((b, t, kv * d), F32), sds((b, t, kv * d), F32),
            sds((b, t, kv * d), BF16), sds((b, kv, tpb, SEL_V_ROWS, tm), BF16),
            sds((b, t, kv * d), BF16), sds((b, kv, t // 128, d, 128), BF16),
            sds((b, kv, N_GATE_ROWS, t), F32),
        ],
        compiler_params=_params(("parallel", "parallel"), VMEM_LIMIT),
        name="proj_heads",
    )(x2, g.reshape(1, dm), w_t, head_gains)


def _group_q(qt_ref, qb=0):
    return jnp.concatenate(
        [qt_ref[r, :, qb * Q_BLOCK:(qb + 1) * Q_BLOCK] for r in range(GROUP_R)], axis=1)


def _padded_q(qt_ref, g, qb=0):
    q4 = _group_q(qt_ref, qb)
    z = jnp.zeros_like(q4)
    return jnp.where(g == 0, jnp.concatenate([q4, z], axis=0), jnp.concatenate([z, q4], axis=0))


def _gate_untranspose(o_t, gate_ref, branch, qb=0):
    outs = []
    for r in range(GROUP_R):
        blk = o_t[:, r * Q_BLOCK:(r + 1) * Q_BLOCK]
        if gate_ref is not None:
            c = r * 3 + branch
            blk = blk * gate_ref[c:c + 1, qb * Q_BLOCK:(qb + 1) * Q_BLOCK]
        outs.append(blk.T)
    return jnp.concatenate(outs, axis=1)


def _banded_kernel(*refs, window, has_sink, gate_branch):
    refs = list(refs)
    sink_ref = refs.pop(0) if has_sink else None
    qt_ref, k_ref, vt_ref = refs[:3]
    gate_ref = refs[3] if gate_branch is not None else None
    o_ref = refs[-1]
    g = pl.program_id(1)
    rq = GROUP_R * Q_BLOCK
    n_sub = window // Q_BLOCK + 1
    span = n_sub * Q_BLOCK
    if has_sink:
        lrow = lax.broadcasted_iota(jnp.int32, (1, rq), 1)
        sink = jnp.zeros((1, rq), F32)
        for r in range(GROUP_R):
            sink = jnp.where((lrow >= r * Q_BLOCK) & (lrow < (r + 1) * Q_BLOCK),
                             sink_ref[g * GROUP_R + r], sink)
    for qb in range(BANDED_QB):
        i = pl.program_id(2) * BANDED_QB + qb
        start = pl.multiple_of(jnp.maximum(i * Q_BLOCK - window, 0), Q_BLOCK)
        st = _dot(k_ref[pl.ds(start, span), :], _padded_q(qt_ref, g, qb))
        kpos = start + lax.broadcasted_iota(jnp.int32, (span, rq), 0)
        lane = lax.broadcasted_iota(jnp.int32, (span, rq), 1)
        diff = (i * Q_BLOCK + (lane & (Q_BLOCK - 1))) - kpos
        st = jnp.where((diff >= 0) & (diff < window), st, NEG_INF)
        m = jnp.max(st, axis=0, keepdims=True)
        if has_sink:
            m = jnp.maximum(m, sink)
        m = jnp.where(m == NEG_INF, 0.0, m)
        p = jnp.exp(st - m)
        denom = jnp.sum(p, axis=0, keepdims=True)
        if has_sink:
            denom = denom + jnp.exp(sink - m)
        pb = p.astype(BF16)
        u0 = start // Q_BLOCK
        o_t = _dot(vt_ref[u0], pb[0:Q_BLOCK])
        for u in range(1, n_sub):
            o_t = o_t + _dot(vt_ref[u0 + u], pb[u * Q_BLOCK:(u + 1) * Q_BLOCK])
        o_t = o_t / jnp.maximum(denom, 1e-30)
        o_ref[qb * Q_BLOCK:(qb + 1) * Q_BLOCK, :] = _gate_untranspose(o_t, gate_ref, gate_branch, qb)


def _banded(qt, k_rows, vt, window, sinks=None, gates=None, gate_branch=None):
    b, h, d, t = qt.shape
    g = h // GROUP_R
    in_specs = []
    args = []
    if sinks is not None:
        in_specs.append(pl.BlockSpec(memory_space=pltpu.SMEM))
        args.append(sinks.astype(F32))
    qw = BANDED_QB * Q_BLOCK
    in_specs += [
        pl.BlockSpec((None, GROUP_R, d, qw), lambda bi, gi, i: (bi, gi, 0, i)),
        pl.BlockSpec((None, t, KV_HEADS * d), lambda bi, gi, i: (bi, 0, 0)),
        pl.BlockSpec((None, None, t // Q_BLOCK, d, Q_BLOCK), lambda bi, gi, i: (bi, gi, 0, 0, 0)),
    ]
    args += [qt, k_rows, vt]
    if gates is not None:
        in_specs.append(pl.BlockSpec((None, None, N_GATE_ROWS, qw),
                                     lambda bi, gi, i: (bi, gi, 0, i)))
        args.append(gates)
    return pl.pallas_call(
        functools.partial(_banded_kernel, window=window, has_sink=sinks is not None,
                          gate_branch=gate_branch if gates is not None else None),
        grid=(b, g, t // qw),
        in_specs=in_specs,
        out_specs=pl.BlockSpec((None, qw, GROUP_R * d), lambda bi, gi, i: (bi, i, gi)),
        out_shape=jax.ShapeDtypeStruct((b, t, h * d), F32),
        compiler_params=_params(("parallel", "parallel", "arbitrary"), VMEM_LIMIT),
        name="banded_attn_w%d" % window,
    )(*args)


def _conv_kernel(x_ref, b_ref, c_ref, xp_ref, cp_ref, w_ref, o_ref):
    ti = pl.program_id(1)
    z = c_ref[...] * x_ref[...]
    zp = jnp.where(ti > 0, cp_ref[...] * xp_ref[...], 0.0)
    row = lax.broadcasted_iota(jnp.int32, z.shape, 0)
    z1 = jnp.where(row == 0, zp[7:8, :], pltpu.roll(z, 1, 0))
    z2 = pltpu.roll(z, 2, 0)
    z2 = jnp.where(row == 0, zp[6:7, :], jnp.where(row == 1, zp[7:8, :], z2))
    w = w_ref[...]
    o_ref[...] = b_ref[...] * (w[0:1, :] * z2 + w[1:2, :] * z1 + w[2:3, :] * z)


def _conv(proj_b, conv_w, tt=512):
    b, t, _ = proj_b.shape
    w = MIX_W
    hb = tt // 8
    prev = lambda bi, ti: (bi, jnp.maximum(ti * hb - 1, 0), 0)
    prev_c = lambda bi, ti: (bi, jnp.maximum(ti * hb - 1, 0), 2)
    return pl.pallas_call(
        _conv_kernel,
        grid=(b, t // tt),
        in_specs=[
            pl.BlockSpec((None, tt, w), lambda bi, ti: (bi, ti, 0)),
            pl.BlockSpec((None, tt, w), lambda bi, ti: (bi, ti, 1)),
            pl.BlockSpec((None, tt, w), lambda bi, ti: (bi, ti, 2)),
            pl.BlockSpec((None, 8, w), prev),
            pl.BlockSpec((None, 8, w), prev_c),
            pl.BlockSpec((8, w), lambda bi, ti: (0, 0)),
        ],
        out_specs=pl.BlockSpec((None, tt, w), lambda bi, ti: (bi, ti, 0)),
        out_shape=jax.ShapeDtypeStruct((b, t, w), F32),
        compiler_params=_params(("parallel", "parallel")),
        name="short_conv",
    )(proj_b, proj_b, proj_b, proj_b, proj_b,
      jnp.pad(conv_w.reshape(conv_w.shape[0], w).astype(F32), ((0, 8 - conv_w.shape[0]), (0, 0))))


def _gelu_tanh(x):
    return x * (0.5 * (1.0 + jnp.tanh(0.7978845608028654 * (x + 0.044715 * (x * x * x)))))


def _compress_kernel(tk_ref, tv_ref, pek_ref, pev_ref, wk1_ref, wk2_ref, wv1_ref, wv2_ref,
                     kg_ref, ko_ref, vo_ref):
    nrow = tk_ref.shape[0]

    def mlp(a, pe_ref, w1_ref, w2_ref):
        a0 = (a + pe_ref[0:1, :]).astype(BF16)
        a1 = (a + pe_ref[1:2, :]).astype(BF16)
        p1 = _dot(a0, w1_ref[0])
        p2 = _dot(a1, w1_ref[1])
        hdn = p1 + pltpu.roll(p2, nrow - 1, 0)
        return _dot(_gelu_tanh(hdn).astype(BF16), w2_ref[...])

    kc = mlp(tk_ref[...], pek_ref, wk1_ref, wk2_ref)
    ko_ref[...] = _rms_rows(kc, kg_ref[...]).astype(BF16)
    vo_ref[...] = mlp(tv_ref[...], pev_ref, wv1_ref, wv2_ref).astype(BF16)


def _compress(kc_rows, vc_rows, pos_k, pos_v, wk1, wk2, wv1, wv2, k_gain):
    b, t, kvd = kc_rows.shape
    kv = KV_HEADS
    d = kvd // kv
    nrow = t // CMP_STRIDE
    wide = CMP_STRIDE * kvd
    hid = wk1.shape[1]

    def expand_w1(w1):
        w = w1.reshape(2, CMP_STRIDE, 1, d, hid)
        per_head = []
        for g in range(kv):
            pads = [jnp.zeros_like(w)] * kv
            pads[g] = w
            per_head.append(jnp.concatenate(pads, axis=2).reshape(2, wide, hid))
        return jnp.stack(per_head).astype(BF16)

    def expand_pe(pe):
        return jnp.broadcast_to(pe.reshape(2, CMP_STRIDE, 1, d), (2, CMP_STRIDE, kv, d)).reshape(2, wide)

    tok = pl.BlockSpec((None, nrow, wide), lambda bi, gi: (bi, 0, 0))
    full = lambda shape: pl.BlockSpec(shape, lambda bi, gi: tuple(0 for _ in shape))
    w1_spec = pl.BlockSpec((None, 2, wide, hid), lambda bi, gi: (gi, 0, 0, 0))
    out = pl.BlockSpec((None, nrow, d), lambda bi, gi: (bi * kv + gi, 0, 0))
    return pl.pallas_call(
        _compress_kernel,
        grid=(b, kv),
        in_specs=[tok, tok, full((2, wide)), full((2, wide)), w1_spec, full((hid, d)),
                  w1_spec, full((hid, d)), full((1, d))],
        out_specs=[out, out],
        out_shape=[jax.ShapeDtypeStruct((b * kv, nrow, d), BF16),
                   jax.ShapeDtypeStruct((b * kv, nrow, d), BF16)],
        compiler_params=_params(("parallel", "arbitrary"), VMEM_LIMIT),
        name="nsa_compress",
    )(kc_rows.reshape(b, nrow, wide), vc_rows.reshape(b, nrow, wide), expand_pe(pos_k),
      expand_pe(pos_v), expand_w1(wk1), wk2.astype(BF16), expand_w1(wv1), wv2.astype(BF16),
      k_gain.reshape(1, d))


def _cmp_attn_kernel(qt_ref, kc_ref, vct_ref, gate_ref, o_ref, selt_ref, imp_ref, *, sel_k):
    step = pl.program_id(2)
    rq = GROUP_R * Q_BLOCK
    nc = kc_ref.shape[0]
    ns = selt_ref.shape[0]
    blk = lax.broadcasted_iota(jnp.int32, (ns, Q_BLOCK), 0)

    def attend(nc_eff):
        for qb in range(CMP_QB):
            i = step * CMP_QB + qb
            st = _dot(kc_ref[0:nc_eff, :], _group_q(qt_ref, qb))
            n = lax.broadcasted_iota(jnp.int32, (nc_eff, rq), 0)
            lane = lax.broadcasted_iota(jnp.int32, (nc_eff, rq), 1)
            qpos = i * Q_BLOCK + (lane & (Q_BLOCK - 1))
            st = jnp.where(n * CMP_STRIDE + (CMP_BLOCK - 1) <= qpos, st, NEG_INF)
            m = jnp.max(st, axis=0, keepdims=True)
            m = jnp.where(m == NEG_INF, 0.0, m)
            p = jnp.exp(st - m)
            denom = jnp.sum(p, axis=0, keepdims=True)
            pb = (p / jnp.maximum(denom, 1e-30)).astype(BF16)
            o_t = _dot(vct_ref[:, 0:nc_eff], pb)
            o_ref[qb * Q_BLOCK:(qb + 1) * Q_BLOCK, :] = _gate_untranspose(o_t, gate_ref, 0, qb)
            ss = lax.broadcasted_iota(jnp.int32, (ns, nc_eff), 0) * SEL_BLOCK
            nn = lax.broadcasted_iota(jnp.int32, (ns, nc_eff), 1) * CMP_STRIDE
            overlap_t = jnp.where((nn < ss + SEL_BLOCK) & (nn + (CMP_BLOCK - 1) >= ss),
                                  1.0, 0.0).astype(BF16)
            imp_r = _dot(overlap_t, pb)
            imp = imp_r[:, 0:Q_BLOCK]
            for r in range(1, GROUP_R):
                imp = imp + imp_r[:, r * Q_BLOCK:(r + 1) * Q_BLOCK]
            qp = i * Q_BLOCK + lax.broadcasted_iota(jnp.int32, (ns, Q_BLOCK), 1)
            cur = qp // SEL_BLOCK
            forced = (blk == 0) | (blk == cur) | (blk == cur - 1)
            imp = jnp.where(forced, jnp.inf, imp)
            imp_ref[qb] = jnp.where(blk <= cur, imp, NEG_INF)

    variant_rows = 128
    n_var = nc // variant_rows
    n_ending = (step + 1) * (CMP_QB * Q_BLOCK // CMP_STRIDE) - 1
    variant = jnp.minimum((n_ending + variant_rows - 1) // variant_rows, n_var) - 1
    for v in range(n_var):
        pl.when(variant == v)(functools.partial(attend, (v + 1) * variant_rows))

    blkf = blk.astype(F32)
    imps = [imp_ref[qb] for qb in range(CMP_QB)]
    sels = [jnp.zeros((ns, Q_BLOCK), F32) for _ in range(CMP_QB)]
    for _ in range(sel_k):
        for qb in range(CMP_QB):
            mx = jnp.max(imps[qb], axis=0, keepdims=True)
            first = jnp.min(jnp.where(imps[qb] == mx, blkf, float(ns)), axis=0, keepdims=True)
            hit = blkf == first
            sels[qb] = jnp.where(hit & (mx > NEG_INF), 1.0, sels[qb])
            imps[qb] = jnp.where(hit, NEG_INF, imps[qb])
    for qb in range(CMP_QB):
        selt_ref[:, qb * Q_BLOCK:(qb + 1) * Q_BLOCK] = sels[qb]


def _cmp_attn(qt, k_cmp, v_cmp_t, gates):
    b, h, d, t = qt.shape
    g = h // GROUP_R
    nc = k_cmp.shape[1]
    ns = t // SEL_BLOCK
    qw = CMP_QB * Q_BLOCK
    assert nc % 128 == 0
    return pl.pallas_call(
        functools.partial(_cmp_attn_kernel, sel_k=min(SEL_TOPK, ns)),
        grid=(b, g, t // qw),
        in_specs=[
            pl.BlockSpec((None, GROUP_R, d, qw), lambda bi, gi, i: (bi, gi, 0, i)),
            pl.BlockSpec((None, nc, d), lambda bi, gi, i: (bi * KV_HEADS + gi, 0, 0)),
            pl.BlockSpec((None, d, nc), lambda bi, gi, i: (bi * KV_HEADS + gi, 0, 0)),
            pl.BlockSpec((None, None, N_GATE_ROWS, qw), lambda bi, gi, i: (bi, gi, 0, i)),
        ],
        out_specs=[
            pl.BlockSpec((None, qw, GROUP_R * d), lambda bi, gi, i: (bi, i, gi)),
            pl.BlockSpec((None, None, ns, qw), lambda bi, gi, i: (bi, gi, 0, i)),
        ],
        out_shape=[jax.ShapeDtypeStruct((b, t, h * d), F32),
                   jax.ShapeDtypeStruct((b, g, ns, t), F32)],
        scratch_shapes=[pltpu.VMEM((CMP_QB, ns, Q_BLOCK), F32)],
        compiler_params=_params(("parallel", "parallel", "arbitrary"), VMEM_LIMIT),
        name="nsa_cmp_topk",
    )(qt, k_cmp, v_cmp_t, gates)


def _sel_attn_kernel(qt_ref, ks_ref, vst_ref, selt_ref, gate_ref, o_ref, sa_ref, sb_ref, pa_ref,
                     pb_ref, ala_ref, alb_ref, mta_ref, mtb_ref, m_ref, acc_ref):
    g = pl.program_id(1)
    i = pl.program_id(2)
    tk = SEL_KEY_TILE
    bpt = tk // SEL_BLOCK
    spt = tk // Q_BLOCK
    d = HEAD_DIM
    m_ref[...] = jnp.full(m_ref.shape, NEG_INF, F32)
    acc_ref[...] = jnp.zeros_like(acc_ref)
    n_tiles = (i * Q_BLOCK + Q_BLOCK + tk - 1) // tk
    qpad = _padded_q(qt_ref, g)
    tri = jnp.where(lax.broadcasted_iota(jnp.int32, (Q_BLOCK, Q_BLOCK), 0)
                    <= lax.broadcasted_iota(jnp.int32, (Q_BLOCK, Q_BLOCK), 1), 0.0, NEG_INF)

    def scores(j, s_ref, mt_ref):
        k0 = pl.multiple_of(j * tk, tk)
        st = _dot(ks_ref[pl.ds(k0, tk), :], qpad)
        sel_rows = selt_ref[pl.ds(pl.multiple_of(j * bpt, bpt), bpt), :]
        bias_rows = jnp.where(sel_rows > 0.5, 0.0, NEG_INF)
        parts = []
        for u in range(spt):
            sub = jnp.concatenate(
                [jnp.broadcast_to(bias_rows[s:s + 1, :], (SEL_BLOCK, Q_BLOCK))
                 for s in range(u * Q_BLOCK // SEL_BLOCK, (u + 1) * Q_BLOCK // SEL_BLOCK)], axis=0)
            parts.append(sub + jnp.where(j * spt + u == i, tri, 0.0))
        bias = jnp.concatenate(parts, axis=0)
        st = st + jnp.concatenate([bias] * GROUP_R, axis=1)
        s_ref[...] = st
        mt_ref[...] = jnp.max(st, axis=0, keepdims=True)

    def softmax(s_ref, mt_ref, p_ref, al_ref):
        m_prev = m_ref[...]
        m_new = jnp.maximum(m_prev, mt_ref[...])
        m_safe = jnp.where(m_new == NEG_INF, 0.0, m_new)
        al_ref[...] = jnp.exp2(m_prev - m_safe)
        p_ref[...] = jnp.exp2((s_ref[...] - m_safe).astype(BF16))
        m_ref[...] = m_new

    def weighted_sum(j, p_ref, al_ref):
        acc_ref[...] = al_ref[...] * acc_ref[...] + _dot(vst_ref[j], p_ref[...])

    pb_ref[...] = jnp.zeros_like(pb_ref)
    alb_ref[...] = jnp.ones_like(alb_ref)
    scores(0, sa_ref, mta_ref)
    last_tile = ks_ref.shape[0] // tk - 1

    def step(jj, carry):
        j = 2 * jj
        weighted_sum(jnp.maximum(j - 1, 0), pb_ref, alb_ref)
        scores(j + 1, sb_ref, mtb_ref)
        softmax(sa_ref, mta_ref, pa_ref, ala_ref)
        weighted_sum(j, pa_ref, ala_ref)
        scores(jnp.minimum(j + 2, last_tile), sa_ref, mta_ref)
        softmax(sb_ref, mtb_ref, pb_ref, alb_ref)
        return carry

    n_pairs = (n_tiles + 1) // 2
    lax.fori_loop(0, n_pairs, step, 0)
    weighted_sum(2 * n_pairs - 1, pb_ref, alb_ref)

    o_t = acc_ref[0:d, :] / jnp.maximum(acc_ref[d:d + 1, :], 1e-30)
    o_ref[...] = _gate_untranspose(o_t, gate_ref, 1)


def _sel_attn(qt, ks_rows, vs_t, sel_t, gates):
    b, h, d, t = qt.shape
    g = h // GROUP_R
    ns = sel_t.shape[2]
    tk = SEL_KEY_TILE
    rq = GROUP_R * Q_BLOCK
    assert (t // tk) % 2 == 0
    return pl.pallas_call(
        _sel_attn_kernel,
        grid=(b, g, t // Q_BLOCK),
        in_specs=[
            pl.BlockSpec((None, GROUP_R, d, Q_BLOCK), lambda bi, gi, i: (bi, gi, 0, i)),
            pl.BlockSpec((None, t, KV_HEADS * d), lambda bi, gi, i: (bi, 0, 0)),
            pl.BlockSpec((None, None, t // tk, SEL_V_ROWS, tk), lambda bi, gi, i: (bi, gi, 0, 0, 0)),
            pl.BlockSpec((None, None, ns, Q_BLOCK), lambda bi, gi, i: (bi, gi, 0, i)),
            pl.BlockSpec((None, None, N_GATE_ROWS, Q_BLOCK), lambda bi, gi, i: (bi, gi, 0, i)),
        ],
        out_specs=pl.BlockSpec((None, Q_BLOCK, GROUP_R * d), lambda bi, gi, i: (bi, i, gi)),
        out_shape=jax.ShapeDtypeStruct((b, t, h * d), F32),
        scratch_shapes=[pltpu.VMEM((tk, rq), F32), pltpu.VMEM((tk, rq), F32),
                        pltpu.VMEM((tk, rq), BF16), pltpu.VMEM((tk, rq), BF16),
                        pltpu.VMEM((1, rq), F32), pltpu.VMEM((1, rq), F32),
                        pltpu.VMEM((1, rq), F32), pltpu.VMEM((1, rq), F32),
                        pltpu.VMEM((1, rq), F32), pltpu.VMEM((SEL_V_ROWS, rq), F32)],
        compiler_params=_params(("parallel", "parallel", "arbitrary"), VMEM_LIMIT),
        name="nsa_selected",
    )(qt, ks_rows, vs_t, sel_t, gates)


def _retention_kernel(lg_ref, q_ref, k_ref, v_ref, gt_ref, cos_ref, sin_ref, ng_ref, o_ref, r_ref):
    ci = pl.program_id(1)
    c = RET_CHUNK
    dk, dv = RET_QK_DIM, RET_V_DIM
    half = dk // 2

    @pl.when(ci == 0)
    def _():
        r_ref[...] = jnp.zeros_like(r_ref)

    cosf = cos_ref[...]
    sinf = sin_ref[...]

    def rot(x):
        return x * cosf + jnp.concatenate([x[:, half:], x[:, :half]], axis=1) * sinf

    ii = lax.broadcasted_iota(jnp.int32, (c, c), 0)
    jj = lax.broadcasted_iota(jnp.int32, (c, c), 1)
    d = (ii - jj).astype(F32)
    jcol = lax.broadcasted_iota(jnp.int32, (c, 1), 0).astype(F32)
    for hh in range(RET_HEADS):
        lg = lg_ref[hh]
        q = rot(q_ref[:, hh * dk:(hh + 1) * dk])
        k = rot(k_ref[:, hh * dk:(hh + 1) * dk]) * (dk ** -0.5)
        vb = v_ref[:, hh * dv:(hh + 1) * dv].astype(BF16)
        dmask = jnp.where(d >= 0, jnp.exp(d * lg), 0.0)
        att = _dot_nt(q.astype(BF16), k.astype(BF16)) * dmask
        o = _dot(att.astype(BF16), vb)
        xi = jnp.exp((jcol + 1.0) * lg)
        zeta = jnp.exp((c - 1.0 - jcol) * lg)
        r_prev = r_ref[hh]
        o = o + _dot((q * xi).astype(BF16), r_prev.astype(BF16))
        s_chunk = _dot_tn((k * zeta).astype(BF16), vb)
        decay = jnp.exp(jnp.zeros((1, dv), F32) + c * lg)
        r_ref[hh] = r_prev * decay + s_chunk
        mu = jnp.mean(o, axis=-1, keepdims=True)
        var = jnp.mean(jnp.square(o - mu), axis=-1, keepdims=True)
        on = (o - mu) * lax.rsqrt(var + EPS)
        gt = gt_ref[:, hh * dv:(hh + 1) * dv]
        o_ref[:, hh * dv:(hh + 1) * dv] = (gt * _sigmoid(gt)) * (on * ng_ref[:, hh * dv:(hh + 1) * dv])


def _retention(proj_d, norm_gain):
    b, t, _ = proj_d.shape
    h, dk = RET_HEADS, RET_QK_DIM
    c = RET_CHUNK
    dv = RET_V_DIM
    assert 2 * h * dk == h * dv
    half = dk // 2
    inv = ROPE_BASE ** (-jnp.arange(half, dtype=F32) / half)
    ang = jnp.arange(t).astype(F32)[:, None] * inv[None, :]
    cos = jnp.cos(ang)
    sin = jnp.sin(ang)
    cosf = jnp.concatenate([cos, cos], axis=-1)
    sinf = jnp.concatenate([-sin, sin], axis=-1)
    log_gamma = jnp.log(1.0 - 2.0 ** (-5.0 - jnp.arange(h, dtype=F32)))
    return pl.pallas_call(
        _retention_kernel,
        grid=(b, t // c),
        in_specs=[
            pl.BlockSpec(memory_space=pltpu.SMEM),
            pl.BlockSpec((None, c, h * dk), lambda bi, ci: (bi, ci, 0)),
            pl.BlockSpec((None, c, h * dk), lambda bi, ci: (bi, ci, 1)),
            pl.BlockSpec((None, c, h * dv), lambda bi, ci: (bi, ci, 1)),
            pl.BlockSpec((None, c, h * dv), lambda bi, ci: (bi, ci, 2)),
            pl.BlockSpec((c, dk), lambda bi, ci: (ci, 0)),
            pl.BlockSpec((c, dk), lambda bi, ci: (ci, 0)),
            pl.BlockSpec((1, h * dv), lambda bi, ci: (0, 0)),
        ],
        out_specs=pl.BlockSpec((None, c, h * dv), lambda bi, ci: (bi, ci, 0)),
        out_shape=jax.ShapeDtypeStruct((b, t, h * dv), F32),
        scratch_shapes=[pltpu.VMEM((h, dk, dv), F32)],
        compiler_params=_params(("parallel", "arbitrary")),
        name="retention",
    )(log_gamma, proj_d, proj_d, proj_d, proj_d, cosf, sinf, norm_gain.reshape(1, h * dv))


def _merge_kernel(x_ref, g_ref, wgate_ref, bias_ref, ya_ref, yb_ref, yc0_ref, yc1_ref, yc2_ref,
                  yd_ref, wb_ref, wo_ref, o_ref):
    x = x_ref[...]
    d = x.shape[1]
    u = _rms_rows(x, g_ref[...]).astype(BF16)
    ys = (ya_ref[...], yb_ref[...], yc0_ref[...] + yc1_ref[...] + yc2_ref[...], yd_ref[...])
    merged = jnp.zeros(x.shape, F32)
    for n in range(N_BRANCH):
        logits = _dot(u, wgate_ref[:, n * d:(n + 1) * d]) + bias_ref[:, n * d:(n + 1) * d]
        merged = merged + _sigmoid(logits) * _dot(ys[n].astype(BF16), wb_ref[n])
    o_ref[...] = x + _dot(merged.astype(BF16), wo_ref[...])


def _merge(x2, g, w_gate, bias, ys, w_branch, w_out, tm=256):
    n, d = x2.shape
    w = MIX_W
    row = lambda width: pl.BlockSpec((tm, width), lambda i: (i, 0))
    full = lambda shape: pl.BlockSpec(shape, lambda i: tuple(0 for _ in shape))
    return pl.pallas_call(
        _merge_kernel,
        grid=(n // tm,),
        in_specs=[row(d), full((1, d)), full((d, N_BRANCH * d)), full((1, N_BRANCH * d))]
        + [row(w)] * 6 + [full((N_BRANCH, w, d)), full((d, d))],
        out_specs=row(d),
        out_shape=jax.ShapeDtypeStruct((n, d), F32),
        compiler_params=_params(("parallel",), VMEM_LIMIT),
        name="merge_out",
    )(x2, g.reshape(1, d), w_gate.astype(BF16), bias.reshape(1, N_BRANCH * d),
      *[y.reshape(n, w) for y in ys], w_branch.astype(BF16), w_out.astype(BF16))


def _mixers(x2, b, t, mix_norm, w_in, merge_gate_bias, swa_q_gain, swa_k_gain, swa_sinks, conv_w,
            nsa_q_gain, nsa_k_gain, cmp_pos_k, cmp_pos_v, cmp_wk1, cmp_wk2, cmp_wv1, cmp_wv2,
            ret_norm_gain, w_branch, w_out):
    hd = HEAD_DIM
    swa_q, swa_kv = SWA_HEADS * hd, SWA_KV_HEADS * hd
    nsa_q, nsa_kv = NSA_HEADS * hd, NSA_KV_HEADS * hd
    ret_qk, ret_v = RET_HEADS * RET_QK_DIM, RET_HEADS * RET_V_DIM
    n_gate = NSA_HEADS * 3
    o_a = 0
    o_b = o_a + swa_q + 2 * swa_kv
    o_c = o_b + 3 * MIX_W
    o_cg = o_c + nsa_q + 6 * nsa_kv
    o_d = o_cg + n_gate
    o_g = o_d + 2 * ret_qk + 2 * ret_v

    per_g = GROUP_R * 3
    w_gates = [jnp.pad(w_in[:, o_cg + gi * per_g:o_cg + (gi + 1) * per_g],
                       ((0, 0), (0, N_GATE_ROWS - per_g))) for gi in range(NSA_KV_HEADS)]
    w_t = jnp.concatenate([w_in[:, o_a:o_b], w_in[:, o_c:o_cg]] + w_gates, axis=1).T.astype(BF16)
    head_gains = jnp.stack([swa_q_gain, swa_k_gain, nsa_q_gain, nsa_k_gain[1], nsa_k_gain[2]])
    head_gains = jnp.broadcast_to(head_gains.astype(F32)[:, :, None], (5, hd, 128))
    (a_qt, a_k, a_vt, c_qt, c_qt2, c_kc, c_vc, c_ks, c_vst, c_kw, c_vwt, gates_c) = _proj_heads(
        x2, b, t, mix_norm, w_t, head_gains)
    proj_b = _norm_matmul(x2, mix_norm, w_in[:, o_b:o_c])
    proj_d = _norm_matmul(x2, mix_norm, w_in[:, o_d:o_g])

    y_a = _banded(a_qt, a_k, a_vt, SWA_WINDOW, sinks=swa_sinks)

    y_b = _conv(proj_b.reshape(b, t, 3 * MIX_W), conv_w)

    k_cmp, v_cmp = _compress(c_kc, c_vc, cmp_pos_k, cmp_pos_v, cmp_wk1, cmp_wk2, cmp_wv1, cmp_wv2,
                             nsa_k_gain[0])
    y_cmp, sel_t = _cmp_attn(c_qt, k_cmp, v_cmp.transpose(0, 2, 1), gates_c)
    y_sel = _sel_attn(c_qt2, c_ks, c_vst, sel_t, gates_c)
    y_win = _banded(c_qt, c_kw, c_vwt, NSA_WINDOW, gates=gates_c, gate_branch=2)

    y_d = _retention(proj_d.reshape(b, t, 2 * ret_qk + 2 * ret_v), ret_norm_gain)

    return _merge(x2, mix_norm, w_in[:, o_g:], merge_gate_bias,
                  (y_a, y_b, y_cmp, y_sel, y_win, y_d), w_branch, w_out)


def kernel(x, ffn1_norm, ffn1_w_gate, ffn1_w_up, ffn1_w_down, mix_norm, w_in, merge_gate_bias, swa_q_gain, swa_k_gain, swa_sinks, conv_w, nsa_q_gain, nsa_k_gain, cmp_pos_k, cmp_pos_v, cmp_wk1, cmp_wk2, cmp_wv1, cmp_wv2, ret_norm_gain, w_branch, w_out, ffn2_norm, ffn2_w_gate, ffn2_w_up, ffn2_w_down):
    b, t, d = x.shape
    x2 = x.reshape(b * t, d)
    for l in range(ffn1_norm.shape[0]):
        x2 = _ffn(x2, ffn1_norm[l], ffn1_w_gate[l], ffn1_w_up[l], ffn1_w_down[l])
        x2 = _mixers(x2, b, t, mix_norm[l], w_in[l], merge_gate_bias[l], swa_q_gain[l],
                     swa_k_gain[l], swa_sinks[l], conv_w[l], nsa_q_gain[l], nsa_k_gain[l],
                     cmp_pos_k[l], cmp_pos_v[l], cmp_wk1[l], cmp_wk2[l], cmp_wv1[l], cmp_wv2[l],
                     ret_norm_gain[l], w_branch[l], w_out[l])
        x2 = _ffn(x2, ffn2_norm[l], ffn2_w_gate[l], ffn2_w_up[l], ffn2_w_down[l])
    return x2.reshape(b, t, d)
```

```python
import functools

import jax
import jax.numpy as jnp
from jax import lax
from jax.experimental import pallas as pl
from jax.experimental.pallas import tpu as pltpu

F32 = jnp.float32
BF16 = jnp.bfloat16

HEAD_DIM = 64
Q_BLOCK = 128
MIX_W = 512
N_BRANCH = 4
SWA_HEADS = 8
SWA_KV_HEADS = 2
SWA_WINDOW = 128
NSA_HEADS = 8
NSA_KV_HEADS = 2
CMP_BLOCK = 32
CMP_STRIDE = 16
SEL_BLOCK = 64
SEL_TOPK = 16
NSA_WINDOW = 512
RET_HEADS = 4
RET_QK_DIM = 64
RET_V_DIM = 128
RET_CHUNK = 128
ROPE_BASE = 10000.0
EPS = 1e-6
GROUP_R = 4
KV_HEADS = 2
N_GATE_ROWS = 16
SEL_KEY_TILE = 512
SEL_V_ROWS = HEAD_DIM + 16
PROJ_TM = 512
BANDED_QB = 4
CMP_QB = 4
VMEM_LIMIT = 52 * 1024 * 1024

NEG_INF = float("-inf")
LOG2_E = 1.4426950408889634


def _params(sem, vmem=None):
    return pltpu.CompilerParams(dimension_semantics=sem, vmem_limit_bytes=vmem)


def _sigmoid(x):
    return 1.0 / (1.0 + jnp.exp(-x))


def _rms_rows(x, g):
    return x * lax.rsqrt(jnp.mean(x * x, axis=-1, keepdims=True) + EPS) * g


def _dot(a, b):
    return jnp.dot(a, b, preferred_element_type=F32)


def _dot_nt(a, b):
    return lax.dot_general(a, b, (((1,), (1,)), ((), ())), preferred_element_type=F32)


def _dot_tn(a, b):
    return lax.dot_general(a, b, (((0,), (0,)), ((), ())), preferred_element_type=F32)


def _ffn_kernel(x_ref, g_ref, wg_ref, wu_ref, wd_ref, o_ref, *, n_chunks):
    x = x_ref[...]
    xn = _rms_rows(x, g_ref[...]).astype(BF16)
    tf = wg_ref.shape[1] // n_chunks
    acc = None
    for f in range(n_chunks):
        a = _dot(xn, wg_ref[:, f * tf:(f + 1) * tf])
        b = _dot(xn, wu_ref[:, f * tf:(f + 1) * tf])
        h = ((a * _sigmoid(a)) * b).astype(BF16)
        part = _dot(h, wd_ref[f * tf:(f + 1) * tf, :])
        acc = part if acc is None else acc + part
    o_ref[...] = x + 0.5 * acc


def _ffn(x2, g, wg, wu, wd, tm=512, n_chunks=2):
    n, d = x2.shape
    dff = wg.shape[1]
    assert dff % (n_chunks * 128) == 0
    resident = lambda shape: pl.BlockSpec(shape, lambda i: (0, 0), pipeline_mode=pl.Buffered(1))
    return pl.pallas_call(
        functools.partial(_ffn_kernel, n_chunks=n_chunks),
        grid=(n // tm,),
        in_specs=[
            pl.BlockSpec((tm, d), lambda i: (i, 0)),
            pl.BlockSpec((1, d), lambda i: (0, 0)),
            resident((d, dff)), resident((d, dff)), resident((dff, d)),
        ],
        out_specs=pl.BlockSpec((tm, d), lambda i: (i, 0)),
        out_shape=jax.ShapeDtypeStruct((n, d), F32),
        compiler_params=_params(("parallel",), VMEM_LIMIT),
        name="ffn",
    )(x2, g.reshape(1, d), wg.astype(BF16), wu.astype(BF16), wd.astype(BF16))


def _norm_matmul_kernel(x_ref, g_ref, w_ref, o_ref):
    xn = _rms_rows(x_ref[...], g_ref[...]).astype(BF16)
    o_ref[...] = _dot(xn, w_ref[...])


def _norm_matmul(x2, g, w, tm=512):
    n, d = x2.shape
    c = w.shape[1]
    return pl.pallas_call(
        _norm_matmul_kernel,
        grid=(n // tm,),
        in_specs=[
            pl.BlockSpec((tm, d), lambda i: (i, 0)),
            pl.BlockSpec((1, d), lambda i: (0, 0)),
            pl.BlockSpec((d, c), lambda i: (0, 0), pipeline_mode=pl.Buffered(1)),
        ],
        out_specs=pl.BlockSpec((tm, c), lambda i: (i, 0)),
        out_shape=jax.ShapeDtypeStruct((n, c), F32),
        compiler_params=_params(("parallel",), VMEM_LIMIT),
        name="in_proj",
    )(x2, g.reshape(1, d), w.astype(BF16))


def _proj_heads_kernel(x_ref, g_ref, wt_ref, hg_ref, aq_ref, ak_ref, av_ref, cq_ref, cq2_ref,
                       ckc_ref, cvc_ref, cks_ref, cvs_ref, ckw_ref, cvw_ref, gt_ref):
    d = HEAD_DIM
    xn = _rms_rows(x_ref[...], g_ref[...]).astype(BF16)
    acc = _dot_nt(wt_ref[...], xn)
    tm = xn.shape[0]
    lane_tiles = tm // 128
    scale = d ** -0.5
    kv_w = KV_HEADS * d

    def head_norm(row0, gain_idx, mult):
        hb = acc[row0:row0 + d]
        gain = jnp.concatenate([hg_ref[gain_idx]] * lane_tiles, axis=1)
        y = hb * lax.rsqrt(jnp.mean(hb * hb, axis=0, keepdims=True) + EPS) * gain
        return y * mult if mult != 1.0 else y

    def q_heads(row0, gain_idx, out_ref, n_heads, log2_ref=None):
        for h in range(n_heads):
            y = head_norm(row0 + h * d, gain_idx, 1.0)
            out_ref[h] = (y * scale).astype(BF16)
            if log2_ref is not None:
                log2_ref[h] = (y * (scale * LOG2_E)).astype(BF16)

    def k_rows(row0, gain_idx):
        return jnp.concatenate([head_norm(row0 + g * d, gain_idx, 1.0) for g in range(KV_HEADS)],
                               axis=0).T

    def v_tiles(row0, out_ref):
        for g in range(KV_HEADS):
            for u in range(lane_tiles):
                out_ref[g, u] = acc[row0 + g * d:row0 + (g + 1) * d,
                                    u * 128:(u + 1) * 128].astype(BF16)

    row = 0
    q_heads(row, 0, aq_ref, SWA_HEADS)
    row += SWA_HEADS * d
    ak_ref[...] = k_rows(row, 1).astype(BF16)
    row += kv_w
    v_tiles(row, av_ref)
    row += kv_w
    q_heads(row, 2, cq_ref, NSA_HEADS, cq2_ref)
    row += NSA_HEADS * d
    ckc_ref[...] = acc[row:row + kv_w].T
    row += kv_w
    cvc_ref[...] = acc[row:row + kv_w].T
    row += kv_w
    cks_ref[...] = k_rows(row, 3).astype(BF16)
    row += kv_w
    for g in range(KV_HEADS):
        cvs_ref[g, 0, 0:d, :] = acc[row + g * d:row + (g + 1) * d].astype(BF16)
        cvs_ref[g, 0, d:SEL_V_ROWS, :] = jnp.ones((SEL_V_ROWS - d, tm), BF16)
    row += kv_w
    ckw_ref[...] = k_rows(row, 4).astype(BF16)
    row += kv_w
    v_tiles(row, cvw_ref)
    row += kv_w
    for g in range(KV_HEADS):
        gt_ref[g] = _sigmoid(acc[row + g * N_GATE_ROWS:row + (g + 1) * N_GATE_ROWS])


def _proj_heads(x2, b, t, g, w_t, head_gains):
    n, dm = x2.shape
    d = HEAD_DIM
    tm = PROJ_TM
    assert tm == SEL_KEY_TILE and t % tm == 0
    rows = w_t.shape[0]
    tpb = t // tm
    lt = tm // 128
    kv = KV_HEADS
    qt_spec = lambda h: pl.BlockSpec((None, h, d, tm), lambda bi, ti: (bi, 0, 0, ti))
    row_spec = pl.BlockSpec((None, tm, kv * d), lambda bi, ti: (bi, ti, 0))
    vt_spec = pl.BlockSpec((None, kv, lt, d, 128), lambda bi, ti: (bi, 0, ti, 0, 0))
    sds = jax.ShapeDtypeStruct
    return pl.pallas_call(
        _proj_heads_kernel,
        grid=(b, tpb),
        in_specs=[
            pl.BlockSpec((tm, dm), lambda bi, ti: (bi * tpb + ti, 0)),
            pl.BlockSpec((1, dm), lambda bi, ti: (0, 0)),
            pl.BlockSpec((rows, dm), lambda bi, ti: (0, 0), pipeline_mode=pl.Buffered(1)),
            pl.BlockSpec(head_gains.shape, lambda bi, ti: (0, 0, 0)),
        ],
        out_specs=[
            qt_spec(SWA_HEADS), row_spec, vt_spec,
            qt_spec(NSA_HEADS), qt_spec(NSA_HEADS), row_spec, row_spec, row_spec,
            pl.BlockSpec((None, kv, 1, SEL_V_ROWS, tm), lambda bi, ti: (bi, 0, ti, 0, 0)),
            row_spec, vt_spec,
            pl.BlockSpec((None, kv, N_GATE_ROWS, tm), lambda bi, ti: (bi, 0, 0, ti)),
        ],
        out_shape=[
            sds((b, SWA_HEADS, d, t), BF16), sds((b, t, kv * d), BF16),
            sds((b, kv, t // 128, d, 128), BF16),
            sds((b, NSA_HEADS, d, t), BF16), sds((b, NSA_HEADS, d, t), BF16),
            sds((b, t, kv * d), F32), sds((b, t, kv * d), F32),
            sds((b, t, kv * d), BF16), sds((b, kv, tpb, SEL_V_ROWS, tm), BF16),
            sds((b, t, kv * d), BF16), sds((b, kv, t // 128, d, 128), BF16),
            sds((b, kv, N_GATE_ROWS, t), F32),
        ],
        compiler_params=_params(("parallel", "parallel"), VMEM_LIMIT),
        name="proj_heads",
    )(x2, g.reshape(1, dm), w_t, head_gains)


def _group_q(qt_ref, qb=0):
    return jnp.concatenate(
        [qt_ref[r, :, qb * Q_BLOCK:(qb + 1) * Q_BLOCK] for r in range(GROUP_R)], axis=1)


def _padded_q(qt_ref, g, qb=0):
    q4 = _group_q(qt_ref, qb)
    z = jnp.zeros_like(q4)
    return jnp.where(g == 0, jnp.concatenate([q4, z], axis=0), jnp.concatenate([z, q4], axis=0))


def _gate_untranspose(o_t, gate_ref, branch, qb=0):
    outs = []
    for r in range(GROUP_R):
        blk = o_t[:, r * Q_BLOCK:(r + 1) * Q_BLOCK]
        if gate_ref is not None:
            c = r * 3 + branch
            blk = blk * gate_ref[c:c + 1, qb * Q_BLOCK:(qb + 1) * Q_BLOCK]
        outs.append(blk.T)
    return jnp.concatenate(outs, axis=1)


def _banded_kernel(*refs, window, has_sink, gate_branch):
    refs = list(refs)
    sink_ref = refs.pop(0) if has_sink else None
    qt_ref, k_ref, vt_ref = refs[:3]
    gate_ref = refs[3] if gate_branch is not None else None
    o_ref = refs[-1]
    g = pl.program_id(1)
    rq = GROUP_R * Q_BLOCK
    n_sub = window // Q_BLOCK + 1
    span = n_sub * Q_BLOCK
    if has_sink:
        lrow = lax.broadcasted_iota(jnp.int32, (1, rq), 1)
        sink = jnp.zeros((1, rq), F32)
        for r in range(GROUP_R):
            sink = jnp.where((lrow >= r * Q_BLOCK) & (lrow < (r + 1) * Q_BLOCK),
                             sink_ref[g * GROUP_R + r], sink)
    for qb in range(BANDED_QB):
        i = pl.program_id(2) * BANDED_QB + qb
        start = pl.multiple_of(jnp.maximum(i * Q_BLOCK - window, 0), Q_BLOCK)
        st = _dot(k_ref[pl.ds(start, span), :], _padded_q(qt_ref, g, qb))
        kpos = start + lax.broadcasted_iota(jnp.int32, (span, rq), 0)
        lane = lax.broadcasted_iota(jnp.int32, (span, rq), 1)
        diff = (i * Q_BLOCK + (lane & (Q_BLOCK - 1))) - kpos
        st = jnp.where((diff >= 0) & (diff < window), st, NEG_INF)
        m = jnp.max(st, axis=0, keepdims=True)
        if has_sink:
            m = jnp.maximum(m, sink)
        m = jnp.where(m == NEG_INF, 0.0, m)
        p = jnp.exp(st - m)
        denom = jnp.sum(p, axis=0, keepdims=True)
        if has_sink:
            denom = denom + jnp.exp(sink - m)
        pb = p.astype(BF16)
        u0 = start // Q_BLOCK
        o_t = _dot(vt_ref[u0], pb[0:Q_BLOCK])
        for u in range(1, n_sub):
            o_t = o_t + _dot(vt_ref[u0 + u], pb[u * Q_BLOCK:(u + 1) * Q_BLOCK])
        o_t = o_t / jnp.maximum(denom, 1e-30)
        o_ref[qb * Q_BLOCK:(qb + 1) * Q_BLOCK, :] = _gate_untranspose(o_t, gate_ref, gate_branch, qb)


def _banded(qt, k_rows, vt, window, sinks=None, gates=None, gate_branch=None):
    b, h, d, t = qt.shape
    g = h // GROUP_R
    in_specs = []
    args = []
    if sinks is not None:
        in_specs.append(pl.BlockSpec(memory_space=pltpu.SMEM))
        args.append(sinks.astype(F32))
    qw = BANDED_QB * Q_BLOCK
    in_specs += [
        pl.BlockSpec((None, GROUP_R, d, qw), lambda bi, gi, i: (bi, gi, 0, i)),
        pl.BlockSpec((None, t, KV_HEADS * d), lambda bi, gi, i: (bi, 0, 0)),
        pl.BlockSpec((None, None, t // Q_BLOCK, d, Q_BLOCK), lambda bi, gi, i: (bi, gi, 0, 0, 0)),
    ]
    args += [qt, k_rows, vt]
    if gates is not None:
        in_specs.append(pl.BlockSpec((None, None, N_GATE_ROWS, qw),
                                     lambda bi, gi, i: (bi, gi, 0, i)))
        args.append(gates)
    return pl.pallas_call(
        functools.partial(_banded_kernel, window=window, has_sink=sinks is not None,
                          gate_branch=gate_branch if gates is not None else None),
        grid=(b, g, t // qw),
        in_specs=in_specs,
        out_specs=pl.BlockSpec((None, qw, GROUP_R * d), lambda bi, gi, i: (bi, i, gi)),
        out_shape=jax.ShapeDtypeStruct((b, t, h * d), F32),
        compiler_params=_params(("parallel", "parallel", "arbitrary"), VMEM_LIMIT),
        name="banded_attn_w%d" % window,
    )(*args)


def _conv_kernel(x_ref, b_ref, c_ref, xp_ref, cp_ref, w_ref, o_ref):
    ti = pl.program_id(1)
    z = c_ref[...] * x_ref[...]
    zp = jnp.where(ti > 0, cp_ref[...] * xp_ref[...], 0.0)
    row = lax.broadcasted_iota(jnp.int32, z.shape, 0)
    z1 = jnp.where(row == 0, zp[7:8, :], pltpu.roll(z, 1, 0))
    z2 = pltpu.roll(z, 2, 0)
    z2 = jnp.where(row == 0, zp[6:7, :], jnp.where(row == 1, zp[7:8, :], z2))
    w = w_ref[...]
    o_ref[...] = b_ref[...] * (w[0:1, :] * z2 + w[1:2, :] * z1 + w[2:3, :] * z)


def _conv(proj_b, conv_w, tt=512):
    b, t, _ = proj_b.shape
    w = MIX_W
    hb = tt // 8
    prev = lambda bi, ti: (bi, jnp.maximum(ti * hb - 1, 0), 0)
    prev_c = lambda bi, ti: (bi, jnp.maximum(ti * hb - 1, 0), 2)
    return pl.pallas_call(
        _conv_kernel,
        grid=(b, t // tt),
        in_specs=[
            pl.BlockSpec((None, tt, w), lambda bi, ti: (bi, ti, 0)),
            pl.BlockSpec((None, tt, w), lambda bi, ti: (bi, ti, 1)),
            pl.BlockSpec((None, tt, w), lambda bi, ti: (bi, ti, 2)),
            pl.BlockSpec((None, 8, w), prev),
            pl.BlockSpec((None, 8, w), prev_c),
            pl.BlockSpec((8, w), lambda bi, ti: (0, 0)),
        ],
        out_specs=pl.BlockSpec((None, tt, w), lambda bi, ti: (bi, ti, 0)),
        out_shape=jax.ShapeDtypeStruct((b, t, w), F32),
        compiler_params=_params(("parallel", "parallel")),
        name="short_conv",
    )(proj_b, proj_b, proj_b, proj_b, proj_b,
      jnp.pad(conv_w.reshape(conv_w.shape[0], w).astype(F32), ((0, 8 - conv_w.shape[0]), (0, 0))))


def _gelu_tanh(x):
    return x * (0.5 * (1.0 + jnp.tanh(0.7978845608028654 * (x + 0.044715 * (x * x * x)))))


def _compress_kernel(tk_ref, tv_ref, pek_ref, pev_ref, wk1_ref, wk2_ref, wv1_ref, wv2_ref,
                     kg_ref, ko_ref, vo_ref):
    nrow = tk_ref.shape[0]

    def mlp(a, pe_ref, w1_ref, w2_ref):
        a0 = (a + pe_ref[0:1, :]).astype(BF16)
        a1 = (a + pe_ref[1:2, :]).astype(BF16)
        p1 = _dot(a0, w1_ref[0])
        p2 = _dot(a1, w1_ref[1])
        hdn = p1 + pltpu.roll(p2, nrow - 1, 0)
        return _dot(_gelu_tanh(hdn).astype(BF16), w2_ref[...])

    kc = mlp(tk_ref[...], pek_ref, wk1_ref, wk2_ref)
    ko_ref[...] = _rms_rows(kc, kg_ref[...]).astype(BF16)
    vo_ref[...] = mlp(tv_ref[...], pev_ref, wv1_ref, wv2_ref).astype(BF16)


def _compress(kc_rows, vc_rows, pos_k, pos_v, wk1, wk2, wv1, wv2, k_gain):
    b, t, kvd = kc_rows.shape
    kv = KV_HEADS
    d = kvd // kv
    nrow = t // CMP_STRIDE
    wide = CMP_STRIDE * kvd
    hid = wk1.shape[1]

    def expand_w1(w1):
        w = w1.reshape(2, CMP_STRIDE, 1, d, hid)
        per_head = []
        for g in range(kv):
            pads = [jnp.zeros_like(w)] * kv
            pads[g] = w
            per_head.append(jnp.concatenate(pads, axis=2).reshape(2, wide, hid))
        return jnp.stack(per_head).astype(BF16)

    def expand_pe(pe):
        return jnp.broadcast_to(pe.reshape(2, CMP_STRIDE, 1, d), (2, CMP_STRIDE, kv, d)).reshape(2, wide)

    tok = pl.BlockSpec((None, nrow, wide), lambda bi, gi: (bi, 0, 0))
    full = lambda shape: pl.BlockSpec(shape, lambda bi, gi: tuple(0 for _ in shape))
    w1_spec = pl.BlockSpec((None, 2, wide, hid), lambda bi, gi: (gi, 0, 0, 0))
    out = pl.BlockSpec((None, nrow, d), lambda bi, gi: (bi * kv + gi, 0, 0))
    return pl.pallas_call(
        _compress_kernel,
        grid=(b, kv),
        in_specs=[tok, tok, full((2, wide)), full((2, wide)), w1_spec, full((hid, d)),
                  w1_spec, full((hid, d)), full((1, d))],
        out_specs=[out, out],
        out_shape=[jax.ShapeDtypeStruct((b * kv, nrow, d), BF16),
                   jax.ShapeDtypeStruct((b * kv, nrow, d), BF16)],
        compiler_params=_params(("parallel", "arbitrary"), VMEM_LIMIT),
        name="nsa_compress",
    )(kc_rows.reshape(b, nrow, wide), vc_rows.reshape(b, nrow, wide), expand_pe(pos_k),
      expand_pe(pos_v), expand_w1(wk1), wk2.astype(BF16), expand_w1(wv1), wv2.astype(BF16),
      k_gain.reshape(1, d))


def _cmp_attn_kernel(qt_ref, kc_ref, vct_ref, gate_ref, o_ref, selt_ref, imp_ref, *, sel_k):
    step = pl.program_id(2)
    rq = GROUP_R * Q_BLOCK
    nc = kc_ref.shape[0]
    ns = selt_ref.shape[0]
    blk = lax.broadcasted_iota(jnp.int32, (ns, Q_BLOCK), 0)

    def attend(nc_eff):
        for qb in range(CMP_QB):
            i = step * CMP_QB + qb
            st = _dot(kc_ref[0:nc_eff, :], _group_q(qt_ref, qb))
            n = lax.broadcasted_iota(jnp.int32, (nc_eff, rq), 0)
            lane = lax.broadcasted_iota(jnp.int32, (nc_eff, rq), 1)
            qpos = i * Q_BLOCK + (lane & (Q_BLOCK - 1))
            st = jnp.where(n * CMP_STRIDE + (CMP_BLOCK - 1) <= qpos, st, NEG_INF)
            m = jnp.max(st, axis=0, keepdims=True)
            m = jnp.where(m == NEG_INF, 0.0, m)
            p = jnp.exp(st - m)
            denom = jnp.sum(p, axis=0, keepdims=True)
            pb = (p / jnp.maximum(denom, 1e-30)).astype(BF16)
            o_t = _dot(vct_ref[:, 0:nc_eff], pb)
            o_ref[qb * Q_BLOCK:(qb + 1) * Q_BLOCK, :] = _gate_untranspose(o_t, gate_ref, 0, qb)
            ss = lax.broadcasted_iota(jnp.int32, (ns, nc_eff), 0) * SEL_BLOCK
            nn = lax.broadcasted_iota(jnp.int32, (ns, nc_eff), 1) * CMP_STRIDE
            overlap_t = jnp.where((nn < ss + SEL_BLOCK) & (nn + (CMP_BLOCK - 1) >= ss),
                                  1.0, 0.0).astype(BF16)
            imp_r = _dot(overlap_t, pb)
            imp = imp_r[:, 0:Q_BLOCK]
            for r in range(1, GROUP_R):
                imp = imp + imp_r[:, r * Q_BLOCK:(r + 1) * Q_BLOCK]
            qp = i * Q_BLOCK + lax.broadcasted_iota(jnp.int32, (ns, Q_BLOCK), 1)
            cur = qp // SEL_BLOCK
            forced = (blk == 0) | (blk == cur) | (blk == cur - 1)
            imp = jnp.where(forced, jnp.inf, imp)
            imp_ref[qb] = jnp.where(blk <= cur, imp, NEG_INF)

    variant_rows = 128
    n_var = nc // variant_rows
    n_ending = (step + 1) * (CMP_QB * Q_BLOCK // CMP_STRIDE) - 1
    variant = jnp.minimum((n_ending + variant_rows - 1) // variant_rows, n_var) - 1
    for v in range(n_var):
        pl.when(variant == v)(functools.partial(attend, (v + 1) * variant_rows))

    blkf = blk.astype(F32)
    imps = [imp_ref[qb] for qb in range(CMP_QB)]
    sels = [jnp.zeros((ns, Q_BLOCK), F32) for _ in range(CMP_QB)]
    for _ in range(sel_k):
        for qb in range(CMP_QB):
            mx = jnp.max(imps[qb], axis=0, keepdims=True)
            first = jnp.min(jnp.where(imps[qb] == mx, blkf, float(ns)), axis=0, keepdims=True)
            hit = blkf == first
            sels[qb] = jnp.where(hit & (mx > NEG_INF), 1.0, sels[qb])
            imps[qb] = jnp.where(hit, NEG_INF, imps[qb])
    for qb in range(CMP_QB):
        selt_ref[:, qb * Q_BLOCK:(qb + 1) * Q_BLOCK] = sels[qb]


def _cmp_attn(qt, k_cmp, v_cmp_t, gates):
    b, h, d, t = qt.shape
    g = h // GROUP_R
    nc = k_cmp.shape[1]
    ns = t // SEL_BLOCK
    qw = CMP_QB * Q_BLOCK
    assert nc % 128 == 0
    return pl.pallas_call(
        functools.partial(_cmp_attn_kernel, sel_k=min(SEL_TOPK, ns)),
        grid=(b, g, t // qw),
        in_specs=[
            pl.BlockSpec((None, GROUP_R, d, qw), lambda bi, gi, i: (bi, gi, 0, i)),
            pl.BlockSpec((None, nc, d), lambda bi, gi, i: (bi * KV_HEADS + gi, 0, 0)),
            pl.BlockSpec((None, d, nc), lambda bi, gi, i: (bi * KV_HEADS + gi, 0, 0)),
            pl.BlockSpec((None, None, N_GATE_ROWS, qw), lambda bi, gi, i: (bi, gi, 0, i)),
        ],
        out_specs=[
            pl.BlockSpec((None, qw, GROUP_R * d), lambda bi, gi, i: (bi, i, gi)),
            pl.BlockSpec((None, None, ns, qw), lambda bi, gi, i: (bi, gi, 0, i)),
        ],
        out_shape=[jax.ShapeDtypeStruct((b, t, h * d), F32),
                   jax.ShapeDtypeStruct((b, g, ns, t), F32)],
        scratch_shapes=[pltpu.VMEM((CMP_QB, ns, Q_BLOCK), F32)],
        compiler_params=_params(("parallel", "parallel", "arbitrary"), VMEM_LIMIT),
        name="nsa_cmp_topk",
    )(qt, k_cmp, v_cmp_t, gates)


def _sel_attn_kernel(qt_ref, ks_ref, vst_ref, selt_ref, gate_ref, o_ref, sa_ref, sb_ref, mta_ref,
                     mtb_ref, m_ref, acc_ref):
    g = pl.program_id(1)
    i = pl.program_id(2)
    tk = SEL_KEY_TILE
    bpt = tk // SEL_BLOCK
    spt = tk // Q_BLOCK
    d = HEAD_DIM
    m_ref[...] = jnp.full(m_ref.shape, NEG_INF, F32)
    acc_ref[...] = jnp.zeros_like(acc_ref)
    n_tiles = (i * Q_BLOCK + Q_BLOCK + tk - 1) // tk
    qpad = _padded_q(qt_ref, g)
    tri = jnp.where(lax.broadcasted_iota(jnp.int32, (Q_BLOCK, Q_BLOCK), 0)
                    <= lax.broadcasted_iota(jnp.int32, (Q_BLOCK, Q_BLOCK), 1), 0.0, NEG_INF)

    def scores(j, s_ref, mt_ref):
        k0 = pl.multiple_of(j * tk, tk)
        st = _dot(ks_ref[pl.ds(k0, tk), :], qpad)
        sel_rows = selt_ref[pl.ds(pl.multiple_of(j * bpt, bpt), bpt), :]
        bias_rows = jnp.where(sel_rows > 0.5, 0.0, NEG_INF)
        parts = []
        for u in range(spt):
            sub = jnp.concatenate(
                [jnp.broadcast_to(bias_rows[s:s + 1, :], (SEL_BLOCK, Q_BLOCK))
                 for s in range(u * Q_BLOCK // SEL_BLOCK, (u + 1) * Q_BLOCK // SEL_BLOCK)], axis=0)
            parts.append(sub + jnp.where(j * spt + u == i, tri, 0.0))
        bias = jnp.concatenate(parts, axis=0)
        st = st + jnp.concatenate([bias] * GROUP_R, axis=1)
        s_ref[...] = st
        mt_ref[...] = jnp.max(st, axis=0, keepdims=True)

    def consume(j, s_ref, mt_ref):
        m_prev = m_ref[...]
        m_new = jnp.maximum(m_prev, mt_ref[...])
        m_safe = jnp.where(m_new == NEG_INF, 0.0, m_new)
        alpha = jnp.exp2(m_prev - m_safe)
        p = jnp.exp2((s_ref[...] - m_safe).astype(BF16))
        acc_ref[...] = alpha * acc_ref[...] + _dot(vst_ref[j], p)
        m_ref[...] = m_new

    scores(0, sa_ref, mta_ref)
    n_pairs = (n_tiles - 1) // 2

    def step(jj, carry):
        j = 2 * jj
        scores(j + 1, sb_ref, mtb_ref)
        consume(j, sa_ref, mta_ref)
        scores(j + 2, sa_ref, mta_ref)
        consume(j + 1, sb_ref, mtb_ref)
        return carry

    lax.fori_loop(0, n_pairs, step, 0)
    j_rem = 2 * n_pairs

    @pl.when(n_tiles - j_rem == 2)
    def _():
        scores(j_rem + 1, sb_ref, mtb_ref)

    consume(j_rem, sa_ref, mta_ref)

    @pl.when(n_tiles - j_rem == 2)
    def _():
        consume(j_rem + 1, sb_ref, mtb_ref)

    o_t = acc_ref[0:d, :] / jnp.maximum(acc_ref[d:d + 1, :], 1e-30)
    o_ref[...] = _gate_untranspose(o_t, gate_ref, 1)


def _sel_attn(qt, ks_rows, vs_t, sel_t, gates):
    b, h, d, t = qt.shape
    g = h // GROUP_R
    ns = sel_t.shape[2]
    tk = SEL_KEY_TILE
    rq = GROUP_R * Q_BLOCK
    return pl.pallas_call(
        _sel_attn_kernel,
        grid=(b, g, t // Q_BLOCK),
        in_specs=[
            pl.BlockSpec((None, GROUP_R, d, Q_BLOCK), lambda bi, gi, i: (bi, gi, 0, i)),
            pl.BlockSpec((None, t, KV_HEADS * d), lambda bi, gi, i: (bi, 0, 0)),
            pl.BlockSpec((None, None, t // tk, SEL_V_ROWS, tk), lambda bi, gi, i: (bi, gi, 0, 0, 0)),
            pl.BlockSpec((None, None, ns, Q_BLOCK), lambda bi, gi, i: (bi, gi, 0, i)),
            pl.BlockSpec((None, None, N_GATE_ROWS, Q_BLOCK), lambda bi, gi, i: (bi, gi, 0, i)),
        ],
        out_specs=pl.BlockSpec((None, Q_BLOCK, GROUP_R * d), lambda bi, gi, i: (bi, i, gi)),
        out_shape=jax.ShapeDtypeStruct((b, t, h * d), F32),
        scratch_shapes=[pltpu.VMEM((tk, rq), F32), pltpu.VMEM((tk, rq), F32),
                        pltpu.VMEM((1, rq), F32), pltpu.VMEM((1, rq), F32),
                        pltpu.VMEM((1, rq), F32), pltpu.VMEM((SEL_V_ROWS, rq), F32)],
        compiler_params=_params(("parallel", "parallel", "arbitrary"), VMEM_LIMIT),
        name="nsa_selected",
    )(qt, ks_rows, vs_t, sel_t, gates)


def _retention_kernel(lg_ref, q_ref, k_ref, v_ref, gt_ref, cos_ref, sin_ref, ng_ref, o_ref, r_ref):
    ci = pl.program_id(1)
    c = RET_CHUNK
    dk, dv = RET_QK_DIM, RET_V_DIM
    half = dk // 2

    @pl.when(ci == 0)
    def _():
        r_ref[...] = jnp.zeros_like(r_ref)

    cosf = cos_ref[...]
    sinf = sin_ref[...]

    def rot(x):
        return x * cosf + jnp.concatenate([x[:, half:], x[:, :half]], axis=1) * sinf

    ii = lax.broadcasted_iota(jnp.int32, (c, c), 0)
    jj = lax.broadcasted_iota(jnp.int32, (c, c), 1)
    d = (ii - jj).astype(F32)
    jcol = lax.broadcasted_iota(jnp.int32, (c, 1), 0).astype(F32)
    for hh in range(RET_HEADS):
        lg = lg_ref[hh]
        dmask = jnp.where(d >= 0, jnp.exp(d * lg), 0.0)
        xi = jnp.exp((jcol + 1.0) * lg)
        zeta = jnp.exp((c - 1.0 - jcol) * lg)
        decay = jnp.exp(jnp.zeros((1, dv), F32) + c * lg)
        for bb in range(q_ref.shape[0]):
            q = rot(q_ref[bb, :, hh * dk:(hh + 1) * dk])
            k = rot(k_ref[bb, :, hh * dk:(hh + 1) * dk]) * (dk ** -0.5)
            vb = v_ref[bb, :, hh * dv:(hh + 1) * dv].astype(BF16)
            att = _dot_nt(q.astype(BF16), k.astype(BF16)) * dmask
            o = _dot(att.astype(BF16), vb)
            r_prev = r_ref[bb, hh]
            o = o + _dot((q * xi).astype(BF16), r_prev.astype(BF16))
            s_chunk = _dot_tn((k * zeta).astype(BF16), vb)
            r_ref[bb, hh] = r_prev * decay + s_chunk
            mu = jnp.mean(o, axis=-1, keepdims=True)
            var = jnp.mean(jnp.square(o - mu), axis=-1, keepdims=True)
            on = (o - mu) * lax.rsqrt(var + EPS)
            gt = gt_ref[bb, :, hh * dv:(hh + 1) * dv]
            o_ref[bb, :, hh * dv:(hh + 1) * dv] = (
                (gt * _sigmoid(gt)) * (on * ng_ref[:, hh * dv:(hh + 1) * dv]))


def _retention(proj_d, norm_gain):
    b, t, _ = proj_d.shape
    h, dk = RET_HEADS, RET_QK_DIM
    c = RET_CHUNK
    dv = RET_V_DIM
    assert 2 * h * dk == h * dv
    nb = 2 if b % 2 == 0 else 1
    half = dk // 2
    inv = ROPE_BASE ** (-jnp.arange(half, dtype=F32) / half)
    ang = jnp.arange(t).astype(F32)[:, None] * inv[None, :]
    cos = jnp.cos(ang)
    sin = jnp.sin(ang)
    cosf = jnp.concatenate([cos, cos], axis=-1)
    sinf = jnp.concatenate([-sin, sin], axis=-1)
    log_gamma = jnp.log(1.0 - 2.0 ** (-5.0 - jnp.arange(h, dtype=F32)))
    return pl.pallas_call(
        _retention_kernel,
        grid=(b // nb, t // c),
        in_specs=[
            pl.BlockSpec(memory_space=pltpu.SMEM),
            pl.BlockSpec((nb, c, h * dk), lambda bi, ci: (bi, ci, 0)),
            pl.BlockSpec((nb, c, h * dk), lambda bi, ci: (bi, ci, 1)),
            pl.BlockSpec((nb, c, h * dv), lambda bi, ci: (bi, ci, 1)),
            pl.BlockSpec((nb, c, h * dv), lambda bi, ci: (bi, ci, 2)),
            pl.BlockSpec((c, dk), lambda bi, ci: (ci, 0)),
            pl.BlockSpec((c, dk), lambda bi, ci: (ci, 0)),
            pl.BlockSpec((1, h * dv), lambda bi, ci: (0, 0)),
        ],
        out_specs=pl.BlockSpec((nb, c, h * dv), lambda bi, ci: (bi, ci, 0)),
        out_shape=jax.ShapeDtypeStruct((b, t, h * dv), F32),
        scratch_shapes=[pltpu.VMEM((nb, h, dk, dv), F32)],
        compiler_params=_params(("parallel", "arbitrary")),
        name="retention",
    )(log_gamma, proj_d, proj_d, proj_d, proj_d, cosf, sinf, norm_gain.reshape(1, h * dv))


def _merge_kernel(x_ref, g_ref, wgate_ref, bias_ref, ya_ref, yb_ref, yc0_ref, yc1_ref, yc2_ref,
                  yd_ref, wb_ref, wo_ref, o_ref):
    x = x_ref[...]
    d = x.shape[1]
    u = _rms_rows(x, g_ref[...]).astype(BF16)
    ys = (ya_ref[...], yb_ref[...], yc0_ref[...] + yc1_ref[...] + yc2_ref[...], yd_ref[...])
    merged = jnp.zeros(x.shape, F32)
    for n in range(N_BRANCH):
        logits = _dot(u, wgate_ref[:, n * d:(n + 1) * d]) + bias_ref[:, n * d:(n + 1) * d]
        merged = merged + _sigmoid(logits) * _dot(ys[n].astype(BF16), wb_ref[n])
    o_ref[...] = x + _dot(merged.astype(BF16), wo_ref[...])


def _merge(x2, g, w_gate, bias, ys, w_branch, w_out, tm=512):
    n, d = x2.shape
    w = MIX_W
    row = lambda width: pl.BlockSpec((tm, width), lambda i: (i, 0))
    full = lambda shape: pl.BlockSpec(shape, lambda i: tuple(0 for _ in shape),
                                      pipeline_mode=pl.Buffered(1))
    return pl.pallas_call(
        _merge_kernel,
        grid=(n // tm,),
        in_specs=[row(d), full((1, d)), full((d, N_BRANCH * d)), full((1, N_BRANCH * d))]
        + [row(w)] * 6 + [full((N_BRANCH, w, d)), full((d, d))],
        out_specs=row(d),
        out_shape=jax.ShapeDtypeStruct((n, d), F32),
        compiler_params=_params(("parallel",), VMEM_LIMIT),
        name="merge_out",
    )(x2, g.reshape(1, d), w_gate.astype(BF16), bias.reshape(1, N_BRANCH * d),
      *[y.reshape(n, w) for y in ys], w_branch.astype(BF16), w_out.astype(BF16))


def _mixers(x2, b, t, mix_norm, w_in, merge_gate_bias, swa_q_gain, swa_k_gain, swa_sinks, conv_w,
            nsa_q_gain, nsa_k_gain, cmp_pos_k, cmp_pos_v, cmp_wk1, cmp_wk2, cmp_wv1, cmp_wv2,
            ret_norm_gain, w_branch, w_out):
    hd = HEAD_DIM
    swa_q, swa_kv = SWA_HEADS * hd, SWA_KV_HEADS * hd
    nsa_q, nsa_kv = NSA_HEADS * hd, NSA_KV_HEADS * hd
    ret_qk, ret_v = RET_HEADS * RET_QK_DIM, RET_HEADS * RET_V_DIM
    n_gate = NSA_HEADS * 3
    o_a = 0
    o_b = o_a + swa_q + 2 * swa_kv
    o_c = o_b + 3 * MIX_W
    o_cg = o_c + nsa_q + 6 * nsa_kv
    o_d = o_cg + n_gate
    o_g = o_d + 2 * ret_qk + 2 * ret_v

    per_g = GROUP_R * 3
    w_gates = [jnp.pad(w_in[:, o_cg + gi * per_g:o_cg + (gi + 1) * per_g],
                       ((0, 0), (0, N_GATE_ROWS - per_g))) for gi in range(NSA_KV_HEADS)]
    w_t = jnp.concatenate([w_in[:, o_a:o_b], w_in[:, o_c:o_cg]] + w_gates, axis=1).T.astype(BF16)
    head_gains = jnp.stack([swa_q_gain, swa_k_gain, nsa_q_gain, nsa_k_gain[1], nsa_k_gain[2]])
    head_gains = jnp.broadcast_to(head_gains.astype(F32)[:, :, None], (5, hd, 128))
    (a_qt, a_k, a_vt, c_qt, c_qt2, c_kc, c_vc, c_ks, c_vst, c_kw, c_vwt, gates_c) = _proj_heads(
        x2, b, t, mix_norm, w_t, head_gains)
    proj_b = _norm_matmul(x2, mix_norm, w_in[:, o_b:o_c])
    proj_d = _norm_matmul(x2, mix_norm, w_in[:, o_d:o_g])

    y_a = _banded(a_qt, a_k, a_vt, SWA_WINDOW, sinks=swa_sinks)

    y_b = _conv(proj_b.reshape(b, t, 3 * MIX_W), conv_w)

    k_cmp, v_cmp = _compress(c_kc, c_vc, cmp_pos_k, cmp_pos_v, cmp_wk1, cmp_wk2, cmp_wv1, cmp_wv2,
                             nsa_k_gain[0])
    y_cmp, sel_t = _cmp_attn(c_qt, k_cmp, v_cmp.transpose(0, 2, 1), gates_c)
    y_sel = _sel_attn(c_qt2, c_ks, c_vst, sel_t, gates_c)
    y_win = _banded(c_qt, c_kw, c_vwt, NSA_WINDOW, gates=gates_c, gate_branch=2)

    y_d = _retention(proj_d.reshape(b, t, 2 * ret_qk + 2 * ret_v), ret_norm_gain)

    return _merge(x2, mix_norm, w_in[:, o_g:], merge_gate_bias,
                  (y_a, y_b, y_cmp, y_sel, y_win, y_d), w_branch, w_out)


def kernel(x, ffn1_norm, ffn1_w_gate, ffn1_w_up, ffn1_w_down, mix_norm, w_in, merge_gate_bias, swa_q_gain, swa_k_gain, swa_sinks, conv_w, nsa_q_gain, nsa_k_gain, cmp_pos_k, cmp_pos_v, cmp_wk1, cmp_wk2, cmp_wv1, cmp_wv2, ret_norm_gain, w_branch, w_out, ffn2_norm, ffn2_w_gate, ffn2_w_up, ffn2_w_down):
    b, t, d = x.shape
    x2 = x.reshape(b * t, d)
    for l in range(ffn1_norm.shape[0]):
        x2 = _ffn(x2, ffn1_norm[l], ffn1_w_gate[l], ffn1_w_up[l], ffn1_w_down[l])
        x2 = _mixers(x2, b, t, mix_norm[l], w_in[l], merge_gate_bias[l], swa_q_gain[l],
                     swa_k_gain[l], swa_sinks[l], conv_w[l], nsa_q_gain[l], nsa_k_gain[l],
                     cmp_pos_k[l], cmp_pos_v[l], cmp_wk1[l], cmp_wk2[l], cmp_wv1[l], cmp_wv2[l],
                     ret_norm_gain[l], w_branch[l], w_out[l])
        x2 = _ffn(x2, ffn2_norm[l], ffn2_w_gate[l], ffn2_w_up[l], ffn2_w_down[l])
    return x2.reshape(b, t, d)
```

```python
import functools

import jax
import jax.numpy as jnp
from jax import lax
from jax.experimental import pallas as pl
from jax.experimental.pallas import tpu as pltpu

F32 = jnp.float32
BF16 = jnp.bfloat16

HEAD_DIM = 64
Q_BLOCK = 128
MIX_W = 512
N_BRANCH = 4
SWA_HEADS = 8
SWA_KV_HEADS = 2
SWA_WINDOW = 128
NSA_HEADS = 8
NSA_KV_HEADS = 2
CMP_BLOCK = 32
CMP_STRIDE = 16
SEL_BLOCK = 64
SEL_TOPK = 16
NSA_WINDOW = 512
RET_HEADS = 4
RET_QK_DIM = 64
RET_V_DIM = 128
RET_CHUNK = 128
ROPE_BASE = 10000.0
EPS = 1e-6
GROUP_R = 4
KV_HEADS = 2
N_GATE_ROWS = 16
SEL_KEY_TILE = 512
SEL_V_ROWS = HEAD_DIM + 16
PROJ_TM = 512
BANDED_QB = 8
CMP_QB = 4
VMEM_LIMIT = 52 * 1024 * 1024

NEG_INF = float("-inf")
LOG2_E = 1.4426950408889634


def _params(sem, vmem=None):
    return pltpu.CompilerParams(dimension_semantics=sem, vmem_limit_bytes=vmem)


def _sigmoid(x):
    return 1.0 / (1.0 + jnp.exp(-x))


def _rms_rows(x, g):
    return x * lax.rsqrt(jnp.mean(x * x, axis=-1, keepdims=True) + EPS) * g


def _dot(a, b):
    return jnp.dot(a, b, preferred_element_type=F32)


def _dot_nt(a, b):
    return lax.dot_general(a, b, (((1,), (1,)), ((), ())), preferred_element_type=F32)


def _dot_tn(a, b):
    return lax.dot_general(a, b, (((0,), (0,)), ((), ())), preferred_element_type=F32)


def _ffn_kernel(x_ref, g_ref, wg_ref, wu_ref, wd_ref, o_ref, *, n_chunks):
    x = x_ref[...]
    xn = _rms_rows(x, g_ref[...]).astype(BF16)
    tf = wg_ref.shape[1] // n_chunks
    acc = None
    for f in range(n_chunks):
        a = _dot(xn, wg_ref[:, f * tf:(f + 1) * tf])
        b = _dot(xn, wu_ref[:, f * tf:(f + 1) * tf])
        h = ((a * _sigmoid(a)) * b).astype(BF16)
        part = _dot(h, wd_ref[f * tf:(f + 1) * tf, :])
        acc = part if acc is None else acc + part
    o_ref[...] = x + 0.5 * acc


def _ffn(x2, g, wg, wu, wd, tm=512, n_chunks=2):
    n, d = x2.shape
    dff = wg.shape[1]
    assert dff % (n_chunks * 128) == 0
    resident = lambda shape: pl.BlockSpec(shape, lambda i: (0, 0), pipeline_mode=pl.Buffered(1))
    return pl.pallas_call(
        functools.partial(_ffn_kernel, n_chunks=n_chunks),
        grid=(n // tm,),
        in_specs=[
            pl.BlockSpec((tm, d), lambda i: (i, 0)),
            pl.BlockSpec((1, d), lambda i: (0, 0)),
            resident((d, dff)), resident((d, dff)), resident((dff, d)),
        ],
        out_specs=pl.BlockSpec((tm, d), lambda i: (i, 0)),
        out_shape=jax.ShapeDtypeStruct((n, d), F32),
        compiler_params=_params(("parallel",), VMEM_LIMIT),
        name="ffn",
    )(x2, g.reshape(1, d), wg.astype(BF16), wu.astype(BF16), wd.astype(BF16))


def _proj_heads_kernel(x_ref, g_ref, wt_ref, hg_ref, aq_ref, ak_ref, av_ref, cq_ref, cq2_ref,
                       ckc_ref, cvc_ref, cks_ref, cvs_ref, ckw_ref, cvw_ref, gt_ref):
    d = HEAD_DIM
    xn = _rms_rows(x_ref[...], g_ref[...]).astype(BF16)
    acc = _dot_nt(wt_ref[...], xn)
    tm = xn.shape[0]
    lane_tiles = tm // 128
    scale = d ** -0.5
    kv_w = KV_HEADS * d

    def head_norm(row0, gain_idx, mult):
        hb = acc[row0:row0 + d]
        gain = jnp.concatenate([hg_ref[gain_idx]] * lane_tiles, axis=1)
        y = hb * lax.rsqrt(jnp.mean(hb * hb, axis=0, keepdims=True) + EPS) * gain
        return y * mult if mult != 1.0 else y

    def q_heads(row0, gain_idx, out_ref, n_heads, log2_ref=None):
        for h in range(n_heads):
            y = head_norm(row0 + h * d, gain_idx, 1.0)
            out_ref[h] = (y * scale).astype(BF16)
            if log2_ref is not None:
                log2_ref[h] = (y * (scale * LOG2_E)).astype(BF16)

    def k_rows(row0, gain_idx):
        return jnp.concatenate([head_norm(row0 + g * d, gain_idx, 1.0) for g in range(KV_HEADS)],
                               axis=0).T

    def v_tiles(row0, out_ref):
        for g in range(KV_HEADS):
            for u in range(lane_tiles):
                out_ref[g, u, 0:d, :] = acc[row0 + g * d:row0 + (g + 1) * d,
                                            u * 128:(u + 1) * 128].astype(BF16)
                out_ref[g, u, d:SEL_V_ROWS, :] = jnp.ones((SEL_V_ROWS - d, 128), BF16)

    row = 0
    q_heads(row, 0, aq_ref, SWA_HEADS)
    row += SWA_HEADS * d
    ak_ref[...] = k_rows(row, 1).astype(BF16)
    row += kv_w
    v_tiles(row, av_ref)
    row += kv_w
    q_heads(row, 2, cq_ref, NSA_HEADS, cq2_ref)
    row += NSA_HEADS * d
    ckc_ref[...] = acc[row:row + kv_w].T
    row += kv_w
    cvc_ref[...] = acc[row:row + kv_w].T
    row += kv_w
    cks_ref[...] = k_rows(row, 3).astype(BF16)
    row += kv_w
    for g in range(KV_HEADS):
        cvs_ref[g, 0, 0:d, :] = acc[row + g * d:row + (g + 1) * d].astype(BF16)
        cvs_ref[g, 0, d:SEL_V_ROWS, :] = jnp.ones((SEL_V_ROWS - d, tm), BF16)
    row += kv_w
    ckw_ref[...] = k_rows(row, 4).astype(BF16)
    row += kv_w
    v_tiles(row, cvw_ref)
    row += kv_w
    for g in range(KV_HEADS):
        gt_ref[g] = _sigmoid(acc[row + g * N_GATE_ROWS:row + (g + 1) * N_GATE_ROWS])


def _proj_heads(x2, b, t, g, w_t, head_gains):
    n, dm = x2.shape
    d = HEAD_DIM
    tm = PROJ_TM
    assert tm == SEL_KEY_TILE and t % tm == 0
    rows = w_t.shape[0]
    tpb = t // tm
    lt = tm // 128
    kv = KV_HEADS
    qt_spec = lambda h: pl.BlockSpec((None, h, d, tm), lambda bi, ti: (bi, 0, 0, ti))
    row_spec = pl.BlockSpec((None, tm, kv * d), lambda bi, ti: (bi, ti, 0))
    vt_spec = pl.BlockSpec((None, kv, lt, SEL_V_ROWS, 128), lambda bi, ti: (bi, 0, ti, 0, 0))
    sds = jax.ShapeDtypeStruct
    return pl.pallas_call(
        _proj_heads_kernel,
        grid=(b, tpb),
        in_specs=[
            pl.BlockSpec((tm, dm), lambda bi, ti: (bi * tpb + ti, 0)),
            pl.BlockSpec((1, dm), lambda bi, ti: (0, 0)),
            pl.BlockSpec((rows, dm), lambda bi, ti: (0, 0), pipeline_mode=pl.Buffered(1)),
            pl.BlockSpec(head_gains.shape, lambda bi, ti: (0, 0, 0)),
        ],
        out_specs=[
            qt_spec(SWA_HEADS), row_spec, vt_spec,
            qt_spec(NSA_HEADS), qt_spec(NSA_HEADS), row_spec, row_spec, row_spec,
            pl.BlockSpec((None, kv, 1, SEL_V_ROWS, tm), lambda bi, ti: (bi, 0, ti, 0, 0)),
            row_spec, vt_spec,
            pl.BlockSpec((None, kv, N_GATE_ROWS, tm), lambda bi, ti: (bi, 0, 0, ti)),
        ],
        out_shape=[
            sds((b, SWA_HEADS, d, t), BF16), sds((b, t, kv * d), BF16),
            sds((b, kv, t // 128, SEL_V_ROWS, 128), BF16),
            sds((b, NSA_HEADS, d, t), BF16), sds((b, NSA_HEADS, d, t), BF16),
            sds((b, t, kv * d), F32), sds((b, t, kv * d), F32),
            sds((b, t, kv * d), BF16), sds((b, kv, tpb, SEL_V_ROWS, tm), BF16),
            sds((b, t, kv * d), BF16), sds((b, kv, t // 128, SEL_V_ROWS, 128), BF16),
            sds((b, kv, N_GATE_ROWS, t), F32),
        ],
        compiler_params=_params(("parallel", "parallel"), VMEM_LIMIT),
        name="proj_heads",
    )(x2, g.reshape(1, dm), w_t, head_gains)


def _group_q(qt_ref, qb=0):
    return jnp.concatenate(
        [qt_ref[r, :, qb * Q_BLOCK:(qb + 1) * Q_BLOCK] for r in range(GROUP_R)], axis=1)


def _padded_q(qt_ref, g, qb=0):
    q4 = _group_q(qt_ref, qb)
    z = jnp.zeros_like(q4)
    return jnp.where(g == 0, jnp.concatenate([q4, z], axis=0), jnp.concatenate([z, q4], axis=0))


def _gate_untranspose(o_t, gate_ref, branch, qb=0):
    outs = []
    for r in range(GROUP_R):
        blk = o_t[:, r * Q_BLOCK:(r + 1) * Q_BLOCK]
        if gate_ref is not None:
            c = r * 3 + branch
            blk = blk * gate_ref[c:c + 1, qb * Q_BLOCK:(qb + 1) * Q_BLOCK]
        outs.append(blk.T)
    return jnp.concatenate(outs, axis=1)


def _banded_kernel(*refs, window, has_sink, gate_branch):
    refs = list(refs)
    sink_ref = refs.pop(0) if has_sink else None
    qt_ref, k_ref, vt_ref = refs[:3]
    gate_ref = refs[3] if gate_branch is not None else None
    o_ref = refs[-1]
    g = pl.program_id(1)
    rq = GROUP_R * Q_BLOCK
    n_sub = window // Q_BLOCK + 1
    span = n_sub * Q_BLOCK
    if has_sink:
        lrow = lax.broadcasted_iota(jnp.int32, (1, rq), 1)
        sink = jnp.zeros((1, rq), F32)
        for r in range(GROUP_R):
            sink = jnp.where((lrow >= r * Q_BLOCK) & (lrow < (r + 1) * Q_BLOCK),
                             sink_ref[g * GROUP_R + r], sink)
    step = pl.program_id(2)
    row = lax.broadcasted_iota(jnp.int32, (span, Q_BLOCK), 0)
    qcol = lax.broadcasted_iota(jnp.int32, (span, Q_BLOCK), 1)

    def band_bias(offset):
        diff = offset + qcol - row
        return jnp.where((diff >= 0) & (diff < window), 0.0, NEG_INF)

    for qb in range(BANDED_QB):
        i = step * BANDED_QB + qb
        start = pl.multiple_of(jnp.maximum(i * Q_BLOCK - window, 0), Q_BLOCK)
        st = _dot(k_ref[pl.ds(start, span), :], _padded_q(qt_ref, g, qb))
        st = st + jnp.concatenate([band_bias(i * Q_BLOCK - start)] * GROUP_R, axis=1)
        m = jnp.max(st, axis=0, keepdims=True)
        if has_sink:
            m = jnp.maximum(m, sink)
        m = jnp.where(m == NEG_INF, 0.0, m)
        pb = jnp.exp((st - m).astype(BF16))
        u0 = start // Q_BLOCK
        o_t = _dot(vt_ref[u0], pb[0:Q_BLOCK])
        for u in range(1, n_sub):
            o_t = o_t + _dot(vt_ref[u0 + u], pb[u * Q_BLOCK:(u + 1) * Q_BLOCK])
        denom = o_t[HEAD_DIM:HEAD_DIM + 1, :]
        if has_sink:
            denom = denom + jnp.exp(sink - m)
        o_t = o_t[0:HEAD_DIM, :] / jnp.maximum(denom, 1e-30)
        o_ref[qb * Q_BLOCK:(qb + 1) * Q_BLOCK, :] = _gate_untranspose(
            o_t, gate_ref, gate_branch, qb)


def _banded(qt, k_rows, vt, window, sinks=None, gates=None, gate_branch=None):
    b, h, d, t = qt.shape
    g = h // GROUP_R
    in_specs = []
    args = []
    if sinks is not None:
        in_specs.append(pl.BlockSpec(memory_space=pltpu.SMEM))
        args.append(sinks.astype(F32))
    qw = BANDED_QB * Q_BLOCK
    in_specs += [
        pl.BlockSpec((None, GROUP_R, d, qw), lambda bi, gi, i: (bi, gi, 0, i)),
        pl.BlockSpec((None, t, KV_HEADS * d), lambda bi, gi, i: (bi, 0, 0)),
        pl.BlockSpec((None, None, t // Q_BLOCK, SEL_V_ROWS, Q_BLOCK),
                     lambda bi, gi, i: (bi, gi, 0, 0, 0)),
    ]
    args += [qt, k_rows, vt]
    if gates is not None:
        in_specs.append(pl.BlockSpec((None, None, N_GATE_ROWS, qw),
                                     lambda bi, gi, i: (bi, gi, 0, i)))
        args.append(gates)
    return pl.pallas_call(
        functools.partial(_banded_kernel, window=window, has_sink=sinks is not None,
                          gate_branch=gate_branch if gates is not None else None),
        grid=(b, g, t // qw),
        in_specs=in_specs,
        out_specs=pl.BlockSpec((None, qw, GROUP_R * d), lambda bi, gi, i: (bi, i, gi)),
        out_shape=jax.ShapeDtypeStruct((b, t, h * d), F32),
        compiler_params=_params(("parallel", "parallel", "arbitrary"), VMEM_LIMIT),
        name="banded_attn_w%d" % window,
    )(*args)


def _proj_conv_kernel(x_ref, g_ref, w_ref, cw_ref, o_ref, tail_ref):
    ti = pl.program_id(1)
    w = MIX_W
    xn = _rms_rows(x_ref[...], g_ref[...]).astype(BF16)
    acc = _dot(xn, w_ref[...])
    z = acc[:, 2 * w:3 * w] * acc[:, 0:w]
    zp = jnp.where(ti > 0, tail_ref[...], 0.0)
    row = lax.broadcasted_iota(jnp.int32, z.shape, 0)
    z1 = jnp.where(row == 0, zp[7:8, :], pltpu.roll(z, 1, 0))
    z2 = pltpu.roll(z, 2, 0)
    z2 = jnp.where(row == 0, zp[6:7, :], jnp.where(row == 1, zp[7:8, :], z2))
    cw = cw_ref[...]
    o_ref[...] = acc[:, w:2 * w] * (cw[0:1, :] * z2 + cw[1:2, :] * z1 + cw[2:3, :] * z)
    tail_ref[...] = z[z.shape[0] - 8:, :]


def _proj_conv(x2, b, t, g, w_b, conv_w, tm=512):
    n, dm = x2.shape
    w = MIX_W
    tpb = t // tm
    return pl.pallas_call(
        _proj_conv_kernel,
        grid=(b, tpb),
        in_specs=[
            pl.BlockSpec((tm, dm), lambda bi, ti: (bi * tpb + ti, 0)),
            pl.BlockSpec((1, dm), lambda bi, ti: (0, 0)),
            pl.BlockSpec((dm, 3 * w), lambda bi, ti: (0, 0), pipeline_mode=pl.Buffered(1)),
            pl.BlockSpec((8, w), lambda bi, ti: (0, 0)),
        ],
        out_specs=pl.BlockSpec((None, tm, w), lambda bi, ti: (bi, ti, 0)),
        out_shape=jax.ShapeDtypeStruct((b, t, w), F32),
        scratch_shapes=[pltpu.VMEM((8, w), F32)],
        compiler_params=_params(("parallel", "arbitrary"), VMEM_LIMIT),
        name="proj_conv",
    )(x2, g.reshape(1, dm), w_b.astype(BF16),
      jnp.pad(conv_w.reshape(conv_w.shape[0], w).astype(F32), ((0, 8 - conv_w.shape[0]), (0, 0))))


def _gelu_tanh(x):
    return x * (0.5 * (1.0 + jnp.tanh(0.7978845608028654 * (x + 0.044715 * (x * x * x)))))


def _compress_kernel(tk_ref, tv_ref, pek_ref, pev_ref, wk1_ref, wk2_ref, wv1_ref, wv2_ref,
                     kg_ref, ko_ref, vo_ref):
    nrow = tk_ref.shape[0]

    def mlp(a, pe_ref, w1_ref, w2_ref):
        a0 = (a + pe_ref[0:1, :]).astype(BF16)
        a1 = (a + pe_ref[1:2, :]).astype(BF16)
        p1 = _dot(a0, w1_ref[0])
        p2 = _dot(a1, w1_ref[1])
        hdn = p1 + pltpu.roll(p2, nrow - 1, 0)
        return _dot(_gelu_tanh(hdn).astype(BF16), w2_ref[...])

    kc = mlp(tk_ref[...], pek_ref, wk1_ref, wk2_ref)
    ko_ref[...] = _rms_rows(kc, kg_ref[...]).astype(BF16)
    vo_ref[...] = mlp(tv_ref[...], pev_ref, wv1_ref, wv2_ref).astype(BF16)


def _compress(kc_rows, vc_rows, pos_k, pos_v, wk1, wk2, wv1, wv2, k_gain):
    b, t, kvd = kc_rows.shape
    kv = KV_HEADS
    d = kvd // kv
    nrow = t // CMP_STRIDE
    wide = CMP_STRIDE * kvd
    hid = wk1.shape[1]

    def expand_w1(w1):
        w = w1.reshape(2, CMP_STRIDE, 1, d, hid)
        per_head = []
        for g in range(kv):
            pads = [jnp.zeros_like(w)] * kv
            pads[g] = w
            per_head.append(jnp.concatenate(pads, axis=2).reshape(2, wide, hid))
        return jnp.stack(per_head).astype(BF16)

    def expand_pe(pe):
        return jnp.broadcast_to(pe.reshape(2, CMP_STRIDE, 1, d), (2, CMP_STRIDE, kv, d)).reshape(2, wide)

    tok = pl.BlockSpec((None, nrow, wide), lambda bi, gi: (bi, 0, 0))
    full = lambda shape: pl.BlockSpec(shape, lambda bi, gi: tuple(0 for _ in shape))
    w1_spec = pl.BlockSpec((None, 2, wide, hid), lambda bi, gi: (gi, 0, 0, 0))
    out = pl.BlockSpec((None, nrow, d), lambda bi, gi: (bi * kv + gi, 0, 0))
    return pl.pallas_call(
        _compress_kernel,
        grid=(b, kv),
        in_specs=[tok, tok, full((2, wide)), full((2, wide)), w1_spec, full((hid, d)),
                  w1_spec, full((hid, d)), full((1, d))],
        out_specs=[out, out],
        out_shape=[jax.ShapeDtypeStruct((b * kv, nrow, d), BF16),
                   jax.ShapeDtypeStruct((b * kv, nrow, d), BF16)],
        compiler_params=_params(("parallel", "arbitrary"), VMEM_LIMIT),
        name="nsa_compress",
    )(kc_rows.reshape(b, nrow, wide), vc_rows.reshape(b, nrow, wide), expand_pe(pos_k),
      expand_pe(pos_v), expand_w1(wk1), wk2.astype(BF16), expand_w1(wv1), wv2.astype(BF16),
      k_gain.reshape(1, d))


def _cmp_attn_kernel(qt_ref, kc_ref, vct_ref, gate_ref, o_ref, selt_ref, imp_ref, *, sel_k):
    step = pl.program_id(2)
    rq = GROUP_R * Q_BLOCK
    nc = kc_ref.shape[0]
    ns = selt_ref.shape[0]
    blk = lax.broadcasted_iota(jnp.int32, (ns, Q_BLOCK), 0)

    def attend(nc_eff):
        for qb in range(CMP_QB):
            i = step * CMP_QB + qb
            st = _dot(kc_ref[0:nc_eff, :], _group_q(qt_ref, qb))
            n = lax.broadcasted_iota(jnp.int32, (nc_eff, rq), 0)
            lane = lax.broadcasted_iota(jnp.int32, (nc_eff, rq), 1)
            qpos = i * Q_BLOCK + (lane & (Q_BLOCK - 1))
            st = jnp.where(n * CMP_STRIDE + (CMP_BLOCK - 1) <= qpos, st, NEG_INF)
            m = jnp.max(st, axis=0, keepdims=True)
            m = jnp.where(m == NEG_INF, 0.0, m)
            p = jnp.exp(st - m)
            denom = jnp.sum(p, axis=0, keepdims=True)
            pb = (p / jnp.maximum(denom, 1e-30)).astype(BF16)
            o_t = _dot(vct_ref[:, 0:nc_eff], pb)
            o_ref[qb * Q_BLOCK:(qb + 1) * Q_BLOCK, :] = _gate_untranspose(o_t, gate_ref, 0, qb)
            ss = lax.broadcasted_iota(jnp.int32, (ns, nc_eff), 0) * SEL_BLOCK
            nn = lax.broadcasted_iota(jnp.int32, (ns, nc_eff), 1) * CMP_STRIDE
            overlap_t = jnp.where((nn < ss + SEL_BLOCK) & (nn + (CMP_BLOCK - 1) >= ss),
                                  1.0, 0.0).astype(BF16)
            imp_r = _dot(overlap_t, pb)
            imp = imp_r[:, 0:Q_BLOCK]
            for r in range(1, GROUP_R):
                imp = imp + imp_r[:, r * Q_BLOCK:(r + 1) * Q_BLOCK]
            qp = i * Q_BLOCK + lax.broadcasted_iota(jnp.int32, (ns, Q_BLOCK), 1)
            cur = qp // SEL_BLOCK
            forced = (blk == 0) | (blk == cur) | (blk == cur - 1)
            imp = jnp.where(forced, jnp.inf, imp)
            imp_ref[qb] = jnp.where(blk <= cur, imp, NEG_INF)

    variant_rows = 128
    n_var = nc // variant_rows
    n_ending = (step + 1) * (CMP_QB * Q_BLOCK // CMP_STRIDE) - 1
    variant = jnp.minimum((n_ending + variant_rows - 1) // variant_rows, n_var) - 1
    for v in range(n_var):
        pl.when(variant == v)(functools.partial(attend, (v + 1) * variant_rows))

    blkf = blk.astype(F32)
    imps = [imp_ref[qb] for qb in range(CMP_QB)]
    sels = [jnp.zeros((ns, Q_BLOCK), F32) for _ in range(CMP_QB)]
    for _ in range(sel_k):
        for qb in range(CMP_QB):
            mx = jnp.max(imps[qb], axis=0, keepdims=True)
            first = jnp.min(jnp.where(imps[qb] == mx, blkf, float(ns)), axis=0, keepdims=True)
            hit = blkf == first
            sels[qb] = jnp.where(hit & (mx > NEG_INF), 1.0, sels[qb])
            imps[qb] = jnp.where(hit, NEG_INF, imps[qb])
    for qb in range(CMP_QB):
        selt_ref[:, qb * Q_BLOCK:(qb + 1) * Q_BLOCK] = sels[qb]


def _cmp_attn(qt, k_cmp, v_cmp_t, gates):
    b, h, d, t = qt.shape
    g = h // GROUP_R
    nc = k_cmp.shape[1]
    ns = t // SEL_BLOCK
    qw = CMP_QB * Q_BLOCK
    assert nc % 128 == 0
    return pl.pallas_call(
        functools.partial(_cmp_attn_kernel, sel_k=min(SEL_TOPK, ns)),
        grid=(b, g, t // qw),
        in_specs=[
            pl.BlockSpec((None, GROUP_R, d, qw), lambda bi, gi, i: (bi, gi, 0, i)),
            pl.BlockSpec((None, nc, d), lambda bi, gi, i: (bi * KV_HEADS + gi, 0, 0)),
            pl.BlockSpec((None, d, nc), lambda bi, gi, i: (bi * KV_HEADS + gi, 0, 0)),
            pl.BlockSpec((None, None, N_GATE_ROWS, qw), lambda bi, gi, i: (bi, gi, 0, i)),
        ],
        out_specs=[
            pl.BlockSpec((None, qw, GROUP_R * d), lambda bi, gi, i: (bi, i, gi)),
            pl.BlockSpec((None, None, ns, qw), lambda bi, gi, i: (bi, gi, 0, i)),
        ],
        out_shape=[jax.ShapeDtypeStruct((b, t, h * d), F32),
                   jax.ShapeDtypeStruct((b, g, ns, t), F32)],
        scratch_shapes=[pltpu.VMEM((CMP_QB, ns, Q_BLOCK), F32)],
        compiler_params=_params(("parallel", "parallel", "arbitrary"), VMEM_LIMIT),
        name="nsa_cmp_topk",
    )(qt, k_cmp, v_cmp_t, gates)


def _sel_attn_kernel(qt_ref, ks_ref, vst_ref, selt_ref, gate_ref, o_ref, sa_ref, sb_ref, mta_ref,
                     mtb_ref, m_ref, acc_ref):
    g = pl.program_id(1)
    i = pl.program_id(2)
    tk = SEL_KEY_TILE
    bpt = tk // SEL_BLOCK
    spt = tk // Q_BLOCK
    d = HEAD_DIM
    m_ref[...] = jnp.full(m_ref.shape, NEG_INF, F32)
    acc_ref[...] = jnp.zeros_like(acc_ref)
    n_tiles = (i * Q_BLOCK + Q_BLOCK + tk - 1) // tk
    qpad = _padded_q(qt_ref, g)
    tri = jnp.where(lax.broadcasted_iota(jnp.int32, (Q_BLOCK, Q_BLOCK), 0)
                    <= lax.broadcasted_iota(jnp.int32, (Q_BLOCK, Q_BLOCK), 1), 0.0, NEG_INF)

    def scores(j, s_ref, mt_ref):
        k0 = pl.multiple_of(j * tk, tk)
        st = _dot(ks_ref[pl.ds(k0, tk), :], qpad)
        sel_rows = selt_ref[pl.ds(pl.multiple_of(j * bpt, bpt), bpt), :]
        bias_rows = jnp.where(sel_rows > 0.5, 0.0, NEG_INF)
        parts = []
        for u in range(spt):
            sub = jnp.concatenate(
                [jnp.broadcast_to(bias_rows[s:s + 1, :], (SEL_BLOCK, Q_BLOCK))
                 for s in range(u * Q_BLOCK // SEL_BLOCK, (u + 1) * Q_BLOCK // SEL_BLOCK)], axis=0)
            parts.append(sub + jnp.where(j * spt + u == i, tri, 0.0))
        bias = jnp.concatenate(parts, axis=0)
        st = st + jnp.concatenate([bias] * GROUP_R, axis=1)
        s_ref[...] = st
        mt_ref[...] = jnp.max(st, axis=0, keepdims=True)

    def consume(j, s_ref, mt_ref):
        m_prev = m_ref[...]
        m_new = jnp.maximum(m_prev, mt_ref[...])
        m_safe = jnp.where(m_new == NEG_INF, 0.0, m_new)
        alpha = jnp.exp2(m_prev - m_safe)
        p = jnp.exp2((s_ref[...] - m_safe).astype(BF16))
        acc_ref[...] = alpha * acc_ref[...] + _dot(vst_ref[j], p)
        m_ref[...] = m_new

    scores(0, sa_ref, mta_ref)
    n_pairs = (n_tiles - 1) // 2

    def step(jj, carry):
        j = 2 * jj
        scores(j + 1, sb_ref, mtb_ref)
        consume(j, sa_ref, mta_ref)
        scores(j + 2, sa_ref, mta_ref)
        consume(j + 1, sb_ref, mtb_ref)
        return carry

    lax.fori_loop(0, n_pairs, step, 0)
    j_rem = 2 * n_pairs

    @pl.when(n_tiles - j_rem == 2)
    def _():
        scores(j_rem + 1, sb_ref, mtb_ref)

    consume(j_rem, sa_ref, mta_ref)

    @pl.when(n_tiles - j_rem == 2)
    def _():
        consume(j_rem + 1, sb_ref, mtb_ref)

    o_t = acc_ref[0:d, :] / jnp.maximum(acc_ref[d:d + 1, :], 1e-30)
    o_ref[...] = _gate_untranspose(o_t, gate_ref, 1)


def _sel_attn(qt, ks_rows, vs_t, sel_t, gates):
    b, h, d, t = qt.shape
    g = h // GROUP_R
    ns = sel_t.shape[2]
    tk = SEL_KEY_TILE
    rq = GROUP_R * Q_BLOCK
    return pl.pallas_call(
        _sel_attn_kernel,
        grid=(b, g, t // Q_BLOCK),
        in_specs=[
            pl.BlockSpec((None, GROUP_R, d, Q_BLOCK), lambda bi, gi, i: (bi, gi, 0, i)),
            pl.BlockSpec((None, t, KV_HEADS * d), lambda bi, gi, i: (bi, 0, 0)),
            pl.BlockSpec((None, None, t // tk, SEL_V_ROWS, tk), lambda bi, gi, i: (bi, gi, 0, 0, 0)),
            pl.BlockSpec((None, None, ns, Q_BLOCK), lambda bi, gi, i: (bi, gi, 0, i)),
            pl.BlockSpec((None, None, N_GATE_ROWS, Q_BLOCK), lambda bi, gi, i: (bi, gi, 0, i)),
        ],
        out_specs=pl.BlockSpec((None, Q_BLOCK, GROUP_R * d), lambda bi, gi, i: (bi, i, gi)),
        out_shape=jax.ShapeDtypeStruct((b, t, h * d), F32),
        scratch_shapes=[pltpu.VMEM((tk, rq), F32), pltpu.VMEM((tk, rq), F32),
                        pltpu.VMEM((1, rq), F32), pltpu.VMEM((1, rq), F32),
                        pltpu.VMEM((1, rq), F32), pltpu.VMEM((SEL_V_ROWS, rq), F32)],
        compiler_params=_params(("parallel", "parallel", "arbitrary"), VMEM_LIMIT),
        name="nsa_selected",
    )(qt, ks_rows, vs_t, sel_t, gates)


def _proj_retention_kernel(lg_ref, x_ref, g_ref, w_ref, cos_ref, sin_ref, ng_ref, o_ref, r_ref):
    ti = pl.program_id(1)
    c = RET_CHUNK
    h, dk, dv = RET_HEADS, RET_QK_DIM, RET_V_DIM
    half = dk // 2
    k0, v0, g0 = h * dk, 2 * h * dk, 2 * h * dk + h * dv

    @pl.when(ti == 0)
    def _():
        r_ref[...] = jnp.zeros_like(r_ref)

    xn = _rms_rows(x_ref[...], g_ref[...]).astype(BF16)
    acc = _dot(xn, w_ref[...])
    ii = lax.broadcasted_iota(jnp.int32, (c, c), 0)
    jj = lax.broadcasted_iota(jnp.int32, (c, c), 1)
    dist = (ii - jj).astype(F32)
    jcol = lax.broadcasted_iota(jnp.int32, (c, 1), 0).astype(F32)
    for hh in range(h):
        lg = lg_ref[hh]
        dmask = jnp.where(dist >= 0, jnp.exp(dist * lg), 0.0)
        xi = jnp.exp((jcol + 1.0) * lg)
        zeta = jnp.exp((c - 1.0 - jcol) * lg)
        decay = jnp.exp(jnp.zeros((1, dv), F32) + c * lg)
        for cc in range(acc.shape[0] // c):
            rows = slice(cc * c, (cc + 1) * c)
            cosf = cos_ref[rows, :]
            sinf = sin_ref[rows, :]

            def rot(x):
                return x * cosf + jnp.concatenate([x[:, half:], x[:, :half]], axis=1) * sinf

            q = rot(acc[rows, hh * dk:(hh + 1) * dk])
            k = rot(acc[rows, k0 + hh * dk:k0 + (hh + 1) * dk]) * (dk ** -0.5)
            vb = acc[rows, v0 + hh * dv:v0 + (hh + 1) * dv].astype(BF16)
            att = _dot_nt(q.astype(BF16), k.astype(BF16)) * dmask
            o = _dot(att.astype(BF16), vb)
            r_prev = r_ref[hh]
            o = o + _dot((q * xi).astype(BF16), r_prev.astype(BF16))
            s_chunk = _dot_tn((k * zeta).astype(BF16), vb)
            r_ref[hh] = r_prev * decay + s_chunk
            mu = jnp.mean(o, axis=-1, keepdims=True)
            var = jnp.mean(jnp.square(o - mu), axis=-1, keepdims=True)
            on = (o - mu) * lax.rsqrt(var + EPS)
            gt = acc[rows, g0 + hh * dv:g0 + (hh + 1) * dv]
            o_ref[rows, hh * dv:(hh + 1) * dv] = (
                (gt * _sigmoid(gt)) * (on * ng_ref[:, hh * dv:(hh + 1) * dv]))


def _proj_retention(x2, b, t, g, w_d, norm_gain, tm=512):
    n, dm = x2.shape
    h, dk, dv = RET_HEADS, RET_QK_DIM, RET_V_DIM
    half = dk // 2
    tpb = t // tm
    inv = ROPE_BASE ** (-jnp.arange(half, dtype=F32) / half)
    ang = jnp.arange(t).astype(F32)[:, None] * inv[None, :]
    cos = jnp.cos(ang)
    sin = jnp.sin(ang)
    cosf = jnp.concatenate([cos, cos], axis=-1)
    sinf = jnp.concatenate([-sin, sin], axis=-1)
    log_gamma = jnp.log(1.0 - 2.0 ** (-5.0 - jnp.arange(h, dtype=F32)))
    return pl.pallas_call(
        _proj_retention_kernel,
        grid=(b, tpb),
        in_specs=[
            pl.BlockSpec(memory_space=pltpu.SMEM),
            pl.BlockSpec((tm, dm), lambda bi, ti: (bi * tpb + ti, 0)),
            pl.BlockSpec((1, dm), lambda bi, ti: (0, 0)),
            pl.BlockSpec(w_d.shape, lambda bi, ti: (0, 0), pipeline_mode=pl.Buffered(1)),
            pl.BlockSpec((tm, dk), lambda bi, ti: (ti, 0)),
            pl.BlockSpec((tm, dk), lambda bi, ti: (ti, 0)),
            pl.BlockSpec((1, h * dv), lambda bi, ti: (0, 0)),
        ],
        out_specs=pl.BlockSpec((None, tm, h * dv), lambda bi, ti: (bi, ti, 0)),
        out_shape=jax.ShapeDtypeStruct((b, t, h * dv), F32),
        scratch_shapes=[pltpu.VMEM((h, dk, dv), F32)],
        compiler_params=_params(("parallel", "arbitrary"), VMEM_LIMIT),
        name="proj_retention",
    )(log_gamma, x2, g.reshape(1, dm), w_d.astype(BF16), cosf, sinf, norm_gain.reshape(1, h * dv))


def _merge_kernel(x_ref, g_ref, wgate_ref, bias_ref, ya_ref, yb_ref, yc0_ref, yc1_ref, yc2_ref,
                  yd_ref, wb_ref, wo_ref, o_ref):
    x = x_ref[...]
    d = x.shape[1]
    u = _rms_rows(x, g_ref[...]).astype(BF16)
    ys = (ya_ref[...], yb_ref[...], yc0_ref[...] + yc1_ref[...] + yc2_ref[...], yd_ref[...])
    merged = jnp.zeros(x.shape, F32)
    for n in range(N_BRANCH):
        logits = _dot(u, wgate_ref[:, n * d:(n + 1) * d]) + bias_ref[:, n * d:(n + 1) * d]
        merged = merged + _sigmoid(logits) * _dot(ys[n].astype(BF16), wb_ref[n])
    o_ref[...] = x + _dot(merged.astype(BF16), wo_ref[...])


def _merge(x2, g, w_gate, bias, ys, w_branch, w_out, tm=512):
    n, d = x2.shape
    w = MIX_W
    row = lambda width: pl.BlockSpec((tm, width), lambda i: (i, 0))
    full = lambda shape: pl.BlockSpec(shape, lambda i: tuple(0 for _ in shape),
                                      pipeline_mode=pl.Buffered(1))
    return pl.pallas_call(
        _merge_kernel,
        grid=(n // tm,),
        in_specs=[row(d), full((1, d)), full((d, N_BRANCH * d)), full((1, N_BRANCH * d))]
        + [row(w)] * 6 + [full((N_BRANCH, w, d)), full((d, d))],
        out_specs=row(d),
        out_shape=jax.ShapeDtypeStruct((n, d), F32),
        compiler_params=_params(("parallel",), VMEM_LIMIT),
        name="merge_out",
    )(x2, g.reshape(1, d), w_gate.astype(BF16), bias.reshape(1, N_BRANCH * d),
      *[y.reshape(n, w) for y in ys], w_branch.astype(BF16), w_out.astype(BF16))


def _mixers(x2, b, t, mix_norm, w_in, merge_gate_bias, swa_q_gain, swa_k_gain, swa_sinks, conv_w,
            nsa_q_gain, nsa_k_gain, cmp_pos_k, cmp_pos_v, cmp_wk1, cmp_wk2, cmp_wv1, cmp_wv2,
            ret_norm_gain, w_branch, w_out):
    hd = HEAD_DIM
    swa_q, swa_kv = SWA_HEADS * hd, SWA_KV_HEADS * hd
    nsa_q, nsa_kv = NSA_HEADS * hd, NSA_KV_HEADS * hd
    ret_qk, ret_v = RET_HEADS * RET_QK_DIM, RET_HEADS * RET_V_DIM
    n_gate = NSA_HEADS * 3
    o_a = 0
    o_b = o_a + swa_q + 2 * swa_kv
    o_c = o_b + 3 * MIX_W
    o_cg = o_c + nsa_q + 6 * nsa_kv
    o_d = o_cg + n_gate
    o_g = o_d + 2 * ret_qk + 2 * ret_v

    per_g = GROUP_R * 3
    w_gates = [jnp.pad(w_in[:, o_cg + gi * per_g:o_cg + (gi + 1) * per_g],
                       ((0, 0), (0, N_GATE_ROWS - per_g))) for gi in range(NSA_KV_HEADS)]
    w_t = jnp.concatenate([w_in[:, o_a:o_b], w_in[:, o_c:o_cg]] + w_gates, axis=1).T.astype(BF16)
    head_gains = jnp.stack([swa_q_gain, swa_k_gain, nsa_q_gain, nsa_k_gain[1], nsa_k_gain[2]])
    head_gains = jnp.broadcast_to(head_gains.astype(F32)[:, :, None], (5, hd, 128))
    (a_qt, a_k, a_vt, c_qt, c_qt2, c_kc, c_vc, c_ks, c_vst, c_kw, c_vwt, gates_c) = _proj_heads(
        x2, b, t, mix_norm, w_t, head_gains)

    y_a = _banded(a_qt, a_k, a_vt, SWA_WINDOW, sinks=swa_sinks)

    y_b = _proj_conv(x2, b, t, mix_norm, w_in[:, o_b:o_c], conv_w)

    k_cmp, v_cmp = _compress(c_kc, c_vc, cmp_pos_k, cmp_pos_v, cmp_wk1, cmp_wk2, cmp_wv1, cmp_wv2,
                             nsa_k_gain[0])
    y_cmp, sel_t = _cmp_attn(c_qt, k_cmp, v_cmp.transpose(0, 2, 1), gates_c)
    y_sel = _sel_attn(c_qt2, c_ks, c_vst, sel_t, gates_c)
    y_win = _banded(c_qt, c_kw, c_vwt, NSA_WINDOW, gates=gates_c, gate_branch=2)

    y_d = _proj_retention(x2, b, t, mix_norm, w_in[:, o_d:o_g], ret_norm_gain)

    return _merge(x2, mix_norm, w_in[:, o_g:], merge_gate_bias,
                  (y_a, y_b, y_cmp, y_sel, y_win, y_d), w_branch, w_out)


def kernel(x, ffn1_norm, ffn1_w_gate, ffn1_w_up, ffn1_w_down, mix_norm, w_in, merge_gate_bias, swa_q_gain, swa_k_gain, swa_sinks, conv_w, nsa_q_gain, nsa_k_gain, cmp_pos_k, cmp_pos_v, cmp_wk1, cmp_wk2, cmp_wv1, cmp_wv2, ret_norm_gain, w_branch, w_out, ffn2_norm, ffn2_w_gate, ffn2_w_up, ffn2_w_down):
    b, t, d = x.shape
    x2 = x.reshape(b * t, d)
    for l in range(ffn1_norm.shape[0]):
        x2 = _ffn(x2, ffn1_norm[l], ffn1_w_gate[l], ffn1_w_up[l], ffn1_w_down[l])
        x2 = _mixers(x2, b, t, mix_norm[l], w_in[l], merge_gate_bias[l], swa_q_gain[l],
                     swa_k_gain[l], swa_sinks[l], conv_w[l], nsa_q_gain[l], nsa_k_gain[l],
                     cmp_pos_k[l], cmp_pos_v[l], cmp_wk1[l], cmp_wk2[l], cmp_wv1[l], cmp_wv2[l],
                     ret_norm_gain[l], w_branch[l], w_out[l])
        x2 = _ffn(x2, ffn2_norm[l], ffn2_w_gate[l], ffn2_w_up[l], ffn2_w_down[l])
    return x2.reshape(b, t, d)
```

```python
import functools

import jax
import jax.numpy as jnp
from jax import lax
from jax.experimental import pallas as pl
from jax.experimental.pallas import tpu as pltpu

F32 = jnp.float32
BF16 = jnp.bfloat16

HEAD_DIM = 64
Q_BLOCK = 128
MIX_W = 512
N_BRANCH = 4
SWA_HEADS = 8
SWA_KV_HEADS = 2
SWA_WINDOW = 128
NSA_HEADS = 8
NSA_KV_HEADS = 2
CMP_BLOCK = 32
CMP_STRIDE = 16
SEL_BLOCK = 64
SEL_TOPK = 16
NSA_WINDOW = 512
RET_HEADS = 4
RET_QK_DIM = 64
RET_V_DIM = 128
RET_CHUNK = 128
ROPE_BASE = 10000.0
EPS = 1e-6
GROUP_R = 4
KV_HEADS = 2
N_GATE_ROWS = 16
SEL_KEY_TILE = 512
SEL_V_ROWS = HEAD_DIM + 16
PROJ_TM = 512
BANDED_QB = 8
CMP_QB = 4
VMEM_LIMIT = 52 * 1024 * 1024

NEG_INF = float("-inf")
LOG2_E = 1.4426950408889634


def _params(sem, vmem=None):
    return pltpu.CompilerParams(dimension_semantics=sem, vmem_limit_bytes=vmem)


def _sigmoid(x):
    return 1.0 / (1.0 + jnp.exp(-x))


def _rms_rows(x, g):
    return x * lax.rsqrt(jnp.mean(x * x, axis=-1, keepdims=True) + EPS) * g


def _dot(a, b):
    return jnp.dot(a, b, preferred_element_type=F32)


def _dot_nt(a, b):
    return lax.dot_general(a, b, (((1,), (1,)), ((), ())), preferred_element_type=F32)


def _dot_tn(a, b):
    return lax.dot_general(a, b, (((0,), (0,)), ((), ())), preferred_element_type=F32)


def _ffn_kernel(x_ref, g_ref, wg_ref, wu_ref, wd_ref, o_ref, *, n_chunks):
    x = x_ref[...]
    xn = _rms_rows(x, g_ref[...]).astype(BF16)
    tf = wg_ref.shape[1] // n_chunks
    acc = None
    for f in range(n_chunks):
        a = _dot(xn, wg_ref[:, f * tf:(f + 1) * tf])
        b = _dot(xn, wu_ref[:, f * tf:(f + 1) * tf])
        h = ((a * _sigmoid(a)) * b).astype(BF16)
        part = _dot(h, wd_ref[f * tf:(f + 1) * tf, :])
        acc = part if acc is None else acc + part
    o_ref[...] = x + 0.5 * acc


def _ffn(x2, g, wg, wu, wd, tm=512, n_chunks=2):
    n, d = x2.shape
    dff = wg.shape[1]
    assert dff % (n_chunks * 128) == 0
    resident = lambda shape: pl.BlockSpec(shape, lambda i: (0, 0), pipeline_mode=pl.Buffered(1))
    return pl.pallas_call(
        functools.partial(_ffn_kernel, n_chunks=n_chunks),
        grid=(n // tm,),
        in_specs=[
            pl.BlockSpec((tm, d), lambda i: (i, 0)),
            pl.BlockSpec((1, d), lambda i: (0, 0)),
            resident((d, dff)), resident((d, dff)), resident((dff, d)),
        ],
        out_specs=pl.BlockSpec((tm, d), lambda i: (i, 0)),
        out_shape=jax.ShapeDtypeStruct((n, d), F32),
        compiler_params=_params(("parallel",), VMEM_LIMIT),
        name="ffn",
    )(x2, g.reshape(1, d), wg.astype(BF16), wu.astype(BF16), wd.astype(BF16))


def _proj_heads_kernel(x_ref, g_ref, wt_ref, hg_ref, aq_ref, ak_ref, av_ref, cq_ref, cq2_ref,
                       ckc_ref, cvc_ref, cks_ref, cvs_ref, ckw_ref, cvw_ref, gt_ref):
    d = HEAD_DIM
    xn = _rms_rows(x_ref[...], g_ref[...]).astype(BF16)
    acc = _dot_nt(wt_ref[...], xn)
    tm = xn.shape[0]
    lane_tiles = tm // 128
    scale = d ** -0.5
    kv_w = KV_HEADS * d

    def head_norm(row0, gain_idx, mult):
        hb = acc[row0:row0 + d]
        gain = jnp.concatenate([hg_ref[gain_idx]] * lane_tiles, axis=1)
        y = hb * lax.rsqrt(jnp.mean(hb * hb, axis=0, keepdims=True) + EPS) * gain
        return y * mult if mult != 1.0 else y

    def q_heads(row0, gain_idx, out_ref, n_heads, log2_ref=None):
        for h in range(n_heads):
            y = head_norm(row0 + h * d, gain_idx, 1.0)
            out_ref[h] = (y * scale).astype(BF16)
            if log2_ref is not None:
                log2_ref[h] = (y * (scale * LOG2_E)).astype(BF16)

    def k_rows(row0, gain_idx):
        return jnp.concatenate([head_norm(row0 + g * d, gain_idx, 1.0) for g in range(KV_HEADS)],
                               axis=0).T

    def v_tiles(row0, out_ref):
        for g in range(KV_HEADS):
            for u in range(lane_tiles):
                out_ref[g, u, 0:d, :] = acc[row0 + g * d:row0 + (g + 1) * d,
                                            u * 128:(u + 1) * 128].astype(BF16)
                out_ref[g, u, d:SEL_V_ROWS, :] = jnp.ones((SEL_V_ROWS - d, 128), BF16)

    row = 0
    q_heads(row, 0, aq_ref, SWA_HEADS)
    row += SWA_HEADS * d
    ak_ref[...] = k_rows(row, 1).astype(BF16)
    row += kv_w
    v_tiles(row, av_ref)
    row += kv_w
    q_heads(row, 2, cq_ref, NSA_HEADS, cq2_ref)
    row += NSA_HEADS * d
    ckc_ref[...] = acc[row:row + kv_w].T
    row += kv_w
    cvc_ref[...] = acc[row:row + kv_w].T
    row += kv_w
    cks_ref[...] = k_rows(row, 3).astype(BF16)
    row += kv_w
    for g in range(KV_HEADS):
        cvs_ref[g, 0, 0:d, :] = acc[row + g * d:row + (g + 1) * d].astype(BF16)
        cvs_ref[g, 0, d:SEL_V_ROWS, :] = jnp.ones((SEL_V_ROWS - d, tm), BF16)
    row += kv_w
    ckw_ref[...] = k_rows(row, 4).astype(BF16)
    row += kv_w
    v_tiles(row, cvw_ref)
    row += kv_w
    for g in range(KV_HEADS):
        gt_ref[g] = _sigmoid(acc[row + g * N_GATE_ROWS:row + (g + 1) * N_GATE_ROWS])


def _proj_heads(x2, b, t, g, w_t, head_gains):
    n, dm = x2.shape
    d = HEAD_DIM
    tm = PROJ_TM
    assert tm == SEL_KEY_TILE and t % tm == 0
    rows = w_t.shape[0]
    tpb = t // tm
    lt = tm // 128
    kv = KV_HEADS
    qt_spec = lambda h: pl.BlockSpec((None, h, d, tm), lambda bi, ti: (bi, 0, 0, ti))
    row_spec = pl.BlockSpec((None, tm, kv * d), lambda bi, ti: (bi, ti, 0))
    vt_spec = pl.BlockSpec((None, kv, lt, SEL_V_ROWS, 128), lambda bi, ti: (bi, 0, ti, 0, 0))
    sds = jax.ShapeDtypeStruct
    return pl.pallas_call(
        _proj_heads_kernel,
        grid=(b, tpb),
        in_specs=[
            pl.BlockSpec((tm, dm), lambda bi, ti: (bi * tpb + ti, 0)),
            pl.BlockSpec((1, dm), lambda bi, ti: (0, 0)),
            pl.BlockSpec((rows, dm), lambda bi, ti: (0, 0), pipeline_mode=pl.Buffered(1)),
            pl.BlockSpec(head_gains.shape, lambda bi, ti: (0, 0, 0)),
        ],
        out_specs=[
            qt_spec(SWA_HEADS), row_spec, vt_spec,
            qt_spec(NSA_HEADS), qt_spec(NSA_HEADS), row_spec, row_spec, row_spec,
            pl.BlockSpec((None, kv, 1, SEL_V_ROWS, tm), lambda bi, ti: (bi, 0, ti, 0, 0)),
            row_spec, vt_spec,
            pl.BlockSpec((None, kv, N_GATE_ROWS, tm), lambda bi, ti: (bi, 0, 0, ti)),
        ],
        out_shape=[
            sds((b, SWA_HEADS, d, t), BF16), sds((b, t, kv * d), BF16),
            sds((b, kv, t // 128, SEL_V_ROWS, 128), BF16),
            sds((b, NSA_HEADS, d, t), BF16), sds((b, NSA_HEADS, d, t), BF16),
            sds((b, t, kv * d), F32), sds((b, t, kv * d), F32),
            sds((b, t, kv * d), BF16), sds((b, kv, tpb, SEL_V_ROWS, tm), BF16),
            sds((b, t, kv * d), BF16), sds((b, kv, t // 128, SEL_V_ROWS, 128), BF16),
            sds((b, kv, N_GATE_ROWS, t), F32),
        ],
        compiler_params=_params(("parallel", "parallel"), VMEM_LIMIT),
        name="proj_heads",
    )(x2, g.reshape(1, dm), w_t, head_gains)


def _group_q(qt_ref, qb=0):
    return jnp.concatenate(
        [qt_ref[r, :, qb * Q_BLOCK:(qb + 1) * Q_BLOCK] for r in range(GROUP_R)], axis=1)


def _padded_q(qt_ref, g, qb=0):
    q4 = _group_q(qt_ref, qb)
    z = jnp.zeros_like(q4)
    return jnp.where(g == 0, jnp.concatenate([q4, z], axis=0), jnp.concatenate([z, q4], axis=0))


def _gate_untranspose(o_t, gate_ref, branch, qb=0):
    outs = []
    for r in range(GROUP_R):
        blk = o_t[:, r * Q_BLOCK:(r + 1) * Q_BLOCK]
        if gate_ref is not None:
            c = r * 3 + branch
            blk = blk * gate_ref[c:c + 1, qb * Q_BLOCK:(qb + 1) * Q_BLOCK]
        outs.append(blk.T)
    return jnp.concatenate(outs, axis=1)


def _banded_kernel(*refs, window, has_sink, gate_branch):
    refs = list(refs)
    sink_ref = refs.pop(0) if has_sink else None
    qt_ref, k_ref, vt_ref = refs[:3]
    gate_ref = refs[3] if gate_branch is not None else None
    o_ref = refs[-1]
    g = pl.program_id(1)
    rq = GROUP_R * Q_BLOCK
    n_sub = window // Q_BLOCK + 1
    span = n_sub * Q_BLOCK
    if has_sink:
        lrow = lax.broadcasted_iota(jnp.int32, (1, rq), 1)
        sink = jnp.zeros((1, rq), F32)
        for r in range(GROUP_R):
            sink = jnp.where((lrow >= r * Q_BLOCK) & (lrow < (r + 1) * Q_BLOCK),
                             sink_ref[g * GROUP_R + r], sink)
    step = pl.program_id(2)
    row = lax.broadcasted_iota(jnp.int32, (span, Q_BLOCK), 0)
    qcol = lax.broadcasted_iota(jnp.int32, (span, Q_BLOCK), 1)

    def band_bias(offset):
        diff = offset + qcol - row
        return jnp.where((diff >= 0) & (diff < window), 0.0, NEG_INF)

    for qb in range(BANDED_QB):
        i = step * BANDED_QB + qb
        start = pl.multiple_of(jnp.maximum(i * Q_BLOCK - window, 0), Q_BLOCK)
        st = _dot(k_ref[pl.ds(start, span), :], _padded_q(qt_ref, g, qb))
        st = st + jnp.concatenate([band_bias(i * Q_BLOCK - start)] * GROUP_R, axis=1)
        m = jnp.max(st, axis=0, keepdims=True)
        if has_sink:
            m = jnp.maximum(m, sink)
        m = jnp.where(m == NEG_INF, 0.0, m)
        pb = jnp.exp((st - m).astype(BF16))
        u0 = start // Q_BLOCK
        o_t = _dot(vt_ref[u0], pb[0:Q_BLOCK])
        for u in range(1, n_sub):
            o_t = o_t + _dot(vt_ref[u0 + u], pb[u * Q_BLOCK:(u + 1) * Q_BLOCK])
        denom = o_t[HEAD_DIM:HEAD_DIM + 1, :]
        if has_sink:
            denom = denom + jnp.exp(sink - m)
        o_t = o_t[0:HEAD_DIM, :] / jnp.maximum(denom, 1e-30)
        o_ref[qb * Q_BLOCK:(qb + 1) * Q_BLOCK, :] = _gate_untranspose(
            o_t, gate_ref, gate_branch, qb)


def _banded(qt, k_rows, vt, window, sinks=None, gates=None, gate_branch=None):
    b, h, d, t = qt.shape
    g = h // GROUP_R
    in_specs = []
    args = []
    if sinks is not None:
        in_specs.append(pl.BlockSpec(memory_space=pltpu.SMEM))
        args.append(sinks.astype(F32))
    qw = BANDED_QB * Q_BLOCK
    in_specs += [
        pl.BlockSpec((None, GROUP_R, d, qw), lambda bi, gi, i: (bi, gi, 0, i)),
        pl.BlockSpec((None, t, KV_HEADS * d), lambda bi, gi, i: (bi, 0, 0)),
        pl.BlockSpec((None, None, t // Q_BLOCK, SEL_V_ROWS, Q_BLOCK),
                     lambda bi, gi, i: (bi, gi, 0, 0, 0)),
    ]
    args += [qt, k_rows, vt]
    if gates is not None:
        in_specs.append(pl.BlockSpec((None, None, N_GATE_ROWS, qw),
                                     lambda bi, gi, i: (bi, gi, 0, i)))
        args.append(gates)
    return pl.pallas_call(
        functools.partial(_banded_kernel, window=window, has_sink=sinks is not None,
                          gate_branch=gate_branch if gates is not None else None),
        grid=(b, g, t // qw),
        in_specs=in_specs,
        out_specs=pl.BlockSpec((None, qw, GROUP_R * d), lambda bi, gi, i: (bi, i, gi)),
        out_shape=jax.ShapeDtypeStruct((b, t, h * d), F32),
        compiler_params=_params(("parallel", "parallel", "arbitrary"), VMEM_LIMIT),
        name="banded_attn_w%d" % window,
    )(*args)


def _proj_conv_kernel(x_ref, g_ref, w_ref, cw_ref, o_ref, tail_ref):
    ti = pl.program_id(1)
    w = MIX_W
    xn = _rms_rows(x_ref[...], g_ref[...]).astype(BF16)
    acc = _dot(xn, w_ref[...])
    z = acc[:, 2 * w:3 * w] * acc[:, 0:w]
    zp = jnp.where(ti > 0, tail_ref[...], 0.0)
    row = lax.broadcasted_iota(jnp.int32, z.shape, 0)
    z1 = jnp.where(row == 0, zp[7:8, :], pltpu.roll(z, 1, 0))
    z2 = pltpu.roll(z, 2, 0)
    z2 = jnp.where(row == 0, zp[6:7, :], jnp.where(row == 1, zp[7:8, :], z2))
    cw = cw_ref[...]
    o_ref[...] = acc[:, w:2 * w] * (cw[0:1, :] * z2 + cw[1:2, :] * z1 + cw[2:3, :] * z)
    tail_ref[...] = z[z.shape[0] - 8:, :]


def _proj_conv(x2, b, t, g, w_b, conv_w, tm=512):
    n, dm = x2.shape
    w = MIX_W
    tpb = t // tm
    return pl.pallas_call(
        _proj_conv_kernel,
        grid=(b, tpb),
        in_specs=[
            pl.BlockSpec((tm, dm), lambda bi, ti: (bi * tpb + ti, 0)),
            pl.BlockSpec((1, dm), lambda bi, ti: (0, 0)),
            pl.BlockSpec((dm, 3 * w), lambda bi, ti: (0, 0), pipeline_mode=pl.Buffered(1)),
            pl.BlockSpec((8, w), lambda bi, ti: (0, 0)),
        ],
        out_specs=pl.BlockSpec((None, tm, w), lambda bi, ti: (bi, ti, 0)),
        out_shape=jax.ShapeDtypeStruct((b, t, w), F32),
        scratch_shapes=[pltpu.VMEM((8, w), F32)],
        compiler_params=_params(("parallel", "arbitrary"), VMEM_LIMIT),
        name="proj_conv",
    )(x2, g.reshape(1, dm), w_b.astype(BF16),
      jnp.pad(conv_w.reshape(conv_w.shape[0], w).astype(F32), ((0, 8 - conv_w.shape[0]), (0, 0))))


def _gelu_tanh(x):
    return x * (0.5 * (1.0 + jnp.tanh(0.7978845608028654 * (x + 0.044715 * (x * x * x)))))


def _compress_kernel(tk_ref, tv_ref, pek_ref, pev_ref, wk1_ref, wk2_ref, wv1_ref, wv2_ref,
                     kg_ref, ko_ref, vo_ref):
    nrow = tk_ref.shape[0]

    def mlp(a, pe_ref, w1_ref, w2_ref):
        a0 = (a + pe_ref[0:1, :]).astype(BF16)
        a1 = (a + pe_ref[1:2, :]).astype(BF16)
        p1 = _dot(a0, w1_ref[0])
        p2 = _dot(a1, w1_ref[1])
        hdn = p1 + pltpu.roll(p2, nrow - 1, 0)
        return _dot(_gelu_tanh(hdn).astype(BF16), w2_ref[...])

    kc = mlp(tk_ref[...], pek_ref, wk1_ref, wk2_ref)
    ko_ref[...] = _rms_rows(kc, kg_ref[...]).astype(BF16)
    vo_ref[...] = mlp(tv_ref[...], pev_ref, wv1_ref, wv2_ref).astype(BF16)


def _compress(kc_rows, vc_rows, pos_k, pos_v, wk1, wk2, wv1, wv2, k_gain):
    b, t, kvd = kc_rows.shape
    kv = KV_HEADS
    d = kvd // kv
    nrow = t // CMP_STRIDE
    wide = CMP_STRIDE * kvd
    hid = wk1.shape[1]

    def expand_w1(w1):
        w = w1.reshape(2, CMP_STRIDE, 1, d, hid)
        per_head = []
        for g in range(kv):
            pads = [jnp.zeros_like(w)] * kv
            pads[g] = w
            per_head.append(jnp.concatenate(pads, axis=2).reshape(2, wide, hid))
        return jnp.stack(per_head).astype(BF16)

    def expand_pe(pe):
        return jnp.broadcast_to(pe.reshape(2, CMP_STRIDE, 1, d), (2, CMP_STRIDE, kv, d)).reshape(2, wide)

    tok = pl.BlockSpec((None, nrow, wide), lambda bi, gi: (bi, 0, 0))
    full = lambda shape: pl.BlockSpec(shape, lambda bi, gi: tuple(0 for _ in shape))
    w1_spec = pl.BlockSpec((None, 2, wide, hid), lambda bi, gi: (gi, 0, 0, 0))
    out = pl.BlockSpec((None, nrow, d), lambda bi, gi: (bi * kv + gi, 0, 0))
    return pl.pallas_call(
        _compress_kernel,
        grid=(b, kv),
        in_specs=[tok, tok, full((2, wide)), full((2, wide)), w1_spec, full((hid, d)),
                  w1_spec, full((hid, d)), full((1, d))],
        out_specs=[out, out],
        out_shape=[jax.ShapeDtypeStruct((b * kv, nrow, d), BF16),
                   jax.ShapeDtypeStruct((b * kv, nrow, d), BF16)],
        compiler_params=_params(("parallel", "arbitrary"), VMEM_LIMIT),
        name="nsa_compress",
    )(kc_rows.reshape(b, nrow, wide), vc_rows.reshape(b, nrow, wide), expand_pe(pos_k),
      expand_pe(pos_v), expand_w1(wk1), wk2.astype(BF16), expand_w1(wv1), wv2.astype(BF16),
      k_gain.reshape(1, d))


def _cmp_attn_kernel(qt_ref, kc_ref, vct_ref, gate_ref, o_ref, selt_ref, imp_ref, *, sel_k):
    step = pl.program_id(2)
    rq = GROUP_R * Q_BLOCK
    nc = kc_ref.shape[0]
    ns = selt_ref.shape[0]
    blk = lax.broadcasted_iota(jnp.int32, (ns, Q_BLOCK), 0)

    def attend(nc_eff):
        n = lax.broadcasted_iota(jnp.int32, (nc_eff, rq), 0)
        lane = lax.broadcasted_iota(jnp.int32, (nc_eff, rq), 1)
        end_minus_q = n * CMP_STRIDE + (CMP_BLOCK - 1) - (lane & (Q_BLOCK - 1))
        ss = lax.broadcasted_iota(jnp.int32, (ns, nc_eff), 0) * SEL_BLOCK
        nn = lax.broadcasted_iota(jnp.int32, (ns, nc_eff), 1) * CMP_STRIDE
        overlap_t = jnp.where((nn < ss + SEL_BLOCK) & (nn + (CMP_BLOCK - 1) >= ss),
                              1.0, 0.0).astype(BF16)
        for qb in range(CMP_QB):
            i = step * CMP_QB + qb
            st = _dot(kc_ref[0:nc_eff, :], _group_q(qt_ref, qb))
            st = jnp.where(end_minus_q <= i * Q_BLOCK, st, NEG_INF)
            m = jnp.max(st, axis=0, keepdims=True)
            m = jnp.where(m == NEG_INF, 0.0, m)
            p = jnp.exp(st - m)
            denom = jnp.sum(p, axis=0, keepdims=True)
            pb = (p / jnp.maximum(denom, 1e-30)).astype(BF16)
            o_t = _dot(vct_ref[:, 0:nc_eff], pb)
            o_ref[qb * Q_BLOCK:(qb + 1) * Q_BLOCK, :] = _gate_untranspose(o_t, gate_ref, 0, qb)
            imp_r = _dot(overlap_t, pb)
            imp = imp_r[:, 0:Q_BLOCK]
            for r in range(1, GROUP_R):
                imp = imp + imp_r[:, r * Q_BLOCK:(r + 1) * Q_BLOCK]
            qp = i * Q_BLOCK + lax.broadcasted_iota(jnp.int32, (ns, Q_BLOCK), 1)
            cur = qp // SEL_BLOCK
            forced = (blk == 0) | (blk == cur) | (blk == cur - 1)
            imp = jnp.where(forced, jnp.inf, imp)
            imp_ref[qb] = jnp.where(blk <= cur, imp, NEG_INF)

    variant_rows = 128
    n_var = nc // variant_rows
    n_ending = (step + 1) * (CMP_QB * Q_BLOCK // CMP_STRIDE) - 1
    variant = jnp.minimum((n_ending + variant_rows - 1) // variant_rows, n_var) - 1
    for v in range(n_var):
        pl.when(variant == v)(functools.partial(attend, (v + 1) * variant_rows))

    blkf = blk.astype(F32)
    imps = [imp_ref[qb] for qb in range(CMP_QB)]
    sels = [jnp.zeros((ns, Q_BLOCK), F32) for _ in range(CMP_QB)]
    for _ in range(sel_k):
        for qb in range(CMP_QB):
            mx = jnp.max(imps[qb], axis=0, keepdims=True)
            first = jnp.min(jnp.where(imps[qb] == mx, blkf, float(ns)), axis=0, keepdims=True)
            hit = blkf == first
            sels[qb] = jnp.where(hit & (mx > NEG_INF), 1.0, sels[qb])
            imps[qb] = jnp.where(hit, NEG_INF, imps[qb])
    for qb in range(CMP_QB):
        selt_ref[:, qb * Q_BLOCK:(qb + 1) * Q_BLOCK] = sels[qb]


def _cmp_attn(qt, k_cmp, v_cmp_t, gates):
    b, h, d, t = qt.shape
    g = h // GROUP_R
    nc = k_cmp.shape[1]
    ns = t // SEL_BLOCK
    qw = CMP_QB * Q_BLOCK
    assert nc % 128 == 0
    return pl.pallas_call(
        functools.partial(_cmp_attn_kernel, sel_k=min(SEL_TOPK, ns)),
        grid=(b, g, t // qw),
        in_specs=[
            pl.BlockSpec((None, GROUP_R, d, qw), lambda bi, gi, i: (bi, gi, 0, i)),
            pl.BlockSpec((None, nc, d), lambda bi, gi, i: (bi * KV_HEADS + gi, 0, 0)),
            pl.BlockSpec((None, d, nc), lambda bi, gi, i: (bi * KV_HEADS + gi, 0, 0)),
            pl.BlockSpec((None, None, N_GATE_ROWS, qw), lambda bi, gi, i: (bi, gi, 0, i)),
        ],
        out_specs=[
            pl.BlockSpec((None, qw, GROUP_R * d), lambda bi, gi, i: (bi, i, gi)),
            pl.BlockSpec((None, None, ns, qw), lambda bi, gi, i: (bi, gi, 0, i)),
        ],
        out_shape=[jax.ShapeDtypeStruct((b, t, h * d), F32),
                   jax.ShapeDtypeStruct((b, g, ns, t), F32)],
        scratch_shapes=[pltpu.VMEM((CMP_QB, ns, Q_BLOCK), F32)],
        compiler_params=_params(("parallel", "parallel", "arbitrary"), VMEM_LIMIT),
        name="nsa_cmp_topk",
    )(qt, k_cmp, v_cmp_t, gates)


def _sel_attn_kernel(qt_ref, ks_ref, vst_ref, selt_ref, gate_ref, o_ref, sa_ref, sb_ref, mta_ref,
                     mtb_ref, m_ref, acc_ref):
    g = pl.program_id(1)
    i = pl.program_id(2)
    tk = SEL_KEY_TILE
    bpt = tk // SEL_BLOCK
    spt = tk // Q_BLOCK
    d = HEAD_DIM
    m_ref[...] = jnp.full(m_ref.shape, NEG_INF, F32)
    acc_ref[...] = jnp.zeros_like(acc_ref)
    n_tiles = (i * Q_BLOCK + Q_BLOCK + tk - 1) // tk
    qpad = _padded_q(qt_ref, g)
    tri = jnp.where(lax.broadcasted_iota(jnp.int32, (Q_BLOCK, Q_BLOCK), 0)
                    <= lax.broadcasted_iota(jnp.int32, (Q_BLOCK, Q_BLOCK), 1), 0.0, NEG_INF)

    def scores(j, s_ref, mt_ref):
        k0 = pl.multiple_of(j * tk, tk)
        st = _dot(ks_ref[pl.ds(k0, tk), :], qpad)
        sel_rows = selt_ref[pl.ds(pl.multiple_of(j * bpt, bpt), bpt), :]
        bias_rows = jnp.where(sel_rows > 0.5, 0.0, NEG_INF)
        parts = []
        for u in range(spt):
            sub = jnp.concatenate(
                [jnp.broadcast_to(bias_rows[s:s + 1, :], (SEL_BLOCK, Q_BLOCK))
                 for s in range(u * Q_BLOCK // SEL_BLOCK, (u + 1) * Q_BLOCK // SEL_BLOCK)], axis=0)
            parts.append(sub + jnp.where(j * spt + u == i, tri, 0.0))
        bias = jnp.concatenate(parts, axis=0)
        st = st + jnp.concatenate([bias] * GROUP_R, axis=1)
        s_ref[...] = st
        mt_ref[...] = jnp.max(st, axis=0, keepdims=True)

    def consume(j, s_ref, mt_ref):
        m_prev = m_ref[...]
        m_new = jnp.maximum(m_prev, mt_ref[...])
        m_safe = jnp.where(m_new == NEG_INF, 0.0, m_new)
        alpha = jnp.exp2(m_prev - m_safe)
        p = jnp.exp2((s_ref[...] - m_safe).astype(BF16))
        acc_ref[...] = alpha * acc_ref[...] + _dot(vst_ref[j], p)
        m_ref[...] = m_new

    scores(0, sa_ref, mta_ref)
    n_pairs = n_tiles // 2

    def step(jj, carry):
        j = 2 * jj
        scores(j + 1, sb_ref, mtb_ref)
        consume(j, sa_ref, mta_ref)
        scores(jnp.minimum(j + 2, n_tiles - 1), sa_ref, mta_ref)
        consume(j + 1, sb_ref, mtb_ref)
        return carry

    lax.fori_loop(0, n_pairs, step, 0)

    @pl.when(n_tiles % 2 == 1)
    def _():
        consume(n_tiles - 1, sa_ref, mta_ref)

    o_t = acc_ref[0:d, :] / jnp.maximum(acc_ref[d:d + 1, :], 1e-30)
    o_ref[...] = _gate_untranspose(o_t, gate_ref, 1)


def _sel_attn(qt, ks_rows, vs_t, sel_t, gates):
    b, h, d, t = qt.shape
    g = h // GROUP_R
    ns = sel_t.shape[2]
    tk = SEL_KEY_TILE
    rq = GROUP_R * Q_BLOCK
    return pl.pallas_call(
        _sel_attn_kernel,
        grid=(b, g, t // Q_BLOCK),
        in_specs=[
            pl.BlockSpec((None, GROUP_R, d, Q_BLOCK), lambda bi, gi, i: (bi, gi, 0, i)),
            pl.BlockSpec((None, t, KV_HEADS * d), lambda bi, gi, i: (bi, 0, 0)),
            pl.BlockSpec((None, None, t // tk, SEL_V_ROWS, tk), lambda bi, gi, i: (bi, gi, 0, 0, 0)),
            pl.BlockSpec((None, None, ns, Q_BLOCK), lambda bi, gi, i: (bi, gi, 0, i)),
            pl.BlockSpec((None, None, N_GATE_ROWS, Q_BLOCK), lambda bi, gi, i: (bi, gi, 0, i)),
        ],
        out_specs=pl.BlockSpec((None, Q_BLOCK, GROUP_R * d), lambda bi, gi, i: (bi, i, gi)),
        out_shape=jax.ShapeDtypeStruct((b, t, h * d), F32),
        scratch_shapes=[pltpu.VMEM((tk, rq), F32), pltpu.VMEM((tk, rq), F32),
                        pltpu.VMEM((1, rq), F32), pltpu.VMEM((1, rq), F32),
                        pltpu.VMEM((1, rq), F32), pltpu.VMEM((SEL_V_ROWS, rq), F32)],
        compiler_params=_params(("parallel", "parallel", "arbitrary"), VMEM_LIMIT),
        name="nsa_selected",
    )(qt, ks_rows, vs_t, sel_t, gates)


def _proj_retention_kernel(lg_ref, x_ref, g_ref, w_ref, cos_ref, sin_ref, ng_ref, o_ref, r_ref):
    ti = pl.program_id(1)
    c = RET_CHUNK
    h, dk, dv = RET_HEADS, RET_QK_DIM, RET_V_DIM
    half = dk // 2
    k0, v0, g0 = h * dk, 2 * h * dk, 2 * h * dk + h * dv

    @pl.when(ti == 0)
    def _():
        r_ref[...] = jnp.zeros_like(r_ref)

    xn = _rms_rows(x_ref[...], g_ref[...]).astype(BF16)
    acc = _dot(xn, w_ref[...])
    ii = lax.broadcasted_iota(jnp.int32, (c, c), 0)
    jj = lax.broadcasted_iota(jnp.int32, (c, c), 1)
    dist = (ii - jj).astype(F32)
    jcol = lax.broadcasted_iota(jnp.int32, (c, 1), 0).astype(F32)
    for hh in range(h):
        lg = lg_ref[hh]
        dmask = jnp.where(dist >= 0, jnp.exp(dist * lg), 0.0)
        xi = jnp.exp((jcol + 1.0) * lg)
        zeta = jnp.exp((c - 1.0 - jcol) * lg)
        decay = jnp.exp(jnp.zeros((1, dv), F32) + c * lg)
        for cc in range(acc.shape[0] // c):
            rows = slice(cc * c, (cc + 1) * c)
            cosf = cos_ref[rows, :]
            sinf = sin_ref[rows, :]

            def rot(x):
                return x * cosf + jnp.concatenate([x[:, half:], x[:, :half]], axis=1) * sinf

            q = rot(acc[rows, hh * dk:(hh + 1) * dk])
            k = rot(acc[rows, k0 + hh * dk:k0 + (hh + 1) * dk]) * (dk ** -0.5)
            vb = acc[rows, v0 + hh * dv:v0 + (hh + 1) * dv].astype(BF16)
            att = _dot_nt(q.astype(BF16), k.astype(BF16)) * dmask
            o = _dot(att.astype(BF16), vb)
            r_prev = r_ref[hh]
            o = o + _dot((q * xi).astype(BF16), r_prev.astype(BF16))
            s_chunk = _dot_tn((k * zeta).astype(BF16), vb)
            r_ref[hh] = r_prev * decay + s_chunk
            mu = jnp.mean(o, axis=-1, keepdims=True)
            var = jnp.mean(jnp.square(o - mu), axis=-1, keepdims=True)
            on = (o - mu) * lax.rsqrt(var + EPS)
            gt = acc[rows, g0 + hh * dv:g0 + (hh + 1) * dv]
            o_ref[rows, hh * dv:(hh + 1) * dv] = (
                (gt * _sigmoid(gt)) * (on * ng_ref[:, hh * dv:(hh + 1) * dv]))


def _proj_retention(x2, b, t, g, w_d, norm_gain, tm=512):
    n, dm = x2.shape
    h, dk, dv = RET_HEADS, RET_QK_DIM, RET_V_DIM
    half = dk // 2
    tpb = t // tm
    inv = ROPE_BASE ** (-jnp.arange(half, dtype=F32) / half)
    ang = jnp.arange(t).astype(F32)[:, None] * inv[None, :]
    cos = jnp.cos(ang)
    sin = jnp.sin(ang)
    cosf = jnp.concatenate([cos, cos], axis=-1)
    sinf = jnp.concatenate([-sin, sin], axis=-1)
    log_gamma = jnp.log(1.0 - 2.0 ** (-5.0 - jnp.arange(h, dtype=F32)))
    return pl.pallas_call(
        _proj_retention_kernel,
        grid=(b, tpb),
        in_specs=[
            pl.BlockSpec(memory_space=pltpu.SMEM),
            pl.BlockSpec((tm, dm), lambda bi, ti: (bi * tpb + ti, 0)),
            pl.BlockSpec((1, dm), lambda bi, ti: (0, 0)),
            pl.BlockSpec(w_d.shape, lambda bi, ti: (0, 0), pipeline_mode=pl.Buffered(1)),
            pl.BlockSpec((tm, dk), lambda bi, ti: (ti, 0)),
            pl.BlockSpec((tm, dk), lambda bi, ti: (ti, 0)),
            pl.BlockSpec((1, h * dv), lambda bi, ti: (0, 0)),
        ],
        out_specs=pl.BlockSpec((None, tm, h * dv), lambda bi, ti: (bi, ti, 0)),
        out_shape=jax.ShapeDtypeStruct((b, t, h * dv), F32),
        scratch_shapes=[pltpu.VMEM((h, dk, dv), F32)],
        compiler_params=_params(("parallel", "arbitrary"), VMEM_LIMIT),
        name="proj_retention",
    )(log_gamma, x2, g.reshape(1, dm), w_d.astype(BF16), cosf, sinf, norm_gain.reshape(1, h * dv))


def _merge_kernel(x_ref, g_ref, wgate_ref, bias_ref, ya_ref, yb_ref, yc0_ref, yc1_ref, yc2_ref,
                  yd_ref, wb_ref, wo_ref, o_ref):
    x = x_ref[...]
    d = x.shape[1]
    u = _rms_rows(x, g_ref[...]).astype(BF16)
    ys = (ya_ref[...], yb_ref[...], yc0_ref[...] + yc1_ref[...] + yc2_ref[...], yd_ref[...])
    merged = jnp.zeros(x.shape, F32)
    for n in range(N_BRANCH):
        logits = _dot(u, wgate_ref[:, n * d:(n + 1) * d]) + bias_ref[:, n * d:(n + 1) * d]
        merged = merged + _sigmoid(logits) * _dot(ys[n].astype(BF16), wb_ref[n])
    o_ref[...] = x + _dot(merged.astype(BF16), wo_ref[...])


def _merge(x2, g, w_gate, bias, ys, w_branch, w_out, tm=512):
    n, d = x2.shape
    w = MIX_W
    row = lambda width: pl.BlockSpec((tm, width), lambda i: (i, 0))
    full = lambda shape: pl.BlockSpec(shape, lambda i: tuple(0 for _ in shape),
                                      pipeline_mode=pl.Buffered(1))
    return pl.pallas_call(
        _merge_kernel,
        grid=(n // tm,),
        in_specs=[row(d), full((1, d)), full((d, N_BRANCH * d)), full((1, N_BRANCH * d))]
        + [row(w)] * 6 + [full((N_BRANCH, w, d)), full((d, d))],
        out_specs=row(d),
        out_shape=jax.ShapeDtypeStruct((n, d), F32),
        compiler_params=_params(("parallel",), VMEM_LIMIT),
        name="merge_out",
    )(x2, g.reshape(1, d), w_gate.astype(BF16), bias.reshape(1, N_BRANCH * d),
      *[y.reshape(n, w) for y in ys], w_branch.astype(BF16), w_out.astype(BF16))


def _mixers(x2, b, t, mix_norm, w_in, merge_gate_bias, swa_q_gain, swa_k_gain, swa_sinks, conv_w,
            nsa_q_gain, nsa_k_gain, cmp_pos_k, cmp_pos_v, cmp_wk1, cmp_wk2, cmp_wv1, cmp_wv2,
            ret_norm_gain, w_branch, w_out):
    hd = HEAD_DIM
    swa_q, swa_kv = SWA_HEADS * hd, SWA_KV_HEADS * hd
    nsa_q, nsa_kv = NSA_HEADS * hd, NSA_KV_HEADS * hd
    ret_qk, ret_v = RET_HEADS * RET_QK_DIM, RET_HEADS * RET_V_DIM
    n_gate = NSA_HEADS * 3
    o_a = 0
    o_b = o_a + swa_q + 2 * swa_kv
    o_c = o_b + 3 * MIX_W
    o_cg = o_c + nsa_q + 6 * nsa_kv
    o_d = o_cg + n_gate
    o_g = o_d + 2 * ret_qk + 2 * ret_v

    per_g = GROUP_R * 3
    w_gates = [jnp.pad(w_in[:, o_cg + gi * per_g:o_cg + (gi + 1) * per_g],
                       ((0, 0), (0, N_GATE_ROWS - per_g))) for gi in range(NSA_KV_HEADS)]
    w_t = jnp.concatenate([w_in[:, o_a:o_b], w_in[:, o_c:o_cg]] + w_gates, axis=1).T.astype(BF16)
    head_gains = jnp.stack([swa_q_gain, swa_k_gain, nsa_q_gain, nsa_k_gain[1], nsa_k_gain[2]])
    head_gains = jnp.broadcast_to(head_gains.astype(F32)[:, :, None], (5, hd, 128))
    (a_qt, a_k, a_vt, c_qt, c_qt2, c_kc, c_vc, c_ks, c_vst, c_kw, c_vwt, gates_c) = _proj_heads(
        x2, b, t, mix_norm, w_t, head_gains)

    y_a = _banded(a_qt, a_k, a_vt, SWA_WINDOW, sinks=swa_sinks)

    y_b = _proj_conv(x2, b, t, mix_norm, w_in[:, o_b:o_c], conv_w)

    k_cmp, v_cmp = _compress(c_kc, c_vc, cmp_pos_k, cmp_pos_v, cmp_wk1, cmp_wk2, cmp_wv1, cmp_wv2,
                             nsa_k_gain[0])
    y_cmp, sel_t = _cmp_attn(c_qt, k_cmp, v_cmp.transpose(0, 2, 1), gates_c)
    y_sel = _sel_attn(c_qt2, c_ks, c_vst, sel_t, gates_c)
    y_win = _banded(c_qt, c_kw, c_vwt, NSA_WINDOW, gates=gates_c, gate_branch=2)

    y_d = _proj_retention(x2, b, t, mix_norm, w_in[:, o_d:o_g], ret_norm_gain)

    return _merge(x2, mix_norm, w_in[:, o_g:], merge_gate_bias,
                  (y_a, y_b, y_cmp, y_sel, y_win, y_d), w_branch, w_out)


def kernel(x, ffn1_norm, ffn1_w_gate, ffn1_w_up, ffn1_w_down, mix_norm, w_in, merge_gate_bias, swa_q_gain, swa_k_gain, swa_sinks, conv_w, nsa_q_gain, nsa_k_gain, cmp_pos_k, cmp_pos_v, cmp_wk1, cmp_wk2, cmp_wv1, cmp_wv2, ret_norm_gain, w_branch, w_out, ffn2_norm, ffn2_w_gate, ffn2_w_up, ffn2_w_down):
    b, t, d = x.shape
    x2 = x.reshape(b * t, d)
    for l in range(ffn1_norm.shape[0]):
        x2 = _ffn(x2, ffn1_norm[l], ffn1_w_gate[l], ffn1_w_up[l], ffn1_w_down[l])
        x2 = _mixers(x2, b, t, mix_norm[l], w_in[l], merge_gate_bias[l], swa_q_gain[l],
                     swa_k_gain[l], swa_sinks[l], conv_w[l], nsa_q_gain[l], nsa_k_gain[l],
                     cmp_pos_k[l], cmp_pos_v[l], cmp_wk1[l], cmp_wk2[l], cmp_wv1[l], cmp_wv2[l],
                     ret_norm_gain[l], w_branch[l], w_out[l])
        x2 = _ffn(x2, ffn2_norm[l], ffn2_w_gate[l], ffn2_w_up[l], ffn2_w_down[l])
    return x2.reshape(b, t, d)
```

```python
import functools

import jax
import jax.numpy as jnp
from jax import lax
from jax.experimental import pallas as pl
from jax.experimental.pallas import tpu as pltpu

F32 = jnp.float32
BF16 = jnp.bfloat16

HEAD_DIM = 64
Q_BLOCK = 128
MIX_W = 512
N_BRANCH = 4
SWA_HEADS = 8
SWA_KV_HEADS = 2
SWA_WINDOW = 128
NSA_HEADS = 8
NSA_KV_HEADS = 2
CMP_BLOCK = 32
CMP_STRIDE = 16
SEL_BLOCK = 64
SEL_TOPK = 16
NSA_WINDOW = 512
RET_HEADS = 4
RET_QK_DIM = 64
RET_V_DIM = 128
RET_CHUNK = 128
ROPE_BASE = 10000.0
EPS = 1e-6
GROUP_R = 4
KV_HEADS = 2
N_GATE_ROWS = 16
SEL_KEY_TILE = 512
SEL_V_ROWS = HEAD_DIM + 16
PROJ_TM = 512
BANDED_QB = 8
CMP_QB = 4
SEL_QB = 4
VMEM_LIMIT = 52 * 1024 * 1024

NEG_INF = float("-inf")
LOG2_E = 1.4426950408889634


def _params(sem, vmem=None):
    return pltpu.CompilerParams(dimension_semantics=sem, vmem_limit_bytes=vmem)


def _sigmoid(x):
    return 1.0 / (1.0 + jnp.exp(-x))


def _rms_rows(x, g):
    return x * lax.rsqrt(jnp.mean(x * x, axis=-1, keepdims=True) + EPS) * g


def _dot(a, b):
    return jnp.dot(a, b, preferred_element_type=F32)


def _dot_nt(a, b):
    return lax.dot_general(a, b, (((1,), (1,)), ((), ())), preferred_element_type=F32)


def _dot_tn(a, b):
    return lax.dot_general(a, b, (((0,), (0,)), ((), ())), preferred_element_type=F32)


def _ffn_kernel(x_ref, g_ref, wg_ref, wu_ref, wd_ref, o_ref, *, n_chunks):
    x = x_ref[...]
    xn = _rms_rows(x, g_ref[...]).astype(BF16)
    tf = wg_ref.shape[1] // n_chunks
    acc = None
    for f in range(n_chunks):
        a = _dot(xn, wg_ref[:, f * tf:(f + 1) * tf])
        b = _dot(xn, wu_ref[:, f * tf:(f + 1) * tf])
        h = ((a * _sigmoid(a)) * b).astype(BF16)
        part = _dot(h, wd_ref[f * tf:(f + 1) * tf, :])
        acc = part if acc is None else acc + part
    o_ref[...] = x + 0.5 * acc


def _ffn(x2, g, wg, wu, wd, tm=512, n_chunks=2):
    n, d = x2.shape
    dff = wg.shape[1]
    assert dff % (n_chunks * 128) == 0
    resident = lambda shape: pl.BlockSpec(shape, lambda i: (0, 0), pipeline_mode=pl.Buffered(1))
    return pl.pallas_call(
        functools.partial(_ffn_kernel, n_chunks=n_chunks),
        grid=(n // tm,),
        in_specs=[
            pl.BlockSpec((tm, d), lambda i: (i, 0)),
            pl.BlockSpec((1, d), lambda i: (0, 0)),
            resident((d, dff)), resident((d, dff)), resident((dff, d)),
        ],
        out_specs=pl.BlockSpec((tm, d), lambda i: (i, 0)),
        out_shape=jax.ShapeDtypeStruct((n, d), F32),
        compiler_params=_params(("parallel",), VMEM_LIMIT),
        name="ffn",
    )(x2, g.reshape(1, d), wg.astype(BF16), wu.astype(BF16), wd.astype(BF16))


def _proj_heads_kernel(x_ref, g_ref, wt_ref, hg_ref, aq_ref, ak_ref, av_ref, cq_ref, cq2_ref,
                       ckc_ref, cvc_ref, cks_ref, cvs_ref, ckw_ref, cvw_ref, gt_ref):
    d = HEAD_DIM
    xn = _rms_rows(x_ref[...], g_ref[...]).astype(BF16)
    acc = _dot_nt(wt_ref[...], xn)
    tm = xn.shape[0]
    lane_tiles = tm // 128
    scale = d ** -0.5
    kv_w = KV_HEADS * d

    def head_norm(row0, gain_idx, mult):
        hb = acc[row0:row0 + d]
        gain = jnp.concatenate([hg_ref[gain_idx]] * lane_tiles, axis=1)
        y = hb * lax.rsqrt(jnp.mean(hb * hb, axis=0, keepdims=True) + EPS) * gain
        return y * mult if mult != 1.0 else y

    def q_heads(row0, gain_idx, out_ref, n_heads, log2_ref=None):
        for h in range(n_heads):
            y = head_norm(row0 + h * d, gain_idx, 1.0)
            out_ref[h] = (y * scale).astype(BF16)
            if log2_ref is not None:
                log2_ref[h] = (y * (scale * LOG2_E)).astype(BF16)

    def k_rows(row0, gain_idx):
        return jnp.concatenate([head_norm(row0 + g * d, gain_idx, 1.0) for g in range(KV_HEADS)],
                               axis=0).T

    def v_tiles(row0, out_ref):
        for g in range(KV_HEADS):
            for u in range(lane_tiles):
                out_ref[g, u, 0:d, :] = acc[row0 + g * d:row0 + (g + 1) * d,
                                            u * 128:(u + 1) * 128].astype(BF16)
                out_ref[g, u, d:SEL_V_ROWS, :] = jnp.ones((SEL_V_ROWS - d, 128), BF16)

    row = 0
    q_heads(row, 0, aq_ref, SWA_HEADS)
    row += SWA_HEADS * d
    ak_ref[...] = k_rows(row, 1).astype(BF16)
    row += kv_w
    v_tiles(row, av_ref)
    row += kv_w
    q_heads(row, 2, cq_ref, NSA_HEADS, cq2_ref)
    row += NSA_HEADS * d
    ckc_ref[...] = acc[row:row + kv_w].T
    row += kv_w
    cvc_ref[...] = acc[row:row + kv_w].T
    row += kv_w
    cks_ref[...] = k_rows(row, 3).astype(BF16)
    row += kv_w
    for g in range(KV_HEADS):
        cvs_ref[g, 0, 0:d, :] = acc[row + g * d:row + (g + 1) * d].astype(BF16)
        cvs_ref[g, 0, d:SEL_V_ROWS, :] = jnp.ones((SEL_V_ROWS - d, tm), BF16)
    row += kv_w
    ckw_ref[...] = k_rows(row, 4).astype(BF16)
    row += kv_w
    v_tiles(row, cvw_ref)
    row += kv_w
    for g in range(KV_HEADS):
        gt_ref[g] = _sigmoid(acc[row + g * N_GATE_ROWS:row + (g + 1) * N_GATE_ROWS])


def _proj_heads(x2, b, t, g, w_t, head_gains):
    n, dm = x2.shape
    d = HEAD_DIM
    tm = PROJ_TM
    assert tm == SEL_KEY_TILE and t % tm == 0
    rows = w_t.shape[0]
    tpb = t // tm
    lt = tm // 128
    kv = KV_HEADS
    qt_spec = lambda h: pl.BlockSpec((None, h, d, tm), lambda bi, ti: (bi, 0, 0, ti))
    row_spec = pl.BlockSpec((None, tm, kv * d), lambda bi, ti: (bi, ti, 0))
    vt_spec = pl.BlockSpec((None, kv, lt, SEL_V_ROWS, 128), lambda bi, ti: (bi, 0, ti, 0, 0))
    sds = jax.ShapeDtypeStruct
    return pl.pallas_call(
        _proj_heads_kernel,
        grid=(b, tpb),
        in_specs=[
            pl.BlockSpec((tm, dm), lambda bi, ti: (bi * tpb + ti, 0)),
            pl.BlockSpec((1, dm), lambda bi, ti: (0, 0)),
            pl.BlockSpec((rows, dm), lambda bi, ti: (0, 0), pipeline_mode=pl.Buffered(1)),
            pl.BlockSpec(head_gains.shape, lambda bi, ti: (0, 0, 0)),
        ],
        out_specs=[
            qt_spec(SWA_HEADS), row_spec, vt_spec,
            qt_spec(NSA_HEADS), qt_spec(NSA_HEADS), row_spec, row_spec, row_spec,
            pl.BlockSpec((None, kv, 1, SEL_V_ROWS, tm), lambda bi, ti: (bi, 0, ti, 0, 0)),
            row_spec, vt_spec,
            pl.BlockSpec((None, kv, N_GATE_ROWS, tm), lambda bi, ti: (bi, 0, 0, ti)),
        ],
        out_shape=[
            sds((b, SWA_HEADS, d, t), BF16), sds((b, t, kv * d), BF16),
            sds((b, kv, t // 128, SEL_V_ROWS, 128), BF16),
            sds((b, NSA_HEADS, d, t), BF16), sds((b, NSA_HEADS, d, t), BF16),
            sds((b, t, kv * d), F32), sds((b, t, kv * d), F32),
            sds((b, t, kv * d), BF16), sds((b, kv, tpb, SEL_V_ROWS, tm), BF16),
            sds((b, t, kv * d), BF16), sds((b, kv, t // 128, SEL_V_ROWS, 128), BF16),
            sds((b, kv, N_GATE_ROWS, t), F32),
        ],
        compiler_params=_params(("parallel", "parallel"), VMEM_LIMIT),
        name="proj_heads",
    )(x2, g.reshape(1, dm), w_t, head_gains)


def _group_q(qt_ref, qb=0):
    return jnp.concatenate(
        [qt_ref[r, :, qb * Q_BLOCK:(qb + 1) * Q_BLOCK] for r in range(GROUP_R)], axis=1)


def _padded_q(qt_ref, g, qb=0):
    q4 = _group_q(qt_ref, qb)
    z = jnp.zeros_like(q4)
    return jnp.where(g == 0, jnp.concatenate([q4, z], axis=0), jnp.concatenate([z, q4], axis=0))


def _gate_untranspose(o_t, gate_ref, branch, qb=0):
    outs = []
    for r in range(GROUP_R):
        blk = o_t[:, r * Q_BLOCK:(r + 1) * Q_BLOCK]
        if gate_ref is not None:
            c = r * 3 + branch
            blk = blk * gate_ref[c:c + 1, qb * Q_BLOCK:(qb + 1) * Q_BLOCK]
        outs.append(blk.T)
    return jnp.concatenate(outs, axis=1)


def _banded_kernel(*refs, window, has_sink, gate_branch):
    refs = list(refs)
    sink_ref = refs.pop(0) if has_sink else None
    qt_ref, k_ref, vt_ref = refs[:3]
    gate_ref = refs[3] if gate_branch is not None else None
    o_ref = refs[-1]
    g = pl.program_id(1)
    rq = GROUP_R * Q_BLOCK
    n_sub = window // Q_BLOCK + 1
    span = n_sub * Q_BLOCK
    if has_sink:
        lrow = lax.broadcasted_iota(jnp.int32, (1, rq), 1)
        sink = jnp.zeros((1, rq), F32)
        for r in range(GROUP_R):
            sink = jnp.where((lrow >= r * Q_BLOCK) & (lrow < (r + 1) * Q_BLOCK),
                             sink_ref[g * GROUP_R + r], sink)
    step = pl.program_id(2)
    row = lax.broadcasted_iota(jnp.int32, (span, Q_BLOCK), 0)
    qcol = lax.broadcasted_iota(jnp.int32, (span, Q_BLOCK), 1)

    def band_bias(offset):
        diff = offset + qcol - row
        return jnp.where((diff >= 0) & (diff < window), 0.0, NEG_INF)

    for qb in range(BANDED_QB):
        i = step * BANDED_QB + qb
        start = pl.multiple_of(jnp.maximum(i * Q_BLOCK - window, 0), Q_BLOCK)
        st = _dot(k_ref[pl.ds(start, span), :], _padded_q(qt_ref, g, qb))
        st = st + jnp.concatenate([band_bias(i * Q_BLOCK - start)] * GROUP_R, axis=1)
        m = jnp.max(st, axis=0, keepdims=True)
        if has_sink:
            m = jnp.maximum(m, sink)
        m = jnp.where(m == NEG_INF, 0.0, m)
        pb = jnp.exp(st - m).astype(BF16)
        u0 = start // Q_BLOCK
        o_t = _dot(vt_ref[u0], pb[0:Q_BLOCK])
        for u in range(1, n_sub):
            o_t = o_t + _dot(vt_ref[u0 + u], pb[u * Q_BLOCK:(u + 1) * Q_BLOCK])
        denom = o_t[HEAD_DIM:HEAD_DIM + 1, :]
        if has_sink:
            denom = denom + jnp.exp(sink - m)
        o_t = o_t[0:HEAD_DIM, :] / jnp.maximum(denom, 1e-30)
        o_ref[qb * Q_BLOCK:(qb + 1) * Q_BLOCK, :] = _gate_untranspose(
            o_t, gate_ref, gate_branch, qb)


def _banded(qt, k_rows, vt, window, sinks=None, gates=None, gate_branch=None):
    b, h, d, t = qt.shape
    g = h // GROUP_R
    in_specs = []
    args = []
    if sinks is not None:
        in_specs.append(pl.BlockSpec(memory_space=pltpu.SMEM))
        args.append(sinks.astype(F32))
    qw = BANDED_QB * Q_BLOCK
    in_specs += [
        pl.BlockSpec((None, GROUP_R, d, qw), lambda bi, gi, i: (bi, gi, 0, i)),
        pl.BlockSpec((None, t, KV_HEADS * d), lambda bi, gi, i: (bi, 0, 0)),
        pl.BlockSpec((None, None, t // Q_BLOCK, SEL_V_ROWS, Q_BLOCK),
                     lambda bi, gi, i: (bi, gi, 0, 0, 0)),
    ]
    args += [qt, k_rows, vt]
    if gates is not None:
        in_specs.append(pl.BlockSpec((None, None, N_GATE_ROWS, qw),
                                     lambda bi, gi, i: (bi, gi, 0, i)))
        args.append(gates)
    return pl.pallas_call(
        functools.partial(_banded_kernel, window=window, has_sink=sinks is not None,
                          gate_branch=gate_branch if gates is not None else None),
        grid=(b, g, t // qw),
        in_specs=in_specs,
        out_specs=pl.BlockSpec((None, qw, GROUP_R * d), lambda bi, gi, i: (bi, i, gi)),
        out_shape=jax.ShapeDtypeStruct((b, t, h * d), F32),
        compiler_params=_params(("parallel", "parallel", "arbitrary"), VMEM_LIMIT),
        name="banded_attn_w%d" % window,
    )(*args)


def _proj_conv_kernel(x_ref, g_ref, w_ref, cw_ref, o_ref, tail_ref):
    ti = pl.program_id(1)
    w = MIX_W
    xn = _rms_rows(x_ref[...], g_ref[...]).astype(BF16)
    acc = _dot(xn, w_ref[...])
    z = acc[:, 2 * w:3 * w] * acc[:, 0:w]
    zp = jnp.where(ti > 0, tail_ref[...], 0.0)
    row = lax.broadcasted_iota(jnp.int32, z.shape, 0)
    z1 = jnp.where(row == 0, zp[7:8, :], pltpu.roll(z, 1, 0))
    z2 = pltpu.roll(z, 2, 0)
    z2 = jnp.where(row == 0, zp[6:7, :], jnp.where(row == 1, zp[7:8, :], z2))
    cw = cw_ref[...]
    o_ref[...] = acc[:, w:2 * w] * (cw[0:1, :] * z2 + cw[1:2, :] * z1 + cw[2:3, :] * z)
    tail_ref[...] = z[z.shape[0] - 8:, :]


def _proj_conv(x2, b, t, g, w_b, conv_w, tm=512):
    n, dm = x2.shape
    w = MIX_W
    tpb = t // tm
    return pl.pallas_call(
        _proj_conv_kernel,
        grid=(b, tpb),
        in_specs=[
            pl.BlockSpec((tm, dm), lambda bi, ti: (bi * tpb + ti, 0)),
            pl.BlockSpec((1, dm), lambda bi, ti: (0, 0)),
            pl.BlockSpec((dm, 3 * w), lambda bi, ti: (0, 0), pipeline_mode=pl.Buffered(1)),
            pl.BlockSpec((8, w), lambda bi, ti: (0, 0)),
        ],
        out_specs=pl.BlockSpec((None, tm, w), lambda bi, ti: (bi, ti, 0)),
        out_shape=jax.ShapeDtypeStruct((b, t, w), F32),
        scratch_shapes=[pltpu.VMEM((8, w), F32)],
        compiler_params=_params(("parallel", "arbitrary"), VMEM_LIMIT),
        name="proj_conv",
    )(x2, g.reshape(1, dm), w_b.astype(BF16),
      jnp.pad(conv_w.reshape(conv_w.shape[0], w).astype(F32), ((0, 8 - conv_w.shape[0]), (0, 0))))


def _gelu_tanh(x):
    return x * (0.5 * (1.0 + jnp.tanh(0.7978845608028654 * (x + 0.044715 * (x * x * x)))))


def _compress_kernel(tk_ref, tv_ref, pek_ref, pev_ref, wk1_ref, wk2_ref, wv1_ref, wv2_ref,
                     kg_ref, ko_ref, vo_ref):
    nrow = tk_ref.shape[0]

    def mlp(a, pe_ref, w1_ref, w2_ref):
        a0 = (a + pe_ref[0:1, :]).astype(BF16)
        a1 = (a + pe_ref[1:2, :]).astype(BF16)
        p1 = _dot(a0, w1_ref[0])
        p2 = _dot(a1, w1_ref[1])
        hdn = p1 + pltpu.roll(p2, nrow - 1, 0)
        return _dot(_gelu_tanh(hdn).astype(BF16), w2_ref[...])

    kc = mlp(tk_ref[...], pek_ref, wk1_ref, wk2_ref)
    ko_ref[...] = _rms_rows(kc, kg_ref[...]).astype(BF16)
    vo_ref[...] = mlp(tv_ref[...], pev_ref, wv1_ref, wv2_ref).astype(BF16)


def _compress(kc_rows, vc_rows, pos_k, pos_v, wk1, wk2, wv1, wv2, k_gain):
    b, t, kvd = kc_rows.shape
    kv = KV_HEADS
    d = kvd // kv
    nrow = t // CMP_STRIDE
    wide = CMP_STRIDE * kvd
    hid = wk1.shape[1]

    def expand_w1(w1):
        w = w1.reshape(2, CMP_STRIDE, 1, d, hid)
        per_head = []
        for g in range(kv):
            pads = [jnp.zeros_like(w)] * kv
            pads[g] = w
            per_head.append(jnp.concatenate(pads, axis=2).reshape(2, wide, hid))
        return jnp.stack(per_head).astype(BF16)

    def expand_pe(pe):
        return jnp.broadcast_to(pe.reshape(2, CMP_STRIDE, 1, d), (2, CMP_STRIDE, kv, d)).reshape(2, wide)

    tok = pl.BlockSpec((None, nrow, wide), lambda bi, gi: (bi, 0, 0))
    full = lambda shape: pl.BlockSpec(shape, lambda bi, gi: tuple(0 for _ in shape))
    w1_spec = pl.BlockSpec((None, 2, wide, hid), lambda bi, gi: (gi, 0, 0, 0))
    out = pl.BlockSpec((None, nrow, d), lambda bi, gi: (bi * kv + gi, 0, 0))
    return pl.pallas_call(
        _compress_kernel,
        grid=(b, kv),
        in_specs=[tok, tok, full((2, wide)), full((2, wide)), w1_spec, full((hid, d)),
                  w1_spec, full((hid, d)), full((1, d))],
        out_specs=[out, out],
        out_shape=[jax.ShapeDtypeStruct((b * kv, nrow, d), BF16),
                   jax.ShapeDtypeStruct((b * kv, nrow, d), BF16)],
        compiler_params=_params(("parallel", "arbitrary"), VMEM_LIMIT),
        name="nsa_compress",
    )(kc_rows.reshape(b, nrow, wide), vc_rows.reshape(b, nrow, wide), expand_pe(pos_k),
      expand_pe(pos_v), expand_w1(wk1), wk2.astype(BF16), expand_w1(wv1), wv2.astype(BF16),
      k_gain.reshape(1, d))


def _cmp_attn_kernel(qt_ref, kc_ref, vct_ref, gate_ref, o_ref, selt_ref, imp_ref, *, sel_k):
    step = pl.program_id(2)
    rq = GROUP_R * Q_BLOCK
    nc = kc_ref.shape[0]
    ns = selt_ref.shape[0]
    blk = lax.broadcasted_iota(jnp.int32, (ns, Q_BLOCK), 0)

    def attend(nc_eff):
        n = lax.broadcasted_iota(jnp.int32, (nc_eff, rq), 0)
        lane = lax.broadcasted_iota(jnp.int32, (nc_eff, rq), 1)
        end_minus_q = n * CMP_STRIDE + (CMP_BLOCK - 1) - (lane & (Q_BLOCK - 1))
        ss = lax.broadcasted_iota(jnp.int32, (ns, nc_eff), 0) * SEL_BLOCK
        nn = lax.broadcasted_iota(jnp.int32, (ns, nc_eff), 1) * CMP_STRIDE
        overlap_t = jnp.where((nn < ss + SEL_BLOCK) & (nn + (CMP_BLOCK - 1) >= ss),
                              1.0, 0.0).astype(BF16)
        for qb in range(CMP_QB):
            i = step * CMP_QB + qb
            st = _dot(kc_ref[0:nc_eff, :], _group_q(qt_ref, qb))
            st = jnp.where(end_minus_q <= i * Q_BLOCK, st, NEG_INF)
            m = jnp.max(st, axis=0, keepdims=True)
            m = jnp.where(m == NEG_INF, 0.0, m)
            p = jnp.exp(st - m)
            denom = jnp.sum(p, axis=0, keepdims=True)
            pb = (p / jnp.maximum(denom, 1e-30)).astype(BF16)
            o_t = _dot(vct_ref[:, 0:nc_eff], pb)
            o_ref[qb * Q_BLOCK:(qb + 1) * Q_BLOCK, :] = _gate_untranspose(o_t, gate_ref, 0, qb)
            imp_r = _dot(overlap_t, pb)
            imp = imp_r[:, 0:Q_BLOCK]
            for r in range(1, GROUP_R):
                imp = imp + imp_r[:, r * Q_BLOCK:(r + 1) * Q_BLOCK]
            qp = i * Q_BLOCK + lax.broadcasted_iota(jnp.int32, (ns, Q_BLOCK), 1)
            cur = qp // SEL_BLOCK
            forced = (blk == 0) | (blk == cur) | (blk == cur - 1)
            imp = jnp.where(forced, jnp.inf, imp)
            imp_ref[qb] = jnp.where(blk <= cur, imp, NEG_INF)

    variant_rows = 128
    n_var = nc // variant_rows
    n_ending = (step + 1) * (CMP_QB * Q_BLOCK // CMP_STRIDE) - 1
    variant = jnp.minimum((n_ending + variant_rows - 1) // variant_rows, n_var) - 1
    for v in range(n_var):
        pl.when(variant == v)(functools.partial(attend, (v + 1) * variant_rows))

    blkf = blk.astype(F32)
    imps = [imp_ref[qb] for qb in range(CMP_QB)]
    sels = [jnp.zeros((ns, Q_BLOCK), F32) for _ in range(CMP_QB)]
    for _ in range(sel_k):
        for qb in range(CMP_QB):
            mx = jnp.max(imps[qb], axis=0, keepdims=True)
            first = jnp.min(jnp.where(imps[qb] == mx, blkf, float(ns)), axis=0, keepdims=True)
            hit = blkf == first
            sels[qb] = jnp.where(hit & (mx > NEG_INF), 1.0, sels[qb])
            imps[qb] = jnp.where(hit, NEG_INF, imps[qb])
    for qb in range(CMP_QB):
        selt_ref[:, qb * Q_BLOCK:(qb + 1) * Q_BLOCK] = sels[qb]


def _cmp_attn(qt, k_cmp, v_cmp_t, gates):
    b, h, d, t = qt.shape
    g = h // GROUP_R
    nc = k_cmp.shape[1]
    ns = t // SEL_BLOCK
    qw = CMP_QB * Q_BLOCK
    assert nc % 128 == 0
    return pl.pallas_call(
        functools.partial(_cmp_attn_kernel, sel_k=min(SEL_TOPK, ns)),
        grid=(b, g, t // qw),
        in_specs=[
            pl.BlockSpec((None, GROUP_R, d, qw), lambda bi, gi, i: (bi, gi, 0, i)),
            pl.BlockSpec((None, nc, d), lambda bi, gi, i: (bi * KV_HEADS + gi, 0, 0)),
            pl.BlockSpec((None, d, nc), lambda bi, gi, i: (bi * KV_HEADS + gi, 0, 0)),
            pl.BlockSpec((None, None, N_GATE_ROWS, qw), lambda bi, gi, i: (bi, gi, 0, i)),
        ],
        out_specs=[
            pl.BlockSpec((None, qw, GROUP_R * d), lambda bi, gi, i: (bi, i, gi)),
            pl.BlockSpec((None, None, ns, qw), lambda bi, gi, i: (bi, gi, 0, i)),
        ],
        out_shape=[jax.ShapeDtypeStruct((b, t, h * d), F32),
                   jax.ShapeDtypeStruct((b, g, ns, t), F32)],
        scratch_shapes=[pltpu.VMEM((CMP_QB, ns, Q_BLOCK), F32)],
        compiler_params=_params(("parallel", "parallel", "arbitrary"), VMEM_LIMIT),
        name="nsa_cmp_topk",
    )(qt, k_cmp, v_cmp_t, gates)


def _sel_attn_kernel(qt_ref, ks_ref, vst_ref, selt_ref, gate_ref, o_ref, sa_ref, sb_ref, mta_ref,
                     mtb_ref, m_ref, acc_ref):
    g = pl.program_id(1)
    tk = SEL_KEY_TILE
    bpt = tk // SEL_BLOCK
    spt = tk // Q_BLOCK
    d = HEAD_DIM
    m_ref[...] = jnp.full(m_ref.shape, NEG_INF, F32)
    acc_ref[...] = jnp.zeros_like(acc_ref)
    last_q = (pl.program_id(2) + 1) * SEL_QB * Q_BLOCK
    n_tiles = (last_q + tk - 1) // tk
    qpads = [_padded_q(qt_ref, g, qb) for qb in range(SEL_QB)]
    tri = jnp.where(lax.broadcasted_iota(jnp.int32, (Q_BLOCK, Q_BLOCK), 0)
                    <= lax.broadcasted_iota(jnp.int32, (Q_BLOCK, Q_BLOCK), 1), 0.0, NEG_INF)

    def scores(j, qb, s_ref, mt_ref):
        i = pl.program_id(2) * SEL_QB + qb
        k0 = pl.multiple_of(j * tk, tk)
        st = _dot(ks_ref[pl.ds(k0, tk), :], qpads[qb])
        sel_rows = selt_ref[pl.ds(pl.multiple_of(j * bpt, bpt), bpt),
                            qb * Q_BLOCK:(qb + 1) * Q_BLOCK]
        bias_rows = jnp.where(sel_rows > 0.5, 0.0, NEG_INF)
        parts = []
        for u in range(spt):
            sub = jnp.concatenate(
                [jnp.broadcast_to(bias_rows[s:s + 1, :], (SEL_BLOCK, Q_BLOCK))
                 for s in range(u * Q_BLOCK // SEL_BLOCK, (u + 1) * Q_BLOCK // SEL_BLOCK)], axis=0)
            parts.append(sub + jnp.where(j * spt + u == i, tri, 0.0))
        bias = jnp.concatenate(parts, axis=0)
        st = st + jnp.concatenate([bias] * GROUP_R, axis=1)
        s_ref[qb] = st
        mt_ref[qb] = jnp.max(st, axis=0, keepdims=True)

    def consume(j, qb, s_ref, mt_ref):
        m_prev = m_ref[qb]
        m_new = jnp.maximum(m_prev, mt_ref[qb])
        m_safe = jnp.where(m_new == NEG_INF, 0.0, m_new)
        alpha = jnp.exp2(m_prev - m_safe)
        p = jnp.exp2((s_ref[qb] - m_safe).astype(BF16))
        acc_ref[qb] = alpha * acc_ref[qb] + _dot(vst_ref[j], p)
        m_ref[qb] = m_new

    for qb in range(SEL_QB):
        scores(0, qb, sa_ref, mta_ref)
    n_pairs = n_tiles // 2

    def step(jj, carry):
        j = 2 * jj
        for qb in range(SEL_QB):
            scores(j + 1, qb, sb_ref, mtb_ref)
            consume(j, qb, sa_ref, mta_ref)
        for qb in range(SEL_QB):
            scores(jnp.minimum(j + 2, n_tiles - 1), qb, sa_ref, mta_ref)
            consume(j + 1, qb, sb_ref, mtb_ref)
        return carry

    lax.fori_loop(0, n_pairs, step, 0)

    @pl.when(n_tiles % 2 == 1)
    def _():
        for qb in range(SEL_QB):
            consume(n_tiles - 1, qb, sa_ref, mta_ref)

    for qb in range(SEL_QB):
        o_t = acc_ref[qb, 0:d, :] / jnp.maximum(acc_ref[qb, d:d + 1, :], 1e-30)
        o_ref[qb * Q_BLOCK:(qb + 1) * Q_BLOCK, :] = _gate_untranspose(o_t, gate_ref, 1, qb)


def _sel_attn(qt, ks_rows, vs_t, sel_t, gates):
    b, h, d, t = qt.shape
    g = h // GROUP_R
    ns = sel_t.shape[2]
    tk = SEL_KEY_TILE
    rq = GROUP_R * Q_BLOCK
    nq = SEL_QB
    qw = nq * Q_BLOCK
    assert tk % qw == 0
    return pl.pallas_call(
        _sel_attn_kernel,
        grid=(b, g, t // qw),
        in_specs=[
            pl.BlockSpec((None, GROUP_R, d, qw), lambda bi, gi, i: (bi, gi, 0, i)),
            pl.BlockSpec((None, t, KV_HEADS * d), lambda bi, gi, i: (bi, 0, 0)),
            pl.BlockSpec((None, None, t // tk, SEL_V_ROWS, tk), lambda bi, gi, i: (bi, gi, 0, 0, 0)),
            pl.BlockSpec((None, None, ns, qw), lambda bi, gi, i: (bi, gi, 0, i)),
            pl.BlockSpec((None, None, N_GATE_ROWS, qw), lambda bi, gi, i: (bi, gi, 0, i)),
        ],
        out_specs=pl.BlockSpec((None, qw, GROUP_R * d), lambda bi, gi, i: (bi, i, gi)),
        out_shape=jax.ShapeDtypeStruct((b, t, h * d), F32),
        scratch_shapes=[pltpu.VMEM((nq, tk, rq), F32), pltpu.VMEM((nq, tk, rq), F32),
                        pltpu.VMEM((nq, 1, rq), F32), pltpu.VMEM((nq, 1, rq), F32),
                        pltpu.VMEM((nq, 1, rq), F32), pltpu.VMEM((nq, SEL_V_ROWS, rq), F32)],
        compiler_params=_params(("parallel", "parallel", "arbitrary"), VMEM_LIMIT),
        name="nsa_selected",
    )(qt, ks_rows, vs_t, sel_t, gates)


def _proj_retention_kernel(lg_ref, x_ref, g_ref, w_ref, cos_ref, sin_ref, ng_ref, o_ref, r_ref):
    ti = pl.program_id(1)
    c = RET_CHUNK
    h, dk, dv = RET_HEADS, RET_QK_DIM, RET_V_DIM
    half = dk // 2
    k0, v0, g0 = h * dk, 2 * h * dk, 2 * h * dk + h * dv

    @pl.when(ti == 0)
    def _():
        r_ref[...] = jnp.zeros_like(r_ref)

    xn = _rms_rows(x_ref[...], g_ref[...]).astype(BF16)
    acc = _dot(xn, w_ref[...])
    ii = lax.broadcasted_iota(jnp.int32, (c, c), 0)
    jj = lax.broadcasted_iota(jnp.int32, (c, c), 1)
    dist = (ii - jj).astype(F32)
    jcol = lax.broadcasted_iota(jnp.int32, (c, 1), 0).astype(F32)
    for hh in range(h):
        lg = lg_ref[hh]
        dmask = jnp.where(dist >= 0, jnp.exp(dist * lg), 0.0)
        xi = jnp.exp((jcol + 1.0) * lg)
        zeta = jnp.exp((c - 1.0 - jcol) * lg)
        decay = jnp.exp(jnp.zeros((1, dv), F32) + c * lg)
        for cc in range(acc.shape[0] // c):
            rows = slice(cc * c, (cc + 1) * c)
            cosf = cos_ref[rows, :]
            sinf = sin_ref[rows, :]

            def rot(x):
                return x * cosf + jnp.concatenate([x[:, half:], x[:, :half]], axis=1) * sinf

            q = rot(acc[rows, hh * dk:(hh + 1) * dk])
            k = rot(acc[rows, k0 + hh * dk:k0 + (hh + 1) * dk]) * (dk ** -0.5)
            vb = acc[rows, v0 + hh * dv:v0 + (hh + 1) * dv].astype(BF16)
            att = _dot_nt(q.astype(BF16), k.astype(BF16)) * dmask
            o = _dot(att.astype(BF16), vb)
            r_prev = r_ref[hh]
            o = o + _dot((q * xi).astype(BF16), r_prev.astype(BF16))
            s_chunk = _dot_tn((k * zeta).astype(BF16), vb)
            r_ref[hh] = r_prev * decay + s_chunk
            mu = jnp.mean(o, axis=-1, keepdims=True)
            var = jnp.mean(jnp.square(o - mu), axis=-1, keepdims=True)
            on = (o - mu) * lax.rsqrt(var + EPS)
            gt = acc[rows, g0 + hh * dv:g0 + (hh + 1) * dv]
            o_ref[rows, hh * dv:(hh + 1) * dv] = (
                (gt * _sigmoid(gt)) * (on * ng_ref[:, hh * dv:(hh + 1) * dv]))


def _proj_retention(x2, b, t, g, w_d, norm_gain, tm=512):
    n, dm = x2.shape
    h, dk, dv = RET_HEADS, RET_QK_DIM, RET_V_DIM
    half = dk // 2
    tpb = t // tm
    inv = ROPE_BASE ** (-jnp.arange(half, dtype=F32) / half)
    ang = jnp.arange(t).astype(F32)[:, None] * inv[None, :]
    cos = jnp.cos(ang)
    sin = jnp.sin(ang)
    cosf = jnp.concatenate([cos, cos], axis=-1)
    sinf = jnp.concatenate([-sin, sin], axis=-1)
    log_gamma = jnp.log(1.0 - 2.0 ** (-5.0 - jnp.arange(h, dtype=F32)))
    return pl.pallas_call(
        _proj_retention_kernel,
        grid=(b, tpb),
        in_specs=[
            pl.BlockSpec(memory_space=pltpu.SMEM),
            pl.BlockSpec((tm, dm), lambda bi, ti: (bi * tpb + ti, 0)),
            pl.BlockSpec((1, dm), lambda bi, ti: (0, 0)),
            pl.BlockSpec(w_d.shape, lambda bi, ti: (0, 0), pipeline_mode=pl.Buffered(1)),
            pl.BlockSpec((tm, dk), lambda bi, ti: (ti, 0)),
            pl.BlockSpec((tm, dk), lambda bi, ti: (ti, 0)),
            pl.BlockSpec((1, h * dv), lambda bi, ti: (0, 0)),
        ],
        out_specs=pl.BlockSpec((None, tm, h * dv), lambda bi, ti: (bi, ti, 0)),
        out_shape=jax.ShapeDtypeStruct((b, t, h * dv), F32),
        scratch_shapes=[pltpu.VMEM((h, dk, dv), F32)],
        compiler_params=_params(("parallel", "arbitrary"), VMEM_LIMIT),
        name="proj_retention",
    )(log_gamma, x2, g.reshape(1, dm), w_d.astype(BF16), cosf, sinf, norm_gain.reshape(1, h * dv))


def _merge_kernel(x_ref, g_ref, wgate_ref, bias_ref, ya_ref, yb_ref, yc0_ref, yc1_ref, yc2_ref,
                  yd_ref, wb_ref, wo_ref, o_ref):
    x = x_ref[...]
    d = x.shape[1]
    u = _rms_rows(x, g_ref[...]).astype(BF16)
    ys = (ya_ref[...], yb_ref[...], yc0_ref[...] + yc1_ref[...] + yc2_ref[...], yd_ref[...])
    merged = jnp.zeros(x.shape, F32)
    for n in range(N_BRANCH):
        logits = _dot(u, wgate_ref[:, n * d:(n + 1) * d]) + bias_ref[:, n * d:(n + 1) * d]
        merged = merged + _sigmoid(logits) * _dot(ys[n].astype(BF16), wb_ref[n])
    o_ref[...] = x + _dot(merged.astype(BF16), wo_ref[...])


def _merge(x2, g, w_gate, bias, ys, w_branch, w_out, tm=512):
    n, d = x2.shape
    w = MIX_W
    row = lambda width: pl.BlockSpec((tm, width), lambda i: (i, 0))
    full = lambda shape: pl.BlockSpec(shape, lambda i: tuple(0 for _ in shape),
                                      pipeline_mode=pl.Buffered(1))
    return pl.pallas_call(
        _merge_kernel,
        grid=(n // tm,),
        in_specs=[row(d), full((1, d)), full((d, N_BRANCH * d)), full((1, N_BRANCH * d))]
        + [row(w)] * 6 + [full((N_BRANCH, w, d)), full((d, d))],
        out_specs=row(d),
        out_shape=jax.ShapeDtypeStruct((n, d), F32),
        compiler_params=_params(("parallel",), VMEM_LIMIT),
        name="merge_out",
    )(x2, g.reshape(1, d), w_gate.astype(BF16), bias.reshape(1, N_BRANCH * d),
      *[y.reshape(n, w) for y in ys], w_branch.astype(BF16), w_out.astype(BF16))


def _mixers(x2, b, t, mix_norm, w_in, merge_gate_bias, swa_q_gain, swa_k_gain, swa_sinks, conv_w,
            nsa_q_gain, nsa_k_gain, cmp_pos_k, cmp_pos_v, cmp_wk1, cmp_wk2, cmp_wv1, cmp_wv2,
            ret_norm_gain, w_branch, w_out):
    hd = HEAD_DIM
    swa_q, swa_kv = SWA_HEADS * hd, SWA_KV_HEADS * hd
    nsa_q, nsa_kv = NSA_HEADS * hd, NSA_KV_HEADS * hd
    ret_qk, ret_v = RET_HEADS * RET_QK_DIM, RET_HEADS * RET_V_DIM
    n_gate = NSA_HEADS * 3
    o_a = 0
    o_b = o_a + swa_q + 2 * swa_kv
    o_c = o_b + 3 * MIX_W
    o_cg = o_c + nsa_q + 6 * nsa_kv
    o_d = o_cg + n_gate
    o_g = o_d + 2 * ret_qk + 2 * ret_v

    per_g = GROUP_R * 3
    w_gates = [jnp.pad(w_in[:, o_cg + gi * per_g:o_cg + (gi + 1) * per_g],
                       ((0, 0), (0, N_GATE_ROWS - per_g))) for gi in range(NSA_KV_HEADS)]
    w_t = jnp.concatenate([w_in[:, o_a:o_b], w_in[:, o_c:o_cg]] + w_gates, axis=1).T.astype(BF16)
    head_gains = jnp.stack([swa_q_gain, swa_k_gain, nsa_q_gain, nsa_k_gain[1], nsa_k_gain[2]])
    head_gains = jnp.broadcast_to(head_gains.astype(F32)[:, :, None], (5, hd, 128))
    (a_qt, a_k, a_vt, c_qt, c_qt2, c_kc, c_vc, c_ks, c_vst, c_kw, c_vwt, gates_c) = _proj_heads(
        x2, b, t, mix_norm, w_t, head_gains)

    y_a = _banded(a_qt, a_k, a_vt, SWA_WINDOW, sinks=swa_sinks)

    y_b = _proj_conv(x2, b, t, mix_norm, w_in[:, o_b:o_c], conv_w)

    k_cmp, v_cmp = _compress(c_kc, c_vc, cmp_pos_k, cmp_pos_v, cmp_wk1, cmp_wk2, cmp_wv1, cmp_wv2,
                             nsa_k_gain[0])
    y_cmp, sel_t = _cmp_attn(c_qt, k_cmp, v_cmp.transpose(0, 2, 1), gates_c)
    y_sel = _sel_attn(c_qt2, c_ks, c_vst, sel_t, gates_c)
    y_win = _banded(c_qt, c_kw, c_vwt, NSA_WINDOW, gates=gates_c, gate_branch=2)

    y_d = _proj_retention(x2, b, t, mix_norm, w_in[:, o_d:o_g], ret_norm_gain)

    return _merge(x2, mix_norm, w_in[:, o_g:], merge_gate_bias,
                  (y_a, y_b, y_cmp, y_sel, y_win, y_d), w_branch, w_out)


def kernel(x, ffn1_norm, ffn1_w_gate, ffn1_w_up, ffn1_w_down, mix_norm, w_in, merge_gate_bias, swa_q_gain, swa_k_gain, swa_sinks, conv_w, nsa_q_gain, nsa_k_gain, cmp_pos_k, cmp_pos_v, cmp_wk1, cmp_wk2, cmp_wv1, cmp_wv2, ret_norm_gain, w_branch, w_out, ffn2_norm, ffn2_w_gate, ffn2_w_up, ffn2_w_down):
    b, t, d = x.shape
    x2 = x.reshape(b * t, d)
    for l in range(ffn1_norm.shape[0]):
        x2 = _ffn(x2, ffn1_norm[l], ffn1_w_gate[l], ffn1_w_up[l], ffn1_w_down[l])
        x2 = _mixers(x2, b, t, mix_norm[l], w_in[l], merge_gate_bias[l], swa_q_gain[l],
                     swa_k_gain[l], swa_sinks[l], conv_w[l], nsa_q_gain[l], nsa_k_gain[l],
                     cmp_pos_k[l], cmp_pos_v[l], cmp_wk1[l], cmp_wk2[l], cmp_wv1[l], cmp_wv2[l],
                     ret_norm_gain[l], w_branch[l], w_out[l])
        x2 = _ffn(x2, ffn2_norm[l], ffn2_w_gate[l], ffn2_w_up[l], ffn2_w_down[l])
    return x2.reshape(b, t, d)
```

```python
import functools

import jax
import jax.numpy as jnp
from jax import lax
from jax.experimental import pallas as pl
from jax.experimental.pallas import tpu as pltpu

F32 = jnp.float32
BF16 = jnp.bfloat16

HEAD_DIM = 64
Q_BLOCK = 128
MIX_W = 512
N_BRANCH = 4
SWA_HEADS = 8
SWA_KV_HEADS = 2
SWA_WINDOW = 128
NSA_HEADS = 8
NSA_KV_HEADS = 2
CMP_BLOCK = 32
CMP_STRIDE = 16
SEL_BLOCK = 64
SEL_TOPK = 16
N_FORCED = 3
NSA_WINDOW = 512
RET_HEADS = 4
RET_QK_DIM = 64
RET_V_DIM = 128
RET_CHUNK = 128
ROPE_BASE = 10000.0
EPS = 1e-6
GROUP_R = 4
KV_HEADS = 2
N_GATE_ROWS = 16
SEL_KEY_TILE = 512
SEL_V_ROWS = HEAD_DIM + 16
PROJ_TM = 512
BANDED_QB = 8
CMP_QB = 4
SEL_QB = 4
VMEM_LIMIT = 52 * 1024 * 1024

NEG_INF = float("-inf")
LOG2_E = 1.4426950408889634


def _params(sem, vmem=None):
    return pltpu.CompilerParams(dimension_semantics=sem, vmem_limit_bytes=vmem)


def _sigmoid(x):
    return 1.0 / (1.0 + jnp.exp(-x))


def _rms_rows(x, g):
    return x * lax.rsqrt(jnp.mean(x * x, axis=-1, keepdims=True) + EPS) * g


def _dot(a, b):
    return jnp.dot(a, b, preferred_element_type=F32)


def _dot_nt(a, b):
    return lax.dot_general(a, b, (((1,), (1,)), ((), ())), preferred_element_type=F32)


def _dot_tn(a, b):
    return lax.dot_general(a, b, (((0,), (0,)), ((), ())), preferred_element_type=F32)


def _ffn_kernel(x_ref, g_ref, wg_ref, wu_ref, wd_ref, o_ref, *, n_chunks):
    x = x_ref[...]
    xn = _rms_rows(x, g_ref[...]).astype(BF16)
    tf = wg_ref.shape[1] // n_chunks
    acc = None
    for f in range(n_chunks):
        a = _dot(xn, wg_ref[:, f * tf:(f + 1) * tf])
        b = _dot(xn, wu_ref[:, f * tf:(f + 1) * tf])
        h = ((a * _sigmoid(a)) * b).astype(BF16)
        part = _dot(h, wd_ref[f * tf:(f + 1) * tf, :])
        acc = part if acc is None else acc + part
    o_ref[...] = x + 0.5 * acc


def _ffn(x2, g, wg, wu, wd, tm=512, n_chunks=2):
    n, d = x2.shape
    dff = wg.shape[1]
    assert dff % (n_chunks * 128) == 0
    resident = lambda shape: pl.BlockSpec(shape, lambda i: (0, 0), pipeline_mode=pl.Buffered(1))
    return pl.pallas_call(
        functools.partial(_ffn_kernel, n_chunks=n_chunks),
        grid=(n // tm,),
        in_specs=[
            pl.BlockSpec((tm, d), lambda i: (i, 0)),
            pl.BlockSpec((1, d), lambda i: (0, 0)),
            resident((d, dff)), resident((d, dff)), resident((dff, d)),
        ],
        out_specs=pl.BlockSpec((tm, d), lambda i: (i, 0)),
        out_shape=jax.ShapeDtypeStruct((n, d), F32),
        compiler_params=_params(("parallel",), VMEM_LIMIT),
        name="ffn",
    )(x2, g.reshape(1, d), wg.astype(BF16), wu.astype(BF16), wd.astype(BF16))


def _proj_heads_kernel(x_ref, g_ref, wt_ref, hg_ref, aq_ref, ak_ref, av_ref, cq_ref, ckc_ref,
                       cvc_ref, cks_ref, cvs_ref, ckw_ref, cvw_ref, gt_ref):
    d = HEAD_DIM
    xn = _rms_rows(x_ref[...], g_ref[...]).astype(BF16)
    acc = _dot_nt(wt_ref[...], xn)
    tm = xn.shape[0]
    lane_tiles = tm // 128
    scale = d ** -0.5
    kv_w = KV_HEADS * d

    def head_norm(row0, gain_idx, mult):
        hb = acc[row0:row0 + d]
        gain = jnp.concatenate([hg_ref[gain_idx]] * lane_tiles, axis=1)
        y = hb * lax.rsqrt(jnp.mean(hb * hb, axis=0, keepdims=True) + EPS) * gain
        return y * mult if mult != 1.0 else y

    def q_heads(row0, gain_idx, out_ref, n_heads):
        for h in range(n_heads):
            out_ref[h] = head_norm(row0 + h * d, gain_idx, scale * LOG2_E).astype(BF16)

    def k_rows(row0, gain_idx):
        return jnp.concatenate([head_norm(row0 + g * d, gain_idx, 1.0) for g in range(KV_HEADS)],
                               axis=0).T

    def v_tiles(row0, out_ref):
        for g in range(KV_HEADS):
            for u in range(lane_tiles):
                out_ref[g, u, 0:d, :] = acc[row0 + g * d:row0 + (g + 1) * d,
                                            u * 128:(u + 1) * 128].astype(BF16)
                out_ref[g, u, d:SEL_V_ROWS, :] = jnp.ones((SEL_V_ROWS - d, 128), BF16)

    row = 0
    q_heads(row, 0, aq_ref, SWA_HEADS)
    row += SWA_HEADS * d
    ak_ref[...] = k_rows(row, 1).astype(BF16)
    row += kv_w
    v_tiles(row, av_ref)
    row += kv_w
    q_heads(row, 2, cq_ref, NSA_HEADS)
    row += NSA_HEADS * d
    ckc_ref[...] = acc[row:row + kv_w].T
    row += kv_w
    cvc_ref[...] = acc[row:row + kv_w].T
    row += kv_w
    cks_ref[...] = k_rows(row, 3).astype(BF16)
    row += kv_w
    for g in range(KV_HEADS):
        cvs_ref[g, 0, 0:d, :] = acc[row + g * d:row + (g + 1) * d].astype(BF16)
        cvs_ref[g, 0, d:SEL_V_ROWS, :] = jnp.ones((SEL_V_ROWS - d, tm), BF16)
    row += kv_w
    ckw_ref[...] = k_rows(row, 4).astype(BF16)
    row += kv_w
    v_tiles(row, cvw_ref)
    row += kv_w
    for g in range(KV_HEADS):
        gt_ref[g] = _sigmoid(acc[row + g * N_GATE_ROWS:row + (g + 1) * N_GATE_ROWS])


def _proj_heads(x2, b, t, g, w_t, head_gains):
    n, dm = x2.shape
    d = HEAD_DIM
    tm = PROJ_TM
    assert tm == SEL_KEY_TILE and t % tm == 0
    rows = w_t.shape[0]
    tpb = t // tm
    lt = tm // 128
    kv = KV_HEADS
    qt_spec = lambda h: pl.BlockSpec((None, h, d, tm), lambda bi, ti: (bi, 0, 0, ti))
    row_spec = pl.BlockSpec((None, tm, kv * d), lambda bi, ti: (bi, ti, 0))
    vt_spec = pl.BlockSpec((None, kv, lt, SEL_V_ROWS, 128), lambda bi, ti: (bi, 0, ti, 0, 0))
    sds = jax.ShapeDtypeStruct
    return pl.pallas_call(
        _proj_heads_kernel,
        grid=(b, tpb),
        in_specs=[
            pl.BlockSpec((tm, dm), lambda bi, ti: (bi * tpb + ti, 0)),
            pl.BlockSpec((1, dm), lambda bi, ti: (0, 0)),
            pl.BlockSpec((rows, dm), lambda bi, ti: (0, 0), pipeline_mode=pl.Buffered(1)),
            pl.BlockSpec(head_gains.shape, lambda bi, ti: (0, 0, 0)),
        ],
        out_specs=[
            qt_spec(SWA_HEADS), row_spec, vt_spec,
            qt_spec(NSA_HEADS), row_spec, row_spec, row_spec,
            pl.BlockSpec((None, kv, 1, SEL_V_ROWS, tm), lambda bi, ti: (bi, 0, ti, 0, 0)),
            row_spec, vt_spec,
            pl.BlockSpec((None, kv, N_GATE_ROWS, tm), lambda bi, ti: (bi, 0, 0, ti)),
        ],
        out_shape=[
            sds((b, SWA_HEADS, d, t), BF16), sds((b, t, kv * d), BF16),
            sds((b, kv, t // 128, SEL_V_ROWS, 128), BF16),
            sds((b, NSA_HEADS, d, t), BF16), sds((b, t, kv * d), F32), sds((b, t, kv * d), F32),
            sds((b, t, kv * d), BF16), sds((b, kv, tpb, SEL_V_ROWS, tm), BF16),
            sds((b, t, kv * d), BF16), sds((b, kv, t // 128, SEL_V_ROWS, 128), BF16),
            sds((b, kv, N_GATE_ROWS, t), F32),
        ],
        compiler_params=_params(("parallel", "parallel"), VMEM_LIMIT),
        name="proj_heads",
    )(x2, g.reshape(1, dm), w_t, head_gains)


def _group_q(qt_ref, qb=0):
    return jnp.concatenate(
        [qt_ref[r, :, qb * Q_BLOCK:(qb + 1) * Q_BLOCK] for r in range(GROUP_R)], axis=1)


def _padded_q(qt_ref, g, qb=0):
    q4 = _group_q(qt_ref, qb)
    z = jnp.zeros_like(q4)
    return jnp.where(g == 0, jnp.concatenate([q4, z], axis=0), jnp.concatenate([z, q4], axis=0))


def _gate_untranspose(o_t, gate_ref, branch, qb=0):
    outs = []
    for r in range(GROUP_R):
        blk = o_t[:, r * Q_BLOCK:(r + 1) * Q_BLOCK]
        if gate_ref is not None:
            c = r * 3 + branch
            blk = blk * gate_ref[c:c + 1, qb * Q_BLOCK:(qb + 1) * Q_BLOCK]
        outs.append(blk.T)
    return jnp.concatenate(outs, axis=1)


def _banded_kernel(*refs, window, has_sink, gate_branch):
    refs = list(refs)
    sink_ref = refs.pop(0) if has_sink else None
    qt_ref, k_ref, vt_ref = refs[:3]
    gate_ref = refs[3] if gate_branch is not None else None
    o_ref = refs[-1]
    g = pl.program_id(1)
    rq = GROUP_R * Q_BLOCK
    n_sub = window // Q_BLOCK + 1
    span = n_sub * Q_BLOCK
    if has_sink:
        lrow = lax.broadcasted_iota(jnp.int32, (1, rq), 1)
        sink = jnp.zeros((1, rq), F32)
        for r in range(GROUP_R):
            sink = jnp.where((lrow >= r * Q_BLOCK) & (lrow < (r + 1) * Q_BLOCK),
                             sink_ref[g * GROUP_R + r], sink)
        sink = sink * LOG2_E
    step = pl.program_id(2)
    row = lax.broadcasted_iota(jnp.int32, (span, Q_BLOCK), 0)
    qcol = lax.broadcasted_iota(jnp.int32, (span, Q_BLOCK), 1)

    def band_bias(offset):
        diff = offset + qcol - row
        return jnp.where((diff >= 0) & (diff < window), 0.0, NEG_INF)

    for qb in range(BANDED_QB):
        i = step * BANDED_QB + qb
        start = pl.multiple_of(jnp.maximum(i * Q_BLOCK - window, 0), Q_BLOCK)
        st = _dot(k_ref[pl.ds(start, span), :], _padded_q(qt_ref, g, qb))
        st = st + jnp.concatenate([band_bias(i * Q_BLOCK - start)] * GROUP_R, axis=1)
        m = jnp.max(st, axis=0, keepdims=True)
        if has_sink:
            m = jnp.maximum(m, sink)
        m = jnp.where(m == NEG_INF, 0.0, m)
        pb = jnp.exp2(st - m).astype(BF16)
        u0 = start // Q_BLOCK
        o_t = _dot(vt_ref[u0], pb[0:Q_BLOCK])
        for u in range(1, n_sub):
            o_t = o_t + _dot(vt_ref[u0 + u], pb[u * Q_BLOCK:(u + 1) * Q_BLOCK])
        denom = o_t[HEAD_DIM:HEAD_DIM + 1, :]
        if has_sink:
            denom = denom + jnp.exp2(sink - m)
        o_t = o_t[0:HEAD_DIM, :] / jnp.maximum(denom, 1e-30)
        o_ref[qb * Q_BLOCK:(qb + 1) * Q_BLOCK, :] = _gate_untranspose(
            o_t, gate_ref, gate_branch, qb)


def _banded(qt, k_rows, vt, window, sinks=None, gates=None, gate_branch=None):
    b, h, d, t = qt.shape
    g = h // GROUP_R
    in_specs = []
    args = []
    if sinks is not None:
        in_specs.append(pl.BlockSpec(memory_space=pltpu.SMEM))
        args.append(sinks.astype(F32))
    qw = BANDED_QB * Q_BLOCK
    in_specs += [
        pl.BlockSpec((None, GROUP_R, d, qw), lambda bi, gi, i: (bi, gi, 0, i)),
        pl.BlockSpec((None, t, KV_HEADS * d), lambda bi, gi, i: (bi, 0, 0)),
        pl.BlockSpec((None, None, t // Q_BLOCK, SEL_V_ROWS, Q_BLOCK),
                     lambda bi, gi, i: (bi, gi, 0, 0, 0)),
    ]
    args += [qt, k_rows, vt]
    if gates is not None:
        in_specs.append(pl.BlockSpec((None, None, N_GATE_ROWS, qw),
                                     lambda bi, gi, i: (bi, gi, 0, i)))
        args.append(gates)
    return pl.pallas_call(
        functools.partial(_banded_kernel, window=window, has_sink=sinks is not None,
                          gate_branch=gate_branch if gates is not None else None),
        grid=(b, g, t // qw),
        in_specs=in_specs,
        out_specs=pl.BlockSpec((None, qw, GROUP_R * d), lambda bi, gi, i: (bi, i, gi)),
        out_shape=jax.ShapeDtypeStruct((b, t, h * d), F32),
        compiler_params=_params(("parallel", "parallel", "arbitrary"), VMEM_LIMIT),
        name="banded_attn_w%d" % window,
    )(*args)


def _proj_conv_kernel(x_ref, g_ref, w_ref, cw_ref, o_ref, tail_ref):
    ti = pl.program_id(1)
    w = MIX_W
    xn = _rms_rows(x_ref[...], g_ref[...]).astype(BF16)
    acc = _dot(xn, w_ref[...])
    z = acc[:, 2 * w:3 * w] * acc[:, 0:w]
    zp = jnp.where(ti > 0, tail_ref[...], 0.0)
    row = lax.broadcasted_iota(jnp.int32, z.shape, 0)
    z1 = jnp.where(row == 0, zp[7:8, :], pltpu.roll(z, 1, 0))
    z2 = pltpu.roll(z, 2, 0)
    z2 = jnp.where(row == 0, zp[6:7, :], jnp.where(row == 1, zp[7:8, :], z2))
    cw = cw_ref[...]
    o_ref[...] = acc[:, w:2 * w] * (cw[0:1, :] * z2 + cw[1:2, :] * z1 + cw[2:3, :] * z)
    tail_ref[...] = z[z.shape[0] - 8:, :]


def _proj_conv(x2, b, t, g, w_b, conv_w, tm=512):
    n, dm = x2.shape
    w = MIX_W
    tpb = t // tm
    return pl.pallas_call(
        _proj_conv_kernel,
        grid=(b, tpb),
        in_specs=[
            pl.BlockSpec((tm, dm), lambda bi, ti: (bi * tpb + ti, 0)),
            pl.BlockSpec((1, dm), lambda bi, ti: (0, 0)),
            pl.BlockSpec((dm, 3 * w), lambda bi, ti: (0, 0), pipeline_mode=pl.Buffered(1)),
            pl.BlockSpec((8, w), lambda bi, ti: (0, 0)),
        ],
        out_specs=pl.BlockSpec((None, tm, w), lambda bi, ti: (bi, ti, 0)),
        out_shape=jax.ShapeDtypeStruct((b, t, w), F32),
        scratch_shapes=[pltpu.VMEM((8, w), F32)],
        compiler_params=_params(("parallel", "arbitrary"), VMEM_LIMIT),
        name="proj_conv",
    )(x2, g.reshape(1, dm), w_b.astype(BF16),
      jnp.pad(conv_w.reshape(conv_w.shape[0], w).astype(F32), ((0, 8 - conv_w.shape[0]), (0, 0))))


def _gelu_tanh(x):
    return x * (0.5 * (1.0 + jnp.tanh(0.7978845608028654 * (x + 0.044715 * (x * x * x)))))


def _compress_kernel(tk_ref, tv_ref, pek_ref, pev_ref, wk1_ref, wk2_ref, wv1_ref, wv2_ref,
                     kg_ref, ko_ref, vo_ref):
    nrow = tk_ref.shape[0]

    def mlp(a, pe_ref, w1_ref, w2_ref):
        a0 = (a + pe_ref[0:1, :]).astype(BF16)
        a1 = (a + pe_ref[1:2, :]).astype(BF16)
        p1 = _dot(a0, w1_ref[0])
        p2 = _dot(a1, w1_ref[1])
        hdn = p1 + pltpu.roll(p2, nrow - 1, 0)
        return _dot(_gelu_tanh(hdn).astype(BF16), w2_ref[...])

    kc = mlp(tk_ref[...], pek_ref, wk1_ref, wk2_ref)
    ko_ref[...] = _rms_rows(kc, kg_ref[...]).astype(BF16)
    vo_ref[...] = mlp(tv_ref[...], pev_ref, wv1_ref, wv2_ref).astype(BF16)


def _compress(kc_rows, vc_rows, pos_k, pos_v, wk1, wk2, wv1, wv2, k_gain):
    b, t, kvd = kc_rows.shape
    kv = KV_HEADS
    d = kvd // kv
    nrow = t // CMP_STRIDE
    wide = CMP_STRIDE * kvd
    hid = wk1.shape[1]

    def expand_w1(w1):
        w = w1.reshape(2, CMP_STRIDE, 1, d, hid)
        per_head = []
        for g in range(kv):
            pads = [jnp.zeros_like(w)] * kv
            pads[g] = w
            per_head.append(jnp.concatenate(pads, axis=2).reshape(2, wide, hid))
        return jnp.stack(per_head).astype(BF16)

    def expand_pe(pe):
        return jnp.broadcast_to(pe.reshape(2, CMP_STRIDE, 1, d), (2, CMP_STRIDE, kv, d)).reshape(2, wide)

    tok = pl.BlockSpec((None, nrow, wide), lambda bi, gi: (bi, 0, 0))
    full = lambda shape: pl.BlockSpec(shape, lambda bi, gi: tuple(0 for _ in shape))
    w1_spec = pl.BlockSpec((None, 2, wide, hid), lambda bi, gi: (gi, 0, 0, 0))
    out = pl.BlockSpec((None, nrow, d), lambda bi, gi: (bi * kv + gi, 0, 0))
    return pl.pallas_call(
        _compress_kernel,
        grid=(b, kv),
        in_specs=[tok, tok, full((2, wide)), full((2, wide)), w1_spec, full((hid, d)),
                  w1_spec, full((hid, d)), full((1, d))],
        out_specs=[out, out],
        out_shape=[jax.ShapeDtypeStruct((b * kv, nrow, d), BF16),
                   jax.ShapeDtypeStruct((b * kv, nrow, d), BF16)],
        compiler_params=_params(("parallel", "arbitrary"), VMEM_LIMIT),
        name="nsa_compress",
    )(kc_rows.reshape(b, nrow, wide), vc_rows.reshape(b, nrow, wide), expand_pe(pos_k),
      expand_pe(pos_v), expand_w1(wk1), wk2.astype(BF16), expand_w1(wv1), wv2.astype(BF16),
      k_gain.reshape(1, d))


def _cmp_attn_kernel(qt_ref, kc_ref, vct_ref, gate_ref, o_ref, selt_ref, imp_ref, *, sel_k):
    step = pl.program_id(2)
    rq = GROUP_R * Q_BLOCK
    nc = kc_ref.shape[0]
    ns = selt_ref.shape[0]
    blk = lax.broadcasted_iota(jnp.int32, (ns, Q_BLOCK), 0)

    def forced_blocks(i):
        cur = (i * Q_BLOCK + lax.broadcasted_iota(jnp.int32, (ns, Q_BLOCK), 1)) // SEL_BLOCK
        return (blk == 0) | (blk == cur) | (blk == cur - 1), cur

    def attend(nc_eff):
        n = lax.broadcasted_iota(jnp.int32, (nc_eff, rq), 0)
        lane = lax.broadcasted_iota(jnp.int32, (nc_eff, rq), 1)
        end_minus_q = n * CMP_STRIDE + (CMP_BLOCK - 1) - (lane & (Q_BLOCK - 1))
        ss = lax.broadcasted_iota(jnp.int32, (ns, nc_eff), 0) * SEL_BLOCK
        nn = lax.broadcasted_iota(jnp.int32, (ns, nc_eff), 1) * CMP_STRIDE
        overlap_t = jnp.where((nn < ss + SEL_BLOCK) & (nn + (CMP_BLOCK - 1) >= ss),
                              1.0, 0.0).astype(BF16)
        lhs = jnp.concatenate([vct_ref[:, 0:nc_eff], overlap_t], axis=0)
        for qb in range(CMP_QB):
            i = step * CMP_QB + qb
            st = _dot(kc_ref[0:nc_eff, :], _group_q(qt_ref, qb))
            st = jnp.where(end_minus_q <= i * Q_BLOCK, st, NEG_INF)
            m = jnp.max(st, axis=0, keepdims=True)
            m = jnp.where(m == NEG_INF, 0.0, m)
            p = jnp.exp2(st - m)
            denom = jnp.sum(p, axis=0, keepdims=True)
            pb = (p / jnp.maximum(denom, 1e-30)).astype(BF16)
            res = _dot(lhs, pb)
            o_ref[qb * Q_BLOCK:(qb + 1) * Q_BLOCK, :] = _gate_untranspose(
                res[0:HEAD_DIM], gate_ref, 0, qb)
            imp = res[HEAD_DIM:, 0:Q_BLOCK]
            for r in range(1, GROUP_R):
                imp = imp + res[HEAD_DIM:, r * Q_BLOCK:(r + 1) * Q_BLOCK]
            forced, cur = forced_blocks(i)
            imp_ref[qb] = jnp.where(forced | (blk > cur), NEG_INF, imp)

    variant_rows = 128
    n_var = nc // variant_rows
    n_ending = (step + 1) * (CMP_QB * Q_BLOCK // CMP_STRIDE) - 1
    variant = jnp.minimum((n_ending + variant_rows - 1) // variant_rows, n_var) - 1
    for v in range(n_var):
        pl.when(variant == v)(functools.partial(attend, (v + 1) * variant_rows))

    blkf = blk.astype(F32)
    imps = [imp_ref[qb] for qb in range(CMP_QB)]
    sels = [jnp.where(forced_blocks(step * CMP_QB + qb)[0], 1.0, 0.0) for qb in range(CMP_QB)]
    for _ in range(sel_k - N_FORCED):
        for qb in range(CMP_QB):
            mx = jnp.max(imps[qb], axis=0, keepdims=True)
            first = jnp.min(jnp.where(imps[qb] == mx, blkf, float(ns)), axis=0, keepdims=True)
            hit = blkf == first
            sels[qb] = jnp.where(hit & (mx > NEG_INF), 1.0, sels[qb])
            imps[qb] = jnp.where(hit, NEG_INF, imps[qb])
    for qb in range(CMP_QB):
        selt_ref[:, qb * Q_BLOCK:(qb + 1) * Q_BLOCK] = sels[qb]


def _cmp_attn(qt, k_cmp, v_cmp_t, gates):
    b, h, d, t = qt.shape
    g = h // GROUP_R
    nc = k_cmp.shape[1]
    ns = t // SEL_BLOCK
    qw = CMP_QB * Q_BLOCK
    assert nc % 128 == 0
    return pl.pallas_call(
        functools.partial(_cmp_attn_kernel, sel_k=min(SEL_TOPK, ns)),
        grid=(b, g, t // qw),
        in_specs=[
            pl.BlockSpec((None, GROUP_R, d, qw), lambda bi, gi, i: (bi, gi, 0, i)),
            pl.BlockSpec((None, nc, d), lambda bi, gi, i: (bi * KV_HEADS + gi, 0, 0)),
            pl.BlockSpec((None, d, nc), lambda bi, gi, i: (bi * KV_HEADS + gi, 0, 0)),
            pl.BlockSpec((None, None, N_GATE_ROWS, qw), lambda bi, gi, i: (bi, gi, 0, i)),
        ],
        out_specs=[
            pl.BlockSpec((None, qw, GROUP_R * d), lambda bi, gi, i: (bi, i, gi)),
            pl.BlockSpec((None, None, ns, qw), lambda bi, gi, i: (bi, gi, 0, i)),
        ],
        out_shape=[jax.ShapeDtypeStruct((b, t, h * d), F32),
                   jax.ShapeDtypeStruct((b, g, ns, t), F32)],
        scratch_shapes=[pltpu.VMEM((CMP_QB, ns, Q_BLOCK), F32)],
        compiler_params=_params(("parallel", "parallel", "arbitrary"), VMEM_LIMIT),
        name="nsa_cmp_topk",
    )(qt, k_cmp, v_cmp_t, gates)


def _sel_attn_kernel(qt_ref, ks_ref, vst_ref, selt_ref, gate_ref, o_ref, sa_ref, sb_ref, mta_ref,
                     mtb_ref, m_ref, acc_ref):
    g = pl.program_id(1)
    tk = SEL_KEY_TILE
    bpt = tk // SEL_BLOCK
    spt = tk // Q_BLOCK
    d = HEAD_DIM
    m_ref[...] = jnp.full(m_ref.shape, NEG_INF, F32)
    acc_ref[...] = jnp.zeros_like(acc_ref)
    last_q = (pl.program_id(2) + 1) * SEL_QB * Q_BLOCK
    n_tiles = (last_q + tk - 1) // tk
    qpads = [_padded_q(qt_ref, g, qb) for qb in range(SEL_QB)]
    tri = jnp.where(lax.broadcasted_iota(jnp.int32, (Q_BLOCK, Q_BLOCK), 0)
                    <= lax.broadcasted_iota(jnp.int32, (Q_BLOCK, Q_BLOCK), 1), 0.0, NEG_INF)

    def scores(j, qb, s_ref, mt_ref):
        i = pl.program_id(2) * SEL_QB + qb
        k0 = pl.multiple_of(j * tk, tk)
        st = _dot(ks_ref[pl.ds(k0, tk), :], qpads[qb])
        sel_rows = selt_ref[pl.ds(pl.multiple_of(j * bpt, bpt), bpt),
                            qb * Q_BLOCK:(qb + 1) * Q_BLOCK]
        bias_rows = jnp.where(sel_rows > 0.5, 0.0, NEG_INF)
        parts = []
        for u in range(spt):
            sub = jnp.concatenate(
                [jnp.broadcast_to(bias_rows[s:s + 1, :], (SEL_BLOCK, Q_BLOCK))
                 for s in range(u * Q_BLOCK // SEL_BLOCK, (u + 1) * Q_BLOCK // SEL_BLOCK)], axis=0)
            parts.append(sub + jnp.where(j * spt + u == i, tri, 0.0))
        bias = jnp.concatenate(parts, axis=0)
        st = st + jnp.concatenate([bias] * GROUP_R, axis=1)
        s_ref[qb] = st
        mt_ref[qb] = jnp.max(st, axis=0, keepdims=True)

    def consume(j, qb, s_ref, mt_ref):
        m_prev = m_ref[qb]
        m_new = jnp.maximum(m_prev, mt_ref[qb])
        m_safe = jnp.where(m_new == NEG_INF, 0.0, m_new)
        alpha = jnp.exp2(m_prev - m_safe)
        p = jnp.exp2((s_ref[qb] - m_safe).astype(BF16))
        acc_ref[qb] = alpha * acc_ref[qb] + _dot(vst_ref[j], p)
        m_ref[qb] = m_new

    for qb in range(SEL_QB):
        scores(0, qb, sa_ref, mta_ref)
    n_pairs = n_tiles // 2

    def step(jj, carry):
        j = 2 * jj
        for qb in range(SEL_QB):
            scores(j + 1, qb, sb_ref, mtb_ref)
            consume(j, qb, sa_ref, mta_ref)
        for qb in range(SEL_QB):
            scores(jnp.minimum(j + 2, n_tiles - 1), qb, sa_ref, mta_ref)
            consume(j + 1, qb, sb_ref, mtb_ref)
        return carry

    lax.fori_loop(0, n_pairs, step, 0)

    @pl.when(n_tiles % 2 == 1)
    def _():
        for qb in range(SEL_QB):
            consume(n_tiles - 1, qb, sa_ref, mta_ref)

    for qb in range(SEL_QB):
        o_t = acc_ref[qb, 0:d, :] / jnp.maximum(acc_ref[qb, d:d + 1, :], 1e-30)
        o_ref[qb * Q_BLOCK:(qb + 1) * Q_BLOCK, :] = _gate_untranspose(o_t, gate_ref, 1, qb)


def _sel_attn(qt, ks_rows, vs_t, sel_t, gates):
    b, h, d, t = qt.shape
    g = h // GROUP_R
    ns = sel_t.shape[2]
    tk = SEL_KEY_TILE
    rq = GROUP_R * Q_BLOCK
    nq = SEL_QB
    qw = nq * Q_BLOCK
    assert tk % qw == 0
    return pl.pallas_call(
        _sel_attn_kernel,
        grid=(b, g, t // qw),
        in_specs=[
            pl.BlockSpec((None, GROUP_R, d, qw), lambda bi, gi, i: (bi, gi, 0, i)),
            pl.BlockSpec((None, t, KV_HEADS * d), lambda bi, gi, i: (bi, 0, 0)),
            pl.BlockSpec((None, None, t // tk, SEL_V_ROWS, tk), lambda bi, gi, i: (bi, gi, 0, 0, 0)),
            pl.BlockSpec((None, None, ns, qw), lambda bi, gi, i: (bi, gi, 0, i)),
            pl.BlockSpec((None, None, N_GATE_ROWS, qw), lambda bi, gi, i: (bi, gi, 0, i)),
        ],
        out_specs=pl.BlockSpec((None, qw, GROUP_R * d), lambda bi, gi, i: (bi, i, gi)),
        out_shape=jax.ShapeDtypeStruct((b, t, h * d), F32),
        scratch_shapes=[pltpu.VMEM((nq, tk, rq), F32), pltpu.VMEM((nq, tk, rq), F32),
                        pltpu.VMEM((nq, 1, rq), F32), pltpu.VMEM((nq, 1, rq), F32),
                        pltpu.VMEM((nq, 1, rq), F32), pltpu.VMEM((nq, SEL_V_ROWS, rq), F32)],
        compiler_params=_params(("parallel", "parallel", "arbitrary"), VMEM_LIMIT),
        name="nsa_selected",
    )(qt, ks_rows, vs_t, sel_t, gates)


def _proj_retention_kernel(lg_ref, x_ref, g_ref, w_ref, cos_ref, sin_ref, ng_ref, o_ref, r_ref):
    ti = pl.program_id(1)
    c = RET_CHUNK
    h, dk, dv = RET_HEADS, RET_QK_DIM, RET_V_DIM
    half = dk // 2
    k0, v0, g0 = h * dk, 2 * h * dk, 2 * h * dk + h * dv

    @pl.when(ti == 0)
    def _():
        r_ref[...] = jnp.zeros_like(r_ref)

    xn = _rms_rows(x_ref[...], g_ref[...]).astype(BF16)
    acc = _dot(xn, w_ref[...])
    ii = lax.broadcasted_iota(jnp.int32, (c, c), 0)
    jj = lax.broadcasted_iota(jnp.int32, (c, c), 1)
    dist = (ii - jj).astype(F32)
    jcol = lax.broadcasted_iota(jnp.int32, (c, 1), 0).astype(F32)
    for hh in range(h):
        lg = lg_ref[hh]
        dmask = jnp.where(dist >= 0, jnp.exp(dist * lg), 0.0)
        xi = jnp.exp((jcol + 1.0) * lg)
        zeta = jnp.exp((c - 1.0 - jcol) * lg)
        decay = jnp.exp(jnp.zeros((1, dv), F32) + c * lg)
        for cc in range(acc.shape[0] // c):
            rows = slice(cc * c, (cc + 1) * c)
            cosf = cos_ref[rows, :]
            sinf = sin_ref[rows, :]

            def rot(x):
                return x * cosf + jnp.concatenate([x[:, half:], x[:, :half]], axis=1) * sinf

            q = rot(acc[rows, hh * dk:(hh + 1) * dk])
            k = rot(acc[rows, k0 + hh * dk:k0 + (hh + 1) * dk]) * (dk ** -0.5)
            vb = acc[rows, v0 + hh * dv:v0 + (hh + 1) * dv].astype(BF16)
            att = _dot_nt(q.astype(BF16), k.astype(BF16)) * dmask
            o = _dot(att.astype(BF16), vb)
            r_prev = r_ref[hh]
            o = o + _dot((q * xi).astype(BF16), r_prev.astype(BF16))
            s_chunk = _dot_tn((k * zeta).astype(BF16), vb)
            r_ref[hh] = r_prev * decay + s_chunk
            mu = jnp.mean(o, axis=-1, keepdims=True)
            var = jnp.mean(jnp.square(o - mu), axis=-1, keepdims=True)
            on = (o - mu) * lax.rsqrt(var + EPS)
            gt = acc[rows, g0 + hh * dv:g0 + (hh + 1) * dv]
            o_ref[rows, hh * dv:(hh + 1) * dv] = (
                (gt * _sigmoid(gt)) * (on * ng_ref[:, hh * dv:(hh + 1) * dv]))


def _proj_retention(x2, b, t, g, w_d, norm_gain, tm=512):
    n, dm = x2.shape
    h, dk, dv = RET_HEADS, RET_QK_DIM, RET_V_DIM
    half = dk // 2
    tpb = t // tm
    inv = ROPE_BASE ** (-jnp.arange(half, dtype=F32) / half)
    ang = jnp.arange(t).astype(F32)[:, None] * inv[None, :]
    cos = jnp.cos(ang)
    sin = jnp.sin(ang)
    cosf = jnp.concatenate([cos, cos], axis=-1)
    sinf = jnp.concatenate([-sin, sin], axis=-1)
    log_gamma = jnp.log(1.0 - 2.0 ** (-5.0 - jnp.arange(h, dtype=F32)))
    return pl.pallas_call(
        _proj_retention_kernel,
        grid=(b, tpb),
        in_specs=[
            pl.BlockSpec(memory_space=pltpu.SMEM),
            pl.BlockSpec((tm, dm), lambda bi, ti: (bi * tpb + ti, 0)),
            pl.BlockSpec((1, dm), lambda bi, ti: (0, 0)),
            pl.BlockSpec(w_d.shape, lambda bi, ti: (0, 0), pipeline_mode=pl.Buffered(1)),
            pl.BlockSpec((tm, dk), lambda bi, ti: (ti, 0)),
            pl.BlockSpec((tm, dk), lambda bi, ti: (ti, 0)),
            pl.BlockSpec((1, h * dv), lambda bi, ti: (0, 0)),
        ],
        out_specs=pl.BlockSpec((None, tm, h * dv), lambda bi, ti: (bi, ti, 0)),
        out_shape=jax.ShapeDtypeStruct((b, t, h * dv), F32),
        scratch_shapes=[pltpu.VMEM((h, dk, dv), F32)],
        compiler_params=_params(("parallel", "arbitrary"), VMEM_LIMIT),
        name="proj_retention",
    )(log_gamma, x2, g.reshape(1, dm), w_d.astype(BF16), cosf, sinf, norm_gain.reshape(1, h * dv))


def _merge_kernel(x_ref, g_ref, wgate_ref, bias_ref, ya_ref, yb_ref, yc0_ref, yc1_ref, yc2_ref,
                  yd_ref, wb_ref, wo_ref, o_ref):
    x = x_ref[...]
    d = x.shape[1]
    u = _rms_rows(x, g_ref[...]).astype(BF16)
    ys = (ya_ref[...], yb_ref[...], yc0_ref[...] + yc1_ref[...] + yc2_ref[...], yd_ref[...])
    merged = jnp.zeros(x.shape, F32)
    for n in range(N_BRANCH):
        logits = _dot(u, wgate_ref[:, n * d:(n + 1) * d]) + bias_ref[:, n * d:(n + 1) * d]
        merged = merged + _sigmoid(logits) * _dot(ys[n].astype(BF16), wb_ref[n])
    o_ref[...] = x + _dot(merged.astype(BF16), wo_ref[...])


def _merge(x2, g, w_gate, bias, ys, w_branch, w_out, tm=512):
    n, d = x2.shape
    w = MIX_W
    row = lambda width: pl.BlockSpec((tm, width), lambda i: (i, 0))
    full = lambda shape: pl.BlockSpec(shape, lambda i: tuple(0 for _ in shape),
                                      pipeline_mode=pl.Buffered(1))
    return pl.pallas_call(
        _merge_kernel,
        grid=(n // tm,),
        in_specs=[row(d), full((1, d)), full((d, N_BRANCH * d)), full((1, N_BRANCH * d))]
        + [row(w)] * 6 + [full((N_BRANCH, w, d)), full((d, d))],
        out_specs=row(d),
        out_shape=jax.ShapeDtypeStruct((n, d), F32),
        compiler_params=_params(("parallel",), VMEM_LIMIT),
        name="merge_out",
    )(x2, g.reshape(1, d), w_gate.astype(BF16), bias.reshape(1, N_BRANCH * d),
      *[y.reshape(n, w) for y in ys], w_branch.astype(BF16), w_out.astype(BF16))


def _mixers(x2, b, t, mix_norm, w_in, merge_gate_bias, swa_q_gain, swa_k_gain, swa_sinks, conv_w,
            nsa_q_gain, nsa_k_gain, cmp_pos_k, cmp_pos_v, cmp_wk1, cmp_wk2, cmp_wv1, cmp_wv2,
            ret_norm_gain, w_branch, w_out):
    hd = HEAD_DIM
    swa_q, swa_kv = SWA_HEADS * hd, SWA_KV_HEADS * hd
    nsa_q, nsa_kv = NSA_HEADS * hd, NSA_KV_HEADS * hd
    ret_qk, ret_v = RET_HEADS * RET_QK_DIM, RET_HEADS * RET_V_DIM
    n_gate = NSA_HEADS * 3
    o_a = 0
    o_b = o_a + swa_q + 2 * swa_kv
    o_c = o_b + 3 * MIX_W
    o_cg = o_c + nsa_q + 6 * nsa_kv
    o_d = o_cg + n_gate
    o_g = o_d + 2 * ret_qk + 2 * ret_v

    per_g = GROUP_R * 3
    w_gates = [jnp.pad(w_in[:, o_cg + gi * per_g:o_cg + (gi + 1) * per_g],
                       ((0, 0), (0, N_GATE_ROWS - per_g))) for gi in range(NSA_KV_HEADS)]
    w_t = jnp.concatenate([w_in[:, o_a:o_b], w_in[:, o_c:o_cg]] + w_gates, axis=1).T.astype(BF16)
    head_gains = jnp.stack([swa_q_gain, swa_k_gain, nsa_q_gain, nsa_k_gain[1], nsa_k_gain[2]])
    head_gains = jnp.broadcast_to(head_gains.astype(F32)[:, :, None], (5, hd, 128))
    (a_qt, a_k, a_vt, c_qt, c_kc, c_vc, c_ks, c_vst, c_kw, c_vwt, gates_c) = _proj_heads(
        x2, b, t, mix_norm, w_t, head_gains)

    y_a = _banded(a_qt, a_k, a_vt, SWA_WINDOW, sinks=swa_sinks)

    y_b = _proj_conv(x2, b, t, mix_norm, w_in[:, o_b:o_c], conv_w)

    k_cmp, v_cmp = _compress(c_kc, c_vc, cmp_pos_k, cmp_pos_v, cmp_wk1, cmp_wk2, cmp_wv1, cmp_wv2,
                             nsa_k_gain[0])
    y_cmp, sel_t = _cmp_attn(c_qt, k_cmp, v_cmp.transpose(0, 2, 1), gates_c)
    y_sel = _sel_attn(c_qt, c_ks, c_vst, sel_t, gates_c)
    y_win = _banded(c_qt, c_kw, c_vwt, NSA_WINDOW, gates=gates_c, gate_branch=2)

    y_d = _proj_retention(x2, b, t, mix_norm, w_in[:, o_d:o_g], ret_norm_gain)

    return _merge(x2, mix_norm, w_in[:, o_g:], merge_gate_bias,
                  (y_a, y_b, y_cmp, y_sel, y_win, y_d), w_branch, w_out)


def kernel(x, ffn1_norm, ffn1_w_gate, ffn1_w_up, ffn1_w_down, mix_norm, w_in, merge_gate_bias, swa_q_gain, swa_k_gain, swa_sinks, conv_w, nsa_q_gain, nsa_k_gain, cmp_pos_k, cmp_pos_v, cmp_wk1, cmp_wk2, cmp_wv1, cmp_wv2, ret_norm_gain, w_branch, w_out, ffn2_norm, ffn2_w_gate, ffn2_w_up, ffn2_w_down):
    b, t, d = x.shape
    x2 = x.reshape(b * t, d)
    for l in range(ffn1_norm.shape[0]):
        x2 = _ffn(x2, ffn1_norm[l], ffn1_w_gate[l], ffn1_w_up[l], ffn1_w_down[l])
        x2 = _mixers(x2, b, t, mix_norm[l], w_in[l], merge_gate_bias[l], swa_q_gain[l],
                     swa_k_gain[l], swa_sinks[l], conv_w[l], nsa_q_gain[l], nsa_k_gain[l],
                     cmp_pos_k[l], cmp_pos_v[l], cmp_wk1[l], cmp_wk2[l], cmp_wv1[l], cmp_wv2[l],
                     ret_norm_gain[l], w_branch[l], w_out[l])
        x2 = _ffn(x2, ffn2_norm[l], ffn2_w_gate[l], ffn2_w_up[l], ffn2_w_down[l])
    return x2.reshape(b, t, d)
```

```python
import functools

import jax
import jax.numpy as jnp
from jax import lax
from jax.experimental import pallas as pl
from jax.experimental.pallas import tpu as pltpu

F32 = jnp.float32
BF16 = jnp.bfloat16

HEAD_DIM = 64
Q_BLOCK = 128
MIX_W = 512
N_BRANCH = 4
SWA_HEADS = 8
SWA_KV_HEADS = 2
SWA_WINDOW = 128
NSA_HEADS = 8
NSA_KV_HEADS = 2
CMP_BLOCK = 32
CMP_STRIDE = 16
SEL_BLOCK = 64
SEL_TOPK = 16
N_FORCED = 3
NSA_WINDOW = 512
RET_HEADS = 4
RET_QK_DIM = 64
RET_V_DIM = 128
RET_CHUNK = 128
ROPE_BASE = 10000.0
EPS = 1e-6
GROUP_R = 4
KV_HEADS = 2
N_GATE_ROWS = 16
SEL_KEY_TILE = 512
SEL_V_ROWS = HEAD_DIM + 16
PROJ_TM = 512
BANDED_QB = 8
CMP_QB = 4
SEL_QB = 4
VMEM_LIMIT = 52 * 1024 * 1024

NEG_INF = float("-inf")
LOG2_E = 1.4426950408889634


def _params(sem, vmem=None):
    return pltpu.CompilerParams(dimension_semantics=sem, vmem_limit_bytes=vmem)


def _sigmoid(x):
    return 1.0 / (1.0 + jnp.exp(-x))


def _rms_rows(x, g):
    return x * lax.rsqrt(jnp.mean(x * x, axis=-1, keepdims=True) + EPS) * g


def _dot(a, b):
    return jnp.dot(a, b, preferred_element_type=F32)


def _dot_nt(a, b):
    return lax.dot_general(a, b, (((1,), (1,)), ((), ())), preferred_element_type=F32)


def _dot_tn(a, b):
    return lax.dot_general(a, b, (((0,), (0,)), ((), ())), preferred_element_type=F32)


def _ffn_kernel(x_ref, g_ref, wg_ref, wu_ref, wd_ref, o_ref, *, n_chunks):
    x = x_ref[...]
    xn = _rms_rows(x, g_ref[...]).astype(BF16)
    tf = wg_ref.shape[1] // n_chunks
    acc = None
    for f in range(n_chunks):
        a = _dot(xn, wg_ref[:, f * tf:(f + 1) * tf])
        b = _dot(xn, wu_ref[:, f * tf:(f + 1) * tf])
        h = ((a * _sigmoid(a)) * b).astype(BF16)
        part = _dot(h, wd_ref[f * tf:(f + 1) * tf, :])
        acc = part if acc is None else acc + part
    o_ref[...] = x + 0.5 * acc


def _ffn(x2, g, wg, wu, wd, tm=512, n_chunks=2):
    n, d = x2.shape
    dff = wg.shape[1]
    assert dff % (n_chunks * 128) == 0
    resident = lambda shape: pl.BlockSpec(shape, lambda i: (0, 0), pipeline_mode=pl.Buffered(1))
    return pl.pallas_call(
        functools.partial(_ffn_kernel, n_chunks=n_chunks),
        grid=(n // tm,),
        in_specs=[
            pl.BlockSpec((tm, d), lambda i: (i, 0)),
            pl.BlockSpec((1, d), lambda i: (0, 0)),
            resident((d, dff)), resident((d, dff)), resident((dff, d)),
        ],
        out_specs=pl.BlockSpec((tm, d), lambda i: (i, 0)),
        out_shape=jax.ShapeDtypeStruct((n, d), F32),
        compiler_params=_params(("parallel",), VMEM_LIMIT),
        name="ffn",
    )(x2, g.reshape(1, d), wg.astype(BF16), wu.astype(BF16), wd.astype(BF16))


def _heads_body(xn, wt_ref, hg_ref, aq_ref, ak_ref, av_ref, cq_ref, ckc_ref, cvc_ref, cks_ref,
                cvs_ref, ckw_ref, cvw_ref, gt_ref):
    d = HEAD_DIM
    acc = _dot_nt(wt_ref[...], xn)
    tm = xn.shape[0]
    lane_tiles = tm // 128
    scale = d ** -0.5
    kv_w = KV_HEADS * d

    def head_norm(row0, gain_idx, mult):
        hb = acc[row0:row0 + d]
        gain = jnp.concatenate([hg_ref[gain_idx]] * lane_tiles, axis=1)
        y = hb * lax.rsqrt(jnp.mean(hb * hb, axis=0, keepdims=True) + EPS) * gain
        return y * mult if mult != 1.0 else y

    def q_heads(row0, gain_idx, out_ref, n_heads):
        for h in range(n_heads):
            out_ref[h] = head_norm(row0 + h * d, gain_idx, scale * LOG2_E).astype(BF16)

    def k_rows(row0, gain_idx):
        return jnp.concatenate([head_norm(row0 + g * d, gain_idx, 1.0) for g in range(KV_HEADS)],
                               axis=0).T

    def v_tiles(row0, out_ref):
        for g in range(KV_HEADS):
            for u in range(lane_tiles):
                out_ref[g, u, 0:d, :] = acc[row0 + g * d:row0 + (g + 1) * d,
                                            u * 128:(u + 1) * 128].astype(BF16)
                out_ref[g, u, d:SEL_V_ROWS, :] = jnp.ones((SEL_V_ROWS - d, 128), BF16)

    row = 0
    q_heads(row, 0, aq_ref, SWA_HEADS)
    row += SWA_HEADS * d
    ak_ref[...] = k_rows(row, 1).astype(BF16)
    row += kv_w
    v_tiles(row, av_ref)
    row += kv_w
    q_heads(row, 2, cq_ref, NSA_HEADS)
    row += NSA_HEADS * d
    ckc_ref[...] = acc[row:row + kv_w].T
    row += kv_w
    cvc_ref[...] = acc[row:row + kv_w].T
    row += kv_w
    cks_ref[...] = k_rows(row, 3).astype(BF16)
    row += kv_w
    for g in range(KV_HEADS):
        cvs_ref[g, 0, 0:d, :] = acc[row + g * d:row + (g + 1) * d].astype(BF16)
        cvs_ref[g, 0, d:SEL_V_ROWS, :] = jnp.ones((SEL_V_ROWS - d, tm), BF16)
    row += kv_w
    ckw_ref[...] = k_rows(row, 4).astype(BF16)
    row += kv_w
    v_tiles(row, cvw_ref)
    row += kv_w
    for g in range(KV_HEADS):
        gt_ref[g] = _sigmoid(acc[row + g * N_GATE_ROWS:row + (g + 1) * N_GATE_ROWS])


N_HEAD_OUTS = 11


def _mixer_proj_kernel(lg_ref, x_ref, g_ref, wt_ref, hg_ref, wb_ref, cw_ref, wd_ref, cos_ref,
                       sin_ref, ng_ref, *rest):
    head_outs = rest[:N_HEAD_OUTS]
    yb_ref, yd_ref, tail_ref, r_ref = rest[N_HEAD_OUTS:]
    ti = pl.program_id(1)
    xn = _rms_rows(x_ref[...], g_ref[...]).astype(BF16)
    _heads_body(xn, wt_ref, hg_ref, *head_outs)
    _conv_body(xn, ti, wb_ref, cw_ref, yb_ref, tail_ref)
    _retention_body(xn, ti, lg_ref, wd_ref, cos_ref, sin_ref, ng_ref, yd_ref, r_ref)


def _mixer_proj(x2, b, t, g, w_t, head_gains, w_b, conv_w, w_d, norm_gain):
    n, dm = x2.shape
    d = HEAD_DIM
    tm = PROJ_TM
    assert tm == SEL_KEY_TILE and t % tm == 0
    tpb = t // tm
    lt = tm // 128
    kv = KV_HEADS
    w = MIX_W
    h, dk, dv = RET_HEADS, RET_QK_DIM, RET_V_DIM
    half = dk // 2
    inv = ROPE_BASE ** (-jnp.arange(half, dtype=F32) / half)
    ang = jnp.arange(t).astype(F32)[:, None] * inv[None, :]
    cos = jnp.cos(ang)
    sin = jnp.sin(ang)
    cosf = jnp.concatenate([cos, cos], axis=-1)
    sinf = jnp.concatenate([-sin, sin], axis=-1)
    log_gamma = jnp.log(1.0 - 2.0 ** (-5.0 - jnp.arange(h, dtype=F32)))
    conv_rows = jnp.pad(conv_w.reshape(conv_w.shape[0], w).astype(F32),
                        ((0, 8 - conv_w.shape[0]), (0, 0)))
    resident = lambda a: pl.BlockSpec(a.shape, lambda bi, ti: tuple(0 for _ in a.shape),
                                      pipeline_mode=pl.Buffered(1))
    qt_spec = lambda nh: pl.BlockSpec((None, nh, d, tm), lambda bi, ti: (bi, 0, 0, ti))
    row_spec = pl.BlockSpec((None, tm, kv * d), lambda bi, ti: (bi, ti, 0))
    vt_spec = pl.BlockSpec((None, kv, lt, SEL_V_ROWS, 128), lambda bi, ti: (bi, 0, ti, 0, 0))
    y_spec = pl.BlockSpec((None, tm, w), lambda bi, ti: (bi, ti, 0))
    sds = jax.ShapeDtypeStruct
    w_b16, w_d16 = w_b.astype(BF16), w_d.astype(BF16)
    return pl.pallas_call(
        _mixer_proj_kernel,
        grid=(b, tpb),
        in_specs=[
            pl.BlockSpec(memory_space=pltpu.SMEM),
            pl.BlockSpec((tm, dm), lambda bi, ti: (bi * tpb + ti, 0)),
            pl.BlockSpec((1, dm), lambda bi, ti: (0, 0)),
            resident(w_t), pl.BlockSpec(head_gains.shape, lambda bi, ti: (0, 0, 0)),
            resident(w_b16), pl.BlockSpec((8, w), lambda bi, ti: (0, 0)),
            resident(w_d16),
            pl.BlockSpec((tm, dk), lambda bi, ti: (ti, 0)),
            pl.BlockSpec((tm, dk), lambda bi, ti: (ti, 0)),
            pl.BlockSpec((1, h * dv), lambda bi, ti: (0, 0)),
        ],
        out_specs=[
            qt_spec(SWA_HEADS), row_spec, vt_spec,
            qt_spec(NSA_HEADS), row_spec, row_spec, row_spec,
            pl.BlockSpec((None, kv, 1, SEL_V_ROWS, tm), lambda bi, ti: (bi, 0, ti, 0, 0)),
            row_spec, vt_spec,
            pl.BlockSpec((None, kv, N_GATE_ROWS, tm), lambda bi, ti: (bi, 0, 0, ti)),
            y_spec, y_spec,
        ],
        out_shape=[
            sds((b, SWA_HEADS, d, t), BF16), sds((b, t, kv * d), BF16),
            sds((b, kv, t // 128, SEL_V_ROWS, 128), BF16),
            sds((b, NSA_HEADS, d, t), BF16), sds((b, t, kv * d), F32), sds((b, t, kv * d), F32),
            sds((b, t, kv * d), BF16), sds((b, kv, tpb, SEL_V_ROWS, tm), BF16),
            sds((b, t, kv * d), BF16), sds((b, kv, t // 128, SEL_V_ROWS, 128), BF16),
            sds((b, kv, N_GATE_ROWS, t), F32),
            sds((b, t, w), F32), sds((b, t, h * dv), F32),
        ],
        scratch_shapes=[pltpu.VMEM((8, w), F32), pltpu.VMEM((h, dk, dv), F32)],
        compiler_params=_params(("parallel", "arbitrary"), VMEM_LIMIT),
        name="mixer_proj",
    )(log_gamma, x2, g.reshape(1, dm), w_t, head_gains, w_b16, conv_rows, w_d16, cosf, sinf,
      norm_gain.reshape(1, h * dv))


def _group_q(qt_ref, qb=0):
    return jnp.concatenate(
        [qt_ref[r, :, qb * Q_BLOCK:(qb + 1) * Q_BLOCK] for r in range(GROUP_R)], axis=1)


def _padded_q(qt_ref, g, qb=0):
    q4 = _group_q(qt_ref, qb)
    z = jnp.zeros_like(q4)
    return jnp.where(g == 0, jnp.concatenate([q4, z], axis=0), jnp.concatenate([z, q4], axis=0))


def _gate_untranspose(o_t, gate_ref, branch, qb=0):
    outs = []
    for r in range(GROUP_R):
        blk = o_t[:, r * Q_BLOCK:(r + 1) * Q_BLOCK]
        if gate_ref is not None:
            c = r * 3 + branch
            blk = blk * gate_ref[c:c + 1, qb * Q_BLOCK:(qb + 1) * Q_BLOCK]
        outs.append(blk.T)
    return jnp.concatenate(outs, axis=1)


def _banded_kernel(*refs, window, has_sink, gate_branch):
    refs = list(refs)
    sink_ref = refs.pop(0) if has_sink else None
    qt_ref, k_ref, vt_ref = refs[:3]
    gate_ref = refs[3] if gate_branch is not None else None
    o_ref = refs[-1]
    g = pl.program_id(1)
    rq = GROUP_R * Q_BLOCK
    n_sub = window // Q_BLOCK + 1
    span = n_sub * Q_BLOCK
    if has_sink:
        lrow = lax.broadcasted_iota(jnp.int32, (1, rq), 1)
        sink = jnp.zeros((1, rq), F32)
        for r in range(GROUP_R):
            sink = jnp.where((lrow >= r * Q_BLOCK) & (lrow < (r + 1) * Q_BLOCK),
                             sink_ref[g * GROUP_R + r], sink)
        sink = sink * LOG2_E
    step = pl.program_id(2)
    row = lax.broadcasted_iota(jnp.int32, (span, Q_BLOCK), 0)
    qcol = lax.broadcasted_iota(jnp.int32, (span, Q_BLOCK), 1)

    def band_bias(offset):
        diff = offset + qcol - row
        return jnp.where((diff >= 0) & (diff < window), 0.0, NEG_INF)

    for qb in range(BANDED_QB):
        i = step * BANDED_QB + qb
        start = pl.multiple_of(jnp.maximum(i * Q_BLOCK - window, 0), Q_BLOCK)
        st = _dot(k_ref[pl.ds(start, span), :], _padded_q(qt_ref, g, qb))
        st = st + jnp.concatenate([band_bias(i * Q_BLOCK - start)] * GROUP_R, axis=1)
        m = jnp.max(st, axis=0, keepdims=True)
        if has_sink:
            m = jnp.maximum(m, sink)
        m = jnp.where(m == NEG_INF, 0.0, m)
        pb = jnp.exp2(st - m).astype(BF16)
        u0 = start // Q_BLOCK
        o_t = _dot(vt_ref[u0], pb[0:Q_BLOCK])
        for u in range(1, n_sub):
            o_t = o_t + _dot(vt_ref[u0 + u], pb[u * Q_BLOCK:(u + 1) * Q_BLOCK])
        denom = o_t[HEAD_DIM:HEAD_DIM + 1, :]
        if has_sink:
            denom = denom + jnp.exp2(sink - m)
        o_t = o_t[0:HEAD_DIM, :] / jnp.maximum(denom, 1e-30)
        o_ref[qb * Q_BLOCK:(qb + 1) * Q_BLOCK, :] = _gate_untranspose(
            o_t, gate_ref, gate_branch, qb)


def _banded(qt, k_rows, vt, window, sinks=None, gates=None, gate_branch=None):
    b, h, d, t = qt.shape
    g = h // GROUP_R
    in_specs = []
    args = []
    if sinks is not None:
        in_specs.append(pl.BlockSpec(memory_space=pltpu.SMEM))
        args.append(sinks.astype(F32))
    qw = BANDED_QB * Q_BLOCK
    in_specs += [
        pl.BlockSpec((None, GROUP_R, d, qw), lambda bi, gi, i: (bi, gi, 0, i)),
        pl.BlockSpec((None, t, KV_HEADS * d), lambda bi, gi, i: (bi, 0, 0)),
        pl.BlockSpec((None, None, t // Q_BLOCK, SEL_V_ROWS, Q_BLOCK),
                     lambda bi, gi, i: (bi, gi, 0, 0, 0)),
    ]
    args += [qt, k_rows, vt]
    if gates is not None:
        in_specs.append(pl.BlockSpec((None, None, N_GATE_ROWS, qw),
                                     lambda bi, gi, i: (bi, gi, 0, i)))
        args.append(gates)
    return pl.pallas_call(
        functools.partial(_banded_kernel, window=window, has_sink=sinks is not None,
                          gate_branch=gate_branch if gates is not None else None),
        grid=(b, g, t // qw),
        in_specs=in_specs,
        out_specs=pl.BlockSpec((None, qw, GROUP_R * d), lambda bi, gi, i: (bi, i, gi)),
        out_shape=jax.ShapeDtypeStruct((b, t, h * d), F32),
        compiler_params=_params(("parallel", "parallel", "arbitrary"), VMEM_LIMIT),
        name="banded_attn_w%d" % window,
    )(*args)


def _conv_body(xn, ti, w_ref, cw_ref, o_ref, tail_ref):
    w = MIX_W
    acc = _dot(xn, w_ref[...])
    z = acc[:, 2 * w:3 * w] * acc[:, 0:w]
    zp = jnp.where(ti > 0, tail_ref[...], 0.0)
    row = lax.broadcasted_iota(jnp.int32, z.shape, 0)
    z1 = jnp.where(row == 0, zp[7:8, :], pltpu.roll(z, 1, 0))
    z2 = pltpu.roll(z, 2, 0)
    z2 = jnp.where(row == 0, zp[6:7, :], jnp.where(row == 1, zp[7:8, :], z2))
    cw = cw_ref[...]
    o_ref[...] = acc[:, w:2 * w] * (cw[0:1, :] * z2 + cw[1:2, :] * z1 + cw[2:3, :] * z)
    tail_ref[...] = z[z.shape[0] - 8:, :]


def _gelu_tanh(x):
    return x * (0.5 * (1.0 + jnp.tanh(0.7978845608028654 * (x + 0.044715 * (x * x * x)))))


def _compress_kernel(tk_ref, tv_ref, pek_ref, pev_ref, wk1_ref, wk2_ref, wv1_ref, wv2_ref,
                     kg_ref, ko_ref, vo_ref):
    nrow = tk_ref.shape[0]

    def mlp(a, pe_ref, w1_ref, w2_ref):
        a0 = (a + pe_ref[0:1, :]).astype(BF16)
        a1 = (a + pe_ref[1:2, :]).astype(BF16)
        p1 = _dot(a0, w1_ref[0])
        p2 = _dot(a1, w1_ref[1])
        hdn = p1 + pltpu.roll(p2, nrow - 1, 0)
        return _dot(_gelu_tanh(hdn).astype(BF16), w2_ref[...])

    kc = mlp(tk_ref[...], pek_ref, wk1_ref, wk2_ref)
    ko_ref[...] = _rms_rows(kc, kg_ref[...]).astype(BF16)
    vo_ref[...] = mlp(tv_ref[...], pev_ref, wv1_ref, wv2_ref).astype(BF16)


def _compress(kc_rows, vc_rows, pos_k, pos_v, wk1, wk2, wv1, wv2, k_gain):
    b, t, kvd = kc_rows.shape
    kv = KV_HEADS
    d = kvd // kv
    nrow = t // CMP_STRIDE
    wide = CMP_STRIDE * kvd
    hid = wk1.shape[1]

    def expand_w1(w1):
        w = w1.reshape(2, CMP_STRIDE, 1, d, hid)
        per_head = []
        for g in range(kv):
            pads = [jnp.zeros_like(w)] * kv
            pads[g] = w
            per_head.append(jnp.concatenate(pads, axis=2).reshape(2, wide, hid))
        return jnp.stack(per_head).astype(BF16)

    def expand_pe(pe):
        return jnp.broadcast_to(pe.reshape(2, CMP_STRIDE, 1, d), (2, CMP_STRIDE, kv, d)).reshape(2, wide)

    tok = pl.BlockSpec((None, nrow, wide), lambda bi, gi: (bi, 0, 0))
    full = lambda shape: pl.BlockSpec(shape, lambda bi, gi: tuple(0 for _ in shape))
    w1_spec = pl.BlockSpec((None, 2, wide, hid), lambda bi, gi: (gi, 0, 0, 0))
    out = pl.BlockSpec((None, nrow, d), lambda bi, gi: (bi * kv + gi, 0, 0))
    return pl.pallas_call(
        _compress_kernel,
        grid=(b, kv),
        in_specs=[tok, tok, full((2, wide)), full((2, wide)), w1_spec, full((hid, d)),
                  w1_spec, full((hid, d)), full((1, d))],
        out_specs=[out, out],
        out_shape=[jax.ShapeDtypeStruct((b * kv, nrow, d), BF16),
                   jax.ShapeDtypeStruct((b * kv, nrow, d), BF16)],
        compiler_params=_params(("parallel", "arbitrary"), VMEM_LIMIT),
        name="nsa_compress",
    )(kc_rows.reshape(b, nrow, wide), vc_rows.reshape(b, nrow, wide), expand_pe(pos_k),
      expand_pe(pos_v), expand_w1(wk1), wk2.astype(BF16), expand_w1(wv1), wv2.astype(BF16),
      k_gain.reshape(1, d))


def _cmp_attn_kernel(qt_ref, kc_ref, vct_ref, gate_ref, o_ref, selt_ref, imp_ref, *, sel_k):
    step = pl.program_id(2)
    rq = GROUP_R * Q_BLOCK
    nc = kc_ref.shape[0]
    ns = selt_ref.shape[0]
    blk = lax.broadcasted_iota(jnp.int32, (ns, Q_BLOCK), 0)

    def forced_blocks(i):
        cur = (i * Q_BLOCK + lax.broadcasted_iota(jnp.int32, (ns, Q_BLOCK), 1)) // SEL_BLOCK
        return (blk == 0) | (blk == cur) | (blk == cur - 1), cur

    def attend(nc_eff):
        n = lax.broadcasted_iota(jnp.int32, (nc_eff, rq), 0)
        lane = lax.broadcasted_iota(jnp.int32, (nc_eff, rq), 1)
        end_minus_q = n * CMP_STRIDE + (CMP_BLOCK - 1) - (lane & (Q_BLOCK - 1))
        ss = lax.broadcasted_iota(jnp.int32, (ns, nc_eff), 0) * SEL_BLOCK
        nn = lax.broadcasted_iota(jnp.int32, (ns, nc_eff), 1) * CMP_STRIDE
        overlap_t = jnp.where((nn < ss + SEL_BLOCK) & (nn + (CMP_BLOCK - 1) >= ss),
                              1.0, 0.0).astype(BF16)
        lhs = jnp.concatenate([vct_ref[:, 0:nc_eff], overlap_t], axis=0)
        for qb in range(CMP_QB):
            i = step * CMP_QB + qb
            st = _dot(kc_ref[0:nc_eff, :], _group_q(qt_ref, qb))
            st = jnp.where(end_minus_q <= i * Q_BLOCK, st, NEG_INF)
            m = jnp.max(st, axis=0, keepdims=True)
            m = jnp.where(m == NEG_INF, 0.0, m)
            p = jnp.exp2(st - m)
            denom = jnp.sum(p, axis=0, keepdims=True)
            pb = (p / jnp.maximum(denom, 1e-30)).astype(BF16)
            res = _dot(lhs, pb)
            o_ref[qb * Q_BLOCK:(qb + 1) * Q_BLOCK, :] = _gate_untranspose(
                res[0:HEAD_DIM], gate_ref, 0, qb)
            imp = res[HEAD_DIM:, 0:Q_BLOCK]
            for r in range(1, GROUP_R):
                imp = imp + res[HEAD_DIM:, r * Q_BLOCK:(r + 1) * Q_BLOCK]
            forced, cur = forced_blocks(i)
            imp_ref[qb] = jnp.where(forced | (blk > cur), NEG_INF, imp)

    variant_rows = 128
    n_var = nc // variant_rows
    n_ending = (step + 1) * (CMP_QB * Q_BLOCK // CMP_STRIDE) - 1
    variant = jnp.minimum((n_ending + variant_rows - 1) // variant_rows, n_var) - 1
    for v in range(n_var):
        pl.when(variant == v)(functools.partial(attend, (v + 1) * variant_rows))

    blkf = blk.astype(F32)
    imps = [imp_ref[qb] for qb in range(CMP_QB)]
    sels = [jnp.where(forced_blocks(step * CMP_QB + qb)[0], 1.0, 0.0) for qb in range(CMP_QB)]
    for _ in range(sel_k - N_FORCED):
        for qb in range(CMP_QB):
            mx = jnp.max(imps[qb], axis=0, keepdims=True)
            first = jnp.min(jnp.where(imps[qb] == mx, blkf, float(ns)), axis=0, keepdims=True)
            hit = blkf == first
            sels[qb] = jnp.where(hit & (mx > NEG_INF), 1.0, sels[qb])
            imps[qb] = jnp.where(hit, NEG_INF, imps[qb])
    for qb in range(CMP_QB):
        selt_ref[:, qb * Q_BLOCK:(qb + 1) * Q_BLOCK] = sels[qb]


def _cmp_attn(qt, k_cmp, v_cmp_t, gates):
    b, h, d, t = qt.shape
    g = h // GROUP_R
    nc = k_cmp.shape[1]
    ns = t // SEL_BLOCK
    qw = CMP_QB * Q_BLOCK
    assert nc % 128 == 0
    return pl.pallas_call(
        functools.partial(_cmp_attn_kernel, sel_k=min(SEL_TOPK, ns)),
        grid=(b, g, t // qw),
        in_specs=[
            pl.BlockSpec((None, GROUP_R, d, qw), lambda bi, gi, i: (bi, gi, 0, i)),
            pl.BlockSpec((None, nc, d), lambda bi, gi, i: (bi * KV_HEADS + gi, 0, 0)),
            pl.BlockSpec((None, d, nc), lambda bi, gi, i: (bi * KV_HEADS + gi, 0, 0)),
            pl.BlockSpec((None, None, N_GATE_ROWS, qw), lambda bi, gi, i: (bi, gi, 0, i)),
        ],
        out_specs=[
            pl.BlockSpec((None, qw, GROUP_R * d), lambda bi, gi, i: (bi, i, gi)),
            pl.BlockSpec((None, None, ns, qw), lambda bi, gi, i: (bi, gi, 0, i)),
        ],
        out_shape=[jax.ShapeDtypeStruct((b, t, h * d), F32),
                   jax.ShapeDtypeStruct((b, g, ns, t), F32)],
        scratch_shapes=[pltpu.VMEM((CMP_QB, ns, Q_BLOCK), F32)],
        compiler_params=_params(("parallel", "parallel", "arbitrary"), VMEM_LIMIT),
        name="nsa_cmp_topk",
    )(qt, k_cmp, v_cmp_t, gates)


def _sel_attn_kernel(qt_ref, ks_ref, vst_ref, selt_ref, gate_ref, o_ref, sa_ref, sb_ref, mta_ref,
                     mtb_ref, m_ref, acc_ref):
    g = pl.program_id(1)
    tk = SEL_KEY_TILE
    bpt = tk // SEL_BLOCK
    spt = tk // Q_BLOCK
    d = HEAD_DIM
    m_ref[...] = jnp.full(m_ref.shape, NEG_INF, F32)
    acc_ref[...] = jnp.zeros_like(acc_ref)
    last_q = (pl.program_id(2) + 1) * SEL_QB * Q_BLOCK
    n_tiles = (last_q + tk - 1) // tk
    qpads = [_padded_q(qt_ref, g, qb) for qb in range(SEL_QB)]
    tri = jnp.where(lax.broadcasted_iota(jnp.int32, (Q_BLOCK, Q_BLOCK), 0)
                    <= lax.broadcasted_iota(jnp.int32, (Q_BLOCK, Q_BLOCK), 1), 0.0, NEG_INF)

    def scores(j, qb, s_ref, mt_ref):
        i = pl.program_id(2) * SEL_QB + qb
        k0 = pl.multiple_of(j * tk, tk)
        st = _dot(ks_ref[pl.ds(k0, tk), :], qpads[qb])
        sel_rows = selt_ref[pl.ds(pl.multiple_of(j * bpt, bpt), bpt),
                            qb * Q_BLOCK:(qb + 1) * Q_BLOCK]
        bias_rows = jnp.where(sel_rows > 0.5, 0.0, NEG_INF)
        parts = []
        for u in range(spt):
            sub = jnp.concatenate(
                [jnp.broadcast_to(bias_rows[s:s + 1, :], (SEL_BLOCK, Q_BLOCK))
                 for s in range(u * Q_BLOCK // SEL_BLOCK, (u + 1) * Q_BLOCK // SEL_BLOCK)], axis=0)
            parts.append(sub + jnp.where(j * spt + u == i, tri, 0.0))
        bias = jnp.concatenate(parts, axis=0)
        st = st + jnp.concatenate([bias] * GROUP_R, axis=1)
        s_ref[qb] = st
        mt_ref[qb] = jnp.max(st, axis=0, keepdims=True)

    def consume(j, qb, s_ref, mt_ref):
        m_prev = m_ref[qb]
        m_new = jnp.maximum(m_prev, mt_ref[qb])
        m_safe = jnp.where(m_new == NEG_INF, 0.0, m_new)
        alpha = jnp.exp2(m_prev - m_safe)
        p = jnp.exp2((s_ref[qb] - m_safe).astype(BF16))
        acc_ref[qb] = alpha * acc_ref[qb] + _dot(vst_ref[j], p)
        m_ref[qb] = m_new

    for qb in range(SEL_QB):
        scores(0, qb, sa_ref, mta_ref)
    n_pairs = n_tiles // 2

    def step(jj, carry):
        j = 2 * jj
        for qb in range(SEL_QB):
            scores(j + 1, qb, sb_ref, mtb_ref)
            consume(j, qb, sa_ref, mta_ref)
        for qb in range(SEL_QB):
            scores(jnp.minimum(j + 2, n_tiles - 1), qb, sa_ref, mta_ref)
            consume(j + 1, qb, sb_ref, mtb_ref)
        return carry

    lax.fori_loop(0, n_pairs, step, 0)

    @pl.when(n_tiles % 2 == 1)
    def _():
        for qb in range(SEL_QB):
            consume(n_tiles - 1, qb, sa_ref, mta_ref)

    for qb in range(SEL_QB):
        o_t = acc_ref[qb, 0:d, :] / jnp.maximum(acc_ref[qb, d:d + 1, :], 1e-30)
        o_ref[qb * Q_BLOCK:(qb + 1) * Q_BLOCK, :] = _gate_untranspose(o_t, gate_ref, 1, qb)


def _sel_attn(qt, ks_rows, vs_t, sel_t, gates):
    b, h, d, t = qt.shape
    g = h // GROUP_R
    ns = sel_t.shape[2]
    tk = SEL_KEY_TILE
    rq = GROUP_R * Q_BLOCK
    nq = SEL_QB
    qw = nq * Q_BLOCK
    assert tk % qw == 0
    return pl.pallas_call(
        _sel_attn_kernel,
        grid=(b, g, t // qw),
        in_specs=[
            pl.BlockSpec((None, GROUP_R, d, qw), lambda bi, gi, i: (bi, gi, 0, i)),
            pl.BlockSpec((None, t, KV_HEADS * d), lambda bi, gi, i: (bi, 0, 0)),
            pl.BlockSpec((None, None, t // tk, SEL_V_ROWS, tk), lambda bi, gi, i: (bi, gi, 0, 0, 0)),
            pl.BlockSpec((None, None, ns, qw), lambda bi, gi, i: (bi, gi, 0, i)),
            pl.BlockSpec((None, None, N_GATE_ROWS, qw), lambda bi, gi, i: (bi, gi, 0, i)),
        ],
        out_specs=pl.BlockSpec((None, qw, GROUP_R * d), lambda bi, gi, i: (bi, i, gi)),
        out_shape=jax.ShapeDtypeStruct((b, t, h * d), F32),
        scratch_shapes=[pltpu.VMEM((nq, tk, rq), F32), pltpu.VMEM((nq, tk, rq), F32),
                        pltpu.VMEM((nq, 1, rq), F32), pltpu.VMEM((nq, 1, rq), F32),
                        pltpu.VMEM((nq, 1, rq), F32), pltpu.VMEM((nq, SEL_V_ROWS, rq), F32)],
        compiler_params=_params(("parallel", "parallel", "arbitrary"), VMEM_LIMIT),
        name="nsa_selected",
    )(qt, ks_rows, vs_t, sel_t, gates)


def _retention_body(xn, ti, lg_ref, w_ref, cos_ref, sin_ref, ng_ref, o_ref, r_ref):
    c = RET_CHUNK
    h, dk, dv = RET_HEADS, RET_QK_DIM, RET_V_DIM
    half = dk // 2
    k0, v0, g0 = h * dk, 2 * h * dk, 2 * h * dk + h * dv

    @pl.when(ti == 0)
    def _():
        r_ref[...] = jnp.zeros_like(r_ref)

    acc = _dot(xn, w_ref[...])
    ii = lax.broadcasted_iota(jnp.int32, (c, c), 0)
    jj = lax.broadcasted_iota(jnp.int32, (c, c), 1)
    dist = (ii - jj).astype(F32)
    jcol = lax.broadcasted_iota(jnp.int32, (c, 1), 0).astype(F32)
    for hh in range(h):
        lg = lg_ref[hh]
        dmask = jnp.where(dist >= 0, jnp.exp(dist * lg), 0.0)
        xi = jnp.exp((jcol + 1.0) * lg)
        zeta = jnp.exp((c - 1.0 - jcol) * lg)
        decay = jnp.exp(jnp.zeros((1, dv), F32) + c * lg)
        for cc in range(acc.shape[0] // c):
            rows = slice(cc * c, (cc + 1) * c)
            cosf = cos_ref[rows, :]
            sinf = sin_ref[rows, :]

            def rot(x):
                return x * cosf + jnp.concatenate([x[:, half:], x[:, :half]], axis=1) * sinf

            q = rot(acc[rows, hh * dk:(hh + 1) * dk])
            k = rot(acc[rows, k0 + hh * dk:k0 + (hh + 1) * dk]) * (dk ** -0.5)
            vb = acc[rows, v0 + hh * dv:v0 + (hh + 1) * dv].astype(BF16)
            att = _dot_nt(q.astype(BF16), k.astype(BF16)) * dmask
            o = _dot(att.astype(BF16), vb)
            r_prev = r_ref[hh]
            o = o + _dot((q * xi).astype(BF16), r_prev.astype(BF16))
            s_chunk = _dot_tn((k * zeta).astype(BF16), vb)
            r_ref[hh] = r_prev * decay + s_chunk
            mu = jnp.mean(o, axis=-1, keepdims=True)
            var = jnp.mean(jnp.square(o - mu), axis=-1, keepdims=True)
            on = (o - mu) * lax.rsqrt(var + EPS)
            gt = acc[rows, g0 + hh * dv:g0 + (hh + 1) * dv]
            o_ref[rows, hh * dv:(hh + 1) * dv] = (
                (gt * _sigmoid(gt)) * (on * ng_ref[:, hh * dv:(hh + 1) * dv]))


def _merge_kernel(x_ref, g_ref, wgate_ref, bias_ref, ya_ref, yb_ref, yc0_ref, yc1_ref, yc2_ref,
                  yd_ref, wb_ref, wo_ref, o_ref):
    x = x_ref[...]
    d = x.shape[1]
    u = _rms_rows(x, g_ref[...]).astype(BF16)
    ys = (ya_ref[...], yb_ref[...], yc0_ref[...] + yc1_ref[...] + yc2_ref[...], yd_ref[...])
    merged = jnp.zeros(x.shape, F32)
    for n in range(N_BRANCH):
        logits = _dot(u, wgate_ref[:, n * d:(n + 1) * d]) + bias_ref[:, n * d:(n + 1) * d]
        merged = merged + _sigmoid(logits) * _dot(ys[n].astype(BF16), wb_ref[n])
    o_ref[...] = x + _dot(merged.astype(BF16), wo_ref[...])


def _merge(x2, g, w_gate, bias, ys, w_branch, w_out, tm=512):
    n, d = x2.shape
    w = MIX_W
    row = lambda width: pl.BlockSpec((tm, width), lambda i: (i, 0))
    full = lambda shape: pl.BlockSpec(shape, lambda i: tuple(0 for _ in shape),
                                      pipeline_mode=pl.Buffered(1))
    return pl.pallas_call(
        _merge_kernel,
        grid=(n // tm,),
        in_specs=[row(d), full((1, d)), full((d, N_BRANCH * d)), full((1, N_BRANCH * d))]
        + [row(w)] * 6 + [full((N_BRANCH, w, d)), full((d, d))],
        out_specs=row(d),
        out_shape=jax.ShapeDtypeStruct((n, d), F32),
        compiler_params=_params(("parallel",), VMEM_LIMIT),
        name="merge_out",
    )(x2, g.reshape(1, d), w_gate.astype(BF16), bias.reshape(1, N_BRANCH * d),
      *[y.reshape(n, w) for y in ys], w_branch.astype(BF16), w_out.astype(BF16))


def _mixers(x2, b, t, mix_norm, w_in, merge_gate_bias, swa_q_gain, swa_k_gain, swa_sinks, conv_w,
            nsa_q_gain, nsa_k_gain, cmp_pos_k, cmp_pos_v, cmp_wk1, cmp_wk2, cmp_wv1, cmp_wv2,
            ret_norm_gain, w_branch, w_out):
    hd = HEAD_DIM
    swa_q, swa_kv = SWA_HEADS * hd, SWA_KV_HEADS * hd
    nsa_q, nsa_kv = NSA_HEADS * hd, NSA_KV_HEADS * hd
    ret_qk, ret_v = RET_HEADS * RET_QK_DIM, RET_HEADS * RET_V_DIM
    n_gate = NSA_HEADS * 3
    o_a = 0
    o_b = o_a + swa_q + 2 * swa_kv
    o_c = o_b + 3 * MIX_W
    o_cg = o_c + nsa_q + 6 * nsa_kv
    o_d = o_cg + n_gate
    o_g = o_d + 2 * ret_qk + 2 * ret_v

    per_g = GROUP_R * 3
    w_gates = [jnp.pad(w_in[:, o_cg + gi * per_g:o_cg + (gi + 1) * per_g],
                       ((0, 0), (0, N_GATE_ROWS - per_g))) for gi in range(NSA_KV_HEADS)]
    w_t = jnp.concatenate([w_in[:, o_a:o_b], w_in[:, o_c:o_cg]] + w_gates, axis=1).T.astype(BF16)
    head_gains = jnp.stack([swa_q_gain, swa_k_gain, nsa_q_gain, nsa_k_gain[1], nsa_k_gain[2]])
    head_gains = jnp.broadcast_to(head_gains.astype(F32)[:, :, None], (5, hd, 128))
    (a_qt, a_k, a_vt, c_qt, c_kc, c_vc, c_ks, c_vst, c_kw, c_vwt, gates_c, y_b, y_d) = _mixer_proj(
        x2, b, t, mix_norm, w_t, head_gains, w_in[:, o_b:o_c], conv_w, w_in[:, o_d:o_g],
        ret_norm_gain)

    y_a = _banded(a_qt, a_k, a_vt, SWA_WINDOW, sinks=swa_sinks)

    k_cmp, v_cmp = _compress(c_kc, c_vc, cmp_pos_k, cmp_pos_v, cmp_wk1, cmp_wk2, cmp_wv1, cmp_wv2,
                             nsa_k_gain[0])
    y_cmp, sel_t = _cmp_attn(c_qt, k_cmp, v_cmp.transpose(0, 2, 1), gates_c)
    y_sel = _sel_attn(c_qt, c_ks, c_vst, sel_t, gates_c)
    y_win = _banded(c_qt, c_kw, c_vwt, NSA_WINDOW, gates=gates_c, gate_branch=2)

    return _merge(x2, mix_norm, w_in[:, o_g:], merge_gate_bias,
                  (y_a, y_b, y_cmp, y_sel, y_win, y_d), w_branch, w_out)


def kernel(x, ffn1_norm, ffn1_w_gate, ffn1_w_up, ffn1_w_down, mix_norm, w_in, merge_gate_bias, swa_q_gain, swa_k_gain, swa_sinks, conv_w, nsa_q_gain, nsa_k_gain, cmp_pos_k, cmp_pos_v, cmp_wk1, cmp_wk2, cmp_wv1, cmp_wv2, ret_norm_gain, w_branch, w_out, ffn2_norm, ffn2_w_gate, ffn2_w_up, ffn2_w_down):
    b, t, d = x.shape
    x2 = x.reshape(b * t, d)
    for l in range(ffn1_norm.shape[0]):
        x2 = _ffn(x2, ffn1_norm[l], ffn1_w_gate[l], ffn1_w_up[l], ffn1_w_down[l])
        x2 = _mixers(x2, b, t, mix_norm[l], w_in[l], merge_gate_bias[l], swa_q_gain[l],
                     swa_k_gain[l], swa_sinks[l], conv_w[l], nsa_q_gain[l], nsa_k_gain[l],
                     cmp_pos_k[l], cmp_pos_v[l], cmp_wk1[l], cmp_wk2[l], cmp_wv1[l], cmp_wv2[l],
                     ret_norm_gain[l], w_branch[l], w_out[l])
        x2 = _ffn(x2, ffn2_norm[l], ffn2_w_gate[l], ffn2_w_up[l], ffn2_w_down[l])
    return x2.reshape(b, t, d)
```

```python
import functools

import jax
import jax.numpy as jnp
from jax import lax
from jax.experimental import pallas as pl
from jax.experimental.pallas import tpu as pltpu

F32 = jnp.float32
BF16 = jnp.bfloat16

HEAD_DIM = 64
Q_BLOCK = 128
MIX_W = 512
N_BRANCH = 4
SWA_HEADS = 8
SWA_KV_HEADS = 2
SWA_WINDOW = 128
NSA_HEADS = 8
NSA_KV_HEADS = 2
CMP_BLOCK = 32
CMP_STRIDE = 16
SEL_BLOCK = 64
SEL_TOPK = 16
N_FORCED = 3
NSA_WINDOW = 512
RET_HEADS = 4
RET_QK_DIM = 64
RET_V_DIM = 128
RET_CHUNK = 128
ROPE_BASE = 10000.0
EPS = 1e-6
KV_HEADS = SWA_KV_HEADS
GROUP_R = SWA_HEADS // SWA_KV_HEADS
assert (NSA_KV_HEADS, NSA_HEADS // NSA_KV_HEADS) == (KV_HEADS, GROUP_R)

LANES = 128
BF16_ROWS = 16
N_GATE_ROWS = 16
SEL_KEY_TILE = 512
SEL_V_ROWS = HEAD_DIM + BF16_ROWS
PROJ_TM = 512
BANDED_QB = 8
CMP_QB = 4
SEL_QB = 4
VMEM_LIMIT = 52 * 1024 * 1024

NEG_INF = float("-inf")
LOG2_E = 1.4426950408889634


def _params(sem, vmem=None):
    return pltpu.CompilerParams(dimension_semantics=sem, vmem_limit_bytes=vmem)


def _sigmoid(x):
    return 1.0 / (1.0 + jnp.exp(-x))


def _rms_rows(x, g):
    return x * lax.rsqrt(jnp.mean(x * x, axis=-1, keepdims=True) + EPS) * g


def _dot(a, b):
    return jnp.dot(a, b, preferred_element_type=F32)


def _dot_nt(a, b):
    return lax.dot_general(a, b, (((1,), (1,)), ((), ())), preferred_element_type=F32)


def _dot_tn(a, b):
    return lax.dot_general(a, b, (((0,), (0,)), ((), ())), preferred_element_type=F32)


def _ffn_kernel(x_ref, g_ref, wg_ref, wu_ref, wd_ref, o_ref, *, n_chunks):
    x = x_ref[...]
    xn = _rms_rows(x, g_ref[...]).astype(BF16)
    tf = wg_ref.shape[1] // n_chunks
    acc = None
    for f in range(n_chunks):
        a = _dot(xn, wg_ref[:, f * tf:(f + 1) * tf])
        b = _dot(xn, wu_ref[:, f * tf:(f + 1) * tf])
        h = ((a * _sigmoid(a)) * b).astype(BF16)
        part = _dot(h, wd_ref[f * tf:(f + 1) * tf, :])
        acc = part if acc is None else acc + part
    o_ref[...] = x + 0.5 * acc


def _ffn(x2, g, wg, wu, wd, tm=1024, n_chunks=11):
    n, d = x2.shape
    dff = wg.shape[1]
    assert dff % (n_chunks * LANES) == 0
    resident = lambda shape: pl.BlockSpec(shape, lambda i: (0, 0), pipeline_mode=pl.Buffered(1))
    return pl.pallas_call(
        functools.partial(_ffn_kernel, n_chunks=n_chunks),
        grid=(n // tm,),
        in_specs=[
            pl.BlockSpec((tm, d), lambda i: (i, 0)),
            pl.BlockSpec((1, d), lambda i: (0, 0)),
            resident((d, dff)), resident((d, dff)), resident((dff, d)),
        ],
        out_specs=pl.BlockSpec((tm, d), lambda i: (i, 0)),
        out_shape=jax.ShapeDtypeStruct((n, d), F32),
        compiler_params=_params(("parallel",), VMEM_LIMIT),
        name="ffn",
    )(x2, g.reshape(1, d), wg.astype(BF16), wu.astype(BF16), wd.astype(BF16))


def _heads_body(xn, wt_ref, hg_ref, aq_ref, ak_ref, av_ref, cq_ref, ckc_ref, cvc_ref, cks_ref,
                cvs_ref, ckw_ref, cvw_ref, gt_ref):
    d = HEAD_DIM
    acc = _dot_nt(wt_ref[...], xn)
    tm = xn.shape[0]
    lane_tiles = tm // LANES
    scale = d ** -0.5
    kv_w = KV_HEADS * d

    def head_norm(row0, gain_idx, mult):
        hb = acc[row0:row0 + d]
        gain = jnp.concatenate([hg_ref[gain_idx]] * lane_tiles, axis=1)
        y = hb * lax.rsqrt(jnp.mean(hb * hb, axis=0, keepdims=True) + EPS) * gain
        return y * mult if mult != 1.0 else y

    def q_heads(row0, gain_idx, out_ref, n_heads):
        for h in range(n_heads):
            out_ref[h] = head_norm(row0 + h * d, gain_idx, scale * LOG2_E).astype(BF16)

    def k_rows(row0, gain_idx):
        return jnp.concatenate([head_norm(row0 + g * d, gain_idx, 1.0) for g in range(KV_HEADS)],
                               axis=0).T

    def v_tiles(row0, out_ref):
        for g in range(KV_HEADS):
            for u in range(tm // Q_BLOCK):
                out_ref[g, u, 0:d, :] = acc[row0 + g * d:row0 + (g + 1) * d,
                                            u * Q_BLOCK:(u + 1) * Q_BLOCK].astype(BF16)
                out_ref[g, u, d:SEL_V_ROWS, :] = jnp.ones((SEL_V_ROWS - d, Q_BLOCK), BF16)

    row = 0
    q_heads(row, 0, aq_ref, SWA_HEADS)
    row += SWA_HEADS * d
    ak_ref[...] = k_rows(row, 1).astype(BF16)
    row += kv_w
    v_tiles(row, av_ref)
    row += kv_w
    q_heads(row, 2, cq_ref, NSA_HEADS)
    row += NSA_HEADS * d
    ckc_ref[...] = acc[row:row + kv_w].T
    row += kv_w
    cvc_ref[...] = acc[row:row + kv_w].T
    row += kv_w
    cks_ref[...] = k_rows(row, 3).astype(BF16)
    row += kv_w
    for g in range(KV_HEADS):
        cvs_ref[g, 0, 0:d, :] = acc[row + g * d:row + (g + 1) * d].astype(BF16)
        cvs_ref[g, 0, d:SEL_V_ROWS, :] = jnp.ones((SEL_V_ROWS - d, tm), BF16)
    row += kv_w
    ckw_ref[...] = k_rows(row, 4).astype(BF16)
    row += kv_w
    v_tiles(row, cvw_ref)
    row += kv_w
    for g in range(KV_HEADS):
        gt_ref[g] = _sigmoid(acc[row + g * N_GATE_ROWS:row + (g + 1) * N_GATE_ROWS])


N_HEAD_OUTS = 11


def _mixer_proj_kernel(lg_ref, x_ref, g_ref, wt_ref, hg_ref, wb_ref, cw_ref, wd_ref, cos_ref,
                       sin_ref, ng_ref, *rest):
    head_outs = rest[:N_HEAD_OUTS]
    yb_ref, yd_ref, tail_ref, r_ref = rest[N_HEAD_OUTS:]
    ti = pl.program_id(1)
    xn = _rms_rows(x_ref[...], g_ref[...]).astype(BF16)
    _heads_body(xn, wt_ref, hg_ref, *head_outs)
    _conv_body(xn, ti, wb_ref, cw_ref, yb_ref, tail_ref)
    _retention_body(xn, ti, lg_ref, wd_ref, cos_ref, sin_ref, ng_ref, yd_ref, r_ref)


def _mixer_proj(x2, b, t, g, w_t, head_gains, w_b, conv_w, w_d, norm_gain):
    n, dm = x2.shape
    d = HEAD_DIM
    tm = PROJ_TM
    assert tm == SEL_KEY_TILE and t % tm == 0
    tpb = t // tm
    q = Q_BLOCK
    kv = KV_HEADS
    w = MIX_W
    h, dk, dv = RET_HEADS, RET_QK_DIM, RET_V_DIM
    half = dk // 2
    inv = ROPE_BASE ** (-jnp.arange(half, dtype=F32) / half)
    ang = jnp.arange(t).astype(F32)[:, None] * inv[None, :]
    cos = jnp.cos(ang)
    sin = jnp.sin(ang)
    cosf = jnp.concatenate([cos, cos], axis=-1)
    sinf = jnp.concatenate([-sin, sin], axis=-1)
    log_gamma = jnp.log(1.0 - 2.0 ** (-5.0 - jnp.arange(h, dtype=F32)))
    conv_rows = jnp.pad(conv_w.reshape(conv_w.shape[0], w).astype(F32),
                        ((0, 8 - conv_w.shape[0]), (0, 0)))
    resident = lambda a: pl.BlockSpec(a.shape, lambda bi, ti: tuple(0 for _ in a.shape),
                                      pipeline_mode=pl.Buffered(1))
    qt_spec = lambda nh: pl.BlockSpec((None, nh, d, tm), lambda bi, ti: (bi, 0, 0, ti))
    row_spec = pl.BlockSpec((None, tm, kv * d), lambda bi, ti: (bi, ti, 0))
    vt_spec = pl.BlockSpec((None, kv, tm // q, SEL_V_ROWS, q), lambda bi, ti: (bi, 0, ti, 0, 0))
    y_spec = pl.BlockSpec((None, tm, w), lambda bi, ti: (bi, ti, 0))
    sds = jax.ShapeDtypeStruct
    w_b16, w_d16 = w_b.astype(BF16), w_d.astype(BF16)
    return pl.pallas_call(
        _mixer_proj_kernel,
        grid=(b, tpb),
        in_specs=[
            pl.BlockSpec(memory_space=pltpu.SMEM),
            pl.BlockSpec((tm, dm), lambda bi, ti: (bi * tpb + ti, 0)),
            pl.BlockSpec((1, dm), lambda bi, ti: (0, 0)),
            resident(w_t), pl.BlockSpec(head_gains.shape, lambda bi, ti: (0, 0, 0)),
            resident(w_b16), pl.BlockSpec((8, w), lambda bi, ti: (0, 0)),
            resident(w_d16),
            pl.BlockSpec((tm, dk), lambda bi, ti: (ti, 0)),
            pl.BlockSpec((tm, dk), lambda bi, ti: (ti, 0)),
            pl.BlockSpec((1, h * dv), lambda bi, ti: (0, 0)),
        ],
        out_specs=[
            qt_spec(SWA_HEADS), row_spec, vt_spec,
            qt_spec(NSA_HEADS), row_spec, row_spec, row_spec,
            pl.BlockSpec((None, kv, 1, SEL_V_ROWS, tm), lambda bi, ti: (bi, 0, ti, 0, 0)),
            row_spec, vt_spec,
            pl.BlockSpec((None, kv, N_GATE_ROWS, tm), lambda bi, ti: (bi, 0, 0, ti)),
            y_spec, y_spec,
        ],
        out_shape=[
            sds((b, SWA_HEADS, d, t), BF16), sds((b, t, kv * d), BF16),
            sds((b, kv, t // q, SEL_V_ROWS, q), BF16),
            sds((b, NSA_HEADS, d, t), BF16), sds((b, t, kv * d), F32), sds((b, t, kv * d), F32),
            sds((b, t, kv * d), BF16), sds((b, kv, tpb, SEL_V_ROWS, tm), BF16),
            sds((b, t, kv * d), BF16), sds((b, kv, t // q, SEL_V_ROWS, q), BF16),
            sds((b, kv, N_GATE_ROWS, t), F32),
            sds((b, t, w), F32), sds((b, t, h * dv), F32),
        ],
        scratch_shapes=[pltpu.VMEM((8, w), F32), pltpu.VMEM((h, dk, dv), F32)],
        compiler_params=_params(("parallel", "arbitrary"), VMEM_LIMIT),
        name="mixer_proj",
    )(log_gamma, x2, g.reshape(1, dm), w_t, head_gains, w_b16, conv_rows, w_d16, cosf, sinf,
      norm_gain.reshape(1, h * dv))


def _group_q(qt_ref, qb=0):
    return jnp.concatenate(
        [qt_ref[r, :, qb * Q_BLOCK:(qb + 1) * Q_BLOCK] for r in range(GROUP_R)], axis=1)


def _padded_q(qt_ref, g, qb=0):
    q4 = _group_q(qt_ref, qb)
    z = jnp.zeros_like(q4)
    return jnp.where(g == 0, jnp.concatenate([q4, z], axis=0), jnp.concatenate([z, q4], axis=0))


def _gate_untranspose(o_t, gate_ref, branch, qb=0):
    outs = []
    for r in range(GROUP_R):
        blk = o_t[:, r * Q_BLOCK:(r + 1) * Q_BLOCK]
        if gate_ref is not None:
            c = r * 3 + branch
            blk = blk * gate_ref[c:c + 1, qb * Q_BLOCK:(qb + 1) * Q_BLOCK]
        outs.append(blk.T)
    return jnp.concatenate(outs, axis=1)


def _banded_kernel(*refs, window, has_sink, gate_branch):
    refs = list(refs)
    sink_ref = refs.pop(0) if has_sink else None
    qt_ref, k_ref, vt_ref = refs[:3]
    gate_ref = refs[3] if gate_branch is not None else None
    o_ref = refs[-1]
    g = pl.program_id(1)
    rq = GROUP_R * Q_BLOCK
    n_sub = window // Q_BLOCK + 1
    span = n_sub * Q_BLOCK
    if has_sink:
        lrow = lax.broadcasted_iota(jnp.int32, (1, rq), 1)
        sink = jnp.zeros((1, rq), F32)
        for r in range(GROUP_R):
            sink = jnp.where((lrow >= r * Q_BLOCK) & (lrow < (r + 1) * Q_BLOCK),
                             sink_ref[g * GROUP_R + r], sink)
        sink = sink * LOG2_E
    step = pl.program_id(2)
    row = lax.broadcasted_iota(jnp.int32, (span, Q_BLOCK), 0)
    qcol = lax.broadcasted_iota(jnp.int32, (span, Q_BLOCK), 1)

    def band_bias(offset):
        diff = offset + qcol - row
        return jnp.where((diff >= 0) & (diff < window), 0.0, NEG_INF)

    for qb in range(BANDED_QB):
        i = step * BANDED_QB + qb
        start = pl.multiple_of(jnp.maximum(i * Q_BLOCK - window, 0), Q_BLOCK)
        st = _dot(k_ref[pl.ds(start, span), :], _padded_q(qt_ref, g, qb))
        st = st + jnp.concatenate([band_bias(i * Q_BLOCK - start)] * GROUP_R, axis=1)
        m = jnp.max(st, axis=0, keepdims=True)
        if has_sink:
            m = jnp.maximum(m, sink)
        m = jnp.where(m == NEG_INF, 0.0, m)
        pb = jnp.exp2(st - m).astype(BF16)
        u0 = start // Q_BLOCK
        o_t = _dot(vt_ref[u0], pb[0:Q_BLOCK])
        for u in range(1, n_sub):
            o_t = o_t + _dot(vt_ref[u0 + u], pb[u * Q_BLOCK:(u + 1) * Q_BLOCK])
        denom = o_t[HEAD_DIM:HEAD_DIM + 1, :]
        if has_sink:
            denom = denom + jnp.exp2(sink - m)
        o_t = o_t[0:HEAD_DIM, :] / jnp.maximum(denom, 1e-30)
        o_ref[qb * Q_BLOCK:(qb + 1) * Q_BLOCK, :] = _gate_untranspose(
            o_t, gate_ref, gate_branch, qb)


def _banded(qt, k_rows, vt, window, sinks=None, gates=None, gate_branch=None):
    b, h, d, t = qt.shape
    g = h // GROUP_R
    in_specs = []
    args = []
    if sinks is not None:
        in_specs.append(pl.BlockSpec(memory_space=pltpu.SMEM))
        args.append(sinks.astype(F32))
    qw = BANDED_QB * Q_BLOCK
    in_specs += [
        pl.BlockSpec((None, GROUP_R, d, qw), lambda bi, gi, i: (bi, gi, 0, i)),
        pl.BlockSpec((None, t, KV_HEADS * d), lambda bi, gi, i: (bi, 0, 0)),
        pl.BlockSpec((None, None, t // Q_BLOCK, SEL_V_ROWS, Q_BLOCK),
                     lambda bi, gi, i: (bi, gi, 0, 0, 0)),
    ]
    args += [qt, k_rows, vt]
    if gates is not None:
        in_specs.append(pl.BlockSpec((None, None, N_GATE_ROWS, qw),
                                     lambda bi, gi, i: (bi, gi, 0, i)))
        args.append(gates)
    return pl.pallas_call(
        functools.partial(_banded_kernel, window=window, has_sink=sinks is not None,
                          gate_branch=gate_branch if gates is not None else None),
        grid=(b, g, t // qw),
        in_specs=in_specs,
        out_specs=pl.BlockSpec((None, qw, GROUP_R * d), lambda bi, gi, i: (bi, i, gi)),
        out_shape=jax.ShapeDtypeStruct((b, t, h * d), F32),
        compiler_params=_params(("parallel", "parallel", "arbitrary"), VMEM_LIMIT),
        name="banded_attn_w%d" % window,
    )(*args)


def _conv_body(xn, ti, w_ref, cw_ref, o_ref, tail_ref):
    w = MIX_W
    acc = _dot(xn, w_ref[...])
    z = acc[:, 2 * w:3 * w] * acc[:, 0:w]
    zp = jnp.where(ti > 0, tail_ref[...], 0.0)
    row = lax.broadcasted_iota(jnp.int32, z.shape, 0)
    z1 = jnp.where(row == 0, zp[7:8, :], pltpu.roll(z, 1, 0))
    z2 = pltpu.roll(z, 2, 0)
    z2 = jnp.where(row == 0, zp[6:7, :], jnp.where(row == 1, zp[7:8, :], z2))
    cw = cw_ref[...]
    o_ref[...] = acc[:, w:2 * w] * (cw[0:1, :] * z2 + cw[1:2, :] * z1 + cw[2:3, :] * z)
    tail_ref[...] = z[z.shape[0] - 8:, :]


def _gelu_tanh(x):
    return x * (0.5 * (1.0 + jnp.tanh(0.7978845608028654 * (x + 0.044715 * (x * x * x)))))


def _compress_kernel(tk_ref, tv_ref, pek_ref, pev_ref, wk1_ref, wk2_ref, wv1_ref, wv2_ref,
                     kg_ref, ko_ref, vo_ref):
    nrow = tk_ref.shape[0]

    def mlp(a, pe_ref, w1_ref, w2_ref):
        a0 = (a + pe_ref[0:1, :]).astype(BF16)
        a1 = (a + pe_ref[1:2, :]).astype(BF16)
        p1 = _dot(a0, w1_ref[0])
        p2 = _dot(a1, w1_ref[1])
        hdn = p1 + pltpu.roll(p2, nrow - 1, 0)
        return _dot(_gelu_tanh(hdn).astype(BF16), w2_ref[...])

    kc = mlp(tk_ref[...], pek_ref, wk1_ref, wk2_ref)
    ko_ref[...] = _rms_rows(kc, kg_ref[...]).astype(BF16)
    vo_ref[...] = mlp(tv_ref[...], pev_ref, wv1_ref, wv2_ref).astype(BF16)


def _compress(kc_rows, vc_rows, pos_k, pos_v, wk1, wk2, wv1, wv2, k_gain):
    b, t, kvd = kc_rows.shape
    kv = KV_HEADS
    d = kvd // kv
    nrow = t // CMP_STRIDE
    wide = CMP_STRIDE * kvd
    hid = wk1.shape[1]

    def expand_w1(w1):
        w = w1.reshape(2, CMP_STRIDE, 1, d, hid)
        per_head = []
        for g in range(kv):
            pads = [jnp.zeros_like(w)] * kv
            pads[g] = w
            per_head.append(jnp.concatenate(pads, axis=2).reshape(2, wide, hid))
        return jnp.stack(per_head).astype(BF16)

    def expand_pe(pe):
        return jnp.broadcast_to(pe.reshape(2, CMP_STRIDE, 1, d), (2, CMP_STRIDE, kv, d)).reshape(2, wide)

    tok = pl.BlockSpec((None, nrow, wide), lambda bi, gi: (bi, 0, 0))
    full = lambda shape: pl.BlockSpec(shape, lambda bi, gi: tuple(0 for _ in shape))
    w1_spec = pl.BlockSpec((None, 2, wide, hid), lambda bi, gi: (gi, 0, 0, 0))
    out = pl.BlockSpec((None, nrow, d), lambda bi, gi: (bi * kv + gi, 0, 0))
    return pl.pallas_call(
        _compress_kernel,
        grid=(b, kv),
        in_specs=[tok, tok, full((2, wide)), full((2, wide)), w1_spec, full((hid, d)),
                  w1_spec, full((hid, d)), full((1, d))],
        out_specs=[out, out],
        out_shape=[jax.ShapeDtypeStruct((b * kv, nrow, d), BF16),
                   jax.ShapeDtypeStruct((b * kv, nrow, d), BF16)],
        compiler_params=_params(("parallel", "arbitrary"), VMEM_LIMIT),
        name="nsa_compress",
    )(kc_rows.reshape(b, nrow, wide), vc_rows.reshape(b, nrow, wide), expand_pe(pos_k),
      expand_pe(pos_v), expand_w1(wk1), wk2.astype(BF16), expand_w1(wv1), wv2.astype(BF16),
      k_gain.reshape(1, d))


def _cmp_attn_kernel(qt_ref, kc_ref, vct_ref, gate_ref, o_ref, selt_ref, imp_ref, *, sel_k):
    step = pl.program_id(2)
    rq = GROUP_R * Q_BLOCK
    nc = kc_ref.shape[0]
    ns = selt_ref.shape[0]
    blk = lax.broadcasted_iota(jnp.int32, (ns, Q_BLOCK), 0)

    def forced_blocks(i):
        cur = (i * Q_BLOCK + lax.broadcasted_iota(jnp.int32, (ns, Q_BLOCK), 1)) // SEL_BLOCK
        return (blk == 0) | (blk == cur) | (blk == cur - 1), cur

    def attend(nc_eff):
        n = lax.broadcasted_iota(jnp.int32, (nc_eff, rq), 0)
        lane = lax.broadcasted_iota(jnp.int32, (nc_eff, rq), 1)
        end_minus_q = n * CMP_STRIDE + (CMP_BLOCK - 1) - (lane & (Q_BLOCK - 1))
        ss = lax.broadcasted_iota(jnp.int32, (ns, nc_eff), 0) * SEL_BLOCK
        nn = lax.broadcasted_iota(jnp.int32, (ns, nc_eff), 1) * CMP_STRIDE
        overlap_t = jnp.where((nn < ss + SEL_BLOCK) & (nn + (CMP_BLOCK - 1) >= ss),
                              1.0, 0.0).astype(BF16)
        lhs = jnp.concatenate([vct_ref[:, 0:nc_eff], overlap_t], axis=0)
        for qb in range(CMP_QB):
            i = step * CMP_QB + qb
            st = _dot(kc_ref[0:nc_eff, :], _group_q(qt_ref, qb))
            st = jnp.where(end_minus_q <= i * Q_BLOCK, st, NEG_INF)
            m = jnp.max(st, axis=0, keepdims=True)
            m = jnp.where(m == NEG_INF, 0.0, m)
            p = jnp.exp2(st - m)
            denom = jnp.sum(p, axis=0, keepdims=True)
            pb = (p / jnp.maximum(denom, 1e-30)).astype(BF16)
            res = _dot(lhs, pb)
            o_ref[qb * Q_BLOCK:(qb + 1) * Q_BLOCK, :] = _gate_untranspose(
                res[0:HEAD_DIM], gate_ref, 0, qb)
            imp = res[HEAD_DIM:, 0:Q_BLOCK]
            for r in range(1, GROUP_R):
                imp = imp + res[HEAD_DIM:, r * Q_BLOCK:(r + 1) * Q_BLOCK]
            forced, cur = forced_blocks(i)
            imp_ref[qb] = jnp.where(forced | (blk > cur), NEG_INF, imp)

    variant_rows = LANES
    n_var = nc // variant_rows
    n_ending = (step + 1) * (CMP_QB * Q_BLOCK // CMP_STRIDE) - 1
    variant = jnp.minimum((n_ending + variant_rows - 1) // variant_rows, n_var) - 1
    for v in range(n_var):
        pl.when(variant == v)(functools.partial(attend, (v + 1) * variant_rows))

    blkf = blk.astype(F32)
    imps = [imp_ref[qb] for qb in range(CMP_QB)]
    sels = [jnp.where(forced_blocks(step * CMP_QB + qb)[0], 1.0, 0.0) for qb in range(CMP_QB)]
    for _ in range(sel_k - N_FORCED):
        for qb in range(CMP_QB):
            mx = jnp.max(imps[qb], axis=0, keepdims=True)
            first = jnp.min(jnp.where(imps[qb] == mx, blkf, float(ns)), axis=0, keepdims=True)
            hit = blkf == first
            sels[qb] = jnp.where(hit & (mx > NEG_INF), 1.0, sels[qb])
            imps[qb] = jnp.where(hit, NEG_INF, imps[qb])
    for qb in range(CMP_QB):
        selt_ref[:, qb * Q_BLOCK:(qb + 1) * Q_BLOCK] = sels[qb]


def _cmp_attn(qt, k_cmp, v_cmp_t, gates):
    b, h, d, t = qt.shape
    g = h // GROUP_R
    nc = k_cmp.shape[1]
    ns = t // SEL_BLOCK
    qw = CMP_QB * Q_BLOCK
    assert nc % LANES == 0 and ns >= N_FORCED
    return pl.pallas_call(
        functools.partial(_cmp_attn_kernel, sel_k=min(SEL_TOPK, ns)),
        grid=(b, g, t // qw),
        in_specs=[
            pl.BlockSpec((None, GROUP_R, d, qw), lambda bi, gi, i: (bi, gi, 0, i)),
            pl.BlockSpec((None, nc, d), lambda bi, gi, i: (bi * KV_HEADS + gi, 0, 0)),
            pl.BlockSpec((None, d, nc), lambda bi, gi, i: (bi * KV_HEADS + gi, 0, 0)),
            pl.BlockSpec((None, None, N_GATE_ROWS, qw), lambda bi, gi, i: (bi, gi, 0, i)),
        ],
        out_specs=[
            pl.BlockSpec((None, qw, GROUP_R * d), lambda bi, gi, i: (bi, i, gi)),
            pl.BlockSpec((None, None, ns, qw), lambda bi, gi, i: (bi, gi, 0, i)),
        ],
        out_shape=[jax.ShapeDtypeStruct((b, t, h * d), F32),
                   jax.ShapeDtypeStruct((b, g, ns, t), F32)],
        scratch_shapes=[pltpu.VMEM((CMP_QB, ns, Q_BLOCK), F32)],
        compiler_params=_params(("parallel", "parallel", "arbitrary"), VMEM_LIMIT),
        name="nsa_cmp_topk",
    )(qt, k_cmp, v_cmp_t, gates)


def _sel_attn_kernel(qt_ref, ks_ref, vst_ref, selt_ref, gate_ref, o_ref, sa_ref, sb_ref, mta_ref,
                     mtb_ref, m_ref, acc_ref):
    g = pl.program_id(1)
    tk = SEL_KEY_TILE
    bpt = tk // SEL_BLOCK
    spt = tk // Q_BLOCK
    d = HEAD_DIM
    m_ref[...] = jnp.full(m_ref.shape, NEG_INF, F32)
    acc_ref[...] = jnp.zeros_like(acc_ref)
    last_q = (pl.program_id(2) + 1) * SEL_QB * Q_BLOCK
    n_tiles = (last_q + tk - 1) // tk
    qpads = [_padded_q(qt_ref, g, qb) for qb in range(SEL_QB)]
    tri = jnp.where(lax.broadcasted_iota(jnp.int32, (Q_BLOCK, Q_BLOCK), 0)
                    <= lax.broadcasted_iota(jnp.int32, (Q_BLOCK, Q_BLOCK), 1), 0.0, NEG_INF)

    def scores(j, qb, s_ref, mt_ref):
        i = pl.program_id(2) * SEL_QB + qb
        k0 = pl.multiple_of(j * tk, tk)
        st = _dot(ks_ref[pl.ds(k0, tk), :], qpads[qb])
        sel_rows = selt_ref[pl.ds(pl.multiple_of(j * bpt, bpt), bpt),
                            qb * Q_BLOCK:(qb + 1) * Q_BLOCK]
        bias_rows = jnp.where(sel_rows > 0.5, 0.0, NEG_INF)
        parts = []
        for u in range(spt):
            sub = jnp.concatenate(
                [jnp.broadcast_to(bias_rows[s:s + 1, :], (SEL_BLOCK, Q_BLOCK))
                 for s in range(u * Q_BLOCK // SEL_BLOCK, (u + 1) * Q_BLOCK // SEL_BLOCK)], axis=0)
            parts.append(sub + jnp.where(j * spt + u == i, tri, 0.0))
        bias = jnp.concatenate(parts, axis=0)
        st = st + jnp.concatenate([bias] * GROUP_R, axis=1)
        s_ref[qb] = st
        mt_ref[qb] = jnp.max(st, axis=0, keepdims=True)

    def consume(j, qb, s_ref, mt_ref):
        m_prev = m_ref[qb]
        m_new = jnp.maximum(m_prev, mt_ref[qb])
        m_safe = jnp.where(m_new == NEG_INF, 0.0, m_new)
        alpha = jnp.exp2(m_prev - m_safe)
        p = jnp.exp2((s_ref[qb] - m_safe).astype(BF16))
        acc_ref[qb] = alpha * acc_ref[qb] + _dot(vst_ref[j], p)
        m_ref[qb] = m_new

    for qb in range(SEL_QB):
        scores(0, qb, sa_ref, mta_ref)
    n_pairs = n_tiles // 2

    def step(jj, carry):
        j = 2 * jj
        for qb in range(SEL_QB):
            scores(j + 1, qb, sb_ref, mtb_ref)
            consume(j, qb, sa_ref, mta_ref)
        for qb in range(SEL_QB):
            scores(jnp.minimum(j + 2, n_tiles - 1), qb, sa_ref, mta_ref)
            consume(j + 1, qb, sb_ref, mtb_ref)
        return carry

    lax.fori_loop(0, n_pairs, step, 0)

    @pl.when(n_tiles % 2 == 1)
    def _():
        for qb in range(SEL_QB):
            consume(n_tiles - 1, qb, sa_ref, mta_ref)

    for qb in range(SEL_QB):
        o_t = acc_ref[qb, 0:d, :] / jnp.maximum(acc_ref[qb, d:d + 1, :], 1e-30)
        o_ref[qb * Q_BLOCK:(qb + 1) * Q_BLOCK, :] = _gate_untranspose(o_t, gate_ref, 1, qb)


def _sel_attn(qt, ks_rows, vs_t, sel_t, gates):
    b, h, d, t = qt.shape
    g = h // GROUP_R
    ns = sel_t.shape[2]
    tk = SEL_KEY_TILE
    rq = GROUP_R * Q_BLOCK
    nq = SEL_QB
    qw = nq * Q_BLOCK
    assert tk % qw == 0
    return pl.pallas_call(
        _sel_attn_kernel,
        grid=(b, g, t // qw),
        in_specs=[
            pl.BlockSpec((None, GROUP_R, d, qw), lambda bi, gi, i: (bi, gi, 0, i)),
            pl.BlockSpec((None, t, KV_HEADS * d), lambda bi, gi, i: (bi, 0, 0)),
            pl.BlockSpec((None, None, t // tk, SEL_V_ROWS, tk), lambda bi, gi, i: (bi, gi, 0, 0, 0)),
            pl.BlockSpec((None, None, ns, qw), lambda bi, gi, i: (bi, gi, 0, i)),
            pl.BlockSpec((None, None, N_GATE_ROWS, qw), lambda bi, gi, i: (bi, gi, 0, i)),
        ],
        out_specs=pl.BlockSpec((None, qw, GROUP_R * d), lambda bi, gi, i: (bi, i, gi)),
        out_shape=jax.ShapeDtypeStruct((b, t, h * d), F32),
        scratch_shapes=[pltpu.VMEM((nq, tk, rq), F32), pltpu.VMEM((nq, tk, rq), F32),
                        pltpu.VMEM((nq, 1, rq), F32), pltpu.VMEM((nq, 1, rq), F32),
                        pltpu.VMEM((nq, 1, rq), F32), pltpu.VMEM((nq, SEL_V_ROWS, rq), F32)],
        compiler_params=_params(("parallel", "parallel", "arbitrary"), VMEM_LIMIT),
        name="nsa_selected",
    )(qt, ks_rows, vs_t, sel_t, gates)


def _retention_body(xn, ti, lg_ref, w_ref, cos_ref, sin_ref, ng_ref, o_ref, r_ref):
    c = RET_CHUNK
    h, dk, dv = RET_HEADS, RET_QK_DIM, RET_V_DIM
    half = dk // 2
    k0, v0, g0 = h * dk, 2 * h * dk, 2 * h * dk + h * dv

    @pl.when(ti == 0)
    def _():
        r_ref[...] = jnp.zeros_like(r_ref)

    acc = _dot(xn, w_ref[...])
    ii = lax.broadcasted_iota(jnp.int32, (c, c), 0)
    jj = lax.broadcasted_iota(jnp.int32, (c, c), 1)
    dist = (ii - jj).astype(F32)
    jcol = lax.broadcasted_iota(jnp.int32, (c, 1), 0).astype(F32)
    for hh in range(h):
        lg = lg_ref[hh]
        dmask = jnp.where(dist >= 0, jnp.exp(dist * lg), 0.0)
        xi = jnp.exp((jcol + 1.0) * lg)
        zeta = jnp.exp((c - 1.0 - jcol) * lg)
        decay = jnp.exp(jnp.zeros((1, dv), F32) + c * lg)
        for cc in range(acc.shape[0] // c):
            rows = slice(cc * c, (cc + 1) * c)
            cosf = cos_ref[rows, :]
            sinf = sin_ref[rows, :]

            def rot(x):
                return x * cosf + jnp.concatenate([x[:, half:], x[:, :half]], axis=1) * sinf

            q = rot(acc[rows, hh * dk:(hh + 1) * dk])
            k = rot(acc[rows, k0 + hh * dk:k0 + (hh + 1) * dk]) * (dk ** -0.5)
            vb = acc[rows, v0 + hh * dv:v0 + (hh + 1) * dv].astype(BF16)
            att = _dot_nt(q.astype(BF16), k.astype(BF16)) * dmask
            o = _dot(att.astype(BF16), vb)
            r_prev = r_ref[hh]
            o = o + _dot((q * xi).astype(BF16), r_prev.astype(BF16))
            s_chunk = _dot_tn((k * zeta).astype(BF16), vb)
            r_ref[hh] = r_prev * decay + s_chunk
            mu = jnp.mean(o, axis=-1, keepdims=True)
            var = jnp.mean(jnp.square(o - mu), axis=-1, keepdims=True)
            on = (o - mu) * lax.rsqrt(var + EPS)
            gt = acc[rows, g0 + hh * dv:g0 + (hh + 1) * dv]
            o_ref[rows, hh * dv:(hh + 1) * dv] = (
                (gt * _sigmoid(gt)) * (on * ng_ref[:, hh * dv:(hh + 1) * dv]))


def _merge_kernel(x_ref, g_ref, wgate_ref, bias_ref, ya_ref, yb_ref, yc0_ref, yc1_ref, yc2_ref,
                  yd_ref, wb_ref, wo_ref, o_ref):
    x = x_ref[...]
    d = x.shape[1]
    u = _rms_rows(x, g_ref[...]).astype(BF16)
    ys = (ya_ref[...], yb_ref[...], yc0_ref[...] + yc1_ref[...] + yc2_ref[...], yd_ref[...])
    merged = jnp.zeros(x.shape, F32)
    for n in range(N_BRANCH):
        logits = _dot(u, wgate_ref[:, n * d:(n + 1) * d]) + bias_ref[:, n * d:(n + 1) * d]
        merged = merged + _sigmoid(logits) * _dot(ys[n].astype(BF16), wb_ref[n])
    o_ref[...] = x + _dot(merged.astype(BF16), wo_ref[...])


def _merge(x2, g, w_gate, bias, ys, w_branch, w_out, tm=512):
    n, d = x2.shape
    w = MIX_W
    row = lambda width: pl.BlockSpec((tm, width), lambda i: (i, 0))
    full = lambda shape: pl.BlockSpec(shape, lambda i: tuple(0 for _ in shape),
                                      pipeline_mode=pl.Buffered(1))
    return pl.pallas_call(
        _merge_kernel,
        grid=(n // tm,),
        in_specs=[row(d), full((1, d)), full((d, N_BRANCH * d)), full((1, N_BRANCH * d))]
        + [row(w)] * 6 + [full((N_BRANCH, w, d)), full((d, d))],
        out_specs=row(d),
        out_shape=jax.ShapeDtypeStruct((n, d), F32),
        compiler_params=_params(("parallel",), VMEM_LIMIT),
        name="merge_out",
    )(x2, g.reshape(1, d), w_gate.astype(BF16), bias.reshape(1, N_BRANCH * d),
      *[y.reshape(n, w) for y in ys], w_branch.astype(BF16), w_out.astype(BF16))


def _mixers(x2, b, t, mix_norm, w_in, merge_gate_bias, swa_q_gain, swa_k_gain, swa_sinks, conv_w,
            nsa_q_gain, nsa_k_gain, cmp_pos_k, cmp_pos_v, cmp_wk1, cmp_wk2, cmp_wv1, cmp_wv2,
            ret_norm_gain, w_branch, w_out):
    hd = HEAD_DIM
    swa_q, swa_kv = SWA_HEADS * hd, SWA_KV_HEADS * hd
    nsa_q, nsa_kv = NSA_HEADS * hd, NSA_KV_HEADS * hd
    ret_qk, ret_v = RET_HEADS * RET_QK_DIM, RET_HEADS * RET_V_DIM
    n_gate = NSA_HEADS * 3
    o_a = 0
    o_b = o_a + swa_q + 2 * swa_kv
    o_c = o_b + 3 * MIX_W
    o_cg = o_c + nsa_q + 6 * nsa_kv
    o_d = o_cg + n_gate
    o_g = o_d + 2 * ret_qk + 2 * ret_v

    per_g = GROUP_R * 3
    w_gates = [jnp.pad(w_in[:, o_cg + gi * per_g:o_cg + (gi + 1) * per_g],
                       ((0, 0), (0, N_GATE_ROWS - per_g))) for gi in range(NSA_KV_HEADS)]
    w_t = jnp.concatenate([w_in[:, o_a:o_b], w_in[:, o_c:o_cg]] + w_gates, axis=1).T.astype(BF16)
    head_gains = jnp.stack([swa_q_gain, swa_k_gain, nsa_q_gain, nsa_k_gain[1], nsa_k_gain[2]])
    head_gains = jnp.broadcast_to(head_gains.astype(F32)[:, :, None], (5, hd, LANES))
    (a_qt, a_k, a_vt, c_qt, c_kc, c_vc, c_ks, c_vst, c_kw, c_vwt, gates_c, y_b, y_d) = _mixer_proj(
        x2, b, t, mix_norm, w_t, head_gains, w_in[:, o_b:o_c], conv_w, w_in[:, o_d:o_g],
        ret_norm_gain)

    y_a = _banded(a_qt, a_k, a_vt, SWA_WINDOW, sinks=swa_sinks)

    k_cmp, v_cmp = _compress(c_kc, c_vc, cmp_pos_k, cmp_pos_v, cmp_wk1, cmp_wk2, cmp_wv1, cmp_wv2,
                             nsa_k_gain[0])
    y_cmp, sel_t = _cmp_attn(c_qt, k_cmp, v_cmp.transpose(0, 2, 1), gates_c)
    y_sel = _sel_attn(c_qt, c_ks, c_vst, sel_t, gates_c)
    y_win = _banded(c_qt, c_kw, c_vwt, NSA_WINDOW, gates=gates_c, gate_branch=2)

    return _merge(x2, mix_norm, w_in[:, o_g:], merge_gate_bias,
                  (y_a, y_b, y_cmp, y_sel, y_win, y_d), w_branch, w_out)


def kernel(x, ffn1_norm, ffn1_w_gate, ffn1_w_up, ffn1_w_down, mix_norm, w_in, merge_gate_bias, swa_q_gain, swa_k_gain, swa_sinks, conv_w, nsa_q_gain, nsa_k_gain, cmp_pos_k, cmp_pos_v, cmp_wk1, cmp_wk2, cmp_wv1, cmp_wv2, ret_norm_gain, w_branch, w_out, ffn2_norm, ffn2_w_gate, ffn2_w_up, ffn2_w_down):
    b, t, d = x.shape
    x2 = x.reshape(b * t, d)
    for l in range(ffn1_norm.shape[0]):
        x2 = _ffn(x2, ffn1_norm[l], ffn1_w_gate[l], ffn1_w_up[l], ffn1_w_down[l])
        x2 = _mixers(x2, b, t, mix_norm[l], w_in[l], merge_gate_bias[l], swa_q_gain[l],
                     swa_k_gain[l], swa_sinks[l], conv_w[l], nsa_q_gain[l], nsa_k_gain[l],
                     cmp_pos_k[l], cmp_pos_v[l], cmp_wk1[l], cmp_wk2[l], cmp_wv1[l], cmp_wv2[l],
                     ret_norm_gain[l], w_branch[l], w_out[l])
        x2 = _ffn(x2, ffn2_norm[l], ffn2_w_gate[l], ffn2_w_up[l], ffn2_w_down[l])
    return x2.reshape(b, t, d)
```

```python
import functools

import jax
import jax.numpy as jnp
from jax import lax
from jax.experimental import pallas as pl
from jax.experimental.pallas import tpu as pltpu

F32 = jnp.float32
BF16 = jnp.bfloat16

HEAD_DIM = 64
Q_BLOCK = 128
MIX_W = 512
N_BRANCH = 4
SWA_HEADS = 8
SWA_KV_HEADS = 2
SWA_WINDOW = 128
NSA_HEADS = 8
NSA_KV_HEADS = 2
CMP_BLOCK = 32
CMP_STRIDE = 16
SEL_BLOCK = 64
SEL_TOPK = 16
N_FORCED = 3
NSA_WINDOW = 512
RET_HEADS = 4
RET_QK_DIM = 64
RET_V_DIM = 128
RET_CHUNK = 128
ROPE_BASE = 10000.0
EPS = 1e-6
KV_HEADS = SWA_KV_HEADS
GROUP_R = SWA_HEADS // SWA_KV_HEADS
assert (NSA_KV_HEADS, NSA_HEADS // NSA_KV_HEADS) == (KV_HEADS, GROUP_R)

LANES = 128
BF16_ROWS = 16
N_GATE_ROWS = 16
SEL_KEY_TILE = 512
SEL_V_ROWS = HEAD_DIM + BF16_ROWS
PROJ_TM = 512
BANDED_QB = 16
CMP_QB = 8
SEL_QB = 4
VMEM_LIMIT = 52 * 1024 * 1024

NEG_INF = float("-inf")
LOG2_E = 1.4426950408889634


def _params(sem, vmem=None):
    return pltpu.CompilerParams(dimension_semantics=sem, vmem_limit_bytes=vmem)


def _sigmoid(x):
    return 1.0 / (1.0 + jnp.exp(-x))


def _rms_rows(x, g):
    return x * lax.rsqrt(jnp.mean(x * x, axis=-1, keepdims=True) + EPS) * g


def _dot(a, b):
    return jnp.dot(a, b, preferred_element_type=F32)


def _dot_nt(a, b):
    return lax.dot_general(a, b, (((1,), (1,)), ((), ())), preferred_element_type=F32)


def _dot_tn(a, b):
    return lax.dot_general(a, b, (((0,), (0,)), ((), ())), preferred_element_type=F32)


def _ffn_kernel(x_ref, g_ref, wg_ref, wu_ref, wd_ref, o_ref, *, n_chunks):
    x = x_ref[...]
    xn = _rms_rows(x, g_ref[...]).astype(BF16)
    tf = wg_ref.shape[1] // n_chunks
    acc = None
    for f in range(n_chunks):
        a = _dot(xn, wg_ref[:, f * tf:(f + 1) * tf])
        b = _dot(xn, wu_ref[:, f * tf:(f + 1) * tf])
        h = ((a * _sigmoid(a)) * b).astype(BF16)
        part = _dot(h, wd_ref[f * tf:(f + 1) * tf, :])
        acc = part if acc is None else acc + part
    o_ref[...] = x + 0.5 * acc


def _ffn(x2, g, wg, wu, wd, tm=1024, n_chunks=11):
    n, d = x2.shape
    dff = wg.shape[1]
    assert dff % (n_chunks * LANES) == 0
    resident = lambda shape: pl.BlockSpec(shape, lambda i: (0, 0), pipeline_mode=pl.Buffered(1))
    return pl.pallas_call(
        functools.partial(_ffn_kernel, n_chunks=n_chunks),
        grid=(n // tm,),
        in_specs=[
            pl.BlockSpec((tm, d), lambda i: (i, 0)),
            pl.BlockSpec((1, d), lambda i: (0, 0)),
            resident((d, dff)), resident((d, dff)), resident((dff, d)),
        ],
        out_specs=pl.BlockSpec((tm, d), lambda i: (i, 0)),
        out_shape=jax.ShapeDtypeStruct((n, d), F32),
        compiler_params=_params(("parallel",), VMEM_LIMIT),
        name="ffn",
    )(x2, g.reshape(1, d), wg.astype(BF16), wu.astype(BF16), wd.astype(BF16))


def _heads_body(xn, wt_ref, hg_ref, aq_ref, ak_ref, av_ref, cq_ref, ckc_ref, cvc_ref, cks_ref,
                cvs_ref, ckw_ref, cvw_ref, gt_ref):
    d = HEAD_DIM
    acc = _dot_nt(wt_ref[...], xn)
    tm = xn.shape[0]
    lane_tiles = tm // LANES
    scale = d ** -0.5
    kv_w = KV_HEADS * d

    def head_norm(row0, gain_idx, mult):
        hb = acc[row0:row0 + d]
        gain = jnp.concatenate([hg_ref[gain_idx]] * lane_tiles, axis=1)
        y = hb * lax.rsqrt(jnp.mean(hb * hb, axis=0, keepdims=True) + EPS) * gain
        return y * mult if mult != 1.0 else y

    def q_heads(row0, gain_idx, out_ref, n_heads):
        for h in range(n_heads):
            out_ref[h] = head_norm(row0 + h * d, gain_idx, scale * LOG2_E).astype(BF16)

    def k_rows(row0, gain_idx):
        return jnp.concatenate([head_norm(row0 + g * d, gain_idx, 1.0) for g in range(KV_HEADS)],
                               axis=0).T

    def v_tiles(row0, out_ref):
        for g in range(KV_HEADS):
            for u in range(tm // Q_BLOCK):
                out_ref[g, u, 0:d, :] = acc[row0 + g * d:row0 + (g + 1) * d,
                                            u * Q_BLOCK:(u + 1) * Q_BLOCK].astype(BF16)
                out_ref[g, u, d:SEL_V_ROWS, :] = jnp.ones((SEL_V_ROWS - d, Q_BLOCK), BF16)

    row = 0
    q_heads(row, 0, aq_ref, SWA_HEADS)
    row += SWA_HEADS * d
    ak_ref[...] = k_rows(row, 1).astype(BF16)
    row += kv_w
    v_tiles(row, av_ref)
    row += kv_w
    q_heads(row, 2, cq_ref, NSA_HEADS)
    row += NSA_HEADS * d
    ckc_ref[...] = acc[row:row + kv_w].T
    row += kv_w
    cvc_ref[...] = acc[row:row + kv_w].T
    row += kv_w
    cks_ref[...] = k_rows(row, 3).astype(BF16)
    row += kv_w
    for g in range(KV_HEADS):
        cvs_ref[g, 0, 0:d, :] = acc[row + g * d:row + (g + 1) * d].astype(BF16)
        cvs_ref[g, 0, d:SEL_V_ROWS, :] = jnp.ones((SEL_V_ROWS - d, tm), BF16)
    row += kv_w
    ckw_ref[...] = k_rows(row, 4).astype(BF16)
    row += kv_w
    v_tiles(row, cvw_ref)
    row += kv_w
    for g in range(KV_HEADS):
        gt_ref[g] = _sigmoid(acc[row + g * N_GATE_ROWS:row + (g + 1) * N_GATE_ROWS])


N_HEAD_OUTS = 11


def _mixer_proj_kernel(lg_ref, x_ref, g_ref, wt_ref, hg_ref, wb_ref, cw_ref, wd_ref, cos_ref,
                       sin_ref, ng_ref, *rest):
    head_outs = rest[:N_HEAD_OUTS]
    yb_ref, yd_ref, tail_ref, r_ref = rest[N_HEAD_OUTS:]
    ti = pl.program_id(1)
    xn = _rms_rows(x_ref[...], g_ref[...]).astype(BF16)
    _heads_body(xn, wt_ref, hg_ref, *head_outs)
    _conv_body(xn, ti, wb_ref, cw_ref, yb_ref, tail_ref)
    _retention_body(xn, ti, lg_ref, wd_ref, cos_ref, sin_ref, ng_ref, yd_ref, r_ref)


def _mixer_proj(x2, b, t, g, w_t, head_gains, w_b, conv_w, w_d, norm_gain):
    n, dm = x2.shape
    d = HEAD_DIM
    tm = PROJ_TM
    assert tm == SEL_KEY_TILE and t % tm == 0
    tpb = t // tm
    q = Q_BLOCK
    kv = KV_HEADS
    w = MIX_W
    h, dk, dv = RET_HEADS, RET_QK_DIM, RET_V_DIM
    half = dk // 2
    inv = ROPE_BASE ** (-jnp.arange(half, dtype=F32) / half)
    ang = jnp.arange(t).astype(F32)[:, None] * inv[None, :]
    cos = jnp.cos(ang)
    sin = jnp.sin(ang)
    cosf = jnp.concatenate([cos, cos], axis=-1)
    sinf = jnp.concatenate([-sin, sin], axis=-1)
    log_gamma = jnp.log(1.0 - 2.0 ** (-5.0 - jnp.arange(h, dtype=F32)))
    conv_rows = jnp.pad(conv_w.reshape(conv_w.shape[0], w).astype(F32),
                        ((0, 8 - conv_w.shape[0]), (0, 0)))
    resident = lambda a: pl.BlockSpec(a.shape, lambda bi, ti: tuple(0 for _ in a.shape),
                                      pipeline_mode=pl.Buffered(1))
    qt_spec = lambda nh: pl.BlockSpec((None, nh, d, tm), lambda bi, ti: (bi, 0, 0, ti))
    row_spec = pl.BlockSpec((None, tm, kv * d), lambda bi, ti: (bi, ti, 0))
    vt_spec = pl.BlockSpec((None, kv, tm // q, SEL_V_ROWS, q), lambda bi, ti: (bi, 0, ti, 0, 0))
    y_spec = pl.BlockSpec((None, tm, w), lambda bi, ti: (bi, ti, 0))
    sds = jax.ShapeDtypeStruct
    w_b16, w_d16 = w_b.astype(BF16), w_d.astype(BF16)
    return pl.pallas_call(
        _mixer_proj_kernel,
        grid=(b, tpb),
        in_specs=[
            pl.BlockSpec(memory_space=pltpu.SMEM),
            pl.BlockSpec((tm, dm), lambda bi, ti: (bi * tpb + ti, 0)),
            pl.BlockSpec((1, dm), lambda bi, ti: (0, 0)),
            resident(w_t), pl.BlockSpec(head_gains.shape, lambda bi, ti: (0, 0, 0)),
            resident(w_b16), pl.BlockSpec((8, w), lambda bi, ti: (0, 0)),
            resident(w_d16),
            pl.BlockSpec((tm, dk), lambda bi, ti: (ti, 0)),
            pl.BlockSpec((tm, dk), lambda bi, ti: (ti, 0)),
            pl.BlockSpec((1, h * dv), lambda bi, ti: (0, 0)),
        ],
        out_specs=[
            qt_spec(SWA_HEADS), row_spec, vt_spec,
            qt_spec(NSA_HEADS), row_spec, row_spec, row_spec,
            pl.BlockSpec((None, kv, 1, SEL_V_ROWS, tm), lambda bi, ti: (bi, 0, ti, 0, 0)),
            row_spec, vt_spec,
            pl.BlockSpec((None, kv, N_GATE_ROWS, tm), lambda bi, ti: (bi, 0, 0, ti)),
            y_spec, y_spec,
        ],
        out_shape=[
            sds((b, SWA_HEADS, d, t), BF16), sds((b, t, kv * d), BF16),
            sds((b, kv, t // q, SEL_V_ROWS, q), BF16),
            sds((b, NSA_HEADS, d, t), BF16), sds((b, t, kv * d), F32), sds((b, t, kv * d), F32),
            sds((b, t, kv * d), BF16), sds((b, kv, tpb, SEL_V_ROWS, tm), BF16),
            sds((b, t, kv * d), BF16), sds((b, kv, t // q, SEL_V_ROWS, q), BF16),
            sds((b, kv, N_GATE_ROWS, t), F32),
            sds((b, t, w), F32), sds((b, t, h * dv), F32),
        ],
        scratch_shapes=[pltpu.VMEM((8, w), F32), pltpu.VMEM((h, dk, dv), F32)],
        compiler_params=_params(("parallel", "arbitrary"), VMEM_LIMIT),
        name="mixer_proj",
    )(log_gamma, x2, g.reshape(1, dm), w_t, head_gains, w_b16, conv_rows, w_d16, cosf, sinf,
      norm_gain.reshape(1, h * dv))


def _group_q(qt_ref, qb=0):
    return jnp.concatenate(
        [qt_ref[r, :, qb * Q_BLOCK:(qb + 1) * Q_BLOCK] for r in range(GROUP_R)], axis=1)


def _padded_q(qt_ref, g, qb=0):
    q4 = _group_q(qt_ref, qb)
    z = jnp.zeros_like(q4)
    return jnp.where(g == 0, jnp.concatenate([q4, z], axis=0), jnp.concatenate([z, q4], axis=0))


def _gate_untranspose(o_t, gate_ref, branch, qb=0):
    outs = []
    for r in range(GROUP_R):
        blk = o_t[:, r * Q_BLOCK:(r + 1) * Q_BLOCK]
        if gate_ref is not None:
            c = r * 3 + branch
            blk = blk * gate_ref[c:c + 1, qb * Q_BLOCK:(qb + 1) * Q_BLOCK]
        outs.append(blk.T)
    return jnp.concatenate(outs, axis=1)


def _banded_kernel(*refs, window, has_sink, gate_branch):
    refs = list(refs)
    sink_ref = refs.pop(0) if has_sink else None
    qt_ref, k_ref, vt_ref = refs[:3]
    gate_ref = refs[3] if gate_branch is not None else None
    o_ref = refs[-1]
    g = pl.program_id(1)
    rq = GROUP_R * Q_BLOCK
    n_sub = window // Q_BLOCK + 1
    span = n_sub * Q_BLOCK
    if has_sink:
        lrow = lax.broadcasted_iota(jnp.int32, (1, rq), 1)
        sink = jnp.zeros((1, rq), F32)
        for r in range(GROUP_R):
            sink = jnp.where((lrow >= r * Q_BLOCK) & (lrow < (r + 1) * Q_BLOCK),
                             sink_ref[g * GROUP_R + r], sink)
        sink = sink * LOG2_E
    step = pl.program_id(2)
    row = lax.broadcasted_iota(jnp.int32, (span, Q_BLOCK), 0)
    qcol = lax.broadcasted_iota(jnp.int32, (span, Q_BLOCK), 1)

    def band_bias(offset):
        diff = offset + qcol - row
        return jnp.where((diff >= 0) & (diff < window), 0.0, NEG_INF)

    for qb in range(BANDED_QB):
        i = step * BANDED_QB + qb
        start = pl.multiple_of(jnp.maximum(i * Q_BLOCK - window, 0), Q_BLOCK)
        st = _dot(k_ref[pl.ds(start, span), :], _padded_q(qt_ref, g, qb))
        st = st + jnp.concatenate([band_bias(i * Q_BLOCK - start)] * GROUP_R, axis=1)
        m = jnp.max(st, axis=0, keepdims=True)
        if has_sink:
            m = jnp.maximum(m, sink)
        m = jnp.where(m == NEG_INF, 0.0, m)
        pb = jnp.exp2(st - m).astype(BF16)
        u0 = start // Q_BLOCK
        o_t = _dot(vt_ref[u0], pb[0:Q_BLOCK])
        for u in range(1, n_sub):
            o_t = o_t + _dot(vt_ref[u0 + u], pb[u * Q_BLOCK:(u + 1) * Q_BLOCK])
        denom = o_t[HEAD_DIM:HEAD_DIM + 1, :]
        if has_sink:
            denom = denom + jnp.exp2(sink - m)
        o_t = o_t[0:HEAD_DIM, :] / jnp.maximum(denom, 1e-30)
        o_ref[qb * Q_BLOCK:(qb + 1) * Q_BLOCK, :] = _gate_untranspose(
            o_t, gate_ref, gate_branch, qb)


def _banded(qt, k_rows, vt, window, sinks=None, gates=None, gate_branch=None):
    b, h, d, t = qt.shape
    g = h // GROUP_R
    in_specs = []
    args = []
    if sinks is not None:
        in_specs.append(pl.BlockSpec(memory_space=pltpu.SMEM))
        args.append(sinks.astype(F32))
    qw = BANDED_QB * Q_BLOCK
    in_specs += [
        pl.BlockSpec((None, GROUP_R, d, qw), lambda bi, gi, i: (bi, gi, 0, i)),
        pl.BlockSpec((None, t, KV_HEADS * d), lambda bi, gi, i: (bi, 0, 0)),
        pl.BlockSpec((None, None, t // Q_BLOCK, SEL_V_ROWS, Q_BLOCK),
                     lambda bi, gi, i: (bi, gi, 0, 0, 0)),
    ]
    args += [qt, k_rows, vt]
    if gates is not None:
        in_specs.append(pl.BlockSpec((None, None, N_GATE_ROWS, qw),
                                     lambda bi, gi, i: (bi, gi, 0, i)))
        args.append(gates)
    return pl.pallas_call(
        functools.partial(_banded_kernel, window=window, has_sink=sinks is not None,
                          gate_branch=gate_branch if gates is not None else None),
        grid=(b, g, t // qw),
        in_specs=in_specs,
        out_specs=pl.BlockSpec((None, qw, GROUP_R * d), lambda bi, gi, i: (bi, i, gi)),
        out_shape=jax.ShapeDtypeStruct((b, t, h * d), F32),
        compiler_params=_params(("parallel", "parallel", "arbitrary"), VMEM_LIMIT),
        name="banded_attn_w%d" % window,
    )(*args)


def _conv_body(xn, ti, w_ref, cw_ref, o_ref, tail_ref):
    w = MIX_W
    acc = _dot(xn, w_ref[...])
    z = acc[:, 2 * w:3 * w] * acc[:, 0:w]
    zp = jnp.where(ti > 0, tail_ref[...], 0.0)
    row = lax.broadcasted_iota(jnp.int32, z.shape, 0)
    z1 = jnp.where(row == 0, zp[7:8, :], pltpu.roll(z, 1, 0))
    z2 = pltpu.roll(z, 2, 0)
    z2 = jnp.where(row == 0, zp[6:7, :], jnp.where(row == 1, zp[7:8, :], z2))
    cw = cw_ref[...]
    o_ref[...] = acc[:, w:2 * w] * (cw[0:1, :] * z2 + cw[1:2, :] * z1 + cw[2:3, :] * z)
    tail_ref[...] = z[z.shape[0] - 8:, :]


def _gelu_tanh(x):
    return x * (0.5 * (1.0 + jnp.tanh(0.7978845608028654 * (x + 0.044715 * (x * x * x)))))


def _compress_kernel(tk_ref, tv_ref, pek_ref, pev_ref, wk1_ref, wk2_ref, wv1_ref, wv2_ref,
                     kg_ref, ko_ref, vo_ref):
    nrow = tk_ref.shape[0]

    def mlp(a, pe_ref, w1_ref, w2_ref):
        a0 = (a + pe_ref[0:1, :]).astype(BF16)
        a1 = (a + pe_ref[1:2, :]).astype(BF16)
        p1 = _dot(a0, w1_ref[0])
        p2 = _dot(a1, w1_ref[1])
        hdn = p1 + pltpu.roll(p2, nrow - 1, 0)
        return _dot(_gelu_tanh(hdn).astype(BF16), w2_ref[...])

    kc = mlp(tk_ref[...], pek_ref, wk1_ref, wk2_ref)
    ko_ref[...] = _rms_rows(kc, kg_ref[...]).astype(BF16)
    vo_ref[...] = mlp(tv_ref[...], pev_ref, wv1_ref, wv2_ref).astype(BF16)


def _compress(kc_rows, vc_rows, pos_k, pos_v, wk1, wk2, wv1, wv2, k_gain):
    b, t, kvd = kc_rows.shape
    kv = KV_HEADS
    d = kvd // kv
    nrow = t // CMP_STRIDE
    wide = CMP_STRIDE * kvd
    hid = wk1.shape[1]

    def expand_w1(w1):
        w = w1.reshape(2, CMP_STRIDE, 1, d, hid)
        per_head = []
        for g in range(kv):
            pads = [jnp.zeros_like(w)] * kv
            pads[g] = w
            per_head.append(jnp.concatenate(pads, axis=2).reshape(2, wide, hid))
        return jnp.stack(per_head).astype(BF16)

    def expand_pe(pe):
        return jnp.broadcast_to(pe.reshape(2, CMP_STRIDE, 1, d), (2, CMP_STRIDE, kv, d)).reshape(2, wide)

    tok = pl.BlockSpec((None, nrow, wide), lambda bi, gi: (bi, 0, 0))
    full = lambda shape: pl.BlockSpec(shape, lambda bi, gi: tuple(0 for _ in shape))
    w1_spec = pl.BlockSpec((None, 2, wide, hid), lambda bi, gi: (gi, 0, 0, 0))
    out = pl.BlockSpec((None, nrow, d), lambda bi, gi: (bi * kv + gi, 0, 0))
    return pl.pallas_call(
        _compress_kernel,
        grid=(b, kv),
        in_specs=[tok, tok, full((2, wide)), full((2, wide)), w1_spec, full((hid, d)),
                  w1_spec, full((hid, d)), full((1, d))],
        out_specs=[out, out],
        out_shape=[jax.ShapeDtypeStruct((b * kv, nrow, d), BF16),
                   jax.ShapeDtypeStruct((b * kv, nrow, d), BF16)],
        compiler_params=_params(("parallel", "arbitrary"), VMEM_LIMIT),
        name="nsa_compress",
    )(kc_rows.reshape(b, nrow, wide), vc_rows.reshape(b, nrow, wide), expand_pe(pos_k),
      expand_pe(pos_v), expand_w1(wk1), wk2.astype(BF16), expand_w1(wv1), wv2.astype(BF16),
      k_gain.reshape(1, d))


def _cmp_attn_kernel(qt_ref, kc_ref, vct_ref, gate_ref, o_ref, selt_ref, imp_ref, *, sel_k):
    step = pl.program_id(2)
    rq = GROUP_R * Q_BLOCK
    nc = kc_ref.shape[0]
    ns = selt_ref.shape[0]
    blk = lax.broadcasted_iota(jnp.int32, (ns, Q_BLOCK), 0)

    def forced_blocks(i):
        cur = (i * Q_BLOCK + lax.broadcasted_iota(jnp.int32, (ns, Q_BLOCK), 1)) // SEL_BLOCK
        return (blk == 0) | (blk == cur) | (blk == cur - 1), cur

    def attend(nc_eff):
        n = lax.broadcasted_iota(jnp.int32, (nc_eff, rq), 0)
        lane = lax.broadcasted_iota(jnp.int32, (nc_eff, rq), 1)
        end_minus_q = n * CMP_STRIDE + (CMP_BLOCK - 1) - (lane & (Q_BLOCK - 1))
        ss = lax.broadcasted_iota(jnp.int32, (ns, nc_eff), 0) * SEL_BLOCK
        nn = lax.broadcasted_iota(jnp.int32, (ns, nc_eff), 1) * CMP_STRIDE
        overlap_t = jnp.where((nn < ss + SEL_BLOCK) & (nn + (CMP_BLOCK - 1) >= ss),
                              1.0, 0.0).astype(BF16)
        lhs = jnp.concatenate([vct_ref[:, 0:nc_eff], overlap_t], axis=0)
        for qb in range(CMP_QB):
            i = step * CMP_QB + qb
            st = _dot(kc_ref[0:nc_eff, :], _group_q(qt_ref, qb))
            st = jnp.where(end_minus_q <= i * Q_BLOCK, st, NEG_INF)
            m = jnp.max(st, axis=0, keepdims=True)
            m = jnp.where(m == NEG_INF, 0.0, m)
            p = jnp.exp2(st - m)
            denom = jnp.sum(p, axis=0, keepdims=True)
            pb = (p / jnp.maximum(denom, 1e-30)).astype(BF16)
            res = _dot(lhs, pb)
            o_ref[qb * Q_BLOCK:(qb + 1) * Q_BLOCK, :] = _gate_untranspose(
                res[0:HEAD_DIM], gate_ref, 0, qb)
            imp = res[HEAD_DIM:, 0:Q_BLOCK]
            for r in range(1, GROUP_R):
                imp = imp + res[HEAD_DIM:, r * Q_BLOCK:(r + 1) * Q_BLOCK]
            forced, cur = forced_blocks(i)
            imp_ref[qb] = jnp.where(forced | (blk > cur), NEG_INF, imp)

    variant_rows = LANES
    n_var = nc // variant_rows
    n_ending = (step + 1) * (CMP_QB * Q_BLOCK // CMP_STRIDE) - 1
    variant = jnp.minimum((n_ending + variant_rows - 1) // variant_rows, n_var) - 1
    for v in range(n_var):
        pl.when(variant == v)(functools.partial(attend, (v + 1) * variant_rows))

    blkf = blk.astype(F32)
    imps = [imp_ref[qb] for qb in range(CMP_QB)]
    sels = [jnp.where(forced_blocks(step * CMP_QB + qb)[0], 1.0, 0.0) for qb in range(CMP_QB)]
    for _ in range(sel_k - N_FORCED):
        for qb in range(CMP_QB):
            mx = jnp.max(imps[qb], axis=0, keepdims=True)
            first = jnp.min(jnp.where(imps[qb] == mx, blkf, float(ns)), axis=0, keepdims=True)
            hit = blkf == first
            sels[qb] = jnp.where(hit & (mx > NEG_INF), 1.0, sels[qb])
            imps[qb] = jnp.where(hit, NEG_INF, imps[qb])
    for qb in range(CMP_QB):
        selt_ref[:, qb * Q_BLOCK:(qb + 1) * Q_BLOCK] = sels[qb]


def _cmp_attn(qt, k_cmp, v_cmp_t, gates):
    b, h, d, t = qt.shape
    g = h // GROUP_R
    nc = k_cmp.shape[1]
    ns = t // SEL_BLOCK
    qw = CMP_QB * Q_BLOCK
    assert nc % LANES == 0 and ns >= N_FORCED
    return pl.pallas_call(
        functools.partial(_cmp_attn_kernel, sel_k=min(SEL_TOPK, ns)),
        grid=(b, g, t // qw),
        in_specs=[
            pl.BlockSpec((None, GROUP_R, d, qw), lambda bi, gi, i: (bi, gi, 0, i)),
            pl.BlockSpec((None, nc, d), lambda bi, gi, i: (bi * KV_HEADS + gi, 0, 0)),
            pl.BlockSpec((None, d, nc), lambda bi, gi, i: (bi * KV_HEADS + gi, 0, 0)),
            pl.BlockSpec((None, None, N_GATE_ROWS, qw), lambda bi, gi, i: (bi, gi, 0, i)),
        ],
        out_specs=[
            pl.BlockSpec((None, qw, GROUP_R * d), lambda bi, gi, i: (bi, i, gi)),
            pl.BlockSpec((None, None, ns, qw), lambda bi, gi, i: (bi, gi, 0, i)),
        ],
        out_shape=[jax.ShapeDtypeStruct((b, t, h * d), F32),
                   jax.ShapeDtypeStruct((b, g, ns, t), F32)],
        scratch_shapes=[pltpu.VMEM((CMP_QB, ns, Q_BLOCK), F32)],
        compiler_params=_params(("parallel", "parallel", "arbitrary"), VMEM_LIMIT),
        name="nsa_cmp_topk",
    )(qt, k_cmp, v_cmp_t, gates)


def _sel_attn_kernel(qt_ref, ks_ref, vst_ref, selt_ref, gate_ref, o_ref, sa_ref, sb_ref, mta_ref,
                     mtb_ref, m_ref, acc_ref):
    g = pl.program_id(1)
    tk = SEL_KEY_TILE
    bpt = tk // SEL_BLOCK
    spt = tk // Q_BLOCK
    d = HEAD_DIM
    m_ref[...] = jnp.full(m_ref.shape, NEG_INF, F32)
    acc_ref[...] = jnp.zeros_like(acc_ref)
    last_q = (pl.program_id(2) + 1) * SEL_QB * Q_BLOCK
    n_tiles = (last_q + tk - 1) // tk
    qpads = [_padded_q(qt_ref, g, qb) for qb in range(SEL_QB)]
    tri = jnp.where(lax.broadcasted_iota(jnp.int32, (Q_BLOCK, Q_BLOCK), 0)
                    <= lax.broadcasted_iota(jnp.int32, (Q_BLOCK, Q_BLOCK), 1), 0.0, NEG_INF)

    def scores(j, qb, s_ref, mt_ref):
        i = pl.program_id(2) * SEL_QB + qb
        k0 = pl.multiple_of(j * tk, tk)
        st = _dot(ks_ref[pl.ds(k0, tk), :], qpads[qb])
        sel_rows = selt_ref[pl.ds(pl.multiple_of(j * bpt, bpt), bpt),
                            qb * Q_BLOCK:(qb + 1) * Q_BLOCK]
        bias_rows = jnp.where(sel_rows > 0.5, 0.0, NEG_INF)
        parts = []
        for u in range(spt):
            sub = jnp.concatenate(
                [jnp.broadcast_to(bias_rows[s:s + 1, :], (SEL_BLOCK, Q_BLOCK))
                 for s in range(u * Q_BLOCK // SEL_BLOCK, (u + 1) * Q_BLOCK // SEL_BLOCK)], axis=0)
            parts.append(sub + jnp.where(j * spt + u == i, tri, 0.0))
        bias = jnp.concatenate(parts, axis=0)
        st = st + jnp.concatenate([bias] * GROUP_R, axis=1)
        s_ref[qb] = st
        mt_ref[qb] = jnp.max(st, axis=0, keepdims=True)

    def consume(j, qb, s_ref, mt_ref):
        m_prev = m_ref[qb]
        m_new = jnp.maximum(m_prev, mt_ref[qb])
        m_safe = jnp.where(m_new == NEG_INF, 0.0, m_new)
        alpha = jnp.exp2(m_prev - m_safe)
        p = jnp.exp2((s_ref[qb] - m_safe).astype(BF16))
        acc_ref[qb] = alpha * acc_ref[qb] + _dot(vst_ref[j], p)
        m_ref[qb] = m_new

    for qb in range(SEL_QB):
        scores(0, qb, sa_ref, mta_ref)
    n_pairs = n_tiles // 2

    def step(jj, carry):
        j = 2 * jj
        for qb in range(SEL_QB):
            scores(j + 1, qb, sb_ref, mtb_ref)
            consume(j, qb, sa_ref, mta_ref)
        for qb in range(SEL_QB):
            scores(jnp.minimum(j + 2, n_tiles - 1), qb, sa_ref, mta_ref)
            consume(j + 1, qb, sb_ref, mtb_ref)
        return carry

    lax.fori_loop(0, n_pairs, step, 0)

    @pl.when(n_tiles % 2 == 1)
    def _():
        for qb in range(SEL_QB):
            consume(n_tiles - 1, qb, sa_ref, mta_ref)

    for qb in range(SEL_QB):
        o_t = acc_ref[qb, 0:d, :] / jnp.maximum(acc_ref[qb, d:d + 1, :], 1e-30)
        o_ref[qb * Q_BLOCK:(qb + 1) * Q_BLOCK, :] = _gate_untranspose(o_t, gate_ref, 1, qb)


def _sel_attn(qt, ks_rows, vs_t, sel_t, gates):
    b, h, d, t = qt.shape
    g = h // GROUP_R
    ns = sel_t.shape[2]
    tk = SEL_KEY_TILE
    rq = GROUP_R * Q_BLOCK
    nq = SEL_QB
    qw = nq * Q_BLOCK
    assert tk % qw == 0
    return pl.pallas_call(
        _sel_attn_kernel,
        grid=(b, g, t // qw),
        in_specs=[
            pl.BlockSpec((None, GROUP_R, d, qw), lambda bi, gi, i: (bi, gi, 0, i)),
            pl.BlockSpec((None, t, KV_HEADS * d), lambda bi, gi, i: (bi, 0, 0)),
            pl.BlockSpec((None, None, t // tk, SEL_V_ROWS, tk), lambda bi, gi, i: (bi, gi, 0, 0, 0)),
            pl.BlockSpec((None, None, ns, qw), lambda bi, gi, i: (bi, gi, 0, i)),
            pl.BlockSpec((None, None, N_GATE_ROWS, qw), lambda bi, gi, i: (bi, gi, 0, i)),
        ],
        out_specs=pl.BlockSpec((None, qw, GROUP_R * d), lambda bi, gi, i: (bi, i, gi)),
        out_shape=jax.ShapeDtypeStruct((b, t, h * d), F32),
        scratch_shapes=[pltpu.VMEM((nq, tk, rq), F32), pltpu.VMEM((nq, tk, rq), F32),
                        pltpu.VMEM((nq, 1, rq), F32), pltpu.VMEM((nq, 1, rq), F32),
                        pltpu.VMEM((nq, 1, rq), F32), pltpu.VMEM((nq, SEL_V_ROWS, rq), F32)],
        compiler_params=_params(("parallel", "parallel", "arbitrary"), VMEM_LIMIT),
        name="nsa_selected",
    )(qt, ks_rows, vs_t, sel_t, gates)


def _retention_body(xn, ti, lg_ref, w_ref, cos_ref, sin_ref, ng_ref, o_ref, r_ref):
    c = RET_CHUNK
    h, dk, dv = RET_HEADS, RET_QK_DIM, RET_V_DIM
    half = dk // 2
    k0, v0, g0 = h * dk, 2 * h * dk, 2 * h * dk + h * dv

    @pl.when(ti == 0)
    def _():
        r_ref[...] = jnp.zeros_like(r_ref)

    acc = _dot(xn, w_ref[...])
    ii = lax.broadcasted_iota(jnp.int32, (c, c), 0)
    jj = lax.broadcasted_iota(jnp.int32, (c, c), 1)
    dist = (ii - jj).astype(F32)
    jcol = lax.broadcasted_iota(jnp.int32, (c, 1), 0).astype(F32)
    for hh in range(h):
        lg = lg_ref[hh]
        dmask = jnp.where(dist >= 0, jnp.exp(dist * lg), 0.0)
        xi = jnp.exp((jcol + 1.0) * lg)
        zeta = jnp.exp((c - 1.0 - jcol) * lg)
        decay = jnp.exp(jnp.zeros((1, dv), F32) + c * lg)
        for cc in range(acc.shape[0] // c):
            rows = slice(cc * c, (cc + 1) * c)
            cosf = cos_ref[rows, :]
            sinf = sin_ref[rows, :]

            def rot(x):
                return x * cosf + jnp.concatenate([x[:, half:], x[:, :half]], axis=1) * sinf

            q = rot(acc[rows, hh * dk:(hh + 1) * dk])
            k = rot(acc[rows, k0 + hh * dk:k0 + (hh + 1) * dk]) * (dk ** -0.5)
            vb = acc[rows, v0 + hh * dv:v0 + (hh + 1) * dv].astype(BF16)
            att = _dot_nt(q.astype(BF16), k.astype(BF16)) * dmask
            o = _dot(att.astype(BF16), vb)
            r_prev = r_ref[hh]
            o = o + _dot((q * xi).astype(BF16), r_prev.astype(BF16))
            s_chunk = _dot_tn((k * zeta).astype(BF16), vb)
            r_ref[hh] = r_prev * decay + s_chunk
            mu = jnp.mean(o, axis=-1, keepdims=True)
            var = jnp.mean(jnp.square(o - mu), axis=-1, keepdims=True)
            on = (o - mu) * lax.rsqrt(var + EPS)
            gt = acc[rows, g0 + hh * dv:g0 + (hh + 1) * dv]
            o_ref[rows, hh * dv:(hh + 1) * dv] = (
                (gt * _sigmoid(gt)) * (on * ng_ref[:, hh * dv:(hh + 1) * dv]))


def _merge_kernel(x_ref, g_ref, wgate_ref, bias_ref, ya_ref, yb_ref, yc0_ref, yc1_ref, yc2_ref,
                  yd_ref, wb_ref, wo_ref, o_ref):
    x = x_ref[...]
    d = x.shape[1]
    u = _rms_rows(x, g_ref[...]).astype(BF16)
    ys = (ya_ref[...], yb_ref[...], yc0_ref[...] + yc1_ref[...] + yc2_ref[...], yd_ref[...])
    merged = jnp.zeros(x.shape, F32)
    for n in range(N_BRANCH):
        logits = _dot(u, wgate_ref[:, n * d:(n + 1) * d]) + bias_ref[:, n * d:(n + 1) * d]
        merged = merged + _sigmoid(logits) * _dot(ys[n].astype(BF16), wb_ref[n])
    o_ref[...] = x + _dot(merged.astype(BF16), wo_ref[...])


def _merge(x2, g, w_gate, bias, ys, w_branch, w_out, tm=512):
    n, d = x2.shape
    w = MIX_W
    row = lambda width: pl.BlockSpec((tm, width), lambda i: (i, 0))
    full = lambda shape: pl.BlockSpec(shape, lambda i: tuple(0 for _ in shape),
                                      pipeline_mode=pl.Buffered(1))
    return pl.pallas_call(
        _merge_kernel,
        grid=(n // tm,),
        in_specs=[row(d), full((1, d)), full((d, N_BRANCH * d)), full((1, N_BRANCH * d))]
        + [row(w)] * 6 + [full((N_BRANCH, w, d)), full((d, d))],
        out_specs=row(d),
        out_shape=jax.ShapeDtypeStruct((n, d), F32),
        compiler_params=_params(("parallel",), VMEM_LIMIT),
        name="merge_out",
    )(x2, g.reshape(1, d), w_gate.astype(BF16), bias.reshape(1, N_BRANCH * d),
      *[y.reshape(n, w) for y in ys], w_branch.astype(BF16), w_out.astype(BF16))


def _mixers(x2, b, t, mix_norm, w_in, merge_gate_bias, swa_q_gain, swa_k_gain, swa_sinks, conv_w,
            nsa_q_gain, nsa_k_gain, cmp_pos_k, cmp_pos_v, cmp_wk1, cmp_wk2, cmp_wv1, cmp_wv2,
            ret_norm_gain, w_branch, w_out):
    hd = HEAD_DIM
    swa_q, swa_kv = SWA_HEADS * hd, SWA_KV_HEADS * hd
    nsa_q, nsa_kv = NSA_HEADS * hd, NSA_KV_HEADS * hd
    ret_qk, ret_v = RET_HEADS * RET_QK_DIM, RET_HEADS * RET_V_DIM
    n_gate = NSA_HEADS * 3
    o_a = 0
    o_b = o_a + swa_q + 2 * swa_kv
    o_c = o_b + 3 * MIX_W
    o_cg = o_c + nsa_q + 6 * nsa_kv
    o_d = o_cg + n_gate
    o_g = o_d + 2 * ret_qk + 2 * ret_v

    per_g = GROUP_R * 3
    w_gates = [jnp.pad(w_in[:, o_cg + gi * per_g:o_cg + (gi + 1) * per_g],
                       ((0, 0), (0, N_GATE_ROWS - per_g))) for gi in range(NSA_KV_HEADS)]
    w_t = jnp.concatenate([w_in[:, o_a:o_b], w_in[:, o_c:o_cg]] + w_gates, axis=1).T.astype(BF16)
    head_gains = jnp.stack([swa_q_gain, swa_k_gain, nsa_q_gain, nsa_k_gain[1], nsa_k_gain[2]])
    head_gains = jnp.broadcast_to(head_gains.astype(F32)[:, :, None], (5, hd, LANES))
    (a_qt, a_k, a_vt, c_qt, c_kc, c_vc, c_ks, c_vst, c_kw, c_vwt, gates_c, y_b, y_d) = _mixer_proj(
        x2, b, t, mix_norm, w_t, head_gains, w_in[:, o_b:o_c], conv_w, w_in[:, o_d:o_g],
        ret_norm_gain)

    y_a = _banded(a_qt, a_k, a_vt, SWA_WINDOW, sinks=swa_sinks)

    k_cmp, v_cmp = _compress(c_kc, c_vc, cmp_pos_k, cmp_pos_v, cmp_wk1, cmp_wk2, cmp_wv1, cmp_wv2,
                             nsa_k_gain[0])
    y_cmp, sel_t = _cmp_attn(c_qt, k_cmp, v_cmp.transpose(0, 2, 1), gates_c)
    y_sel = _sel_attn(c_qt, c_ks, c_vst, sel_t, gates_c)
    y_win = _banded(c_qt, c_kw, c_vwt, NSA_WINDOW, gates=gates_c, gate_branch=2)

    return _merge(x2, mix_norm, w_in[:, o_g:], merge_gate_bias,
                  (y_a, y_b, y_cmp, y_sel, y_win, y_d), w_branch, w_out)


def kernel(x, ffn1_norm, ffn1_w_gate, ffn1_w_up, ffn1_w_down, mix_norm, w_in, merge_gate_bias, swa_q_gain, swa_k_gain, swa_sinks, conv_w, nsa_q_gain, nsa_k_gain, cmp_pos_k, cmp_pos_v, cmp_wk1, cmp_wk2, cmp_wv1, cmp_wv2, ret_norm_gain, w_branch, w_out, ffn2_norm, ffn2_w_gate, ffn2_w_up, ffn2_w_down):
    b, t, d = x.shape
    x2 = x.reshape(b * t, d)
    for l in range(ffn1_norm.shape[0]):
        x2 = _ffn(x2, ffn1_norm[l], ffn1_w_gate[l], ffn1_w_up[l], ffn1_w_down[l])
        x2 = _mixers(x2, b, t, mix_norm[l], w_in[l], merge_gate_bias[l], swa_q_gain[l],
                     swa_k_gain[l], swa_sinks[l], conv_w[l], nsa_q_gain[l], nsa_k_gain[l],
                     cmp_pos_k[l], cmp_pos_v[l], cmp_wk1[l], cmp_wk2[l], cmp_wv1[l], cmp_wv2[l],
                     ret_norm_gain[l], w_branch[l], w_out[l])
        x2 = _ffn(x2, ffn2_norm[l], ffn2_w_gate[l], ffn2_w_up[l], ffn2_w_down[l])
    return x2.reshape(b, t, d)
```

```python
import functools

import jax
import jax.numpy as jnp
from jax import lax
from jax.experimental import pallas as pl
from jax.experimental.pallas import tpu as pltpu

F32 = jnp.float32
BF16 = jnp.bfloat16

HEAD_DIM = 64
Q_BLOCK = 128
MIX_W = 512
N_BRANCH = 4
SWA_HEADS = 8
SWA_KV_HEADS = 2
SWA_WINDOW = 128
NSA_HEADS = 8
NSA_KV_HEADS = 2
CMP_BLOCK = 32
CMP_STRIDE = 16
SEL_BLOCK = 64
SEL_TOPK = 16
N_FORCED = 3
NSA_WINDOW = 512
RET_HEADS = 4
RET_QK_DIM = 64
RET_V_DIM = 128
RET_CHUNK = 128
ROPE_BASE = 10000.0
EPS = 1e-6
KV_HEADS = SWA_KV_HEADS
GROUP_R = SWA_HEADS // SWA_KV_HEADS
assert (NSA_KV_HEADS, NSA_HEADS // NSA_KV_HEADS) == (KV_HEADS, GROUP_R)

LANES = 128
BF16_ROWS = 16
N_GATE_ROWS = 16
SEL_KEY_TILE = 512
SEL_V_ROWS = HEAD_DIM + BF16_ROWS
PROJ_TM = 512
BANDED_QB = 16
CMP_QB = 8
SEL_QB = 4
VMEM_LIMIT = 52 * 1024 * 1024

NEG_INF = float("-inf")
LOG2_E = 1.4426950408889634


def _params(sem, vmem=None):
    return pltpu.CompilerParams(dimension_semantics=sem, vmem_limit_bytes=vmem)


def _sigmoid(x):
    return 1.0 / (1.0 + jnp.exp(-x))


def _rms_rows(x, g):
    return x * lax.rsqrt(jnp.mean(x * x, axis=-1, keepdims=True) + EPS) * g


def _dot(a, b):
    return jnp.dot(a, b, preferred_element_type=F32)


def _dot_nt(a, b):
    return lax.dot_general(a, b, (((1,), (1,)), ((), ())), preferred_element_type=F32)


def _dot_tn(a, b):
    return lax.dot_general(a, b, (((0,), (0,)), ((), ())), preferred_element_type=F32)


def _ffn_kernel(x_ref, g_ref, wg_ref, wu_ref, wd_ref, o_ref, *, n_chunks):
    x = x_ref[...]
    xn = _rms_rows(x, g_ref[...]).astype(BF16)
    tf = wg_ref.shape[1] // n_chunks
    acc = None
    for f in range(n_chunks):
        a = _dot(xn, wg_ref[:, f * tf:(f + 1) * tf])
        b = _dot(xn, wu_ref[:, f * tf:(f + 1) * tf])
        h = ((a * _sigmoid(a)) * b).astype(BF16)
        part = _dot(h, wd_ref[f * tf:(f + 1) * tf, :])
        acc = part if acc is None else acc + part
    o_ref[...] = x + 0.5 * acc


def _ffn(x2, g, wg, wu, wd, tm=1024, n_chunks=11):
    n, d = x2.shape
    dff = wg.shape[1]
    assert dff % (n_chunks * LANES) == 0
    resident = lambda shape: pl.BlockSpec(shape, lambda i: (0, 0), pipeline_mode=pl.Buffered(1))
    return pl.pallas_call(
        functools.partial(_ffn_kernel, n_chunks=n_chunks),
        grid=(n // tm,),
        in_specs=[
            pl.BlockSpec((tm, d), lambda i: (i, 0)),
            pl.BlockSpec((1, d), lambda i: (0, 0)),
            resident((d, dff)), resident((d, dff)), resident((dff, d)),
        ],
        out_specs=pl.BlockSpec((tm, d), lambda i: (i, 0)),
        out_shape=jax.ShapeDtypeStruct((n, d), F32),
        compiler_params=_params(("parallel",), VMEM_LIMIT),
        name="ffn",
    )(x2, g.reshape(1, d), wg.astype(BF16), wu.astype(BF16), wd.astype(BF16))


def _heads_body(xn, wt_ref, hg_ref, aq_ref, ak_ref, av_ref, cq_ref, ckc_ref, cvc_ref, cks_ref,
                cvs_ref, ckw_ref, cvw_ref, gt_ref):
    d = HEAD_DIM
    acc = _dot_nt(wt_ref[...], xn)
    tm = xn.shape[0]
    lane_tiles = tm // LANES
    scale = d ** -0.5
    kv_w = KV_HEADS * d

    def head_norm(row0, gain_idx, mult):
        hb = acc[row0:row0 + d]
        gain = jnp.concatenate([hg_ref[gain_idx]] * lane_tiles, axis=1)
        y = hb * lax.rsqrt(jnp.mean(hb * hb, axis=0, keepdims=True) + EPS) * gain
        return y * mult if mult != 1.0 else y

    def q_heads(row0, gain_idx, out_ref, n_heads):
        for h in range(n_heads):
            out_ref[h] = head_norm(row0 + h * d, gain_idx, scale * LOG2_E).astype(BF16)

    def k_rows(row0, gain_idx):
        return jnp.concatenate([head_norm(row0 + g * d, gain_idx, 1.0) for g in range(KV_HEADS)],
                               axis=0).T

    def v_tiles(row0, out_ref):
        for g in range(KV_HEADS):
            for u in range(tm // Q_BLOCK):
                out_ref[g, u, 0:d, :] = acc[row0 + g * d:row0 + (g + 1) * d,
                                            u * Q_BLOCK:(u + 1) * Q_BLOCK].astype(BF16)
                out_ref[g, u, d:SEL_V_ROWS, :] = jnp.ones((SEL_V_ROWS - d, Q_BLOCK), BF16)

    row = 0
    q_heads(row, 0, aq_ref, SWA_HEADS)
    row += SWA_HEADS * d
    ak_ref[...] = k_rows(row, 1).astype(BF16)
    row += kv_w
    v_tiles(row, av_ref)
    row += kv_w
    q_heads(row, 2, cq_ref, NSA_HEADS)
    row += NSA_HEADS * d
    ckc_ref[...] = acc[row:row + kv_w].T
    row += kv_w
    cvc_ref[...] = acc[row:row + kv_w].T
    row += kv_w
    cks_ref[...] = k_rows(row, 3).astype(BF16)
    row += kv_w
    for g in range(KV_HEADS):
        cvs_ref[g, 0, 0:d, :] = acc[row + g * d:row + (g + 1) * d].astype(BF16)
        cvs_ref[g, 0, d:SEL_V_ROWS, :] = jnp.ones((SEL_V_ROWS - d, tm), BF16)
    row += kv_w
    ckw_ref[...] = k_rows(row, 4).astype(BF16)
    row += kv_w
    v_tiles(row, cvw_ref)
    row += kv_w
    for g in range(KV_HEADS):
        gt_ref[g] = _sigmoid(acc[row + g * N_GATE_ROWS:row + (g + 1) * N_GATE_ROWS])


N_HEAD_OUTS = 11


def _mixer_proj_kernel(lg_ref, x_ref, g_ref, wt_ref, hg_ref, wb_ref, cw_ref, wd_ref, cos_ref,
                       sin_ref, ng_ref, *rest):
    head_outs = rest[:N_HEAD_OUTS]
    yb_ref, yd_ref, tail_ref, r_ref = rest[N_HEAD_OUTS:]
    ti = pl.program_id(1)
    xn = _rms_rows(x_ref[...], g_ref[...]).astype(BF16)
    _heads_body(xn, wt_ref, hg_ref, *head_outs)
    _conv_body(xn, ti, wb_ref, cw_ref, yb_ref, tail_ref)
    _retention_body(xn, ti, lg_ref, wd_ref, cos_ref, sin_ref, ng_ref, yd_ref, r_ref)


def _mixer_proj(x2, b, t, g, w_t, head_gains, w_b, conv_w, w_d, norm_gain):
    n, dm = x2.shape
    d = HEAD_DIM
    tm = PROJ_TM
    assert tm == SEL_KEY_TILE and t % tm == 0
    tpb = t // tm
    q = Q_BLOCK
    kv = KV_HEADS
    w = MIX_W
    h, dk, dv = RET_HEADS, RET_QK_DIM, RET_V_DIM
    half = dk // 2
    inv = ROPE_BASE ** (-jnp.arange(half, dtype=F32) / half)
    ang = jnp.arange(t).astype(F32)[:, None] * inv[None, :]
    cos = jnp.cos(ang)
    sin = jnp.sin(ang)
    cosf = jnp.concatenate([cos, cos], axis=-1)
    sinf = jnp.concatenate([-sin, sin], axis=-1)
    log_gamma = jnp.log(1.0 - 2.0 ** (-5.0 - jnp.arange(h, dtype=F32)))
    conv_rows = jnp.pad(conv_w.reshape(conv_w.shape[0], w).astype(F32),
                        ((0, 8 - conv_w.shape[0]), (0, 0)))
    resident = lambda a: pl.BlockSpec(a.shape, lambda bi, ti: tuple(0 for _ in a.shape),
                                      pipeline_mode=pl.Buffered(1))
    qt_spec = lambda nh: pl.BlockSpec((None, nh, d, tm), lambda bi, ti: (bi, 0, 0, ti))
    row_spec = pl.BlockSpec((None, tm, kv * d), lambda bi, ti: (bi, ti, 0))
    vt_spec = pl.BlockSpec((None, kv, tm // q, SEL_V_ROWS, q), lambda bi, ti: (bi, 0, ti, 0, 0))
    y_spec = pl.BlockSpec((None, tm, w), lambda bi, ti: (bi, ti, 0))
    sds = jax.ShapeDtypeStruct
    w_b16, w_d16 = w_b.astype(BF16), w_d.astype(BF16)
    return pl.pallas_call(
        _mixer_proj_kernel,
        grid=(b, tpb),
        in_specs=[
            pl.BlockSpec(memory_space=pltpu.SMEM),
            pl.BlockSpec((tm, dm), lambda bi, ti: (bi * tpb + ti, 0)),
            pl.BlockSpec((1, dm), lambda bi, ti: (0, 0)),
            resident(w_t), pl.BlockSpec(head_gains.shape, lambda bi, ti: (0, 0, 0)),
            resident(w_b16), pl.BlockSpec((8, w), lambda bi, ti: (0, 0)),
            resident(w_d16),
            pl.BlockSpec((tm, dk), lambda bi, ti: (ti, 0)),
            pl.BlockSpec((tm, dk), lambda bi, ti: (ti, 0)),
            pl.BlockSpec((1, h * dv), lambda bi, ti: (0, 0)),
        ],
        out_specs=[
            qt_spec(SWA_HEADS), row_spec, vt_spec,
            qt_spec(NSA_HEADS), row_spec, row_spec, row_spec,
            pl.BlockSpec((None, kv, 1, SEL_V_ROWS, tm), lambda bi, ti: (bi, 0, ti, 0, 0)),
            row_spec, vt_spec,
            pl.BlockSpec((None, kv, N_GATE_ROWS, tm), lambda bi, ti: (bi, 0, 0, ti)),
            y_spec, y_spec,
        ],
        out_shape=[
            sds((b, SWA_HEADS, d, t), BF16), sds((b, t, kv * d), BF16),
            sds((b, kv, t // q, SEL_V_ROWS, q), BF16),
            sds((b, NSA_HEADS, d, t), BF16), sds((b, t, kv * d), F32), sds((b, t, kv * d), F32),
            sds((b, t, kv * d), BF16), sds((b, kv, tpb, SEL_V_ROWS, tm), BF16),
            sds((b, t, kv * d), BF16), sds((b, kv, t // q, SEL_V_ROWS, q), BF16),
            sds((b, kv, N_GATE_ROWS, t), F32),
            sds((b, t, w), F32), sds((b, t, h * dv), F32),
        ],
        scratch_shapes=[pltpu.VMEM((8, w), F32), pltpu.VMEM((h, dk, dv), F32)],
        compiler_params=_params(("parallel", "arbitrary"), VMEM_LIMIT),
        name="mixer_proj",
    )(log_gamma, x2, g.reshape(1, dm), w_t, head_gains, w_b16, conv_rows, w_d16, cosf, sinf,
      norm_gain.reshape(1, h * dv))


def _group_q(qt_ref, qb=0):
    return jnp.concatenate(
        [qt_ref[r, :, qb * Q_BLOCK:(qb + 1) * Q_BLOCK] for r in range(GROUP_R)], axis=1)


def _padded_q(qt_ref, g, qb=0):
    q4 = _group_q(qt_ref, qb)
    z = jnp.zeros_like(q4)
    return jnp.where(g == 0, jnp.concatenate([q4, z], axis=0), jnp.concatenate([z, q4], axis=0))


def _gate_untranspose(o_t, gate_ref, branch, qb=0):
    outs = []
    for r in range(GROUP_R):
        blk = o_t[:, r * Q_BLOCK:(r + 1) * Q_BLOCK]
        if gate_ref is not None:
            c = r * 3 + branch
            blk = blk * gate_ref[c:c + 1, qb * Q_BLOCK:(qb + 1) * Q_BLOCK]
        outs.append(blk.T)
    return jnp.concatenate(outs, axis=1)


def _window_block(qpad, i, k_ref, vt_ref, window):
    n_sub = window // Q_BLOCK + 1
    span = n_sub * Q_BLOCK
    start = pl.multiple_of(jnp.maximum(i * Q_BLOCK - window, 0), Q_BLOCK)
    st = _dot(k_ref[pl.ds(start, span), :], qpad)
    row = lax.broadcasted_iota(jnp.int32, (span, Q_BLOCK), 0)
    qcol = lax.broadcasted_iota(jnp.int32, (span, Q_BLOCK), 1)
    diff = (i * Q_BLOCK - start) + qcol - row
    bias = jnp.where((diff >= 0) & (diff < window), 0.0, NEG_INF)
    st = st + jnp.concatenate([bias] * GROUP_R, axis=1)
    m = jnp.max(st, axis=0, keepdims=True)
    m = jnp.where(m == NEG_INF, 0.0, m)
    pb = jnp.exp2(st - m).astype(BF16)
    u0 = start // Q_BLOCK
    o_t = _dot(vt_ref[u0], pb[0:Q_BLOCK])
    for u in range(1, n_sub):
        o_t = o_t + _dot(vt_ref[u0 + u], pb[u * Q_BLOCK:(u + 1) * Q_BLOCK])
    return o_t[0:HEAD_DIM, :] / jnp.maximum(o_t[HEAD_DIM:HEAD_DIM + 1, :], 1e-30)


def _banded_kernel(*refs, window, has_sink, gate_branch):
    refs = list(refs)
    sink_ref = refs.pop(0) if has_sink else None
    qt_ref, k_ref, vt_ref = refs[:3]
    gate_ref = refs[3] if gate_branch is not None else None
    o_ref = refs[-1]
    g = pl.program_id(1)
    rq = GROUP_R * Q_BLOCK
    n_sub = window // Q_BLOCK + 1
    span = n_sub * Q_BLOCK
    if has_sink:
        lrow = lax.broadcasted_iota(jnp.int32, (1, rq), 1)
        sink = jnp.zeros((1, rq), F32)
        for r in range(GROUP_R):
            sink = jnp.where((lrow >= r * Q_BLOCK) & (lrow < (r + 1) * Q_BLOCK),
                             sink_ref[g * GROUP_R + r], sink)
        sink = sink * LOG2_E
    step = pl.program_id(2)
    row = lax.broadcasted_iota(jnp.int32, (span, Q_BLOCK), 0)
    qcol = lax.broadcasted_iota(jnp.int32, (span, Q_BLOCK), 1)

    def band_bias(offset):
        diff = offset + qcol - row
        return jnp.where((diff >= 0) & (diff < window), 0.0, NEG_INF)

    for qb in range(BANDED_QB):
        i = step * BANDED_QB + qb
        start = pl.multiple_of(jnp.maximum(i * Q_BLOCK - window, 0), Q_BLOCK)
        st = _dot(k_ref[pl.ds(start, span), :], _padded_q(qt_ref, g, qb))
        st = st + jnp.concatenate([band_bias(i * Q_BLOCK - start)] * GROUP_R, axis=1)
        m = jnp.max(st, axis=0, keepdims=True)
        if has_sink:
            m = jnp.maximum(m, sink)
        m = jnp.where(m == NEG_INF, 0.0, m)
        pb = jnp.exp2(st - m).astype(BF16)
        u0 = start // Q_BLOCK
        o_t = _dot(vt_ref[u0], pb[0:Q_BLOCK])
        for u in range(1, n_sub):
            o_t = o_t + _dot(vt_ref[u0 + u], pb[u * Q_BLOCK:(u + 1) * Q_BLOCK])
        denom = o_t[HEAD_DIM:HEAD_DIM + 1, :]
        if has_sink:
            denom = denom + jnp.exp2(sink - m)
        o_t = o_t[0:HEAD_DIM, :] / jnp.maximum(denom, 1e-30)
        o_ref[qb * Q_BLOCK:(qb + 1) * Q_BLOCK, :] = _gate_untranspose(
            o_t, gate_ref, gate_branch, qb)


def _banded(qt, k_rows, vt, window, sinks=None, gates=None, gate_branch=None):
    b, h, d, t = qt.shape
    g = h // GROUP_R
    in_specs = []
    args = []
    if sinks is not None:
        in_specs.append(pl.BlockSpec(memory_space=pltpu.SMEM))
        args.append(sinks.astype(F32))
    qw = BANDED_QB * Q_BLOCK
    in_specs += [
        pl.BlockSpec((None, GROUP_R, d, qw), lambda bi, gi, i: (bi, gi, 0, i)),
        pl.BlockSpec((None, t, KV_HEADS * d), lambda bi, gi, i: (bi, 0, 0)),
        pl.BlockSpec((None, None, t // Q_BLOCK, SEL_V_ROWS, Q_BLOCK),
                     lambda bi, gi, i: (bi, gi, 0, 0, 0)),
    ]
    args += [qt, k_rows, vt]
    if gates is not None:
        in_specs.append(pl.BlockSpec((None, None, N_GATE_ROWS, qw),
                                     lambda bi, gi, i: (bi, gi, 0, i)))
        args.append(gates)
    return pl.pallas_call(
        functools.partial(_banded_kernel, window=window, has_sink=sinks is not None,
                          gate_branch=gate_branch if gates is not None else None),
        grid=(b, g, t // qw),
        in_specs=in_specs,
        out_specs=pl.BlockSpec((None, qw, GROUP_R * d), lambda bi, gi, i: (bi, i, gi)),
        out_shape=jax.ShapeDtypeStruct((b, t, h * d), F32),
        compiler_params=_params(("parallel", "parallel", "arbitrary"), VMEM_LIMIT),
        name="banded_attn_w%d" % window,
    )(*args)


def _conv_body(xn, ti, w_ref, cw_ref, o_ref, tail_ref):
    w = MIX_W
    acc = _dot(xn, w_ref[...])
    z = acc[:, 2 * w:3 * w] * acc[:, 0:w]
    zp = jnp.where(ti > 0, tail_ref[...], 0.0)
    row = lax.broadcasted_iota(jnp.int32, z.shape, 0)
    z1 = jnp.where(row == 0, zp[7:8, :], pltpu.roll(z, 1, 0))
    z2 = pltpu.roll(z, 2, 0)
    z2 = jnp.where(row == 0, zp[6:7, :], jnp.where(row == 1, zp[7:8, :], z2))
    cw = cw_ref[...]
    o_ref[...] = acc[:, w:2 * w] * (cw[0:1, :] * z2 + cw[1:2, :] * z1 + cw[2:3, :] * z)
    tail_ref[...] = z[z.shape[0] - 8:, :]


def _gelu_tanh(x):
    return x * (0.5 * (1.0 + jnp.tanh(0.7978845608028654 * (x + 0.044715 * (x * x * x)))))


def _compress_kernel(tk_ref, tv_ref, pek_ref, pev_ref, wk1_ref, wk2_ref, wv1_ref, wv2_ref,
                     kg_ref, ko_ref, vo_ref):
    nrow = tk_ref.shape[0]

    def mlp(a, pe_ref, w1_ref, w2_ref):
        a0 = (a + pe_ref[0:1, :]).astype(BF16)
        a1 = (a + pe_ref[1:2, :]).astype(BF16)
        p1 = _dot(a0, w1_ref[0])
        p2 = _dot(a1, w1_ref[1])
        hdn = p1 + pltpu.roll(p2, nrow - 1, 0)
        return _dot(_gelu_tanh(hdn).astype(BF16), w2_ref[...])

    kc = mlp(tk_ref[...], pek_ref, wk1_ref, wk2_ref)
    ko_ref[...] = _rms_rows(kc, kg_ref[...]).astype(BF16)
    vo_ref[...] = mlp(tv_ref[...], pev_ref, wv1_ref, wv2_ref).astype(BF16)


def _compress(kc_rows, vc_rows, pos_k, pos_v, wk1, wk2, wv1, wv2, k_gain):
    b, t, kvd = kc_rows.shape
    kv = KV_HEADS
    d = kvd // kv
    nrow = t // CMP_STRIDE
    wide = CMP_STRIDE * kvd
    hid = wk1.shape[1]

    def expand_w1(w1):
        w = w1.reshape(2, CMP_STRIDE, 1, d, hid)
        per_head = []
        for g in range(kv):
            pads = [jnp.zeros_like(w)] * kv
            pads[g] = w
            per_head.append(jnp.concatenate(pads, axis=2).reshape(2, wide, hid))
        return jnp.stack(per_head).astype(BF16)

    def expand_pe(pe):
        return jnp.broadcast_to(pe.reshape(2, CMP_STRIDE, 1, d), (2, CMP_STRIDE, kv, d)).reshape(2, wide)

    tok = pl.BlockSpec((None, nrow, wide), lambda bi, gi: (bi, 0, 0))
    full = lambda shape: pl.BlockSpec(shape, lambda bi, gi: tuple(0 for _ in shape))
    w1_spec = pl.BlockSpec((None, 2, wide, hid), lambda bi, gi: (gi, 0, 0, 0))
    out = pl.BlockSpec((None, nrow, d), lambda bi, gi: (bi * kv + gi, 0, 0))
    return pl.pallas_call(
        _compress_kernel,
        grid=(b, kv),
        in_specs=[tok, tok, full((2, wide)), full((2, wide)), w1_spec, full((hid, d)),
                  w1_spec, full((hid, d)), full((1, d))],
        out_specs=[out, out],
        out_shape=[jax.ShapeDtypeStruct((b * kv, nrow, d), BF16),
                   jax.ShapeDtypeStruct((b * kv, nrow, d), BF16)],
        compiler_params=_params(("parallel", "arbitrary"), VMEM_LIMIT),
        name="nsa_compress",
    )(kc_rows.reshape(b, nrow, wide), vc_rows.reshape(b, nrow, wide), expand_pe(pos_k),
      expand_pe(pos_v), expand_w1(wk1), wk2.astype(BF16), expand_w1(wv1), wv2.astype(BF16),
      k_gain.reshape(1, d))


def _cmp_attn_kernel(qt_ref, kc_ref, vct_ref, gate_ref, o_ref, selt_ref, imp_ref, *, sel_k):
    step = pl.program_id(2)
    rq = GROUP_R * Q_BLOCK
    nc = kc_ref.shape[0]
    ns = selt_ref.shape[0]
    blk = lax.broadcasted_iota(jnp.int32, (ns, Q_BLOCK), 0)

    def forced_blocks(i):
        cur = (i * Q_BLOCK + lax.broadcasted_iota(jnp.int32, (ns, Q_BLOCK), 1)) // SEL_BLOCK
        return (blk == 0) | (blk == cur) | (blk == cur - 1), cur

    def attend(nc_eff):
        n = lax.broadcasted_iota(jnp.int32, (nc_eff, rq), 0)
        lane = lax.broadcasted_iota(jnp.int32, (nc_eff, rq), 1)
        end_minus_q = n * CMP_STRIDE + (CMP_BLOCK - 1) - (lane & (Q_BLOCK - 1))
        ss = lax.broadcasted_iota(jnp.int32, (ns, nc_eff), 0) * SEL_BLOCK
        nn = lax.broadcasted_iota(jnp.int32, (ns, nc_eff), 1) * CMP_STRIDE
        overlap_t = jnp.where((nn < ss + SEL_BLOCK) & (nn + (CMP_BLOCK - 1) >= ss),
                              1.0, 0.0).astype(BF16)
        lhs = jnp.concatenate([vct_ref[:, 0:nc_eff], overlap_t], axis=0)
        for qb in range(CMP_QB):
            i = step * CMP_QB + qb
            st = _dot(kc_ref[0:nc_eff, :], _group_q(qt_ref, qb))
            st = jnp.where(end_minus_q <= i * Q_BLOCK, st, NEG_INF)
            m = jnp.max(st, axis=0, keepdims=True)
            m = jnp.where(m == NEG_INF, 0.0, m)
            p = jnp.exp2(st - m)
            denom = jnp.sum(p, axis=0, keepdims=True)
            pb = (p / jnp.maximum(denom, 1e-30)).astype(BF16)
            res = _dot(lhs, pb)
            o_ref[qb * Q_BLOCK:(qb + 1) * Q_BLOCK, :] = _gate_untranspose(
                res[0:HEAD_DIM], gate_ref, 0, qb)
            imp = res[HEAD_DIM:, 0:Q_BLOCK]
            for r in range(1, GROUP_R):
                imp = imp + res[HEAD_DIM:, r * Q_BLOCK:(r + 1) * Q_BLOCK]
            forced, cur = forced_blocks(i)
            imp_ref[qb] = jnp.where(forced | (blk > cur), NEG_INF, imp)

    variant_rows = LANES
    n_var = nc // variant_rows
    n_ending = (step + 1) * (CMP_QB * Q_BLOCK // CMP_STRIDE) - 1
    variant = jnp.minimum((n_ending + variant_rows - 1) // variant_rows, n_var) - 1
    for v in range(n_var):
        pl.when(variant == v)(functools.partial(attend, (v + 1) * variant_rows))

    blkf = blk.astype(F32)
    imps = [imp_ref[qb] for qb in range(CMP_QB)]
    sels = [jnp.where(forced_blocks(step * CMP_QB + qb)[0], 1.0, 0.0) for qb in range(CMP_QB)]
    for _ in range(sel_k - N_FORCED):
        for qb in range(CMP_QB):
            mx = jnp.max(imps[qb], axis=0, keepdims=True)
            first = jnp.min(jnp.where(imps[qb] == mx, blkf, float(ns)), axis=0, keepdims=True)
            hit = blkf == first
            sels[qb] = jnp.where(hit & (mx > NEG_INF), 1.0, sels[qb])
            imps[qb] = jnp.where(hit, NEG_INF, imps[qb])
    for qb in range(CMP_QB):
        selt_ref[:, qb * Q_BLOCK:(qb + 1) * Q_BLOCK] = sels[qb]


def _cmp_attn(qt, k_cmp, v_cmp_t, gates):
    b, h, d, t = qt.shape
    g = h // GROUP_R
    nc = k_cmp.shape[1]
    ns = t // SEL_BLOCK
    qw = CMP_QB * Q_BLOCK
    assert nc % LANES == 0 and ns >= N_FORCED
    return pl.pallas_call(
        functools.partial(_cmp_attn_kernel, sel_k=min(SEL_TOPK, ns)),
        grid=(b, g, t // qw),
        in_specs=[
            pl.BlockSpec((None, GROUP_R, d, qw), lambda bi, gi, i: (bi, gi, 0, i)),
            pl.BlockSpec((None, nc, d), lambda bi, gi, i: (bi * KV_HEADS + gi, 0, 0)),
            pl.BlockSpec((None, d, nc), lambda bi, gi, i: (bi * KV_HEADS + gi, 0, 0)),
            pl.BlockSpec((None, None, N_GATE_ROWS, qw), lambda bi, gi, i: (bi, gi, 0, i)),
        ],
        out_specs=[
            pl.BlockSpec((None, qw, GROUP_R * d), lambda bi, gi, i: (bi, i, gi)),
            pl.BlockSpec((None, None, ns, qw), lambda bi, gi, i: (bi, gi, 0, i)),
        ],
        out_shape=[jax.ShapeDtypeStruct((b, t, h * d), F32),
                   jax.ShapeDtypeStruct((b, g, ns, t), F32)],
        scratch_shapes=[pltpu.VMEM((CMP_QB, ns, Q_BLOCK), F32)],
        compiler_params=_params(("parallel", "parallel", "arbitrary"), VMEM_LIMIT),
        name="nsa_cmp_topk",
    )(qt, k_cmp, v_cmp_t, gates)


def _sel_attn_kernel(qt_ref, ks_ref, vst_ref, selt_ref, gate_ref, kw_ref, vwt_ref, o_ref, sa_ref,
                     sb_ref, mta_ref, mtb_ref, m_ref, acc_ref):
    g = pl.program_id(1)
    tk = SEL_KEY_TILE
    bpt = tk // SEL_BLOCK
    spt = tk // Q_BLOCK
    d = HEAD_DIM
    m_ref[...] = jnp.full(m_ref.shape, NEG_INF, F32)
    acc_ref[...] = jnp.zeros_like(acc_ref)
    last_q = (pl.program_id(2) + 1) * SEL_QB * Q_BLOCK
    n_tiles = (last_q + tk - 1) // tk
    qpads = [_padded_q(qt_ref, g, qb) for qb in range(SEL_QB)]
    tri = jnp.where(lax.broadcasted_iota(jnp.int32, (Q_BLOCK, Q_BLOCK), 0)
                    <= lax.broadcasted_iota(jnp.int32, (Q_BLOCK, Q_BLOCK), 1), 0.0, NEG_INF)

    def scores(j, qb, s_ref, mt_ref):
        i = pl.program_id(2) * SEL_QB + qb
        k0 = pl.multiple_of(j * tk, tk)
        st = _dot(ks_ref[pl.ds(k0, tk), :], qpads[qb])
        sel_rows = selt_ref[pl.ds(pl.multiple_of(j * bpt, bpt), bpt),
                            qb * Q_BLOCK:(qb + 1) * Q_BLOCK]
        bias_rows = jnp.where(sel_rows > 0.5, 0.0, NEG_INF)
        parts = []
        for u in range(spt):
            sub = jnp.concatenate(
                [jnp.broadcast_to(bias_rows[s:s + 1, :], (SEL_BLOCK, Q_BLOCK))
                 for s in range(u * Q_BLOCK // SEL_BLOCK, (u + 1) * Q_BLOCK // SEL_BLOCK)], axis=0)
            parts.append(sub + jnp.where(j * spt + u == i, tri, 0.0))
        bias = jnp.concatenate(parts, axis=0)
        st = st + jnp.concatenate([bias] * GROUP_R, axis=1)
        s_ref[qb] = st
        mt_ref[qb] = jnp.max(st, axis=0, keepdims=True)

    def consume(j, qb, s_ref, mt_ref):
        m_prev = m_ref[qb]
        m_new = jnp.maximum(m_prev, mt_ref[qb])
        m_safe = jnp.where(m_new == NEG_INF, 0.0, m_new)
        alpha = jnp.exp2(m_prev - m_safe)
        p = jnp.exp2((s_ref[qb] - m_safe).astype(BF16))
        acc_ref[qb] = alpha * acc_ref[qb] + _dot(vst_ref[j], p)
        m_ref[qb] = m_new

    for qb in range(SEL_QB):
        scores(0, qb, sa_ref, mta_ref)
        o_win = _window_block(qpads[qb], pl.program_id(2) * SEL_QB + qb, kw_ref, vwt_ref,
                              NSA_WINDOW)
        o_ref[qb * Q_BLOCK:(qb + 1) * Q_BLOCK, :] = _gate_untranspose(o_win, gate_ref, 2, qb)
    n_pairs = n_tiles // 2

    def step(jj, carry):
        j = 2 * jj
        for qb in range(SEL_QB):
            scores(j + 1, qb, sb_ref, mtb_ref)
            consume(j, qb, sa_ref, mta_ref)
        for qb in range(SEL_QB):
            scores(jnp.minimum(j + 2, n_tiles - 1), qb, sa_ref, mta_ref)
            consume(j + 1, qb, sb_ref, mtb_ref)
        return carry

    lax.fori_loop(0, n_pairs, step, 0)

    @pl.when(n_tiles % 2 == 1)
    def _():
        for qb in range(SEL_QB):
            consume(n_tiles - 1, qb, sa_ref, mta_ref)

    for qb in range(SEL_QB):
        o_t = acc_ref[qb, 0:d, :] / jnp.maximum(acc_ref[qb, d:d + 1, :], 1e-30)
        o_ref[qb * Q_BLOCK:(qb + 1) * Q_BLOCK, :] += _gate_untranspose(o_t, gate_ref, 1, qb)


def _sel_attn(qt, ks_rows, vs_t, sel_t, gates, kw_rows, vw_t):
    b, h, d, t = qt.shape
    g = h // GROUP_R
    ns = sel_t.shape[2]
    tk = SEL_KEY_TILE
    rq = GROUP_R * Q_BLOCK
    nq = SEL_QB
    qw = nq * Q_BLOCK
    assert tk % qw == 0
    return pl.pallas_call(
        _sel_attn_kernel,
        grid=(b, g, t // qw),
        in_specs=[
            pl.BlockSpec((None, GROUP_R, d, qw), lambda bi, gi, i: (bi, gi, 0, i)),
            pl.BlockSpec((None, t, KV_HEADS * d), lambda bi, gi, i: (bi, 0, 0)),
            pl.BlockSpec((None, None, t // tk, SEL_V_ROWS, tk), lambda bi, gi, i: (bi, gi, 0, 0, 0)),
            pl.BlockSpec((None, None, ns, qw), lambda bi, gi, i: (bi, gi, 0, i)),
            pl.BlockSpec((None, None, N_GATE_ROWS, qw), lambda bi, gi, i: (bi, gi, 0, i)),
            pl.BlockSpec((None, t, KV_HEADS * d), lambda bi, gi, i: (bi, 0, 0)),
            pl.BlockSpec((None, None, t // Q_BLOCK, SEL_V_ROWS, Q_BLOCK),
                         lambda bi, gi, i: (bi, gi, 0, 0, 0)),
        ],
        out_specs=pl.BlockSpec((None, qw, GROUP_R * d), lambda bi, gi, i: (bi, i, gi)),
        out_shape=jax.ShapeDtypeStruct((b, t, h * d), F32),
        scratch_shapes=[pltpu.VMEM((nq, tk, rq), F32), pltpu.VMEM((nq, tk, rq), F32),
                        pltpu.VMEM((nq, 1, rq), F32), pltpu.VMEM((nq, 1, rq), F32),
                        pltpu.VMEM((nq, 1, rq), F32), pltpu.VMEM((nq, SEL_V_ROWS, rq), F32)],
        compiler_params=_params(("parallel", "parallel", "arbitrary"), VMEM_LIMIT),
        name="nsa_selected",
    )(qt, ks_rows, vs_t, sel_t, gates, kw_rows, vw_t)


def _retention_body(xn, ti, lg_ref, w_ref, cos_ref, sin_ref, ng_ref, o_ref, r_ref):
    c = RET_CHUNK
    h, dk, dv = RET_HEADS, RET_QK_DIM, RET_V_DIM
    half = dk // 2
    k0, v0, g0 = h * dk, 2 * h * dk, 2 * h * dk + h * dv

    @pl.when(ti == 0)
    def _():
        r_ref[...] = jnp.zeros_like(r_ref)

    acc = _dot(xn, w_ref[...])
    ii = lax.broadcasted_iota(jnp.int32, (c, c), 0)
    jj = lax.broadcasted_iota(jnp.int32, (c, c), 1)
    dist = (ii - jj).astype(F32)
    jcol = lax.broadcasted_iota(jnp.int32, (c, 1), 0).astype(F32)
    for hh in range(h):
        lg = lg_ref[hh]
        dmask = jnp.where(dist >= 0, jnp.exp(dist * lg), 0.0)
        xi = jnp.exp((jcol + 1.0) * lg)
        zeta = jnp.exp((c - 1.0 - jcol) * lg)
        decay = jnp.exp(jnp.zeros((1, dv), F32) + c * lg)
        for cc in range(acc.shape[0] // c):
            rows = slice(cc * c, (cc + 1) * c)
            cosf = cos_ref[rows, :]
            sinf = sin_ref[rows, :]

            def rot(x):
                return x * cosf + jnp.concatenate([x[:, half:], x[:, :half]], axis=1) * sinf

            q = rot(acc[rows, hh * dk:(hh + 1) * dk])
            k = rot(acc[rows, k0 + hh * dk:k0 + (hh + 1) * dk]) * (dk ** -0.5)
            vb = acc[rows, v0 + hh * dv:v0 + (hh + 1) * dv].astype(BF16)
            att = _dot_nt(q.astype(BF16), k.astype(BF16)) * dmask
            o = _dot(att.astype(BF16), vb)
            r_prev = r_ref[hh]
            o = o + _dot((q * xi).astype(BF16), r_prev.astype(BF16))
            s_chunk = _dot_tn((k * zeta).astype(BF16), vb)
            r_ref[hh] = r_prev * decay + s_chunk
            mu = jnp.mean(o, axis=-1, keepdims=True)
            var = jnp.mean(jnp.square(o - mu), axis=-1, keepdims=True)
            on = (o - mu) * lax.rsqrt(var + EPS)
            gt = acc[rows, g0 + hh * dv:g0 + (hh + 1) * dv]
            o_ref[rows, hh * dv:(hh + 1) * dv] = (
                (gt * _sigmoid(gt)) * (on * ng_ref[:, hh * dv:(hh + 1) * dv]))


def _merge_kernel(x_ref, g_ref, wgate_ref, bias_ref, ya_ref, yb_ref, yc0_ref, yc1_ref, yd_ref,
                  wb_ref, wo_ref, o_ref):
    x = x_ref[...]
    d = x.shape[1]
    u = _rms_rows(x, g_ref[...]).astype(BF16)
    ys = (ya_ref[...], yb_ref[...], yc0_ref[...] + yc1_ref[...], yd_ref[...])
    merged = jnp.zeros(x.shape, F32)
    for n in range(N_BRANCH):
        logits = _dot(u, wgate_ref[:, n * d:(n + 1) * d]) + bias_ref[:, n * d:(n + 1) * d]
        merged = merged + _sigmoid(logits) * _dot(ys[n].astype(BF16), wb_ref[n])
    o_ref[...] = x + _dot(merged.astype(BF16), wo_ref[...])


def _merge(x2, g, w_gate, bias, ys, w_branch, w_out, tm=512):
    n, d = x2.shape
    w = MIX_W
    row = lambda width: pl.BlockSpec((tm, width), lambda i: (i, 0))
    full = lambda shape: pl.BlockSpec(shape, lambda i: tuple(0 for _ in shape),
                                      pipeline_mode=pl.Buffered(1))
    return pl.pallas_call(
        _merge_kernel,
        grid=(n // tm,),
        in_specs=[row(d), full((1, d)), full((d, N_BRANCH * d)), full((1, N_BRANCH * d))]
        + [row(w)] * len(ys) + [full((N_BRANCH, w, d)), full((d, d))],
        out_specs=row(d),
        out_shape=jax.ShapeDtypeStruct((n, d), F32),
        compiler_params=_params(("parallel",), VMEM_LIMIT),
        name="merge_out",
    )(x2, g.reshape(1, d), w_gate.astype(BF16), bias.reshape(1, N_BRANCH * d),
      *[y.reshape(n, w) for y in ys], w_branch.astype(BF16), w_out.astype(BF16))


def _mixers(x2, b, t, mix_norm, w_in, merge_gate_bias, swa_q_gain, swa_k_gain, swa_sinks, conv_w,
            nsa_q_gain, nsa_k_gain, cmp_pos_k, cmp_pos_v, cmp_wk1, cmp_wk2, cmp_wv1, cmp_wv2,
            ret_norm_gain, w_branch, w_out):
    hd = HEAD_DIM
    swa_q, swa_kv = SWA_HEADS * hd, SWA_KV_HEADS * hd
    nsa_q, nsa_kv = NSA_HEADS * hd, NSA_KV_HEADS * hd
    ret_qk, ret_v = RET_HEADS * RET_QK_DIM, RET_HEADS * RET_V_DIM
    n_gate = NSA_HEADS * 3
    o_a = 0
    o_b = o_a + swa_q + 2 * swa_kv
    o_c = o_b + 3 * MIX_W
    o_cg = o_c + nsa_q + 6 * nsa_kv
    o_d = o_cg + n_gate
    o_g = o_d + 2 * ret_qk + 2 * ret_v

    per_g = GROUP_R * 3
    w_gates = [jnp.pad(w_in[:, o_cg + gi * per_g:o_cg + (gi + 1) * per_g],
                       ((0, 0), (0, N_GATE_ROWS - per_g))) for gi in range(NSA_KV_HEADS)]
    w_t = jnp.concatenate([w_in[:, o_a:o_b], w_in[:, o_c:o_cg]] + w_gates, axis=1).T.astype(BF16)
    head_gains = jnp.stack([swa_q_gain, swa_k_gain, nsa_q_gain, nsa_k_gain[1], nsa_k_gain[2]])
    head_gains = jnp.broadcast_to(head_gains.astype(F32)[:, :, None], (5, hd, LANES))
    (a_qt, a_k, a_vt, c_qt, c_kc, c_vc, c_ks, c_vst, c_kw, c_vwt, gates_c, y_b, y_d) = _mixer_proj(
        x2, b, t, mix_norm, w_t, head_gains, w_in[:, o_b:o_c], conv_w, w_in[:, o_d:o_g],
        ret_norm_gain)

    y_a = _banded(a_qt, a_k, a_vt, SWA_WINDOW, sinks=swa_sinks)

    k_cmp, v_cmp = _compress(c_kc, c_vc, cmp_pos_k, cmp_pos_v, cmp_wk1, cmp_wk2, cmp_wv1, cmp_wv2,
                             nsa_k_gain[0])
    y_cmp, sel_t = _cmp_attn(c_qt, k_cmp, v_cmp.transpose(0, 2, 1), gates_c)
    y_sel_win = _sel_attn(c_qt, c_ks, c_vst, sel_t, gates_c, c_kw, c_vwt)

    return _merge(x2, mix_norm, w_in[:, o_g:], merge_gate_bias,
                  (y_a, y_b, y_cmp, y_sel_win, y_d), w_branch, w_out)


def kernel(x, ffn1_norm, ffn1_w_gate, ffn1_w_up, ffn1_w_down, mix_norm, w_in, merge_gate_bias, swa_q_gain, swa_k_gain, swa_sinks, conv_w, nsa_q_gain, nsa_k_gain, cmp_pos_k, cmp_pos_v, cmp_wk1, cmp_wk2, cmp_wv1, cmp_wv2, ret_norm_gain, w_branch, w_out, ffn2_norm, ffn2_w_gate, ffn2_w_up, ffn2_w_down):
    b, t, d = x.shape
    x2 = x.reshape(b * t, d)
    for l in range(ffn1_norm.shape[0]):
        x2 = _ffn(x2, ffn1_norm[l], ffn1_w_gate[l], ffn1_w_up[l], ffn1_w_down[l])
        x2 = _mixers(x2, b, t, mix_norm[l], w_in[l], merge_gate_bias[l], swa_q_gain[l],
                     swa_k_gain[l], swa_sinks[l], conv_w[l], nsa_q_gain[l], nsa_k_gain[l],
                     cmp_pos_k[l], cmp_pos_v[l], cmp_wk1[l], cmp_wk2[l], cmp_wv1[l], cmp_wv2[l],
                     ret_norm_gain[l], w_branch[l], w_out[l])
        x2 = _ffn(x2, ffn2_norm[l], ffn2_w_gate[l], ffn2_w_up[l], ffn2_w_down[l])
    return x2.reshape(b, t, d)
```
